```python
import math
import jax, jax.numpy as jnp
from jax import lax
import numpy as np

D_MODEL = 1024
BATCH = 32
SEQ = 256
DEPTH = 2
DEC_BATCH = 2
DEC_SEQ = 2048
PAST_LEN = 512

GRID_W = 64
EPS = 1e-6
H_A = 8
DK = 64
DV = 64
W_A = H_A * DV
QKV_W = 2 * H_A * DK + H_A * DV
CONV_K = 3
CHUNK = 64
W_B = 512
HY_ORDER = 2
HY_EMB = 33
HY_HID = 64
HY_DECAY_TARGET = 1e-2
HY_FAST_PCT = 0.3
HY_SLOW_PCT = 1.5
G_C = 8
DC = 64
W_C = G_C * DC
N_BRANCH = 3
D_FF = ((8 * D_MODEL + 3 * 256 - 1) // (3 * 256)) * 256
OFF_Z = QKV_W
OFF_B = OFF_Z + W_A
OFF_A = OFF_B + 2 * H_A
OFF_HY = OFF_A + 2 * H_A
OFF_FN = OFF_HY + (HY_ORDER + 1) * W_B
OFF_GATE = OFF_FN + W_C
D_IN = OFF_GATE + N_BRANCH * D_MODEL

kernel_name = "hybrid_deltanet_hyena_fnet_diffusion_step"


def rmsnorm(x, g):
    xf = x.astype(jnp.float32)
    y = xf * lax.rsqrt(jnp.mean(xf * xf, axis=-1, keepdims=True) + EPS)
    return (y * g.astype(jnp.float32)).astype(x.dtype)


def l2norm(t):
    return t * lax.rsqrt(jnp.sum(t * t, axis=-1, keepdims=True) + EPS)


def centred_dwconv(x, w):
    k = w.shape[0]
    p = k // 2
    L = x.shape[1]
    xp = jnp.pad(x, ((0, 0), (p, p), (0, 0)))
    out = xp[:, 0:L] * w[0]
    for i in range(1, k):
        out = out + xp[:, i:i + L] * w[i]
    return out


def grid_pos_embed(n_tokens):
    rows = n_tokens // GRID_W
    r = jnp.repeat(jnp.arange(rows), GRID_W).astype(jnp.float32)
    col = jnp.tile(jnp.arange(GRID_W), rows).astype(jnp.float32)
    quarter = D_MODEL // 4
    omega = 1.0 / (10000.0 ** (jnp.arange(quarter, dtype=jnp.float32) / quarter))

    def emb(pos):
        a = pos[:, None] * omega[None, :]
        return jnp.concatenate([jnp.sin(a), jnp.cos(a)], axis=-1)

    return jnp.concatenate([emb(r), emb(col)], axis=-1)


def gated_delta_chunked(q, k, v, g, beta, s0):
    bn, L, H, _ = q.shape
    n = L // CHUNK

    def chunks(t):
        t = t.reshape((bn, n, CHUNK, H) + t.shape[3:])
        return jnp.moveaxis(t, 3, 1)

    q, k, v, g, beta = chunks(q), chunks(k), chunks(v), chunks(g), chunks(beta)
    gc = jnp.cumsum(g, axis=-1)
    idx = jnp.arange(CHUNK)
    incl = idx[:, None] >= idx[None, :]
    strict = idx[:, None] > idx[None, :]
    decay = jnp.exp(jnp.where(incl, gc[..., :, None] - gc[..., None, :], -jnp.inf))
    kb = k * beta[..., None]
    a_low = jnp.where(strict, jnp.einsum('bhnid,bhnjd->bhnij', kb, k) * decay, 0.0)
    m = a_low + jnp.eye(CHUNK, dtype=a_low.dtype)
    u = lax.linalg.triangular_solve(m, v * beta[..., None], left_side=True, lower=True, unit_diagonal=True)
    w = lax.linalg.triangular_solve(m, kb * jnp.exp(gc)[..., None], left_side=True, lower=True, unit_diagonal=True)
    qk = jnp.einsum('bhnid,bhnjd->bhnij', q, k) * decay
    qg = q * jnp.exp(gc)[..., None]
    kg = k * jnp.exp(gc[..., -1:] - gc)[..., None]
    glast = jnp.exp(gc[..., -1])

    def step(s, inp):
        u_n, w_n, qk_n, qg_n, kg_n, gl_n = inp
        v_new = u_n - jnp.einsum('bhcd,bhde->bhce', w_n, s)
        o = jnp.einsum('bhcd,bhde->bhce', qg_n, s) + jnp.einsum('bhij,bhje->bhie', qk_n, v_new)
        s = s * gl_n[..., None, None] + jnp.einsum('bhcd,bhce->bhde', kg_n, v_new)
        return s, o

    xs = (jnp.moveaxis(u, 2, 0), jnp.moveaxis(w, 2, 0), jnp.moveaxis(qk, 2, 0),
          jnp.moveaxis(qg, 2, 0), jnp.moveaxis(kg, 2, 0), jnp.moveaxis(glast, 2, 0))
    s_fin, o = lax.scan(step, s0, xs)
    o = jnp.transpose(o, (1, 0, 3, 2, 4)).reshape(bn, L, H, v.shape[-1])
    return o, s_fin


def delta_mixer(qkv_raw, z, b_raw, a_raw, conv_w, a_log, dt_bias, norm_w, s0):
    bn, L, _ = qkv_raw.shape
    f32 = jnp.float32
    qkv = jax.nn.silu(centred_dwconv(qkv_raw, conv_w).astype(f32))
    q = l2norm(qkv[..., :H_A * DK].reshape(bn, L, H_A, DK)) * (DK ** -0.5)
    k = l2norm(qkv[..., H_A * DK:2 * H_A * DK].reshape(bn, L, H_A, DK))
    v = qkv[..., 2 * H_A * DK:].reshape(bn, L, H_A, DV)
    beta = jax.nn.sigmoid(b_raw.astype(f32))
    g = -jnp.exp(a_log.astype(f32)) * jax.nn.softplus(a_raw.astype(f32) + dt_bias.astype(f32))
    flip = lambda t: t[:, ::-1]
    o_f, s_f = gated_delta_chunked(q, k, v, g[:, :, 0], beta[:, :, 0], s0[:, 0])
    o_b, s_b = gated_delta_chunked(flip(q), flip(k), flip(v), flip(g[:, :, 1]), flip(beta[:, :, 1]), s0[:, 1])
    o = o_f + flip(o_b)
    o = rmsnorm(o, norm_w) * jax.nn.silu(z.astype(f32).reshape(bn, L, H_A, DV))
    return o.reshape(bn, L, W_A), jnp.stack([s_f, s_b], axis=1)


def hyena_filters(L, w1, b1, freq, w2, b2, w3):
    f32 = jnp.float32
    bands = (HY_EMB - 1) // 2
    t = jnp.linspace(0.0, 1.0, L, dtype=f32)[:, None]
    wpos = (2.0 * math.pi / L) * jnp.arange(L, dtype=f32)[:, None]
    fr = jnp.linspace(1e-4, bands - 1, bands, dtype=f32)[None, :]
    zpos = jnp.concatenate([t, jnp.cos(fr * wpos), -jnp.sin(fr * wpos)], axis=-1)
    fq = freq.astype(f32)
    h = jnp.sin(fq * (zpos @ w1.astype(f32) + b1.astype(f32)))
    h = jnp.sin(fq * (h @ w2.astype(f32) + b2.astype(f32)))
    h = h @ w3.astype(f32)
    deltas = jnp.abs(jnp.linspace(math.log(HY_DECAY_TARGET) / HY_SLOW_PCT,
                                  math.log(HY_DECAY_TARGET) / HY_FAST_PCT, W_B, dtype=f32))
    window = jnp.exp(-t * deltas[None, :])
    return h.reshape(L, 2 * HY_ORDER, W_B) * window[:, None, :]


def hyena_mixer(xh, conv_w, w1, b1, freq, w2, b2, w3, bias):
    bn, L, _ = xh.shape
    uc = centred_dwconv(xh, conv_w).astype(jnp.float32)
    x1, x2, v = jnp.split(uc, 3, axis=-1)
    hf = hyena_filters(L, w1, b1, freq, w2, b2, w3)
    hspec = jnp.fft.rfft(hf, n=2 * L, axis=0)
    hk = hspec[:, 0::2] + jnp.conj(hspec[:, 1::2])
    bias = bias.astype(jnp.float32)
    z = v
    for o, gate in enumerate((x1, x2)):
        zs = jnp.fft.rfft(z, n=2 * L, axis=1)
        conv = jnp.fft.irfft(zs * hk[None, :, o], n=2 * L, axis=1)[:, :L]
        z = gate * (conv + bias[o] * z)
    return z


def fourier_mixer(xc):
    bn, L, _ = xc.shape
    y = jnp.fft.fft2(xc.astype(jnp.float32).reshape(bn, L, G_C, DC), axes=(1, 3), norm='ortho').real
    return y.reshape(bn, L, W_C)


def parallel_mixer(h, s0, p):
    bn, L, _ = h.shape
    proj = h @ p['w_in']
    y_a, s_fin = delta_mixer(proj[..., :QKV_W], proj[..., OFF_Z:OFF_B],
                             proj[..., OFF_B:OFF_A].reshape(bn, L, 2, H_A),
                             proj[..., OFF_A:OFF_HY].reshape(bn, L, 2, H_A),
                             p['conv_qkv'], p['a_log'], p['dt_bias'], p['norm_a'], s0)
    y_b = hyena_mixer(proj[..., OFF_HY:OFF_FN], p['conv_hy'], p['hy_w1'], p['hy_b1'], p['hy_freq'],
                      p['hy_w2'], p['hy_b2'], p['hy_w3'], p['hy_bias'])
    y_c = fourier_mixer(proj[..., OFF_FN:OFF_GATE])
    gates = jax.nn.sigmoid(proj[..., OFF_GATE:].astype(jnp.float32)).reshape(bn, L, N_BRANCH, D_MODEL)
    merged = (gates[:, :, 0] * (y_a @ p['w_pa']) + gates[:, :, 1] * (y_b @ p['w_pb'])
              + gates[:, :, 2] * (y_c @ p['w_pc']))
    return merged.astype(h.dtype) @ p['w_o'], s_fin


def swiglu(h, w_gu, w_down):
    gu = h @ w_gu
    gate, up = jnp.split(gu, 2, axis=-1)
    return (jax.nn.silu(gate) * up) @ w_down


def trunk_layer(x, mod, s0, p):
    sh1, sc1, g1, sh2, sc2, g2 = jnp.split(mod, 6, axis=-1)
    h = rmsnorm(x, p['norm1_g']) * (1.0 + sc1) + sh1
    y, s_fin = parallel_mixer(h, s0, p)
    x = x + g1 * y
    h = rmsnorm(x, p['norm2_g']) * (1.0 + sc2) + sh2
    x = x + g2 * swiglu(h, p['w_gu'], p['w_down'])
    return x, s_fin


def setup_inputs(seed: int = 0) -> dict:
    key = jax.random.key(seed)
    keys = list(jax.random.split(key, 32))
    f32 = jnp.float32

    def nrm(shape, scale):
        return jax.random.normal(keys.pop(), shape, f32) * scale

    def gain(shape):
        return 1.0 + nrm(shape, 0.02)

    x_prompt = nrm((BATCH, SEQ, D_MODEL), 1.0)
    x_sample = nrm((DEC_BATCH, DEC_SEQ, D_MODEL), 1.0)
    state_delta = nrm((DEC_BATCH, DEPTH, 2, H_A, DK, DV), 0.1)
    c = nrm((DEC_BATCH, D_MODEL), 1.0)
    c_ctx = nrm((D_MODEL,), 1.0)
    w_mod = nrm((DEPTH, D_MODEL, 6 * D_MODEL), 0.5 * D_MODEL ** -0.5)
    b_mod = nrm((DEPTH, 6 * D_MODEL), 0.01)
    norm1_g = gain((DEPTH, D_MODEL))
    norm2_g = gain((DEPTH, D_MODEL))
    w_in = nrm((DEPTH, D_MODEL, D_IN), D_MODEL ** -0.5)
    conv_qkv = nrm((DEPTH, CONV_K, QKV_W), CONV_K ** -0.5)
    a_log = jnp.log(jax.random.uniform(keys.pop(), (DEPTH, 2, H_A), f32, 1.0, 16.0))
    dt = jnp.exp(jax.random.uniform(keys.pop(), (DEPTH, 2, H_A), f32, math.log(1e-3), math.log(1e-1)))
    dt_bias = dt + jnp.log(-jnp.expm1(-dt))
    norm_a = gain((DEPTH, DV))
    conv_hy = nrm((DEPTH, CONV_K, (HY_ORDER + 1) * W_B), CONV_K ** -0.5)
    hy_w1 = nrm((DEPTH, HY_EMB, HY_HID), HY_EMB ** -0.5)
    hy_b1 = nrm((DEPTH, HY_HID), 0.02)
    hy_freq = gain((DEPTH, HY_HID))
    hy_w2 = nrm((DEPTH, HY_HID, HY_HID), HY_HID ** -0.5)
    hy_b2 = nrm((DEPTH, HY_HID), 0.02)
    hy_w3 = nrm((DEPTH, HY_HID, 2 * HY_ORDER * W_B), 0.02)
    hy_bias = nrm((DEPTH, HY_ORDER, W_B), 0.5)
    w_pa = nrm((DEPTH, W_A, D_MODEL), W_A ** -0.5)
    w_pb = nrm((DEPTH, W_B, D_MODEL), W_B ** -0.5)
    w_pc = nrm((DEPTH, W_C, D_MODEL), W_C ** -0.5)
    w_o = nrm((DEPTH, D_MODEL, D_MODEL), D_MODEL ** -0.5)
    w_gu = nrm((DEPTH, D_MODEL, 2 * D_FF), D_MODEL ** -0.5)
    w_down = nrm((DEPTH, D_FF, D_MODEL), D_FF ** -0.5)
    norm_f = gain((D_MODEL,))
    return {"x_prompt": x_prompt, "x_sample": x_sample, "state_delta": state_delta,
            "c": c, "c_ctx": c_ctx, "w_mod": w_mod, "b_mod": b_mod,
            "norm1_g": norm1_g, "norm2_g": norm2_g, "w_in": w_in, "conv_qkv": conv_qkv,
            "a_log": a_log, "dt_bias": dt_bias, "norm_a": norm_a, "conv_hy": conv_hy,
            "hy_w1": hy_w1, "hy_b1": hy_b1, "hy_freq": hy_freq, "hy_w2": hy_w2,
            "hy_b2": hy_b2, "hy_w3": hy_w3, "hy_bias": hy_bias, "w_pa": w_pa,
            "w_pb": w_pb, "w_pc": w_pc, "w_o": w_o, "w_gu": w_gu, "w_down": w_down,
            "norm_f": norm_f}


def reference(x_prompt, x_sample, state_delta, c, c_ctx, w_mod, b_mod, norm1_g, norm2_g,
              w_in, conv_qkv, a_log, dt_bias, norm_a, conv_hy, hy_w1, hy_b1, hy_freq,
              hy_w2, hy_b2, hy_w3, hy_bias, w_pa, w_pb, w_pc, w_o, w_gu, w_down, norm_f):
    xp = x_prompt
    xs = x_sample + grid_pos_embed(x_sample.shape[1]).astype(x_sample.dtype)[None]
    s_zero = jnp.zeros((x_prompt.shape[0], 2, H_A, DK, DV), jnp.float32)
    ctx_states = []
    for l in range(DEPTH):
        p = {'w_in': w_in[l], 'conv_qkv': conv_qkv[l], 'a_log': a_log[l], 'dt_bias': dt_bias[l],
             'norm_a': norm_a[l], 'conv_hy': conv_hy[l], 'hy_w1': hy_w1[l], 'hy_b1': hy_b1[l],
             'hy_freq': hy_freq[l], 'hy_w2': hy_w2[l], 'hy_b2': hy_b2[l], 'hy_w3': hy_w3[l],
             'hy_bias': hy_bias[l], 'w_pa': w_pa[l], 'w_pb': w_pb[l], 'w_pc': w_pc[l],
             'w_o': w_o[l], 'w_gu': w_gu[l], 'w_down': w_down[l],
             'norm1_g': norm1_g[l], 'norm2_g': norm2_g[l]}
        mod_ctx = (jax.nn.silu(c_ctx) @ w_mod[l] + b_mod[l])[None, None, :]
        mod_lat = (jax.nn.silu(c) @ w_mod[l] + b_mod[l])[:, None, :]
        xp, s_ctx = trunk_layer(xp, mod_ctx, s_zero, p)
        ctx_states.append(s_ctx)
        xs, _ = trunk_layer(xs, mod_lat, state_delta[:, l].astype(jnp.float32), p)
    y_prompt = rmsnorm(xp, norm_f)
    y_sample = rmsnorm(xs, norm_f)
    new_state_delta = jnp.stack(ctx_states, axis=1).astype(x_prompt.dtype)
    return (y_prompt, y_sample, new_state_delta)
```

```python
import functools
import math

import jax
import jax.numpy as jnp
from jax import lax
from jax.experimental import pallas as pl
from jax.experimental.pallas import tpu as pltpu

F32 = jnp.float32
BF16 = jnp.bfloat16

D_MODEL = 1024
N_CTX_SEQ = 32
L_CTX = 256
DEPTH = 2
N_LAT_SEQ = 2
L_LAT = 2048
GRID_W = 64
EPS = 1e-6
H_A = 8
DK = 64
DV = 64
W_A = H_A * DV
QKV_W = 2 * H_A * DK + H_A * DV
CHUNK = 64
W_B = 512
HY_EMB = 33
HY_HID = 64
HY_DECAY_TARGET = 1e-2
HY_FAST_PCT = 0.3
HY_SLOW_PCT = 1.5
G_C = 8
DC = 64
W_C = G_C * DC
D_FF = ((8 * D_MODEL + 3 * 256 - 1) // (3 * 256)) * 256
OFF_Z = QKV_W
OFF_B = OFF_Z + W_A
OFF_A = OFF_B + 2 * H_A
OFF_HY = OFF_A + 2 * H_A
OFF_FN = OFF_HY + 3 * W_B
OFF_GATE = OFF_FN + W_C

T_CTX = N_CTX_SEQ * L_CTX
T_LAT = N_LAT_SEQ * L_LAT
T_ALL = T_CTX + T_LAT
ROW_TILE = 256
LANE = 128
PD_W = QKV_W + W_A + LANE
HEAD_GROUP = 4
GROUP_W = HEAD_GROUP * DK
SOLVE_PASSES = 3
VMEM_LIMIT = 56 * 1024 * 1024


def _cparams(sem, vmem=None):
    return pltpu.CompilerParams(dimension_semantics=sem, vmem_limit_bytes=vmem)


def _dot(a, b):
    return jnp.dot(a, b, preferred_element_type=F32)


def _dot_nt(a, b):
    return lax.dot_general(a, b, (((1,), (1,)), ((), ())), preferred_element_type=F32)


def _dot_tn(a, b):
    return lax.dot_general(a, b, (((0,), (0,)), ((), ())), preferred_element_type=F32)


def _split(a, n):
    parts = []
    rem = a
    for i in range(n):
        p = rem.astype(BF16)
        parts.append(p)
        if i + 1 < n:
            rem = rem - p.astype(F32)
    return parts


def _mm(a, b, passes):
    if passes == 1:
        return _dot(a.astype(BF16), b.astype(BF16))
    ah, al = _split(a, 2)
    bh, bl = _split(b, 2)
    return _dot(ah, bh) + (_dot(ah, bl) + _dot(al, bh))


def _mm_const_rhs(a, c, n):
    out = None
    for p in _split(a, n):
        t = _dot(p, c)
        out = t if out is None else out + t
    return out


def _mm_const_lhs(c, b, n):
    out = None
    for p in _split(b, n):
        t = _dot(c, p)
        out = t if out is None else out + t
    return out


def _sigmoid(x):
    return 1.0 / (1.0 + jnp.exp(-x))


def _silu(x):
    return x * _sigmoid(x)


def _softplus(x):
    return jnp.maximum(x, 0.0) + jnp.log(1.0 + jnp.exp(-jnp.abs(x)))


def _mod_row_block(i):
    n_ctx = T_CTX // ROW_TILE
    per_lat = L_LAT // ROW_TILE
    return jnp.where(i < n_ctx, 0, 1 + (i - n_ctx) // per_lat)


MOD_TN = 512


def _mod_kernel(st_ref, w_ref, b_ref, out_ref):
    s = _silu(st_ref[...])
    w = w_ref[0]
    rows = [jnp.sum(s[:, r:r + 1] * w, axis=0, keepdims=True) + b_ref[0] for r in range(3)]
    rows.append(jnp.zeros((5, MOD_TN), F32))
    out_ref[0] = jnp.concatenate(rows, axis=0)


def _mod_call(st, w_mod, b_mod):
    n6 = 6 * D_MODEL
    return pl.pallas_call(
        _mod_kernel,
        grid=(DEPTH, n6 // MOD_TN),
        in_specs=[
            pl.BlockSpec((D_MODEL, 8), lambda l, j: (0, 0)),
            pl.BlockSpec((1, D_MODEL, MOD_TN), lambda l, j: (l, 0, j)),
            pl.BlockSpec((1, 1, MOD_TN), lambda l, j: (l, 0, j)),
        ],
        out_specs=pl.BlockSpec((1, 8, MOD_TN), lambda l, j: (l, 0, j)),
        out_shape=jax.ShapeDtypeStruct((DEPTH, 8, n6), F32),
        compiler_params=_cparams(("parallel", "parallel")),
        name="adaln_mod",
    )(st, w_mod, b_mod.reshape(DEPTH, 1, n6))


INPROJ_TN = 512


def _rms_mod(x, g, scale, shift):
    ms = jnp.mean(x * x, axis=-1, keepdims=True)
    return (x * lax.rsqrt(ms + EPS) * g) * (1.0 + scale) + shift


def _inproj_kernel(x_ref, mod_ref, g_ref, wd_ref, wh_ref, wf_ref, wg_ref,
                   pd_ref, ph_ref, pf_ref, pg_ref):
    m = mod_ref[0]
    h = _rms_mod(x_ref[...], g_ref[...], m[1:2], m[0:1]).astype(BF16)
    for w_ref, o_ref in ((wd_ref, pd_ref), (wh_ref, ph_ref), (wf_ref, pf_ref), (wg_ref, pg_ref)):
        n = w_ref.shape[1]
        for c in range(0, n, INPROJ_TN):
            e = min(c + INPROJ_TN, n)
            o_ref[:, c:e] = _dot(h, w_ref[:, c:e])


def _inproj_call(x, mod3, g, wd, wh, wf, wg):
    widths = (wd.shape[1], wh.shape[1], wf.shape[1], wg.shape[1])
    row = lambda i: (i, 0)
    const = lambda i: (0, 0)
    return pl.pallas_call(
        _inproj_kernel,
        grid=(T_ALL // ROW_TILE,),
        in_specs=[
            pl.BlockSpec((ROW_TILE, D_MODEL), row),
            pl.BlockSpec((1, 6, D_MODEL), lambda i: (_mod_row_block(i), 0, 0)),
            pl.BlockSpec((1, D_MODEL), const),
        ] + [pl.BlockSpec((D_MODEL, w), const) for w in widths],
        out_specs=[pl.BlockSpec((ROW_TILE, w), row) for w in widths],
        out_shape=[jax.ShapeDtypeStruct((T_ALL, w), F32) for w in widths],
        compiler_params=_cparams(("parallel",), VMEM_LIMIT),
        name="inproj",
    )(x, mod3, g.reshape(1, D_MODEL), wd, wh, wf, wg)


def _conv3_rows(cur, prev_row, next_row, w):
    n = cur.shape[0]
    ridx = lax.broadcasted_iota(jnp.int32, cur.shape, 0)
    up = jnp.where(ridx == 0, prev_row, pltpu.roll(cur, 1, 0))
    dn = jnp.where(ridx == n - 1, next_row, pltpu.roll(cur, n - 1, 0))
    return up * w[0:1] + cur * w[1:2] + dn * w[2:3]


def _halo_specs(width, rows_per_block, n_row_blocks, blk_of):
    per = rows_per_block // 8
    last = n_row_blocks * per - 1
    prev = pl.BlockSpec((8, width), lambda *a: (jnp.maximum(blk_of(*a) * per - 1, 0), 0))
    nxt = pl.BlockSpec((8, width), lambda *a: (jnp.minimum((blk_of(*a) + 1) * per, last), 0))
    return prev, nxt


def _hyprep_kernel(blocks_ctx, per_lat, ph_ref, prev_ref, next_ref, w_ref, v_ref, x12_ref):
    i = pl.program_id(0)
    lat_pos = (i - blocks_ctx) % per_lat
    first = (i < blocks_ctx) | (lat_pos == 0)
    last = (i < blocks_ctx) | (lat_pos == per_lat - 1)
    prev_row = jnp.where(first, 0.0, prev_ref[7:8, :])
    next_row = jnp.where(last, 0.0, next_ref[0:1, :])
    uc = _conv3_rows(ph_ref[...], prev_row, next_row, w_ref[...])
    x12_ref[0] = uc[:, 0:W_B]
    x12_ref[1] = uc[:, W_B:2 * W_B]
    v_ref[...] = uc[:, 2 * W_B:3 * W_B]


def _hyprep_call(ph, conv_w):
    nblk = T_ALL // ROW_TILE
    blk = lambda i: i
    prev, nxt = _halo_specs(3 * W_B, ROW_TILE, nblk, blk)
    kern = functools.partial(_hyprep_kernel, T_CTX // ROW_TILE, L_LAT // ROW_TILE)
    assert L_CTX == ROW_TILE
    return pl.pallas_call(
        kern,
        grid=(nblk,),
        in_specs=[pl.BlockSpec((ROW_TILE, 3 * W_B), lambda i: (i, 0)), prev, nxt,
                  pl.BlockSpec((3, 3 * W_B), lambda i: (0, 0))],
        out_specs=[pl.BlockSpec((ROW_TILE, W_B), lambda i: (i, 0)),
                   pl.BlockSpec((2, ROW_TILE, W_B), lambda i: (0, i, 0))],
        out_shape=[jax.ShapeDtypeStruct((T_ALL, W_B), F32),
                   jax.ShapeDtypeStruct((2, T_ALL, W_B), F32)],
        compiler_params=_cparams(("parallel",)),
        name="hyena_prep",
    )(ph, ph, ph, conv_w)


FILT_RT = 256


def _filter_kernel(L, zpos_ref, w1_ref, b1_ref, fq_ref, w2_ref, b2_ref, w3_ref, win_ref,
                   ftop_ref, fbot_ref, k_ref, hs_s, hm_s, krl_s):
    rt = pl.program_id(1)

    @pl.when(rt == 0)
    def _():
        fq = fq_ref[...]
        alt_acc = jnp.zeros((1, W_B), F32)
        for r0 in range(0, L, FILT_RT):
            rows = slice(r0, r0 + FILT_RT)
            h = jnp.sin(fq * (_mm(zpos_ref[rows, :], w1_ref[...], 3) + b1_ref[...]))
            h = jnp.sin(fq * (_mm(h, w2_ref[...], 3) + b2_ref[...]))
            hf = _mm(h, w3_ref[...], 3)
            win = win_ref[rows, :]
            fw = hf[:, 0:W_B] * win
            bw = hf[:, W_B:2 * W_B] * win
            hsum = fw + bw
            hs_s[rows, :] = hsum.astype(BF16)
            hm_s[rows, :] = (fw - bw).astype(BF16)
            t = lax.broadcasted_iota(jnp.int32, hsum.shape, 0)
            alt_acc = alt_acc + jnp.sum(jnp.where(t % 2 == 0, hsum, -hsum), axis=0, keepdims=True)
        krl_s[...] = jnp.broadcast_to(alt_acc, krl_s.shape)

    p1 = _dot(ftop_ref[...], hs_s[...])
    p2 = _dot(fbot_ref[...], hm_s[...])
    grow = rt * FILT_RT + lax.broadcasted_iota(jnp.int32, p1.shape, 0)
    k_ref[0, 0] = p1
    k_ref[0, 1] = jnp.where(grow == 0, krl_s[0:1, :], p1)
    k_ref[0, 2] = jnp.where(grow == 0, 0.0, p2)


def _filter_call(L, zpos, w1p, b1, fq, w2, b2, w3, win, fwd):
    nrt = L // FILT_RT
    c2 = lambda o, r: (0, 0)
    return pl.pallas_call(
        functools.partial(_filter_kernel, L),
        grid=(2, nrt),
        in_specs=[
            pl.BlockSpec((L, LANE), c2),
            pl.BlockSpec((LANE, HY_HID), c2),
            pl.BlockSpec((1, HY_HID), c2),
            pl.BlockSpec((1, HY_HID), c2),
            pl.BlockSpec((HY_HID, HY_HID), c2),
            pl.BlockSpec((1, HY_HID), c2),
            pl.BlockSpec((HY_HID, 2 * W_B), lambda o, r: (0, o)),
            pl.BlockSpec((L, W_B), c2),
            pl.BlockSpec((FILT_RT, L), lambda o, r: (r, 0)),
            pl.BlockSpec((FILT_RT, L), lambda o, r: (nrt + r, 0)),
        ],
        out_specs=pl.BlockSpec((1, 3, FILT_RT, W_B), lambda o, r: (o, 0, r, 0)),
        out_shape=jax.ShapeDtypeStruct((2, 3, L, W_B), F32),
        scratch_shapes=[pltpu.VMEM((L, W_B), BF16), pltpu.VMEM((L, W_B), BF16),
                        pltpu.VMEM((8, W_B), F32)],
        compiler_params=_cparams(("parallel", "arbitrary"), VMEM_LIMIT),
        name=f"hyena_filter_{L}",
    )(zpos, w1p, b1, fq, w2, b2, w3, win, fwd, fwd)


def _hyena_kernel(nft, v_ref, gate_ref, k_ref, bias_ref, ftop_ref, fbot_ref, itop_ref, ibot_ref,
                  out_ref, zf_s, zb_s, acc_s):
    o = pl.program_id(1)
    f = pl.program_id(2)

    @pl.when((o == 0) & (f == 0))
    def _():
        v = v_ref[...]
        zf_s[...] = v
        zb_s[...] = v.astype(BF16)

    zb = zb_s[...]
    top = _dot(ftop_ref[...], zb)
    bot = _dot(fbot_ref[...], zb)
    krt = k_ref[0, 0]
    krb = k_ref[0, 1]
    ki = k_ref[0, 2]
    yt = (top * krt - bot * ki).astype(BF16)
    yb = (top * ki + bot * krb).astype(BF16)
    part = _dot(itop_ref[...], yt) + _dot(ibot_ref[...], yb)

    @pl.when(f == 0)
    def _():
        acc_s[...] = part

    @pl.when(f != 0)
    def _():
        acc_s[...] += part

    @pl.when(f == nft - 1)
    def _():
        znew = gate_ref[0] * (acc_s[...] + bias_ref[0] * zf_s[...])
        zf_s[...] = znew
        zb_s[...] = znew.astype(BF16)
        out_ref[...] = znew


def _hyena_call(v, x12, kspec, bias, fwd, inv, n_seq, L, row_blk0, ft):
    nft = L // ft
    seq_blk = lambda s, o, f: (row_blk0 + s, 0)
    return pl.pallas_call(
        functools.partial(_hyena_kernel, nft),
        grid=(n_seq, 2, nft),
        in_specs=[
            pl.BlockSpec((L, W_B), seq_blk),
            pl.BlockSpec((1, L, W_B), lambda s, o, f: (o, row_blk0 + s, 0)),
            pl.BlockSpec((1, 3, ft, W_B), lambda s, o, f: (o, 0, f, 0)),
            pl.BlockSpec((1, 1, W_B), lambda s, o, f: (o, 0, 0)),
            pl.BlockSpec((ft, L), lambda s, o, f: (f, 0)),
            pl.BlockSpec((ft, L), lambda s, o, f: (nft + f, 0)),
            pl.BlockSpec((L, ft), lambda s, o, f: (0, f)),
            pl.BlockSpec((L, ft), lambda s, o, f: (0, nft + f)),
        ],
        out_specs=pl.BlockSpec((L, W_B), lambda s, o, f: (s, 0)),
        out_shape=jax.ShapeDtypeStruct((n_seq * L, W_B), F32),
        scratch_shapes=[pltpu.VMEM((L, W_B), F32), pltpu.VMEM((L, W_B), BF16),
                        pltpu.VMEM((L, W_B), F32)],
        compiler_params=_cparams(("parallel", "arbitrary", "arbitrary"), VMEM_LIMIT),
        name=f"hyena_conv_{L}",
    )(v, x12, kspec, bias, fwd, fwd, inv, inv)


FNET_RT = 256


def _fnet_kernel(L, x_ref, m_ref, bdc_ref, bds_ref, out_ref, xcs_s):
    r = pl.program_id(1)

    @pl.when(r == 0)
    def _():
        for r0 in range(0, L, FNET_RT):
            xb = x_ref[r0:r0 + FNET_RT, :].astype(BF16)
            xcs_s[r0:r0 + FNET_RT, :] = _dot(xb, bdc_ref[...]).astype(BF16)
            xcs_s[L + r0:L + r0 + FNET_RT, :] = _dot(xb, bds_ref[...]).astype(BF16)

    out_ref[...] = _dot(m_ref[...], xcs_s[...]) * (1.0 / math.sqrt(DC * L))


def _fnet_call(pf, mcs, bdc, bds, n_seq, L, row_blk0):
    nrt = L // FNET_RT
    return pl.pallas_call(
        functools.partial(_fnet_kernel, L),
        grid=(n_seq, nrt),
        in_specs=[
            pl.BlockSpec((L, W_C), lambda s, r: (row_blk0 + s, 0)),
            pl.BlockSpec((FNET_RT, 2 * L), lambda s, r: (r, 0)),
            pl.BlockSpec((W_C, W_C), lambda s, r: (0, 0)),
            pl.BlockSpec((W_C, W_C), lambda s, r: (0, 0)),
        ],
        out_specs=pl.BlockSpec((FNET_RT, W_C), lambda s, r: (s * nrt + r, 0)),
        out_shape=jax.ShapeDtypeStruct((n_seq * L, W_C), F32),
        scratch_shapes=[pltpu.VMEM((2 * L, W_C), BF16)],
        compiler_params=_cparams(("parallel", "arbitrary"), VMEM_LIMIT),
        name=f"fnet_{L}",
    )(pf, mcs, bdc, bds)


DELTA_RB = 256
CHUNKS_PER_RB = DELTA_RB // CHUNK


def _block_diag(y, same_head):
    return jnp.where(same_head, jnp.concatenate([y] * HEAD_GROUP, axis=0), 0.0)


def _delta_kernel(nb, zero_init, *refs):
    if zero_init:
        (pd_ref, prev_ref, next_ref, cw_ref, par_ref, esel_ref, ones_ref,
         o_ref, sfin_ref, q_s, k_s, v_s, be_s, gc_s, gr_s, st_s) = refs
        s0_ref = None
    else:
        (pd_ref, prev_ref, next_ref, cw_ref, par_ref, esel_ref, ones_ref, s0_ref,
         o_ref, sfin_ref, q_s, k_s, v_s, be_s, gc_s, gr_s, st_s) = refs
    d = pl.program_id(0)
    j = pl.program_id(2)
    fwd = d == 0
    pos = jnp.where(fwd, j, nb - 1 - j)

    ri = lax.broadcasted_iota(jnp.int32, (CHUNK, GROUP_W), 0)
    cj = lax.broadcasted_iota(jnp.int32, (CHUNK, GROUP_W), 1) % CHUNK
    ixj = ri ^ cj
    later = jnp.where(fwd, ri, cj)
    earlier = jnp.where(fwd, cj, ri)
    incl = later >= earlier
    strict = later > earlier
    eye = ixj == 0
    br = lax.broadcasted_iota(jnp.int32, (GROUP_W, GROUP_W), 0) // CHUNK
    bc = lax.broadcasted_iota(jnp.int32, (GROUP_W, GROUP_W), 1) // CHUNK
    same_head = br == bc
    ti = lax.broadcasted_iota(jnp.int32, (CHUNK, CHUNK), 0)
    tm = lax.broadcasted_iota(jnp.int32, (CHUNK, CHUNK), 1)
    tri = jnp.where(jnp.where(fwd, tm, ti) <= jnp.where(fwd, ti, tm), 1.0, 0.0).astype(BF16)
    ri8 = lax.broadcasted_iota(jnp.int32, (CHUNK, H_A * DK), 0)
    cj8 = lax.broadcasted_iota(jnp.int32, (CHUNK, H_A * DK), 1) % CHUNK
    tt = jnp.where(fwd, ri8, cj8) <= jnp.where(fwd, cj8, ri8)

    @pl.when(j == 0)
    def _():
        for g in range(H_A // HEAD_GROUP):
            if zero_init:
                st_s[g] = jnp.zeros((GROUP_W, GROUP_W), F32)
            else:
                nat = s0_ref[0, 0, g * GROUP_W:(g + 1) * GROUP_W, :]
                st_s[g] = jnp.where(same_head, jnp.concatenate([nat] * HEAD_GROUP, axis=1), 0.0)

    cw = cw_ref[...]
    a_neg = -jnp.exp(par_ref[0:1, :])
    dtb = par_ref[1:2, :]
    e_beta = esel_ref[0, 0]
    e_g = esel_ref[0, 1]
    for r in range(CHUNKS_PER_RB):
        rows = slice(r * CHUNK, (r + 1) * CHUNK)
        cur = pd_ref[rows, 0:QKV_W]
        if r == 0:
            prev_row = jnp.where(pos > 0, prev_ref[7:8, :], 0.0)
        else:
            prev_row = pd_ref[r * CHUNK - 1:r * CHUNK, 0:QKV_W]
        if r == CHUNKS_PER_RB - 1:
            next_row = jnp.where(pos < nb - 1, next_ref[0:1, :], 0.0)
        else:
            next_row = pd_ref[(r + 1) * CHUNK:(r + 1) * CHUNK + 1, 0:QKV_W]
        qkv = _silu(_conv3_rows(cur, prev_row, next_row, cw))
        q = qkv[:, 0:H_A * DK]
        k = qkv[:, H_A * DK:2 * H_A * DK]
        qn = q * lax.rsqrt(_mm_const_rhs(q * q, ones_ref[...], 2) + EPS) * (DK ** -0.5)
        kn = k * lax.rsqrt(_mm_const_rhs(k * k, ones_ref[...], 2) + EPS)
        ba = pd_ref[rows, OFF_B:OFF_B + LANE]
        beta = _mm_const_rhs(_sigmoid(ba), e_beta, 3)
        gexp = _mm_const_rhs(a_neg * _softplus(ba + dtb), e_g, 3)
        q_s[rows, :] = qn
        k_s[rows, :] = kn
        v_s[rows, :] = qkv[:, 2 * H_A * DK:]
        be_s[rows, :] = beta
        gc_s[rows, :] = _mm_const_lhs(tri, gexp, 3)
        grow = jnp.sum(jnp.where(tt, gexp, 0.0), axis=0, keepdims=True)
        gr_s[r * 8:(r + 1) * 8, :] = jnp.broadcast_to(grow, (8, H_A * DK))

    def chunk_step(c, carry):
        ci = jnp.where(fwd, c, CHUNKS_PER_RB - 1 - c)
        r0 = pl.multiple_of(ci * CHUNK, CHUNK)
        g8 = pl.multiple_of(ci * 8, 8)
        for g in range(H_A // HEAD_GROUP):
            lanes = slice(g * GROUP_W, (g + 1) * GROUP_W)
            qn = q_s[pl.ds(r0, CHUNK), lanes]
            kn = k_s[pl.ds(r0, CHUNK), lanes]
            vv = v_s[pl.ds(r0, CHUNK), lanes]
            be = be_s[pl.ds(r0, CHUNK), lanes]
            gcc = gc_s[pl.ds(r0, CHUNK), lanes]
            gcr = gr_s[pl.ds(g8, 1), lanes]
            kb = kn * be
            kbd = _block_diag(kn, same_head).astype(BF16)
            kq = _dot_nt(jnp.concatenate([kb, qn], axis=0).astype(BF16), kbd)
            dec = jnp.exp(jnp.where(incl, gcc - gcr, -1e30))
            a = jnp.where(strict, kq[0:CHUNK] * dec, 0.0)
            p = kq[CHUNK:2 * CHUNK] * dec
            x = jnp.where(eye, 1.0, 0.0) - jnp.where(ixj == 1, a, 0.0)
            for lvl in range(1, 6):
                rmat = jnp.where((ixj >> lvl) == 1, a, 0.0)
                t1 = _mm(x, _block_diag(rmat, same_head), SOLVE_PASSES)
                x = x - _mm(t1, _block_diag(x, same_head), SOLVE_PASSES)
            eg = jnp.exp(gcc)
            rhs = jnp.concatenate([_block_diag(vv * be, same_head),
                                   _block_diag(kb * eg, same_head)], axis=1)
            uw = _mm(x, rhs, SOLVE_PASSES)
            u = uw[:, 0:GROUP_W]
            w = uw[:, GROUP_W:2 * GROUP_W]
            s = st_s[g]
            sb = s.astype(BF16)
            ws_qs = _dot(jnp.concatenate([w, qn * eg], axis=0).astype(BF16), sb)
            v_new = u - ws_qs[0:CHUNK]
            o = ws_qs[CHUNK:2 * CHUNK] + _dot(p.astype(BF16), _block_diag(v_new, same_head).astype(BF16))
            gcl = jnp.where(fwd, gcc[CHUNK - 1:CHUNK, :], gcc[0:1, :])
            kg = kn * jnp.exp(gcl - gcc)
            upd = _dot_tn(kg.astype(BF16), v_new.astype(BF16))
            st_s[g] = s * jnp.exp(gcl) + jnp.where(same_head, upd, 0.0)
            o_ref[0, pl.ds(r0, CHUNK), lanes] = o
        return carry

    lax.fori_loop(0, CHUNKS_PER_RB, chunk_step, 0)

    @pl.when(j == nb - 1)
    def _():
        for g in range(H_A // HEAD_GROUP):
            s = st_s[g]
            nat = s[:, 0:DV]
            for hh in range(1, HEAD_GROUP):
                nat = nat + s[:, hh * DV:(hh + 1) * DV]
            sfin_ref[0, 0, g * GROUP_W:(g + 1) * GROUP_W, :] = nat


def _delta_call(pd, conv_w, par, esel, ones_bd, s0, n_seq, nb, blk0):
    zero_init = s0 is None
    n_blocks_all = T_ALL // DELTA_RB
    blk_of = lambda d, s, j: blk0 + s * nb + j + d * (nb - 1 - 2 * j)
    prev, nxt = _halo_specs(QKV_W, DELTA_RB, n_blocks_all, blk_of)
    in_specs = [
        pl.BlockSpec((DELTA_RB, PD_W), lambda d, s, j: (blk_of(d, s, j), 0)),
        prev, nxt,
        pl.BlockSpec((3, QKV_W), lambda d, s, j: (0, 0)),
        pl.BlockSpec((8, LANE), lambda d, s, j: (0, 0)),
        pl.BlockSpec((1, 2, LANE, H_A * DK), lambda d, s, j: (d, 0, 0, 0)),
        pl.BlockSpec((H_A * DK, H_A * DK), lambda d, s, j: (0, 0)),
    ]
    args = [pd, pd, pd, conv_w, par, esel, ones_bd]
    if not zero_init:
        in_specs.append(pl.BlockSpec((1, 1, H_A * DK, DV), lambda d, s, j: (s, d, 0, 0)))
        args.append(s0)
    rows = n_seq * nb * DELTA_RB
    out_blk = lambda d, s, j: (d, s * nb + j + d * (nb - 1 - 2 * j), 0)
    return pl.pallas_call(
        functools.partial(_delta_kernel, nb, zero_init),
        grid=(2, n_seq, nb),
        in_specs=in_specs,
        out_specs=[pl.BlockSpec((1, DELTA_RB, W_A), out_blk),
                   pl.BlockSpec((1, 1, H_A * DK, DV), lambda d, s, j: (s, d, 0, 0))],
        out_shape=[jax.ShapeDtypeStruct((2, rows, W_A), F32),
                   jax.ShapeDtypeStruct((n_seq, 2, H_A * DK, DV), F32)],
        scratch_shapes=[pltpu.VMEM((DELTA_RB, H_A * DK), F32) for _ in range(5)]
        + [pltpu.VMEM((CHUNKS_PER_RB * 8, H_A * DK), F32),
           pltpu.VMEM((H_A // HEAD_GROUP, GROUP_W, GROUP_W), F32)],
        compiler_params=_cparams(("parallel", "parallel", "arbitrary"), VMEM_LIMIT),
        name=f"deltanet_nb{nb}",
    )(*args)


def _merge_kernel(o2_ref, z_ref, yb_ref, yc_ref, pg_ref, x_ref, mod_ref, na_ref, ones_ref,
                  wpa_ref, wpb_ref, wpc_ref, wo_ref, out_ref):
    o = o2_ref[0] + o2_ref[1]
    ms = _mm_const_rhs(o * o, ones_ref[...], 2) * (1.0 / DV)
    ya = (o * lax.rsqrt(ms + EPS) * na_ref[...]) * _silu(z_ref[...])
    merged = (_sigmoid(pg_ref[:, 0:D_MODEL]) * _dot(ya.astype(BF16), wpa_ref[...])
              + _sigmoid(pg_ref[:, D_MODEL:2 * D_MODEL]) * _dot(yb_ref[...].astype(BF16), wpb_ref[...])
              + _sigmoid(pg_ref[:, 2 * D_MODEL:3 * D_MODEL]) * _dot(yc_ref[...].astype(BF16), wpc_ref[...]))
    y = _dot(merged.astype(BF16), wo_ref[...])
    out_ref[...] = x_ref[...] + mod_ref[0][2:3] * y


def _merge_call(o2, pd, yb, yc, pg, x, mod3, na512, ones_bd, wpa, wpb, wpc, wo):
    row = lambda i: (i, 0)
    const = lambda i: (0, 0)
    return pl.pallas_call(
        _merge_kernel,
        grid=(T_ALL // ROW_TILE,),
        in_specs=[
            pl.BlockSpec((2, ROW_TILE, W_A), lambda i: (0, i, 0)),
            pl.BlockSpec((ROW_TILE, W_A), lambda i: (i, OFF_Z // W_A)),
            pl.BlockSpec((ROW_TILE, W_B), row),
            pl.BlockSpec((ROW_TILE, W_C), row),
            pl.BlockSpec((ROW_TILE, 3 * D_MODEL), row),
            pl.BlockSpec((ROW_TILE, D_MODEL), row),
            pl.BlockSpec((1, 6, D_MODEL), lambda i: (_mod_row_block(i), 0, 0)),
            pl.BlockSpec((1, W_A), const),
            pl.BlockSpec((W_A, W_A), const),
            pl.BlockSpec((W_A, D_MODEL), const),
            pl.BlockSpec((W_B, D_MODEL), const),
            pl.BlockSpec((W_C, D_MODEL), const),
            pl.BlockSpec((D_MODEL, D_MODEL), const),
        ],
        out_specs=pl.BlockSpec((ROW_TILE, D_MODEL), row),
        out_shape=jax.ShapeDtypeStruct((T_ALL, D_MODEL), F32),
        compiler_params=_cparams(("parallel",), VMEM_LIMIT),
        name="branch_merge",
    )(o2, pd, yb, yc, pg, x, mod3, na512, ones_bd, wpa, wpb, wpc, wo)


FFN_TN = D_FF // 2


def _ffn_kernel(final_norm, x_ref, mod_ref, g_ref, wgu_ref, wdn_ref, nf_ref, out_ref):
    m = mod_ref[0]
    x = x_ref[...]
    h = _rms_mod(x, g_ref[...], m[4:5], m[3:4]).astype(BF16)
    acc = None
    for c in range(0, D_FF, FFN_TN):
        gate = _dot(h, wgu_ref[:, c:c + FFN_TN])
        up = _dot(h, wgu_ref[:, D_FF + c:D_FF + c + FFN_TN])
        part = _dot((_silu(gate) * up).astype(BF16), wdn_ref[c:c + FFN_TN, :])
        acc = part if acc is None else acc + part
    xn = x + m[5:6] * acc
    if final_norm:
        ms = jnp.mean(xn * xn, axis=-1, keepdims=True)
        xn = xn * lax.rsqrt(ms + EPS) * nf_ref[...]
    out_ref[...] = xn


def _ffn_call(x, mod3, g, wgu, wdn, nf, final_norm):
    row = lambda i: (i, 0)
    const = lambda i: (0, 0)
    return pl.pallas_call(
        functools.partial(_ffn_kernel, final_norm),
        grid=(T_ALL // ROW_TILE,),
        in_specs=[
            pl.BlockSpec((ROW_TILE, D_MODEL), row),
            pl.BlockSpec((1, 6, D_MODEL), lambda i: (_mod_row_block(i), 0, 0)),
            pl.BlockSpec((1, D_MODEL), const),
            pl.BlockSpec((D_MODEL, 2 * D_FF), const),
            pl.BlockSpec((D_FF, D_MODEL), const),
            pl.BlockSpec((1, D_MODEL), const),
        ],
        out_specs=pl.BlockSpec((ROW_TILE, D_MODEL), row),
        out_shape=jax.ShapeDtypeStruct((T_ALL, D_MODEL), F32),
        compiler_params=_cparams(("parallel",), VMEM_LIMIT),
        name="ffn_final" if final_norm else "ffn",
    )(x, mod3, g.reshape(1, D_MODEL), wgu, wdn, nf.reshape(1, D_MODEL))


def _grid_pos_embed(n_tokens):
    rows = n_tokens // GRID_W
    r = jnp.repeat(jnp.arange(rows), GRID_W).astype(F32)
    col = jnp.tile(jnp.arange(GRID_W), rows).astype(F32)
    quarter = D_MODEL // 4
    omega = 1.0 / (10000.0 ** (jnp.arange(quarter, dtype=F32) / quarter))

    def emb(pos):
        a = pos[:, None] * omega[None, :]
        return jnp.concatenate([jnp.sin(a), jnp.cos(a)], axis=-1)

    return jnp.concatenate([emb(r), emb(col)], axis=-1)


def _cos_sin_table(n_rows, n_cols, period):
    r = jnp.arange(n_rows, dtype=jnp.int32)[:, None]
    c = jnp.arange(n_cols, dtype=jnp.int32)[None, :]
    ang = ((r * c) % period).astype(F32) * (2.0 * math.pi / period)
    return jnp.cos(ang), jnp.sin(ang)


def _hyena_dft(L):
    n = 2 * L
    cos, sin = _cos_sin_table(L, L, n)
    alt = jnp.where(jnp.arange(L) % 2 == 0, 1.0, -1.0).astype(F32)
    first = (jnp.arange(L) == 0)
    fbot = jnp.where(first[:, None], alt[None, :], -sin)
    fwd = jnp.concatenate([cos, fbot], axis=0)
    wf = jnp.where(first, 1.0 / n, 2.0 / n).astype(F32)
    itop = cos * wf[None, :]
    ibot = jnp.where(first[None, :], alt[:, None] / n, -sin * (2.0 / n))
    inv = jnp.concatenate([itop, ibot], axis=1)
    return fwd.astype(BF16), inv.astype(BF16)


def _hyena_positions(L):
    bands = (HY_EMB - 1) // 2
    t = jnp.linspace(0.0, 1.0, L, dtype=F32)[:, None]
    wpos = (2.0 * math.pi / L) * jnp.arange(L, dtype=F32)[:, None]
    fr = jnp.linspace(1e-4, bands - 1, bands, dtype=F32)[None, :]
    zpos = jnp.concatenate([t, jnp.cos(fr * wpos), -jnp.sin(fr * wpos)], axis=-1)
    zpos = jnp.pad(zpos, ((0, 0), (0, LANE - HY_EMB)))
    deltas = jnp.abs(jnp.linspace(math.log(HY_DECAY_TARGET) / HY_SLOW_PCT,
                                  math.log(HY_DECAY_TARGET) / HY_FAST_PCT, W_B, dtype=F32))
    window = jnp.exp(-t * deltas[None, :])
    return zpos, window


def _fnet_tables(L):
    cl, sl = _cos_sin_table(L, L, L)
    mcs = jnp.concatenate([cl, -sl], axis=1).astype(BF16)
    return mcs


def _group_tables():
    c64, s64 = _cos_sin_table(DC, DC, DC)
    eye = jnp.eye(G_C, dtype=F32)
    return jnp.kron(eye, c64).astype(BF16), jnp.kron(eye, s64).astype(BF16)


def _head_tables():
    ones_bd = jnp.kron(jnp.eye(H_A, dtype=F32), jnp.ones((DK, DK), F32)).astype(BF16)
    lane = jnp.arange(LANE)[:, None]
    head = (jnp.arange(H_A * DK) // DK)[None, :]
    sel = []
    for d in range(2):
        sel.append(jnp.stack([(lane == d * H_A + head), (lane == 2 * H_A + d * H_A + head)]))
    esel = jnp.stack(sel).astype(BF16)
    return ones_bd, esel


def kernel(x_prompt, x_sample, state_delta, c, c_ctx, w_mod, b_mod, norm1_g, norm2_g, w_in, conv_qkv, a_log, dt_bias, norm_a, conv_hy, hy_w1, hy_b1, hy_freq, hy_w2, hy_b2, hy_w3, hy_bias, w_pa, w_pb, w_pc, w_o, w_gu, w_down, norm_f):
    assert x_prompt.shape == (N_CTX_SEQ, L_CTX, D_MODEL) and x_sample.shape == (N_LAT_SEQ, L_LAT, D_MODEL)
    xs = x_sample + _grid_pos_embed(L_LAT)[None]
    x = jnp.concatenate([x_prompt.reshape(T_CTX, D_MODEL), xs.reshape(T_LAT, D_MODEL)], axis=0)

    st = jnp.pad(jnp.concatenate([c_ctx[None], c], axis=0).T, ((0, 0), (0, 8 - 1 - N_LAT_SEQ)))
    mod = _mod_call(st, w_mod, b_mod).reshape(DEPTH, 8, 6, D_MODEL)

    ones_bd, esel = _head_tables()
    bdc, bds = _group_tables()
    tables = {}
    for L in (L_CTX, L_LAT):
        fwd, inv = _hyena_dft(L)
        zpos, window = _hyena_positions(L)
        tables[L] = (fwd, inv, zpos, window, _fnet_tables(L))

    ctx_states = []
    for l in range(DEPTH):
        mod3 = mod[l, 0:3]
        wl = w_in[l].astype(BF16)
        wd = jnp.concatenate([wl[:, 0:OFF_A + 2 * H_A],
                              jnp.zeros((D_MODEL, LANE - 4 * H_A), BF16)], axis=1)
        wh = wl[:, OFF_HY:OFF_FN]
        wf = wl[:, OFF_FN:OFF_GATE]
        wg = wl[:, OFF_GATE:]
        pd, ph, pf, pg = _inproj_call(x, mod3, norm1_g[l], wd, wh, wf, wg)

        par = jnp.zeros((8, LANE), F32)
        par = par.at[0, 2 * H_A:4 * H_A].set(a_log[l].reshape(-1))
        par = par.at[1, 2 * H_A:4 * H_A].set(dt_bias[l].reshape(-1))
        o_ctx, s_ctx = _delta_call(pd, conv_qkv[l], par, esel, ones_bd, None,
                                   N_CTX_SEQ, L_CTX // DELTA_RB, 0)
        s0 = state_delta[:, l].astype(F32).reshape(N_LAT_SEQ, 2, H_A * DK, DV)
        o_lat, _ = _delta_call(pd, conv_qkv[l], par, esel, ones_bd, s0,
                               N_LAT_SEQ, L_LAT // DELTA_RB, T_CTX // DELTA_RB)
        o2 = jnp.concatenate([o_ctx, o_lat], axis=1)
        ctx_states.append(s_ctx.reshape(N_CTX_SEQ, 2, H_A, DK, DV))

        v, x12 = _hyprep_call(ph, conv_hy[l])
        w1p = jnp.pad(hy_w1[l], ((0, LANE - HY_EMB), (0, 0)))
        yb_parts, yc_parts = [], []
        for L, n_seq, blk0, ft in ((L_CTX, N_CTX_SEQ, 0, 256), (L_LAT, N_LAT_SEQ, T_CTX // L_LAT, 256)):
            fwd, inv, zpos, window, mcs = tables[L]
            kspec = _filter_call(L, zpos, w1p, hy_b1[l][None], hy_freq[l][None], hy_w2[l],
                                 hy_b2[l][None], hy_w3[l], window, fwd)
            yb_parts.append(_hyena_call(v, x12, kspec, hy_bias[l][:, None, :], fwd, inv,
                                        n_seq, L, blk0, ft))
            yc_parts.append(_fnet_call(pf, mcs, bdc, bds, n_seq, L, blk0))
        yb = jnp.concatenate(yb_parts, axis=0)
        yc = jnp.concatenate(yc_parts, axis=0)

        na512 = jnp.tile(norm_a[l], H_A)[None]
        x = _merge_call(o2, pd, yb, yc, pg, x, mod3, na512, ones_bd,
                        w_pa[l].astype(BF16), w_pb[l].astype(BF16), w_pc[l].astype(BF16),
                        w_o[l].astype(BF16))
        x = _ffn_call(x, mod3, norm2_g[l], w_gu[l].astype(BF16), w_down[l].astype(BF16),
                      norm_f, l == DEPTH - 1)

    y_prompt = x[:T_CTX].reshape(N_CTX_SEQ, L_CTX, D_MODEL)
    y_sample = x[T_CTX:].reshape(N_LAT_SEQ, L_LAT, D_MODEL)
    new_state = jnp.stack(ctx_states, axis=1).astype(x_prompt.dtype)
    return (y_prompt, y_sample, new_state)
```

```python
import functools
import math

import jax
import jax.numpy as jnp
from jax import lax
from jax.experimental import pallas as pl
from jax.experimental.pallas import tpu as pltpu

F32 = jnp.float32
BF16 = jnp.bfloat16

D_MODEL = 1024
N_CTX_SEQ = 32
L_CTX = 256
DEPTH = 2
N_LAT_SEQ = 2
L_LAT = 2048
GRID_W = 64
EPS = 1e-6
H_A = 8
DK = 64
DV = 64
W_A = H_A * DV
QKV_W = 2 * H_A * DK + H_A * DV
CHUNK = 64
W_B = 512
HY_EMB = 33
HY_HID = 64
HY_DECAY_TARGET = 1e-2
HY_FAST_PCT = 0.3
HY_SLOW_PCT = 1.5
G_C = 8
DC = 64
W_C = G_C * DC
D_FF = ((8 * D_MODEL + 3 * 256 - 1) // (3 * 256)) * 256
OFF_Z = QKV_W
OFF_B = OFF_Z + W_A
OFF_A = OFF_B + 2 * H_A
OFF_HY = OFF_A + 2 * H_A
OFF_FN = OFF_HY + 3 * W_B
OFF_GATE = OFF_FN + W_C

T_CTX = N_CTX_SEQ * L_CTX
T_LAT = N_LAT_SEQ * L_LAT
T_ALL = T_CTX + T_LAT
ROW_TILE = 256
LANE = 128
PD_W = QKV_W + W_A + LANE
HEAD_GROUP = 4
GROUP_W = HEAD_GROUP * DK
VMEM_LIMIT = 56 * 1024 * 1024


def _cparams(sem, vmem=None):
    return pltpu.CompilerParams(dimension_semantics=sem, vmem_limit_bytes=vmem)


def _dot(a, b):
    return jnp.dot(a, b, preferred_element_type=F32)


def _dot_nt(a, b):
    return lax.dot_general(a, b, (((1,), (1,)), ((), ())), preferred_element_type=F32)


def _dot_tn(a, b):
    return lax.dot_general(a, b, (((0,), (0,)), ((), ())), preferred_element_type=F32)


def _split(a, n):
    parts = []
    rem = a
    for i in range(n):
        p = rem.astype(BF16)
        parts.append(p)
        if i + 1 < n:
            rem = rem - p.astype(F32)
    return parts


def _mm(a, b, passes):
    if passes == 1:
        return _dot(a.astype(BF16), b.astype(BF16))
    ah, al = _split(a, 2)
    bh, bl = _split(b, 2)
    return _dot(ah, bh) + (_dot(ah, bl) + _dot(al, bh))


def _mm_const_rhs(a, c, n):
    out = None
    for p in _split(a, n):
        t = _dot(p, c)
        out = t if out is None else out + t
    return out


def _mm_const_lhs(c, b, n):
    out = None
    for p in _split(b, n):
        t = _dot(c, p)
        out = t if out is None else out + t
    return out


def _sigmoid(x):
    return 1.0 / (1.0 + jnp.exp(-x))


def _silu(x):
    return x * _sigmoid(x)


def _softplus(x):
    return jnp.maximum(x, 0.0) + jnp.log(1.0 + jnp.exp(-jnp.abs(x)))


def _mod_row_block(i):
    n_ctx = T_CTX // ROW_TILE
    per_lat = L_LAT // ROW_TILE
    return jnp.where(i < n_ctx, 0, 1 + (i - n_ctx) // per_lat)


MOD_TN = 512


def _mod_kernel(st_ref, w_ref, b_ref, out_ref):
    s = _silu(st_ref[...])
    w = w_ref[0]
    rows = [jnp.sum(s[:, r:r + 1] * w, axis=0, keepdims=True) + b_ref[0] for r in range(3)]
    rows.append(jnp.zeros((5, MOD_TN), F32))
    out_ref[0] = jnp.concatenate(rows, axis=0)


def _mod_call(st, w_mod, b_mod):
    n6 = 6 * D_MODEL
    return pl.pallas_call(
        _mod_kernel,
        grid=(DEPTH, n6 // MOD_TN),
        in_specs=[
            pl.BlockSpec((D_MODEL, 8), lambda l, j: (0, 0)),
            pl.BlockSpec((1, D_MODEL, MOD_TN), lambda l, j: (l, 0, j)),
            pl.BlockSpec((1, 1, MOD_TN), lambda l, j: (l, 0, j)),
        ],
        out_specs=pl.BlockSpec((1, 8, MOD_TN), lambda l, j: (l, 0, j)),
        out_shape=jax.ShapeDtypeStruct((DEPTH, 8, n6), F32),
        compiler_params=_cparams(("parallel", "parallel")),
        name="adaln_mod",
    )(st, w_mod, b_mod.reshape(DEPTH, 1, n6))


INPROJ_TN = 512


def _rms_mod(x, g, scale, shift):
    ms = jnp.mean(x * x, axis=-1, keepdims=True)
    return (x * lax.rsqrt(ms + EPS) * g) * (1.0 + scale) + shift


def _inproj_kernel(x_ref, mod_ref, g_ref, wd_ref, wh_ref, wf_ref, wg_ref,
                   pd_ref, ph_ref, pf_ref, pg_ref):
    m = mod_ref[0]
    h = _rms_mod(x_ref[...], g_ref[...], m[1:2], m[0:1]).astype(BF16)
    for w_ref, o_ref in ((wd_ref, pd_ref), (wh_ref, ph_ref), (wf_ref, pf_ref), (wg_ref, pg_ref)):
        n = w_ref.shape[1]
        for c in range(0, n, INPROJ_TN):
            e = min(c + INPROJ_TN, n)
            o_ref[:, c:e] = _dot(h, w_ref[:, c:e])


def _inproj_call(x, mod3, g, wd, wh, wf, wg):
    widths = (wd.shape[1], wh.shape[1], wf.shape[1], wg.shape[1])
    row = lambda i: (i, 0)
    const = lambda i: (0, 0)
    return pl.pallas_call(
        _inproj_kernel,
        grid=(T_ALL // ROW_TILE,),
        in_specs=[
            pl.BlockSpec((ROW_TILE, D_MODEL), row),
            pl.BlockSpec((1, 6, D_MODEL), lambda i: (_mod_row_block(i), 0, 0)),
            pl.BlockSpec((1, D_MODEL), const),
        ] + [pl.BlockSpec((D_MODEL, w), const) for w in widths],
        out_specs=[pl.BlockSpec((ROW_TILE, w), row) for w in widths],
        out_shape=[jax.ShapeDtypeStruct((T_ALL, w), F32) for w in widths],
        compiler_params=_cparams(("parallel",), VMEM_LIMIT),
        name="inproj",
    )(x, mod3, g.reshape(1, D_MODEL), wd, wh, wf, wg)


def _conv3_rows(cur, prev_row, next_row, w):
    n = cur.shape[0]
    ridx = lax.broadcasted_iota(jnp.int32, cur.shape, 0)
    up = jnp.where(ridx == 0, prev_row, pltpu.roll(cur, 1, 0))
    dn = jnp.where(ridx == n - 1, next_row, pltpu.roll(cur, n - 1, 0))
    return up * w[0:1] + cur * w[1:2] + dn * w[2:3]


def _halo_specs(width, rows_per_block, n_row_blocks, blk_of):
    per = rows_per_block // 8
    last = n_row_blocks * per - 1
    prev = pl.BlockSpec((8, width), lambda *a: (jnp.maximum(blk_of(*a) * per - 1, 0), 0))
    nxt = pl.BlockSpec((8, width), lambda *a: (jnp.minimum((blk_of(*a) + 1) * per, last), 0))
    return prev, nxt


def _hyprep_kernel(blocks_ctx, per_lat, ph_ref, prev_ref, next_ref, w_ref, v_ref, x12_ref):
    i = pl.program_id(0)
    lat_pos = (i - blocks_ctx) % per_lat
    first = (i < blocks_ctx) | (lat_pos == 0)
    last = (i < blocks_ctx) | (lat_pos == per_lat - 1)
    prev_row = jnp.where(first, 0.0, prev_ref[7:8, :])
    next_row = jnp.where(last, 0.0, next_ref[0:1, :])
    uc = _conv3_rows(ph_ref[...], prev_row, next_row, w_ref[...])
    x12_ref[0] = uc[:, 0:W_B]
    x12_ref[1] = uc[:, W_B:2 * W_B]
    v_ref[...] = uc[:, 2 * W_B:3 * W_B]


def _hyprep_call(ph, conv_w):
    nblk = T_ALL // ROW_TILE
    blk = lambda i: i
    prev, nxt = _halo_specs(3 * W_B, ROW_TILE, nblk, blk)
    kern = functools.partial(_hyprep_kernel, T_CTX // ROW_TILE, L_LAT // ROW_TILE)
    assert L_CTX == ROW_TILE
    return pl.pallas_call(
        kern,
        grid=(nblk,),
        in_specs=[pl.BlockSpec((ROW_TILE, 3 * W_B), lambda i: (i, 0)), prev, nxt,
                  pl.BlockSpec((3, 3 * W_B), lambda i: (0, 0))],
        out_specs=[pl.BlockSpec((ROW_TILE, W_B), lambda i: (i, 0)),
                   pl.BlockSpec((2, ROW_TILE, W_B), lambda i: (0, i, 0))],
        out_shape=[jax.ShapeDtypeStruct((T_ALL, W_B), F32),
                   jax.ShapeDtypeStruct((2, T_ALL, W_B), F32)],
        compiler_params=_cparams(("parallel",)),
        name="hyena_prep",
    )(ph, ph, ph, conv_w)


FILT_RT = 256


def _filter_kernel(L, zpos_ref, w1_ref, b1_ref, fq_ref, w2_ref, b2_ref, w3_ref, win_ref,
                   ftop_ref, fbot_ref, k_ref, hs_s, hm_s, krl_s):
    rt = pl.program_id(1)

    @pl.when(rt == 0)
    def _():
        fq = fq_ref[...]
        alt_acc = jnp.zeros((1, W_B), F32)
        for r0 in range(0, L, FILT_RT):
            rows = slice(r0, r0 + FILT_RT)
            h = jnp.sin(fq * (_mm(zpos_ref[rows, :], w1_ref[...], 3) + b1_ref[...]))
            h = jnp.sin(fq * (_mm(h, w2_ref[...], 3) + b2_ref[...]))
            hf = _mm(h, w3_ref[...], 3)
            win = win_ref[rows, :]
            fw = hf[:, 0:W_B] * win
            bw = hf[:, W_B:2 * W_B] * win
            hsum = fw + bw
            hs_s[rows, :] = hsum.astype(BF16)
            hm_s[rows, :] = (fw - bw).astype(BF16)
            t = lax.broadcasted_iota(jnp.int32, hsum.shape, 0)
            alt_acc = alt_acc + jnp.sum(jnp.where(t % 2 == 0, hsum, -hsum), axis=0, keepdims=True)
        krl_s[...] = jnp.broadcast_to(alt_acc, krl_s.shape)

    p1 = _dot(ftop_ref[...], hs_s[...])
    p2 = _dot(fbot_ref[...], hm_s[...])
    grow = rt * FILT_RT + lax.broadcasted_iota(jnp.int32, p1.shape, 0)
    k_ref[0, 0] = p1
    k_ref[0, 1] = jnp.where(grow == 0, krl_s[0:1, :], p1)
    k_ref[0, 2] = jnp.where(grow == 0, 0.0, p2)


def _filter_call(L, zpos, w1p, b1, fq, w2, b2, w3, win, fwd):
    nrt = L // FILT_RT
    c2 = lambda o, r: (0, 0)
    return pl.pallas_call(
        functools.partial(_filter_kernel, L),
        grid=(2, nrt),
        in_specs=[
            pl.BlockSpec((L, LANE), c2),
            pl.BlockSpec((LANE, HY_HID), c2),
            pl.BlockSpec((1, HY_HID), c2),
            pl.BlockSpec((1, HY_HID), c2),
            pl.BlockSpec((HY_HID, HY_HID), c2),
            pl.BlockSpec((1, HY_HID), c2),
            pl.BlockSpec((HY_HID, 2 * W_B), lambda o, r: (0, o)),
            pl.BlockSpec((L, W_B), c2),
            pl.BlockSpec((FILT_RT, L), lambda o, r: (r, 0)),
            pl.BlockSpec((FILT_RT, L), lambda o, r: (nrt + r, 0)),
        ],
        out_specs=pl.BlockSpec((1, 3, FILT_RT, W_B), lambda o, r: (o, 0, r, 0)),
        out_shape=jax.ShapeDtypeStruct((2, 3, L, W_B), F32),
        scratch_shapes=[pltpu.VMEM((L, W_B), BF16), pltpu.VMEM((L, W_B), BF16),
                        pltpu.VMEM((8, W_B), F32)],
        compiler_params=_cparams(("parallel", "arbitrary"), VMEM_LIMIT),
        name=f"hyena_filter_{L}",
    )(zpos, w1p, b1, fq, w2, b2, w3, win, fwd, fwd)


def _hyena_kernel(nft, v_ref, gate_ref, k_ref, bias_ref, ftop_ref, fbot_ref, itop_ref, ibot_ref,
                  out_ref, zf_s, zb_s, acc_s):
    o = pl.program_id(1)
    f = pl.program_id(2)

    @pl.when((o == 0) & (f == 0))
    def _():
        v = v_ref[...]
        zf_s[...] = v
        zb_s[...] = v.astype(BF16)

    zb = zb_s[...]
    top = _dot(ftop_ref[...], zb)
    bot = _dot(fbot_ref[...], zb)
    krt = k_ref[0, 0]
    krb = k_ref[0, 1]
    ki = k_ref[0, 2]
    yt = (top * krt - bot * ki).astype(BF16)
    yb = (top * ki + bot * krb).astype(BF16)
    part = _dot(itop_ref[...], yt) + _dot(ibot_ref[...], yb)

    @pl.when(f == 0)
    def _():
        acc_s[...] = part

    @pl.when(f != 0)
    def _():
        acc_s[...] += part

    @pl.when(f == nft - 1)
    def _():
        znew = gate_ref[0] * (acc_s[...] + bias_ref[0] * zf_s[...])
        zf_s[...] = znew
        zb_s[...] = znew.astype(BF16)
        out_ref[...] = znew


def _hyena_call(v, x12, kspec, bias, fwd, inv, n_seq, L, row_blk0, ft):
    nft = L // ft
    seq_blk = lambda s, o, f: (row_blk0 + s, 0)
    return pl.pallas_call(
        functools.partial(_hyena_kernel, nft),
        grid=(n_seq, 2, nft),
        in_specs=[
            pl.BlockSpec((L, W_B), seq_blk),
            pl.BlockSpec((1, L, W_B), lambda s, o, f: (o, row_blk0 + s, 0)),
            pl.BlockSpec((1, 3, ft, W_B), lambda s, o, f: (o, 0, f, 0)),
            pl.BlockSpec((1, 1, W_B), lambda s, o, f: (o, 0, 0)),
            pl.BlockSpec((ft, L), lambda s, o, f: (f, 0)),
            pl.BlockSpec((ft, L), lambda s, o, f: (nft + f, 0)),
            pl.BlockSpec((L, ft), lambda s, o, f: (0, f)),
            pl.BlockSpec((L, ft), lambda s, o, f: (0, nft + f)),
        ],
        out_specs=pl.BlockSpec((L, W_B), lambda s, o, f: (s, 0)),
        out_shape=jax.ShapeDtypeStruct((n_seq * L, W_B), F32),
        scratch_shapes=[pltpu.VMEM((L, W_B), F32), pltpu.VMEM((L, W_B), BF16),
                        pltpu.VMEM((L, W_B), F32)],
        compiler_params=_cparams(("parallel", "arbitrary", "arbitrary"), VMEM_LIMIT),
        name=f"hyena_conv_{L}",
    )(v, x12, kspec, bias, fwd, fwd, inv, inv)


FNET_RT = 256


def _fnet_kernel(L, x_ref, m_ref, bdc_ref, bds_ref, out_ref, xcs_s):
    r = pl.program_id(1)

    @pl.when(r == 0)
    def _():
        for r0 in range(0, L, FNET_RT):
            xb = x_ref[r0:r0 + FNET_RT, :].astype(BF16)
            xcs_s[r0:r0 + FNET_RT, :] = _dot(xb, bdc_ref[...]).astype(BF16)
            xcs_s[L + r0:L + r0 + FNET_RT, :] = _dot(xb, bds_ref[...]).astype(BF16)

    out_ref[...] = _dot(m_ref[...], xcs_s[...]) * (1.0 / math.sqrt(DC * L))


def _fnet_call(pf, mcs, bdc, bds, n_seq, L, row_blk0):
    nrt = L // FNET_RT
    return pl.pallas_call(
        functools.partial(_fnet_kernel, L),
        grid=(n_seq, nrt),
        in_specs=[
            pl.BlockSpec((L, W_C), lambda s, r: (row_blk0 + s, 0)),
            pl.BlockSpec((FNET_RT, 2 * L), lambda s, r: (r, 0)),
            pl.BlockSpec((W_C, W_C), lambda s, r: (0, 0)),
            pl.BlockSpec((W_C, W_C), lambda s, r: (0, 0)),
        ],
        out_specs=pl.BlockSpec((FNET_RT, W_C), lambda s, r: (s * nrt + r, 0)),
        out_shape=jax.ShapeDtypeStruct((n_seq * L, W_C), F32),
        scratch_shapes=[pltpu.VMEM((2 * L, W_C), BF16)],
        compiler_params=_cparams(("parallel", "arbitrary"), VMEM_LIMIT),
        name=f"fnet_{L}",
    )(pf, mcs, bdc, bds)


DELTA_RB = 256
CHUNKS_PER_RB = DELTA_RB // CHUNK


def _block_diag(y, head_mask):
    return jnp.concatenate([y.astype(BF16)] * HEAD_GROUP, axis=0) * head_mask


def _delta_kernel(nb, zero_init, *refs):
    if zero_init:
        (pd_ref, prev_ref, next_ref, cw_ref, par_ref, esel_ref, ones_ref,
         o_ref, sfin_ref, u_s, w_s, p_s, qg_s, kg_s, gl_s, st_s) = refs
        s0_ref = None
    else:
        (pd_ref, prev_ref, next_ref, cw_ref, par_ref, esel_ref, ones_ref, s0_ref,
         o_ref, sfin_ref, u_s, w_s, p_s, qg_s, kg_s, gl_s, st_s) = refs
    d = pl.program_id(0)
    j = pl.program_id(2)
    fwd = d == 0
    pos = jnp.where(fwd, j, nb - 1 - j)

    ri = lax.broadcasted_iota(jnp.int32, (CHUNK, GROUP_W), 0)
    cj = lax.broadcasted_iota(jnp.int32, (CHUNK, GROUP_W), 1) % CHUNK
    ixj = ri ^ cj
    later = jnp.where(fwd, ri, cj)
    earlier = jnp.where(fwd, cj, ri)
    incl = later >= earlier
    strict = later > earlier
    eye = ixj == 0
    br = lax.broadcasted_iota(jnp.int32, (GROUP_W, GROUP_W), 0) // CHUNK
    bc = lax.broadcasted_iota(jnp.int32, (GROUP_W, GROUP_W), 1) // CHUNK
    same_head = br == bc
    head_mask = jnp.where(same_head, 1.0, 0.0).astype(BF16)
    ti = lax.broadcasted_iota(jnp.int32, (CHUNK, CHUNK), 0)
    tm = lax.broadcasted_iota(jnp.int32, (CHUNK, CHUNK), 1)
    tri = jnp.where(jnp.where(fwd, tm, ti) <= jnp.where(fwd, ti, tm), 1.0, 0.0).astype(BF16)
    ri8 = lax.broadcasted_iota(jnp.int32, (CHUNK, H_A * DK), 0)
    cj8 = lax.broadcasted_iota(jnp.int32, (CHUNK, H_A * DK), 1) % CHUNK
    tt = jnp.where(fwd, ri8, cj8) <= jnp.where(fwd, cj8, ri8)

    @pl.when(j == 0)
    def _():
        for g in range(H_A // HEAD_GROUP):
            if zero_init:
                st_s[g] = jnp.zeros((GROUP_W, GROUP_W), F32)
            else:
                nat = s0_ref[0, 0, g * GROUP_W:(g + 1) * GROUP_W, :]
                st_s[g] = jnp.where(same_head, jnp.concatenate([nat] * HEAD_GROUP, axis=1), 0.0)

    cw = cw_ref[...]
    a_neg = -jnp.exp(par_ref[0:1, :])
    dtb = par_ref[1:2, :]
    e_beta = esel_ref[0, 0]
    e_g = esel_ref[0, 1]
    units = []
    for r in range(CHUNKS_PER_RB):
        rows = slice(r * CHUNK, (r + 1) * CHUNK)
        cur = pd_ref[rows, 0:QKV_W]
        if r == 0:
            prev_row = jnp.where(pos > 0, prev_ref[7:8, :], 0.0)
        else:
            prev_row = pd_ref[r * CHUNK - 1:r * CHUNK, 0:QKV_W]
        if r == CHUNKS_PER_RB - 1:
            next_row = jnp.where(pos < nb - 1, next_ref[0:1, :], 0.0)
        else:
            next_row = pd_ref[(r + 1) * CHUNK:(r + 1) * CHUNK + 1, 0:QKV_W]
        qkv = _silu(_conv3_rows(cur, prev_row, next_row, cw))
        q = qkv[:, 0:H_A * DK]
        k = qkv[:, H_A * DK:2 * H_A * DK]
        qn = q * lax.rsqrt(_mm_const_rhs(q * q, ones_ref[...], 2) + EPS) * (DK ** -0.5)
        kn = k * lax.rsqrt(_mm_const_rhs(k * k, ones_ref[...], 2) + EPS)
        ba = pd_ref[rows, OFF_B:OFF_B + LANE]
        beta = _mm_const_rhs(_sigmoid(ba), e_beta, 3)
        gexp = _mm_const_rhs(a_neg * _softplus(ba + dtb), e_g, 3)
        vv8 = qkv[:, 2 * H_A * DK:]
        gcc8 = _mm_const_lhs(tri, gexp, 3)
        gcr8 = jnp.sum(jnp.where(tt, gexp, 0.0), axis=0, keepdims=True)
        for g in range(H_A // HEAD_GROUP):
            lanes = slice(g * GROUP_W, (g + 1) * GROUP_W)
            qg_, kn_, be, gcc = qn[:, lanes], kn[:, lanes], beta[:, lanes], gcc8[:, lanes]
            kb = kn_ * be
            kbd = _block_diag(kn_, head_mask)
            kq = _dot_nt(jnp.concatenate([kb, qg_], axis=0).astype(BF16), kbd)
            dec = jnp.exp(jnp.where(incl, gcc - gcr8[:, lanes], -1e30))
            a = jnp.where(strict, kq[0:CHUNK] * dec, 0.0)
            eg = jnp.exp(gcc)
            gcl = jnp.where(fwd, gcc[CHUNK - 1:CHUNK, :], gcc[0:1, :])
            units.append((rows, lanes, a, vv8[:, lanes] * be, kb * eg))
            p_s[rows, lanes] = kq[CHUNK:2 * CHUNK] * dec
            qg_s[rows, lanes] = qg_ * eg
            kg_s[rows, lanes] = kn_ * jnp.exp(gcl - gcc)
            gl_s[r * 8:(r + 1) * 8, lanes] = jnp.broadcast_to(jnp.exp(gcl), (8, GROUP_W))

    xs = [jnp.where(eye, 1.0, 0.0) - jnp.where(ixj == 1, un[2], 0.0) for un in units]
    for lvl in range(1, 6):
        t1s = [_dot(x.astype(BF16), _block_diag(jnp.where((ixj >> lvl) == 1, un[2], 0.0), head_mask))
               for x, un in zip(xs, units)]
        xs = [x - _dot(t1.astype(BF16), _block_diag(x, head_mask)) for x, t1 in zip(xs, t1s)]
    for x, (rows, lanes, _, vb, kbe) in zip(xs, units):
        rhs = jnp.concatenate([_block_diag(vb, head_mask), _block_diag(kbe, head_mask)], axis=1)
        uw = _dot(x.astype(BF16), rhs)
        u_s[rows, lanes] = uw[:, 0:GROUP_W]
        w_s[rows, lanes] = uw[:, GROUP_W:2 * GROUP_W]

    def chunk_step(c, carry):
        ci = jnp.where(fwd, c, CHUNKS_PER_RB - 1 - c)
        r0 = pl.multiple_of(ci * CHUNK, CHUNK)
        g8 = pl.multiple_of(ci * 8, 8)
        for g in range(H_A // HEAD_GROUP):
            lanes = slice(g * GROUP_W, (g + 1) * GROUP_W)
            s = st_s[g]
            wq = jnp.concatenate([w_s[pl.ds(r0, CHUNK), lanes], qg_s[pl.ds(r0, CHUNK), lanes]], axis=0)
            ws_qs = _dot(wq.astype(BF16), s.astype(BF16))
            v_new = u_s[pl.ds(r0, CHUNK), lanes] - ws_qs[0:CHUNK]
            o = ws_qs[CHUNK:2 * CHUNK] + _dot(p_s[pl.ds(r0, CHUNK), lanes].astype(BF16),
                                              _block_diag(v_new, head_mask))
            upd = _dot_tn(kg_s[pl.ds(r0, CHUNK), lanes].astype(BF16), v_new.astype(BF16))
            st_s[g] = s * gl_s[pl.ds(g8, 1), lanes] + jnp.where(same_head, upd, 0.0)
            o_ref[0, pl.ds(r0, CHUNK), lanes] = o
        return carry

    lax.fori_loop(0, CHUNKS_PER_RB, chunk_step, 0, unroll=True)

    @pl.when(j == nb - 1)
    def _():
        for g in range(H_A // HEAD_GROUP):
            s = st_s[g]
            nat = s[:, 0:DV]
            for hh in range(1, HEAD_GROUP):
                nat = nat + s[:, hh * DV:(hh + 1) * DV]
            sfin_ref[0, 0, g * GROUP_W:(g + 1) * GROUP_W, :] = nat


def _delta_call(pd, conv_w, par, esel, ones_bd, s0, n_seq, nb, blk0):
    zero_init = s0 is None
    n_blocks_all = T_ALL // DELTA_RB
    blk_of = lambda d, s, j: blk0 + s * nb + j + d * (nb - 1 - 2 * j)
    prev, nxt = _halo_specs(QKV_W, DELTA_RB, n_blocks_all, blk_of)
    in_specs = [
        pl.BlockSpec((DELTA_RB, PD_W), lambda d, s, j: (blk_of(d, s, j), 0)),
        prev, nxt,
        pl.BlockSpec((3, QKV_W), lambda d, s, j: (0, 0)),
        pl.BlockSpec((8, LANE), lambda d, s, j: (0, 0)),
        pl.BlockSpec((1, 2, LANE, H_A * DK), lambda d, s, j: (d, 0, 0, 0)),
        pl.BlockSpec((H_A * DK, H_A * DK), lambda d, s, j: (0, 0)),
    ]
    args = [pd, pd, pd, conv_w, par, esel, ones_bd]
    if not zero_init:
        in_specs.append(pl.BlockSpec((1, 1, H_A * DK, DV), lambda d, s, j: (s, d, 0, 0)))
        args.append(s0)
    rows = n_seq * nb * DELTA_RB
    out_blk = lambda d, s, j: (d, s * nb + j + d * (nb - 1 - 2 * j), 0)
    return pl.pallas_call(
        functools.partial(_delta_kernel, nb, zero_init),
        grid=(2, n_seq, nb),
        in_specs=in_specs,
        out_specs=[pl.BlockSpec((1, DELTA_RB, W_A), out_blk),
                   pl.BlockSpec((1, 1, H_A * DK, DV), lambda d, s, j: (s, d, 0, 0))],
        out_shape=[jax.ShapeDtypeStruct((2, rows, W_A), F32),
                   jax.ShapeDtypeStruct((n_seq, 2, H_A * DK, DV), F32)],
        scratch_shapes=[pltpu.VMEM((DELTA_RB, H_A * DK), F32) for _ in range(5)]
        + [pltpu.VMEM((CHUNKS_PER_RB * 8, H_A * DK), F32),
           pltpu.VMEM((H_A // HEAD_GROUP, GROUP_W, GROUP_W), F32)],
        compiler_params=_cparams(("parallel", "parallel", "arbitrary"), VMEM_LIMIT),
        name=f"deltanet_nb{nb}",
    )(*args)


def _merge_kernel(o2_ref, z_ref, yb_ref, yc_ref, pg_ref, x_ref, mod_ref, na_ref, ones_ref,
                  wpa_ref, wpb_ref, wpc_ref, wo_ref, out_ref):
    o = o2_ref[0] + o2_ref[1]
    ms = _mm_const_rhs(o * o, ones_ref[...], 2) * (1.0 / DV)
    ya = (o * lax.rsqrt(ms + EPS) * na_ref[...]) * _silu(z_ref[...])
    merged = (_sigmoid(pg_ref[:, 0:D_MODEL]) * _dot(ya.astype(BF16), wpa_ref[...])
              + _sigmoid(pg_ref[:, D_MODEL:2 * D_MODEL]) * _dot(yb_ref[...].astype(BF16), wpb_ref[...])
              + _sigmoid(pg_ref[:, 2 * D_MODEL:3 * D_MODEL]) * _dot(yc_ref[...].astype(BF16), wpc_ref[...]))
    y = _dot(merged.astype(BF16), wo_ref[...])
    out_ref[...] = x_ref[...] + mod_ref[0][2:3] * y


def _merge_call(o2, pd, yb, yc, pg, x, mod3, na512, ones_bd, wpa, wpb, wpc, wo):
    row = lambda i: (i, 0)
    const = lambda i: (0, 0)
    return pl.pallas_call(
        _merge_kernel,
        grid=(T_ALL // ROW_TILE,),
        in_specs=[
            pl.BlockSpec((2, ROW_TILE, W_A), lambda i: (0, i, 0)),
            pl.BlockSpec((ROW_TILE, W_A), lambda i: (i, OFF_Z // W_A)),
            pl.BlockSpec((ROW_TILE, W_B), row),
            pl.BlockSpec((ROW_TILE, W_C), row),
            pl.BlockSpec((ROW_TILE, 3 * D_MODEL), row),
            pl.BlockSpec((ROW_TILE, D_MODEL), row),
            pl.BlockSpec((1, 6, D_MODEL), lambda i: (_mod_row_block(i), 0, 0)),
            pl.BlockSpec((1, W_A), const),
            pl.BlockSpec((W_A, W_A), const),
            pl.BlockSpec((W_A, D_MODEL), const),
            pl.BlockSpec((W_B, D_MODEL), const),
            pl.BlockSpec((W_C, D_MODEL), const),
            pl.BlockSpec((D_MODEL, D_MODEL), const),
        ],
        out_specs=pl.BlockSpec((ROW_TILE, D_MODEL), row),
        out_shape=jax.ShapeDtypeStruct((T_ALL, D_MODEL), F32),
        compiler_params=_cparams(("parallel",), VMEM_LIMIT),
        name="branch_merge",
    )(o2, pd, yb, yc, pg, x, mod3, na512, ones_bd, wpa, wpb, wpc, wo)


FFN_TN = D_FF // 2


def _ffn_kernel(final_norm, x_ref, mod_ref, g_ref, wgu_ref, wdn_ref, nf_ref, out_ref):
    m = mod_ref[0]
    x = x_ref[...]
    h = _rms_mod(x, g_ref[...], m[4:5], m[3:4]).astype(BF16)
    acc = None
    for c in range(0, D_FF, FFN_TN):
        gate = _dot(h, wgu_ref[:, c:c + FFN_TN])
        up = _dot(h, wgu_ref[:, D_FF + c:D_FF + c + FFN_TN])
        part = _dot((_silu(gate) * up).astype(BF16), wdn_ref[c:c + FFN_TN, :])
        acc = part if acc is None else acc + part
    xn = x + m[5:6] * acc
    if final_norm:
        ms = jnp.mean(xn * xn, axis=-1, keepdims=True)
        xn = xn * lax.rsqrt(ms + EPS) * nf_ref[...]
    out_ref[...] = xn


def _ffn_call(x, mod3, g, wgu, wdn, nf, final_norm):
    row = lambda i: (i, 0)
    const = lambda i: (0, 0)
    return pl.pallas_call(
        functools.partial(_ffn_kernel, final_norm),
        grid=(T_ALL // ROW_TILE,),
        in_specs=[
            pl.BlockSpec((ROW_TILE, D_MODEL), row),
            pl.BlockSpec((1, 6, D_MODEL), lambda i: (_mod_row_block(i), 0, 0)),
            pl.BlockSpec((1, D_MODEL), const),
            pl.BlockSpec((D_MODEL, 2 * D_FF), const),
            pl.BlockSpec((D_FF, D_MODEL), const),
            pl.BlockSpec((1, D_MODEL), const),
        ],
        out_specs=pl.BlockSpec((ROW_TILE, D_MODEL), row),
        out_shape=jax.ShapeDtypeStruct((T_ALL, D_MODEL), F32),
        compiler_params=_cparams(("parallel",), VMEM_LIMIT),
        name="ffn_final" if final_norm else "ffn",
    )(x, mod3, g.reshape(1, D_MODEL), wgu, wdn, nf.reshape(1, D_MODEL))


def _grid_pos_embed(n_tokens):
    rows = n_tokens // GRID_W
    r = jnp.repeat(jnp.arange(rows), GRID_W).astype(F32)
    col = jnp.tile(jnp.arange(GRID_W), rows).astype(F32)
    quarter = D_MODEL // 4
    omega = 1.0 / (10000.0 ** (jnp.arange(quarter, dtype=F32) / quarter))

    def emb(pos):
        a = pos[:, None] * omega[None, :]
        return jnp.concatenate([jnp.sin(a), jnp.cos(a)], axis=-1)

    return jnp.concatenate([emb(r), emb(col)], axis=-1)


def _cos_sin_table(n_rows, n_cols, period):
    r = jnp.arange(n_rows, dtype=jnp.int32)[:, None]
    c = jnp.arange(n_cols, dtype=jnp.int32)[None, :]
    ang = ((r * c) % period).astype(F32) * (2.0 * math.pi / period)
    return jnp.cos(ang), jnp.sin(ang)


def _hyena_dft(L):
    n = 2 * L
    cos, sin = _cos_sin_table(L, L, n)
    alt = jnp.where(jnp.arange(L) % 2 == 0, 1.0, -1.0).astype(F32)
    first = (jnp.arange(L) == 0)
    fbot = jnp.where(first[:, None], alt[None, :], -sin)
    fwd = jnp.concatenate([cos, fbot], axis=0)
    wf = jnp.where(first, 1.0 / n, 2.0 / n).astype(F32)
    itop = cos * wf[None, :]
    ibot = jnp.where(first[None, :], alt[:, None] / n, -sin * (2.0 / n))
    inv = jnp.concatenate([itop, ibot], axis=1)
    return fwd.astype(BF16), inv.astype(BF16)


def _hyena_positions(L):
    bands = (HY_EMB - 1) // 2
    t = jnp.linspace(0.0, 1.0, L, dtype=F32)[:, None]
    wpos = (2.0 * math.pi / L) * jnp.arange(L, dtype=F32)[:, None]
    fr = jnp.linspace(1e-4, bands - 1, bands, dtype=F32)[None, :]
    zpos = jnp.concatenate([t, jnp.cos(fr * wpos), -jnp.sin(fr * wpos)], axis=-1)
    zpos = jnp.pad(zpos, ((0, 0), (0, LANE - HY_EMB)))
    deltas = jnp.abs(jnp.linspace(math.log(HY_DECAY_TARGET) / HY_SLOW_PCT,
                                  math.log(HY_DECAY_TARGET) / HY_FAST_PCT, W_B, dtype=F32))
    window = jnp.exp(-t * deltas[None, :])
    return zpos, window


def _fnet_tables(L):
    cl, sl = _cos_sin_table(L, L, L)
    mcs = jnp.concatenate([cl, -sl], axis=1).astype(BF16)
    return mcs


def _group_tables():
    c64, s64 = _cos_sin_table(DC, DC, DC)
    eye = jnp.eye(G_C, dtype=F32)
    return jnp.kron(eye, c64).astype(BF16), jnp.kron(eye, s64).astype(BF16)


def _head_tables():
    ones_bd = jnp.kron(jnp.eye(H_A, dtype=F32), jnp.ones((DK, DK), F32)).astype(BF16)
    lane = jnp.arange(LANE)[:, None]
    head = (jnp.arange(H_A * DK) // DK)[None, :]
    sel = []
    for d in range(2):
        sel.append(jnp.stack([(lane == d * H_A + head), (lane == 2 * H_A + d * H_A + head)]))
    esel = jnp.stack(sel).astype(BF16)
    return ones_bd, esel


def kernel(x_prompt, x_sample, state_delta, c, c_ctx, w_mod, b_mod, norm1_g, norm2_g, w_in, conv_qkv, a_log, dt_bias, norm_a, conv_hy, hy_w1, hy_b1, hy_freq, hy_w2, hy_b2, hy_w3, hy_bias, w_pa, w_pb, w_pc, w_o, w_gu, w_down, norm_f):
    assert x_prompt.shape == (N_CTX_SEQ, L_CTX, D_MODEL) and x_sample.shape == (N_LAT_SEQ, L_LAT, D_MODEL)
    xs = x_sample + _grid_pos_embed(L_LAT)[None]
    x = jnp.concatenate([x_prompt.reshape(T_CTX, D_MODEL), xs.reshape(T_LAT, D_MODEL)], axis=0)

    st = jnp.pad(jnp.concatenate([c_ctx[None], c], axis=0).T, ((0, 0), (0, 8 - 1 - N_LAT_SEQ)))
    mod = _mod_call(st, w_mod, b_mod).reshape(DEPTH, 8, 6, D_MODEL)

    ones_bd, esel = _head_tables()
    bdc, bds = _group_tables()
    tables = {}
    for L in (L_CTX, L_LAT):
        fwd, inv = _hyena_dft(L)
        zpos, window = _hyena_positions(L)
        tables[L] = (fwd, inv, zpos, window, _fnet_tables(L))

    ctx_states = []
    for l in range(DEPTH):
        mod3 = mod[l, 0:3]
        wl = w_in[l].astype(BF16)
        wd = jnp.concatenate([wl[:, 0:OFF_A + 2 * H_A],
                              jnp.zeros((D_MODEL, LANE - 4 * H_A), BF16)], axis=1)
        wh = wl[:, OFF_HY:OFF_FN]
        wf = wl[:, OFF_FN:OFF_GATE]
        wg = wl[:, OFF_GATE:]
        pd, ph, pf, pg = _inproj_call(x, mod3, norm1_g[l], wd, wh, wf, wg)

        par = jnp.zeros((8, LANE), F32)
        par = par.at[0, 2 * H_A:4 * H_A].set(a_log[l].reshape(-1))
        par = par.at[1, 2 * H_A:4 * H_A].set(dt_bias[l].reshape(-1))
        o_ctx, s_ctx = _delta_call(pd, conv_qkv[l], par, esel, ones_bd, None,
                                   N_CTX_SEQ, L_CTX // DELTA_RB, 0)
        s0 = state_delta[:, l].astype(F32).reshape(N_LAT_SEQ, 2, H_A * DK, DV)
        o_lat, _ = _delta_call(pd, conv_qkv[l], par, esel, ones_bd, s0,
                               N_LAT_SEQ, L_LAT // DELTA_RB, T_CTX // DELTA_RB)
        o2 = jnp.concatenate([o_ctx, o_lat], axis=1)
        ctx_states.append(s_ctx.reshape(N_CTX_SEQ, 2, H_A, DK, DV))

        v, x12 = _hyprep_call(ph, conv_hy[l])
        w1p = jnp.pad(hy_w1[l], ((0, LANE - HY_EMB), (0, 0)))
        yb_parts, yc_parts = [], []
        for L, n_seq, blk0, ft in ((L_CTX, N_CTX_SEQ, 0, 256), (L_LAT, N_LAT_SEQ, T_CTX // L_LAT, 256)):
            fwd, inv, zpos, window, mcs = tables[L]
            kspec = _filter_call(L, zpos, w1p, hy_b1[l][None], hy_freq[l][None], hy_w2[l],
                                 hy_b2[l][None], hy_w3[l], window, fwd)
            yb_parts.append(_hyena_call(v, x12, kspec, hy_bias[l][:, None, :], fwd, inv,
                                        n_seq, L, blk0, ft))
            yc_parts.append(_fnet_call(pf, mcs, bdc, bds, n_seq, L, blk0))
        yb = jnp.concatenate(yb_parts, axis=0)
        yc = jnp.concatenate(yc_parts, axis=0)

        na512 = jnp.tile(norm_a[l], H_A)[None]
        x = _merge_call(o2, pd, yb, yc, pg, x, mod3, na512, ones_bd,
                        w_pa[l].astype(BF16), w_pb[l].astype(BF16), w_pc[l].astype(BF16),
                        w_o[l].astype(BF16))
        x = _ffn_call(x, mod3, norm2_g[l], w_gu[l].astype(BF16), w_down[l].astype(BF16),
                      norm_f, l == DEPTH - 1)

    y_prompt = x[:T_CTX].reshape(N_CTX_SEQ, L_CTX, D_MODEL)
    y_sample = x[T_CTX:].reshape(N_LAT_SEQ, L_LAT, D_MODEL)
    new_state = jnp.stack(ctx_states, axis=1).astype(x_prompt.dtype)
    return (y_prompt, y_sample, new_state)
```

```python
import functools
import math

import jax
import jax.numpy as jnp
from jax import lax
from jax.experimental import pallas as pl
from jax.experimental.pallas import tpu as pltpu

F32 = jnp.float32
BF16 = jnp.bfloat16

D_MODEL = 1024
N_CTX_SEQ = 32
L_CTX = 256
DEPTH = 2
N_LAT_SEQ = 2
L_LAT = 2048
GRID_W = 64
EPS = 1e-6
H_A = 8
DK = 64
DV = 64
W_A = H_A * DV
QKV_W = 2 * H_A * DK + H_A * DV
CHUNK = 64
W_B = 512
HY_EMB = 33
HY_HID = 64
HY_DECAY_TARGET = 1e-2
HY_FAST_PCT = 0.3
HY_SLOW_PCT = 1.5
G_C = 8
DC = 64
W_C = G_C * DC
D_FF = ((8 * D_MODEL + 3 * 256 - 1) // (3 * 256)) * 256
OFF_Z = QKV_W
OFF_B = OFF_Z + W_A
OFF_A = OFF_B + 2 * H_A
OFF_HY = OFF_A + 2 * H_A
OFF_FN = OFF_HY + 3 * W_B
OFF_GATE = OFF_FN + W_C

T_CTX = N_CTX_SEQ * L_CTX
T_LAT = N_LAT_SEQ * L_LAT
T_ALL = T_CTX + T_LAT
ROW_TILE = 256
N_CTX_TILES = T_CTX // ROW_TILE
N_LAT_TILES = T_LAT // ROW_TILE
LANE = 128
PD_W = QKV_W + W_A + LANE
HEAD_GROUP = 4
GROUP_W = HEAD_GROUP * DK
N_GROUPS = H_A // HEAD_GROUP
VMEM_LIMIT = 56 * 1024 * 1024


def _cparams(sem, vmem=None):
    return pltpu.CompilerParams(dimension_semantics=sem, vmem_limit_bytes=vmem)


def _dot(a, b):
    return jnp.dot(a, b, preferred_element_type=F32)


def _dot_nt(a, b):
    return lax.dot_general(a, b, (((1,), (1,)), ((), ())), preferred_element_type=F32)


def _dot_tn(a, b):
    return lax.dot_general(a, b, (((0,), (0,)), ((), ())), preferred_element_type=F32)


def _split(a, n):
    parts = []
    rem = a
    for i in range(n):
        p = rem.astype(BF16)
        parts.append(p)
        if i + 1 < n:
            rem = rem - p.astype(F32)
    return parts


def _mm3(a, b):
    ah, al = _split(a, 2)
    bh, bl = _split(b, 2)
    return _dot(ah, bh) + (_dot(ah, bl) + _dot(al, bh))


def _mm_const_rhs(a, c, n):
    out = None
    for p in _split(a, n):
        t = _dot(p, c)
        out = t if out is None else out + t
    return out


def _mm_const_lhs(c, b, n):
    out = None
    for p in _split(b, n):
        t = _dot(c, p)
        out = t if out is None else out + t
    return out


def _sigmoid(x):
    return 1.0 / (1.0 + jnp.exp(-x))


def _silu(x):
    return x * _sigmoid(x)


def _softplus(x):
    return jnp.maximum(x, 0.0) + jnp.log(1.0 + jnp.exp(-jnp.abs(x)))


def _mod_row_block(i):
    per_lat = L_LAT // ROW_TILE
    return jnp.where(i < N_CTX_TILES, 0, 1 + (i - N_CTX_TILES) // per_lat)


def _ctx_tile(i):
    return jnp.minimum(i, N_CTX_TILES - 1)


def _lat_tile(i):
    return jnp.maximum(i - N_CTX_TILES, 0)


def _ctx_lat_specs(width):
    return (pl.BlockSpec((ROW_TILE, width), lambda i: (_ctx_tile(i), 0)),
            pl.BlockSpec((ROW_TILE, width), lambda i: (_lat_tile(i), 0)))


MOD_TN = 512


def _mod_kernel(st_ref, w_ref, b_ref, out_ref):
    s = _silu(st_ref[...])
    w = w_ref[0]
    rows = [jnp.sum(s[:, r:r + 1] * w, axis=0, keepdims=True) + b_ref[0] for r in range(3)]
    rows.append(jnp.zeros((5, MOD_TN), F32))
    out_ref[0] = jnp.concatenate(rows, axis=0)


def _mod_call(st, w_mod, b_mod):
    n6 = 6 * D_MODEL
    return pl.pallas_call(
        _mod_kernel,
        grid=(DEPTH, n6 // MOD_TN),
        in_specs=[
            pl.BlockSpec((D_MODEL, 8), lambda l, j: (0, 0)),
            pl.BlockSpec((1, D_MODEL, MOD_TN), lambda l, j: (l, 0, j)),
            pl.BlockSpec((1, 1, MOD_TN), lambda l, j: (l, 0, j)),
        ],
        out_specs=pl.BlockSpec((1, 8, MOD_TN), lambda l, j: (l, 0, j)),
        out_shape=jax.ShapeDtypeStruct((DEPTH, 8, n6), F32),
        compiler_params=_cparams(("parallel", "parallel")),
        name="adaln_mod",
    )(st, w_mod, b_mod.reshape(DEPTH, 1, n6))


INPROJ_TN = 512


def _rms_mod(x, g, scale, shift):
    ms = jnp.mean(x * x, axis=-1, keepdims=True)
    return (x * lax.rsqrt(ms + EPS) * g) * (1.0 + scale) + shift


def _inproj_kernel(first, *refs):
    if first:
        (xc_ref, xl_ref, pos_ref, mod_ref, g_ref, wd_ref, wh_ref, wf_ref, wg_ref,
         pd_ref, ph_ref, pf_ref, pg_ref, x_ref) = refs
        x = jnp.where(pl.program_id(0) < N_CTX_TILES, xc_ref[...], xl_ref[...] + pos_ref[...])
        x_ref[...] = x
    else:
        (xin_ref, mod_ref, g_ref, wd_ref, wh_ref, wf_ref, wg_ref, pd_ref, ph_ref, pf_ref, pg_ref) = refs
        x = xin_ref[...]
    m = mod_ref[0]
    h = _rms_mod(x, g_ref[...], m[1:2], m[0:1]).astype(BF16)
    for w_ref, o_ref in ((wd_ref, pd_ref), (wh_ref, ph_ref), (wf_ref, pf_ref), (wg_ref, pg_ref)):
        n = w_ref.shape[1]
        for c in range(0, n, INPROJ_TN):
            e = min(c + INPROJ_TN, n)
            o_ref[:, c:e] = _dot(h, w_ref[:, c:e])


def _inproj_call(xs, mod3, g, wd, wh, wf, wg):
    first = len(xs) == 3
    widths = (wd.shape[1], wh.shape[1], wf.shape[1], wg.shape[1])
    row = lambda i: (i, 0)
    const = lambda i: (0, 0)
    if first:
        per_lat = L_LAT // ROW_TILE
        x_specs = list(_ctx_lat_specs(D_MODEL)) + [
            pl.BlockSpec((ROW_TILE, D_MODEL), lambda i: (_lat_tile(i) % per_lat, 0))]
    else:
        x_specs = [pl.BlockSpec((ROW_TILE, D_MODEL), row)]
    out_widths = widths + ((D_MODEL,) if first else ())
    return pl.pallas_call(
        functools.partial(_inproj_kernel, first),
        grid=(T_ALL // ROW_TILE,),
        in_specs=x_specs + [
            pl.BlockSpec((1, 6, D_MODEL), lambda i: (_mod_row_block(i), 0, 0)),
            pl.BlockSpec((1, D_MODEL), const),
        ] + [pl.BlockSpec((D_MODEL, w), const) for w in widths],
        out_specs=[pl.BlockSpec((ROW_TILE, w), row) for w in out_widths],
        out_shape=[jax.ShapeDtypeStruct((T_ALL, w), F32) for w in out_widths],
        compiler_params=_cparams(("parallel",), VMEM_LIMIT),
        name="inproj_first" if first else "inproj",
    )(*xs, mod3, g.reshape(1, D_MODEL), wd, wh, wf, wg)


def _conv3_rows(cur, prev_row, next_row, w):
    n = cur.shape[0]
    ridx = lax.broadcasted_iota(jnp.int32, cur.shape, 0)
    up = jnp.where(ridx == 0, prev_row, pltpu.roll(cur, 1, 0))
    dn = jnp.where(ridx == n - 1, next_row, pltpu.roll(cur, n - 1, 0))
    return up * w[0:1] + cur * w[1:2] + dn * w[2:3]


def _halo_specs(width, rows_per_block, n_row_blocks, blk_of):
    per = rows_per_block // 8
    last = n_row_blocks * per - 1
    prev = pl.BlockSpec((8, width), lambda *a: (jnp.maximum(blk_of(*a) * per - 1, 0), 0))
    nxt = pl.BlockSpec((8, width), lambda *a: (jnp.minimum((blk_of(*a) + 1) * per, last), 0))
    return prev, nxt


def _hyprep_kernel(per_lat, ph_ref, prev_ref, next_ref, w_ref, v_ref, x12_ref):
    i = pl.program_id(0)
    lat_pos = (i - N_CTX_TILES) % per_lat
    first = (i < N_CTX_TILES) | (lat_pos == 0)
    last = (i < N_CTX_TILES) | (lat_pos == per_lat - 1)
    prev_row = jnp.where(first, 0.0, prev_ref[7:8, :])
    next_row = jnp.where(last, 0.0, next_ref[0:1, :])
    uc = _conv3_rows(ph_ref[...], prev_row, next_row, w_ref[...])
    x12_ref[0] = uc[:, 0:W_B]
    x12_ref[1] = uc[:, W_B:2 * W_B]
    v_ref[...] = uc[:, 2 * W_B:3 * W_B]


def _hyprep_call(ph, conv_w):
    nblk = T_ALL // ROW_TILE
    prev, nxt = _halo_specs(3 * W_B, ROW_TILE, nblk, lambda i: i)
    assert L_CTX == ROW_TILE
    return pl.pallas_call(
        functools.partial(_hyprep_kernel, L_LAT // ROW_TILE),
        grid=(nblk,),
        in_specs=[pl.BlockSpec((ROW_TILE, 3 * W_B), lambda i: (i, 0)), prev, nxt,
                  pl.BlockSpec((3, 3 * W_B), lambda i: (0, 0))],
        out_specs=[pl.BlockSpec((ROW_TILE, W_B), lambda i: (i, 0)),
                   pl.BlockSpec((2, ROW_TILE, W_B), lambda i: (0, i, 0))],
        out_shape=[jax.ShapeDtypeStruct((T_ALL, W_B), F32),
                   jax.ShapeDtypeStruct((2, T_ALL, W_B), F32)],
        compiler_params=_cparams(("parallel",)),
        name="hyena_prep",
    )(ph, ph, ph, conv_w)


FILT_RT = 256


def _alternating_sum(x):
    t = lax.broadcasted_iota(jnp.int32, x.shape, 0)
    return jnp.sum(jnp.where(t % 2 == 0, x, -x), axis=0, keepdims=True)


def _filter_kernel(L, zpos_ref, w1_ref, b1_ref, fq_ref, w2_ref, b2_ref, w3_ref, win_ref,
                   cos_ref, nsin_ref, k_ref, hs_s, hm_s, krl_s):
    rt = pl.program_id(1)

    @pl.when(rt == 0)
    def _():
        fq = fq_ref[...]
        alt_acc = jnp.zeros((1, W_B), F32)
        for r0 in range(0, L, FILT_RT):
            rows = slice(r0, r0 + FILT_RT)
            h = jnp.sin(fq * (_mm3(zpos_ref[rows, :], w1_ref[...]) + b1_ref[...]))
            h = jnp.sin(fq * (_mm3(h, w2_ref[...]) + b2_ref[...]))
            hf = _mm3(h, w3_ref[...])
            win = win_ref[rows, :]
            fw = hf[:, 0:W_B] * win
            bw = hf[:, W_B:2 * W_B] * win
            hsum = fw + bw
            hs_s[rows, :] = hsum.astype(BF16)
            hm_s[rows, :] = (fw - bw).astype(BF16)
            alt_acc = alt_acc + _alternating_sum(hsum)
        krl_s[...] = jnp.broadcast_to(alt_acc, krl_s.shape)

    p1 = _dot(cos_ref[...], hs_s[...])
    p2 = _dot(nsin_ref[...], hm_s[...])
    first = (rt * FILT_RT + lax.broadcasted_iota(jnp.int32, p1.shape, 0)) == 0
    k_ref[0, 0] = p1
    k_ref[0, 1] = jnp.where(first, krl_s[0:1, :], p1)
    k_ref[0, 2] = jnp.where(first, 0.0, p2)


def _filter_call(L, zpos, w1p, b1, fq, w2, b2, w3, win, cos, nsin):
    nrt = L // FILT_RT
    c2 = lambda o, r: (0, 0)
    return pl.pallas_call(
        functools.partial(_filter_kernel, L),
        grid=(2, nrt),
        in_specs=[
            pl.BlockSpec((L, LANE), c2),
            pl.BlockSpec((LANE, HY_HID), c2),
            pl.BlockSpec((1, HY_HID), c2),
            pl.BlockSpec((1, HY_HID), c2),
            pl.BlockSpec((HY_HID, HY_HID), c2),
            pl.BlockSpec((1, HY_HID), c2),
            pl.BlockSpec((HY_HID, 2 * W_B), lambda o, r: (0, o)),
            pl.BlockSpec((L, W_B), c2),
            pl.BlockSpec((FILT_RT, L), lambda o, r: (r, 0)),
            pl.BlockSpec((FILT_RT, L), lambda o, r: (r, 0)),
        ],
        out_specs=pl.BlockSpec((1, 3, FILT_RT, W_B), lambda o, r: (o, 0, r, 0)),
        out_shape=jax.ShapeDtypeStruct((2, 3, L, W_B), F32),
        scratch_shapes=[pltpu.VMEM((L, W_B), BF16), pltpu.VMEM((L, W_B), BF16),
                        pltpu.VMEM((8, W_B), F32)],
        compiler_params=_cparams(("parallel", "arbitrary"), VMEM_LIMIT),
        name=f"hyena_filter_{L}",
    )(zpos, w1p, b1, fq, w2, b2, w3, win, cos, nsin)


def _hyena_kernel(L, ft, nft, v_ref, gate_ref, k_ref, bias_ref, crow_ref, srow_ref, ccol_ref, scol_ref,
                  out_ref, zf_s, zb_s, acc_s, nyq_s):
    o = pl.program_id(1)
    f = pl.program_id(2)

    @pl.when((o == 0) & (f == 0))
    def _():
        v = v_ref[...]
        zf_s[...] = v
        zb_s[...] = v.astype(BF16)

    @pl.when(f == 0)
    def _():
        nyq_s[...] = jnp.broadcast_to(_alternating_sum(zf_s[...]), nyq_s.shape)

    zb = zb_s[...]
    top = _dot(crow_ref[...], zb)
    first = (f * ft + lax.broadcasted_iota(jnp.int32, top.shape, 0)) == 0
    bot = jnp.where(first, nyq_s[0:1, :], _dot(srow_ref[...], zb))
    krt = k_ref[0, 0]
    krb = k_ref[0, 1]
    ki = k_ref[0, 2]
    yt = top * krt - bot * ki
    yb = top * ki + bot * krb
    part = (_dot(ccol_ref[...], jnp.where(first, 0.5 * yt, yt).astype(BF16))
            + _dot(scol_ref[...], yb.astype(BF16)))

    @pl.when(f == 0)
    def _():
        t = lax.broadcasted_iota(jnp.int32, part.shape, 0)
        acc_s[...] = part + jnp.where(t % 2 == 0, 0.5, -0.5) * yb[0:1, :]

    @pl.when(f != 0)
    def _():
        acc_s[...] += part

    @pl.when(f == nft - 1)
    def _():
        znew = gate_ref[0] * (acc_s[...] * (1.0 / L) + bias_ref[0] * zf_s[...])
        zf_s[...] = znew
        zb_s[...] = znew.astype(BF16)
        out_ref[...] = znew


def _hyena_call(v, x12, kspec, bias, cos, nsin, n_seq, L, row_blk0, ft):
    nft = L // ft
    return pl.pallas_call(
        functools.partial(_hyena_kernel, L, ft, nft),
        grid=(n_seq, 2, nft),
        in_specs=[
            pl.BlockSpec((L, W_B), lambda s, o, f: (row_blk0 + s, 0)),
            pl.BlockSpec((1, L, W_B), lambda s, o, f: (o, row_blk0 + s, 0)),
            pl.BlockSpec((1, 3, ft, W_B), lambda s, o, f: (o, 0, f, 0)),
            pl.BlockSpec((1, 1, W_B), lambda s, o, f: (o, 0, 0)),
            pl.BlockSpec((ft, L), lambda s, o, f: (f, 0)),
            pl.BlockSpec((ft, L), lambda s, o, f: (f, 0)),
            pl.BlockSpec((L, ft), lambda s, o, f: (0, f)),
            pl.BlockSpec((L, ft), lambda s, o, f: (0, f)),
        ],
        out_specs=pl.BlockSpec((L, W_B), lambda s, o, f: (s, 0)),
        out_shape=jax.ShapeDtypeStruct((n_seq * L, W_B), F32),
        scratch_shapes=[pltpu.VMEM((L, W_B), F32), pltpu.VMEM((L, W_B), BF16),
                        pltpu.VMEM((L, W_B), F32), pltpu.VMEM((8, W_B), F32)],
        compiler_params=_cparams(("parallel", "arbitrary", "arbitrary"), VMEM_LIMIT),
        name=f"hyena_conv_{L}",
    )(v, x12, kspec, bias, cos, nsin, cos, nsin)


FNET_RT = 256


def _fnet_kernel(L, x_ref, cos_ref, nsin_ref, bdc_ref, bds_ref, out_ref, xc_s, xs_s):
    r = pl.program_id(1)

    @pl.when(r == 0)
    def _():
        for r0 in range(0, L, FNET_RT):
            xb = x_ref[r0:r0 + FNET_RT, :].astype(BF16)
            xc_s[r0:r0 + FNET_RT, :] = _dot(xb, bdc_ref[...]).astype(BF16)
            xs_s[r0:r0 + FNET_RT, :] = _dot(xb, bds_ref[...]).astype(BF16)

    y = _dot(cos_ref[...], xc_s[...]) + _dot(nsin_ref[...], xs_s[...])
    out_ref[...] = y * (1.0 / math.sqrt(DC * L))


def _fnet_call(pf, cos, nsin, bdc, bds, n_seq, L, row_blk0):
    nrt = L // FNET_RT
    return pl.pallas_call(
        functools.partial(_fnet_kernel, L),
        grid=(n_seq, nrt),
        in_specs=[
            pl.BlockSpec((L, W_C), lambda s, r: (row_blk0 + s, 0)),
            pl.BlockSpec((FNET_RT, L), lambda s, r: (r, 0)),
            pl.BlockSpec((FNET_RT, L), lambda s, r: (r, 0)),
            pl.BlockSpec((W_C, W_C), lambda s, r: (0, 0)),
            pl.BlockSpec((W_C, W_C), lambda s, r: (0, 0)),
        ],
        out_specs=pl.BlockSpec((FNET_RT, W_C), lambda s, r: (s * nrt + r, 0)),
        out_shape=jax.ShapeDtypeStruct((n_seq * L, W_C), F32),
        scratch_shapes=[pltpu.VMEM((L, W_C), BF16), pltpu.VMEM((L, W_C), BF16)],
        compiler_params=_cparams(("parallel", "arbitrary"), VMEM_LIMIT),
        name=f"fnet_{L}",
    )(pf, cos, nsin, bdc, bds)


DELTA_RB = 256
CHUNKS_PER_RB = DELTA_RB // CHUNK


def _block_diag(y, head_mask):
    return jnp.concatenate([y.astype(BF16)] * HEAD_GROUP, axis=0) * head_mask


def _delta_block_prep(pd_ref, prev_ref, next_ref, has_prev, has_next, cw, a_neg, dtb, ones_bd, head_mask):
    chunks = []
    for r in range(CHUNKS_PER_RB):
        rows = slice(r * CHUNK, (r + 1) * CHUNK)
        cur = pd_ref[rows, 0:QKV_W]
        if r == 0:
            prev_row = jnp.where(has_prev, prev_ref[7:8, :], 0.0)
        else:
            prev_row = pd_ref[r * CHUNK - 1:r * CHUNK, 0:QKV_W]
        if r == CHUNKS_PER_RB - 1:
            next_row = jnp.where(has_next, next_ref[0:1, :], 0.0)
        else:
            next_row = pd_ref[(r + 1) * CHUNK:(r + 1) * CHUNK + 1, 0:QKV_W]
        qkv = _silu(_conv3_rows(cur, prev_row, next_row, cw))
        q = qkv[:, 0:H_A * DK]
        k = qkv[:, H_A * DK:2 * H_A * DK]
        qn = q * lax.rsqrt(_mm_const_rhs(q * q, ones_bd, 2) + EPS) * (DK ** -0.5)
        kn = k * lax.rsqrt(_mm_const_rhs(k * k, ones_bd, 2) + EPS)
        ba = pd_ref[rows, OFF_B:OFF_B + LANE]
        gram = []
        for g in range(N_GROUPS):
            lanes = slice(g * GROUP_W, (g + 1) * GROUP_W)
            lhs = jnp.concatenate([kn[:, lanes], qn[:, lanes]], axis=0).astype(BF16)
            gram.append(_dot_nt(lhs, _block_diag(kn[:, lanes], head_mask)))
        chunks.append(dict(qn=qn, kn=kn, v=qkv[:, 2 * H_A * DK:], sig=_sigmoid(ba),
                           glog=a_neg * _softplus(ba + dtb), gram=gram))
    return chunks


def _delta_kernel(nb, zero_init, *refs):
    shared = nb == 1
    it = iter(refs)
    blocks = [(next(it), next(it), next(it))]
    if not shared:
        blocks.append((next(it), next(it), next(it)))
    cw_ref, par_ref, esel_ref, ones_ref = next(it), next(it), next(it), next(it)
    s0_ref = None if zero_init else next(it)
    o_refs = (next(it), next(it))
    sfin_ref = next(it)
    u_s, w_s, p_s, qg_s, kg_s, gl_s, st_s = (next(it) for _ in range(7))
    j = pl.program_id(1)

    ri = lax.broadcasted_iota(jnp.int32, (CHUNK, GROUP_W), 0)
    cj = lax.broadcasted_iota(jnp.int32, (CHUNK, GROUP_W), 1) % CHUNK
    ixj = ri ^ cj
    eye = ixj == 0
    br = lax.broadcasted_iota(jnp.int32, (GROUP_W, GROUP_W), 0) // CHUNK
    bc = lax.broadcasted_iota(jnp.int32, (GROUP_W, GROUP_W), 1) // CHUNK
    same_head = br == bc
    head_mask = jnp.where(same_head, 1.0, 0.0).astype(BF16)
    ti = lax.broadcasted_iota(jnp.int32, (CHUNK, CHUNK), 0)
    tm = lax.broadcasted_iota(jnp.int32, (CHUNK, CHUNK), 1)
    ri8 = lax.broadcasted_iota(jnp.int32, (CHUNK, H_A * DK), 0)
    cj8 = lax.broadcasted_iota(jnp.int32, (CHUNK, H_A * DK), 1) % CHUNK
    incl = (ri >= cj, ri <= cj)
    strict = (ri > cj, ri < cj)
    tri = tuple(jnp.where(m, 1.0, 0.0).astype(BF16) for m in (tm <= ti, tm >= ti))
    tt = (ri8 <= cj8, ri8 >= cj8)
    last_row = (CHUNK - 1, 0)

    @pl.when(j == 0)
    def _():
        for d in range(2):
            for g in range(N_GROUPS):
                if zero_init:
                    st_s[d, g] = jnp.zeros((GROUP_W, GROUP_W), F32)
                else:
                    nat = s0_ref[0, d, g * GROUP_W:(g + 1) * GROUP_W, :]
                    st_s[d, g] = jnp.where(same_head, jnp.concatenate([nat] * HEAD_GROUP, axis=1), 0.0)

    cw = cw_ref[...]
    a_neg = -jnp.exp(par_ref[0:1, :])
    dtb = par_ref[1:2, :]
    pos = (j, nb - 1 - j)
    prepped = [_delta_block_prep(*blocks[b], pos[b] > 0, pos[b] < nb - 1, cw, a_neg, dtb,
                                 ones_ref[...], head_mask) for b in range(len(blocks))]
    units = []
    for d in range(2):
        for r, ch in enumerate(prepped[0 if shared else d]):
            rows = slice(r * CHUNK, (r + 1) * CHUNK)
            beta = _mm_const_rhs(ch["sig"], esel_ref[d, 0], 3)
            gexp = _mm_const_rhs(ch["glog"], esel_ref[d, 1], 3)
            gcc8 = _mm_const_lhs(tri[d], gexp, 3)
            gcr8 = jnp.sum(jnp.where(tt[d], gexp, 0.0), axis=0, keepdims=True)
            for g in range(N_GROUPS):
                lanes = slice(g * GROUP_W, (g + 1) * GROUP_W)
                qn, kn, be, gcc = ch["qn"][:, lanes], ch["kn"][:, lanes], beta[:, lanes], gcc8[:, lanes]
                kq = ch["gram"][g]
                dec = jnp.exp(jnp.where(incl[d], gcc - gcr8[:, lanes], -1e30))
                a = jnp.where(strict[d], kq[0:CHUNK] * be * dec, 0.0)
                eg = jnp.exp(gcc)
                gcl = gcc[last_row[d]:last_row[d] + 1, :]
                units.append((d, rows, lanes, a, ch["v"][:, lanes] * be, kn * be * eg))
                p_s[d, rows, lanes] = kq[CHUNK:2 * CHUNK] * dec
                qg_s[d, rows, lanes] = qn * eg
                kg_s[d, rows, lanes] = kn * jnp.exp(gcl - gcc)
                gl_s[d, r * 8:(r + 1) * 8, lanes] = jnp.broadcast_to(jnp.exp(gcl), (8, GROUP_W))

    xs = [jnp.where(eye, 1.0, 0.0) - jnp.where(ixj == 1, un[3], 0.0) for un in units]
    for lvl in range(1, 6):
        t1s = [_dot(x.astype(BF16), _block_diag(jnp.where((ixj >> lvl) == 1, un[3], 0.0), head_mask))
               for x, un in zip(xs, units)]
        xs = [x - _dot(t1.astype(BF16), _block_diag(x, head_mask)) for x, t1 in zip(xs, t1s)]
    for x, (d, rows, lanes, _, vb, kbe) in zip(xs, units):
        rhs = jnp.concatenate([_block_diag(vb, head_mask), _block_diag(kbe, head_mask)], axis=1)
        uw = _dot(x.astype(BF16), rhs)
        u_s[d, rows, lanes] = uw[:, 0:GROUP_W]
        w_s[d, rows, lanes] = uw[:, GROUP_W:2 * GROUP_W]

    for c in range(CHUNKS_PER_RB):
        for d in range(2):
            r = c if d == 0 else CHUNKS_PER_RB - 1 - c
            rows = slice(r * CHUNK, (r + 1) * CHUNK)
            for g in range(N_GROUPS):
                lanes = slice(g * GROUP_W, (g + 1) * GROUP_W)
                s = st_s[d, g]
                wq = jnp.concatenate([w_s[d, rows, lanes], qg_s[d, rows, lanes]], axis=0)
                ws_qs = _dot(wq.astype(BF16), s.astype(BF16))
                v_new = u_s[d, rows, lanes] - ws_qs[0:CHUNK]
                o = ws_qs[CHUNK:2 * CHUNK] + _dot(p_s[d, rows, lanes].astype(BF16),
                                                  _block_diag(v_new, head_mask))
                upd = _dot_tn(kg_s[d, rows, lanes].astype(BF16), v_new.astype(BF16))
                st_s[d, g] = s * gl_s[d, r * 8:r * 8 + 1, lanes] + jnp.where(same_head, upd, 0.0)
                o_refs[d][rows, lanes] = o

    @pl.when(j == nb - 1)
    def _():
        for d in range(2):
            for g in range(N_GROUPS):
                s = st_s[d, g]
                nat = s[:, 0:DV]
                for hh in range(1, HEAD_GROUP):
                    nat = nat + s[:, hh * DV:(hh + 1) * DV]
                sfin_ref[0, d, g * GROUP_W:(g + 1) * GROUP_W, :] = nat


def _delta_call(pd, conv_w, par, esel, ones_bd, s0, n_seq, nb, blk0):
    zero_init = s0 is None
    n_blocks_all = T_ALL // DELTA_RB
    blk_of = (lambda s, j: blk0 + s * nb + j, lambda s, j: blk0 + s * nb + nb - 1 - j)
    in_specs, args = [], []
    for d in range(1 if nb == 1 else 2):
        prev, nxt = _halo_specs(QKV_W, DELTA_RB, n_blocks_all, blk_of[d])
        in_specs += [pl.BlockSpec((DELTA_RB, PD_W), lambda s, j, d=d: (blk_of[d](s, j), 0)), prev, nxt]
        args += [pd, pd, pd]
    in_specs += [
        pl.BlockSpec((3, QKV_W), lambda s, j: (0, 0)),
        pl.BlockSpec((8, LANE), lambda s, j: (0, 0)),
        pl.BlockSpec((2, 2, LANE, H_A * DK), lambda s, j: (0, 0, 0, 0)),
        pl.BlockSpec((H_A * DK, H_A * DK), lambda s, j: (0, 0)),
    ]
    args += [conv_w, par, esel, ones_bd]
    if not zero_init:
        in_specs.append(pl.BlockSpec((1, 2, H_A * DK, DV), lambda s, j: (s, 0, 0, 0)))
        args.append(s0)
    rows = n_seq * nb * DELTA_RB
    dir_buf = lambda n: pltpu.VMEM((2, n, H_A * DK), F32)
    return pl.pallas_call(
        functools.partial(_delta_kernel, nb, zero_init),
        grid=(n_seq, nb),
        in_specs=in_specs,
        out_specs=[pl.BlockSpec((DELTA_RB, W_A), lambda s, j: (s * nb + j, 0)),
                   pl.BlockSpec((DELTA_RB, W_A), lambda s, j: (s * nb + nb - 1 - j, 0)),
                   pl.BlockSpec((1, 2, H_A * DK, DV), lambda s, j: (s, 0, 0, 0))],
        out_shape=[jax.ShapeDtypeStruct((rows, W_A), F32),
                   jax.ShapeDtypeStruct((rows, W_A), F32),
                   jax.ShapeDtypeStruct((n_seq, 2, H_A * DK, DV), F32)],
        scratch_shapes=[dir_buf(DELTA_RB) for _ in range(5)]
        + [dir_buf(CHUNKS_PER_RB * 8), pltpu.VMEM((2, N_GROUPS, GROUP_W, GROUP_W), F32)],
        compiler_params=_cparams(("parallel", "arbitrary"), VMEM_LIMIT),
        name=f"deltanet_nb{nb}",
    )(*args)


def _merge_kernel(ofc_ref, ofl_ref, obc_ref, obl_ref, z_ref, ybc_ref, ybl_ref, ycc_ref, ycl_ref,
                  pg_ref, x_ref, mod_ref, na_ref, ones_ref, wpa_ref, wpb_ref, wpc_ref, wo_ref, out_ref):
    is_ctx = pl.program_id(0) < N_CTX_TILES
    o = jnp.where(is_ctx, ofc_ref[...] + obc_ref[...], ofl_ref[...] + obl_ref[...])
    yb = jnp.where(is_ctx, ybc_ref[...], ybl_ref[...])
    yc = jnp.where(is_ctx, ycc_ref[...], ycl_ref[...])
    ms = _mm_const_rhs(o * o, ones_ref[...], 2) * (1.0 / DV)
    ya = (o * lax.rsqrt(ms + EPS) * na_ref[...]) * _silu(z_ref[...])
    merged = (_sigmoid(pg_ref[:, 0:D_MODEL]) * _dot(ya.astype(BF16), wpa_ref[...])
              + _sigmoid(pg_ref[:, D_MODEL:2 * D_MODEL]) * _dot(yb.astype(BF16), wpb_ref[...])
              + _sigmoid(pg_ref[:, 2 * D_MODEL:3 * D_MODEL]) * _dot(yc.astype(BF16), wpc_ref[...]))
    y = _dot(merged.astype(BF16), wo_ref[...])
    out_ref[...] = x_ref[...] + mod_ref[0][2:3] * y


def _merge_call(o_ctx, o_lat, pd, yb, yc, pg, x, mod3, na512, ones_bd, wpa, wpb, wpc, wo):
    row = lambda i: (i, 0)
    const = lambda i: (0, 0)
    assert W_A == W_B == W_C
    return pl.pallas_call(
        _merge_kernel,
        grid=(T_ALL // ROW_TILE,),
        in_specs=[
            *_ctx_lat_specs(W_A), *_ctx_lat_specs(W_A),
            pl.BlockSpec((ROW_TILE, W_A), lambda i: (i, OFF_Z // W_A)),
            *_ctx_lat_specs(W_B), *_ctx_lat_specs(W_C),
            pl.BlockSpec((ROW_TILE, 3 * D_MODEL), row),
            pl.BlockSpec((ROW_TILE, D_MODEL), row),
            pl.BlockSpec((1, 6, D_MODEL), lambda i: (_mod_row_block(i), 0, 0)),
            pl.BlockSpec((1, W_A), const),
            pl.BlockSpec((W_A, W_A), const),
            pl.BlockSpec((W_A, D_MODEL), const),
            pl.BlockSpec((W_B, D_MODEL), const),
            pl.BlockSpec((W_C, D_MODEL), const),
            pl.BlockSpec((D_MODEL, D_MODEL), const),
        ],
        out_specs=pl.BlockSpec((ROW_TILE, D_MODEL), row),
        out_shape=jax.ShapeDtypeStruct((T_ALL, D_MODEL), F32),
        compiler_params=_cparams(("parallel",), VMEM_LIMIT),
        name="branch_merge",
    )(o_ctx[0], o_lat[0], o_ctx[1], o_lat[1], pd, yb[0], yb[1], yc[0], yc[1], pg, x, mod3, na512,
      ones_bd, wpa, wpb, wpc, wo)


FFN_TN = D_FF // 2


def _ffn_kernel(final_norm, x_ref, mod_ref, g_ref, wgu_ref, wdn_ref, nf_ref, out_ref):
    m = mod_ref[0]
    x = x_ref[...]
    h = _rms_mod(x, g_ref[...], m[4:5], m[3:4]).astype(BF16)
    acc = None
    for c in range(0, D_FF, FFN_TN):
        gate = _dot(h, wgu_ref[:, c:c + FFN_TN])
        up = _dot(h, wgu_ref[:, D_FF + c:D_FF + c + FFN_TN])
        part = _dot((_silu(gate) * up).astype(BF16), wdn_ref[c:c + FFN_TN, :])
        acc = part if acc is None else acc + part
    xn = x + m[5:6] * acc
    if final_norm:
        ms = jnp.mean(xn * xn, axis=-1, keepdims=True)
        xn = xn * lax.rsqrt(ms + EPS) * nf_ref[...]
    out_ref[...] = xn


def _ffn_call(x, mod3, g, wgu, wdn, nf, final_norm):
    row = lambda i: (i, 0)
    const = lambda i: (0, 0)
    return pl.pallas_call(
        functools.partial(_ffn_kernel, final_norm),
        grid=(T_ALL // ROW_TILE,),
        in_specs=[
            pl.BlockSpec((ROW_TILE, D_MODEL), row),
            pl.BlockSpec((1, 6, D_MODEL), lambda i: (_mod_row_block(i), 0, 0)),
            pl.BlockSpec((1, D_MODEL), const),
            pl.BlockSpec((D_MODEL, 2 * D_FF), const),
            pl.BlockSpec((D_FF, D_MODEL), const),
            pl.BlockSpec((1, D_MODEL), const),
        ],
        out_specs=pl.BlockSpec((ROW_TILE, D_MODEL), row),
        out_shape=jax.ShapeDtypeStruct((T_ALL, D_MODEL), F32),
        compiler_params=_cparams(("parallel",), VMEM_LIMIT),
        name="ffn_final" if final_norm else "ffn",
    )(x, mod3, g.reshape(1, D_MODEL), wgu, wdn, nf.reshape(1, D_MODEL))


TABLE_SPLIT = 64


def _grid_pos_embed(n_tokens):
    rows = n_tokens // GRID_W
    r = jnp.repeat(jnp.arange(rows), GRID_W).astype(F32)
    col = jnp.tile(jnp.arange(GRID_W), rows).astype(F32)
    quarter = D_MODEL // 4
    omega = 1.0 / (10000.0 ** (jnp.arange(quarter, dtype=F32) / quarter))

    def emb(pos):
        a = pos[:, None] * omega[None, :]
        return jnp.concatenate([jnp.sin(a), jnp.cos(a)], axis=-1)

    return jnp.concatenate([emb(r), emb(col)], axis=-1)


def _cos_nsin_tables(n, period):
    t = jnp.arange(n, dtype=jnp.int32)[None, :]

    def cs(r):
        ang = ((r * t) % period).astype(F32) * (2.0 * math.pi / period)
        return jnp.cos(ang), jnp.sin(ang)

    ca, sa = cs(jnp.arange(n // TABLE_SPLIT, dtype=jnp.int32)[:, None] * TABLE_SPLIT)
    cb, sb = cs(jnp.arange(TABLE_SPLIT, dtype=jnp.int32)[:, None])
    ca, sa, cb, sb = lax.optimization_barrier((ca, sa, cb, sb))
    ca, sa = ca[:, None, :], sa[:, None, :]
    cos = (ca * cb[None] - sa * sb[None]).reshape(n, n)
    nsin = (-(sa * cb[None] + ca * sb[None])).reshape(n, n)
    return cos.astype(BF16), nsin.astype(BF16)


def _hyena_positions(L):
    bands = (HY_EMB - 1) // 2
    t = jnp.linspace(0.0, 1.0, L, dtype=F32)[:, None]
    wpos = (2.0 * math.pi / L) * jnp.arange(L, dtype=F32)[:, None]
    fr = jnp.linspace(1e-4, bands - 1, bands, dtype=F32)[None, :]
    zpos = jnp.concatenate([t, jnp.cos(fr * wpos), -jnp.sin(fr * wpos)], axis=-1)
    zpos = jnp.pad(zpos, ((0, 0), (0, LANE - HY_EMB)))
    deltas = jnp.abs(jnp.linspace(math.log(HY_DECAY_TARGET) / HY_SLOW_PCT,
                                  math.log(HY_DECAY_TARGET) / HY_FAST_PCT, W_B, dtype=F32))
    window = jnp.exp(-t * deltas[None, :])
    return zpos, window


def _group_tables():
    r = jnp.arange(DC, dtype=jnp.int32)
    ang = ((r[:, None] * r[None, :]) % DC).astype(F32) * (2.0 * math.pi / DC)
    eye = jnp.eye(G_C, dtype=F32)
    return jnp.kron(eye, jnp.cos(ang)).astype(BF16), jnp.kron(eye, jnp.sin(ang)).astype(BF16)


def _head_tables():
    ones_bd = jnp.kron(jnp.eye(H_A, dtype=F32), jnp.ones((DK, DK), F32)).astype(BF16)
    lane = jnp.arange(LANE)[:, None]
    head = (jnp.arange(H_A * DK) // DK)[None, :]
    sel = []
    for d in range(2):
        sel.append(jnp.stack([(lane == d * H_A + head), (lane == 2 * H_A + d * H_A + head)]))
    esel = jnp.stack(sel).astype(BF16)
    return ones_bd, esel


def kernel(x_prompt, x_sample, state_delta, c, c_ctx, w_mod, b_mod, norm1_g, norm2_g, w_in, conv_qkv, a_log, dt_bias, norm_a, conv_hy, hy_w1, hy_b1, hy_freq, hy_w2, hy_b2, hy_w3, hy_bias, w_pa, w_pb, w_pc, w_o, w_gu, w_down, norm_f):
    assert x_prompt.shape == (N_CTX_SEQ, L_CTX, D_MODEL) and x_sample.shape == (N_LAT_SEQ, L_LAT, D_MODEL)
    st = jnp.pad(jnp.concatenate([c_ctx[None], c], axis=0).T, ((0, 0), (0, 8 - 1 - N_LAT_SEQ)))
    mod = _mod_call(st, w_mod, b_mod).reshape(DEPTH, 8, 6, D_MODEL)

    ones_bd, esel = _head_tables()
    bdc, bds = _group_tables()
    seqs = ((L_CTX, N_CTX_SEQ, 0), (L_LAT, N_LAT_SEQ, T_CTX // L_LAT))
    tables = {L: (_cos_nsin_tables(L, 2 * L), _cos_nsin_tables(L, L), _hyena_positions(L))
              for L, _, _ in seqs}

    x = None
    ctx_states = []
    for l in range(DEPTH):
        mod3 = mod[l, 0:3]
        wl = w_in[l].astype(BF16)
        wd = jnp.concatenate([wl[:, 0:OFF_HY], jnp.zeros((D_MODEL, LANE - 4 * H_A), BF16)], axis=1)
        wh = wl[:, OFF_HY:OFF_FN]
        wf = wl[:, OFF_FN:OFF_GATE]
        wg = wl[:, OFF_GATE:]
        if l == 0:
            xs = (x_prompt.reshape(T_CTX, D_MODEL), x_sample.reshape(T_LAT, D_MODEL), _grid_pos_embed(L_LAT))
            pd, ph, pf, pg, x = _inproj_call(xs, mod3, norm1_g[l], wd, wh, wf, wg)
        else:
            pd, ph, pf, pg = _inproj_call((x,), mod3, norm1_g[l], wd, wh, wf, wg)

        par = jnp.zeros((8, LANE), F32)
        par = par.at[0, 2 * H_A:4 * H_A].set(a_log[l].reshape(-1))
        par = par.at[1, 2 * H_A:4 * H_A].set(dt_bias[l].reshape(-1))
        *o_ctx, s_ctx = _delta_call(pd, conv_qkv[l], par, esel, ones_bd, None,
                                    N_CTX_SEQ, L_CTX // DELTA_RB, 0)
        s0 = state_delta[:, l].astype(F32).reshape(N_LAT_SEQ, 2, H_A * DK, DV)
        *o_lat, _ = _delta_call(pd, conv_qkv[l], par, esel, ones_bd, s0,
                                N_LAT_SEQ, L_LAT // DELTA_RB, T_CTX // DELTA_RB)
        ctx_states.append(s_ctx.reshape(N_CTX_SEQ, 2, H_A, DK, DV))

        v, x12 = _hyprep_call(ph, conv_hy[l])
        w1p = jnp.pad(hy_w1[l], ((0, LANE - HY_EMB), (0, 0)))
        yb, yc = [], []
        for L, n_seq, blk0 in seqs:
            (cos2, nsin2), (cos1, nsin1), (zpos, window) = tables[L]
            kspec = _filter_call(L, zpos, w1p, hy_b1[l][None], hy_freq[l][None], hy_w2[l],
                                 hy_b2[l][None], hy_w3[l], window, cos2, nsin2)
            yb.append(_hyena_call(v, x12, kspec, hy_bias[l][:, None, :], cos2, nsin2, n_seq, L, blk0, 256))
            yc.append(_fnet_call(pf, cos1, nsin1, bdc, bds, n_seq, L, blk0))

        na512 = jnp.tile(norm_a[l], H_A)[None]
        x = _merge_call(o_ctx, o_lat, pd, yb, yc, pg, x, mod3, na512, ones_bd,
                        w_pa[l].astype(BF16), w_pb[l].astype(BF16), w_pc[l].astype(BF16),
                        w_o[l].astype(BF16))
        x = _ffn_call(x, mod3, norm2_g[l], w_gu[l].astype(BF16), w_down[l].astype(BF16),
                      norm_f, l == DEPTH - 1)

    y_prompt = x[:T_CTX].reshape(N_CTX_SEQ, L_CTX, D_MODEL)
    y_sample = x[T_CTX:].reshape(N_LAT_SEQ, L_LAT, D_MODEL)
    new_state = jnp.stack(ctx_states, axis=1).astype(x_prompt.dtype)
    return (y_prompt, y_sample, new_state)
```

```python
import functools
import math

import jax
import jax.numpy as jnp
from jax import lax
from jax.experimental import pallas as pl
from jax.experimental.pallas import tpu as pltpu

F32 = jnp.float32
BF16 = jnp.bfloat16

D_MODEL = 1024
N_CTX_SEQ = 32
L_CTX = 256
DEPTH = 2
N_LAT_SEQ = 2
L_LAT = 2048
GRID_W = 64
EPS = 1e-6
H_A = 8
DK = 64
DV = 64
W_A = H_A * DV
QKV_W = 2 * H_A * DK + H_A * DV
CHUNK = 64
W_B = 512
HY_EMB = 33
HY_HID = 64
HY_DECAY_TARGET = 1e-2
HY_FAST_PCT = 0.3
HY_SLOW_PCT = 1.5
G_C = 8
DC = 64
W_C = G_C * DC
D_FF = ((8 * D_MODEL + 3 * 256 - 1) // (3 * 256)) * 256
OFF_Z = QKV_W
OFF_B = OFF_Z + W_A
OFF_A = OFF_B + 2 * H_A
OFF_HY = OFF_A + 2 * H_A
OFF_FN = OFF_HY + 3 * W_B
OFF_GATE = OFF_FN + W_C

T_CTX = N_CTX_SEQ * L_CTX
T_LAT = N_LAT_SEQ * L_LAT
T_ALL = T_CTX + T_LAT
ROW_TILE = 256
N_CTX_TILES = T_CTX // ROW_TILE
N_LAT_TILES = T_LAT // ROW_TILE
LANE = 128
PD_W = QKV_W + W_A + LANE
HEAD_GROUP = 4
GROUP_W = HEAD_GROUP * DK
N_GROUPS = H_A // HEAD_GROUP
VMEM_LIMIT = 56 * 1024 * 1024


def _cparams(sem, vmem=None):
    return pltpu.CompilerParams(dimension_semantics=sem, vmem_limit_bytes=vmem)


def _dot(a, b):
    return jnp.dot(a, b, preferred_element_type=F32)


def _dot_nt(a, b):
    return lax.dot_general(a, b, (((1,), (1,)), ((), ())), preferred_element_type=F32)


def _dot_tn(a, b):
    return lax.dot_general(a, b, (((0,), (0,)), ((), ())), preferred_element_type=F32)


def _split(a, n):
    parts = []
    rem = a
    for i in range(n):
        p = rem.astype(BF16)
        parts.append(p)
        if i + 1 < n:
            rem = rem - p.astype(F32)
    return parts


def _mm3(a, b):
    ah, al = _split(a, 2)
    bh, bl = _split(b, 2)
    return _dot(ah, bh) + (_dot(ah, bl) + _dot(al, bh))


def _mm_const_rhs(a, c, n):
    out = None
    for p in _split(a, n):
        t = _dot(p, c)
        out = t if out is None else out + t
    return out


def _mm_const_lhs(c, b, n):
    out = None
    for p in _split(b, n):
        t = _dot(c, p)
        out = t if out is None else out + t
    return out


def _sigmoid(x):
    return 1.0 / (1.0 + jnp.exp(-x))


def _silu(x):
    return x * _sigmoid(x)


def _softplus(x):
    return jnp.maximum(x, 0.0) + jnp.log(1.0 + jnp.exp(-jnp.abs(x)))


def _mod_row_block(i):
    per_lat = L_LAT // ROW_TILE
    return jnp.where(i < N_CTX_TILES, 0, 1 + (i - N_CTX_TILES) // per_lat)


def _ctx_tile(i):
    return jnp.minimum(i, N_CTX_TILES - 1)


def _lat_tile(i):
    return jnp.maximum(i - N_CTX_TILES, 0)


def _ctx_lat_specs(width):
    return (pl.BlockSpec((ROW_TILE, width), lambda i: (_ctx_tile(i), 0)),
            pl.BlockSpec((ROW_TILE, width), lambda i: (_lat_tile(i), 0)))


MOD_TN = 512


def _mod_kernel(st_ref, w_ref, b_ref, out_ref):
    s = _silu(st_ref[...])
    w = w_ref[0]
    rows = [jnp.sum(s[:, r:r + 1] * w, axis=0, keepdims=True) + b_ref[0] for r in range(3)]
    rows.append(jnp.zeros((5, MOD_TN), F32))
    out_ref[0] = jnp.concatenate(rows, axis=0)


def _mod_call(st, w_mod, b_mod):
    n6 = 6 * D_MODEL
    return pl.pallas_call(
        _mod_kernel,
        grid=(DEPTH, n6 // MOD_TN),
        in_specs=[
            pl.BlockSpec((D_MODEL, 8), lambda l, j: (0, 0)),
            pl.BlockSpec((1, D_MODEL, MOD_TN), lambda l, j: (l, 0, j)),
            pl.BlockSpec((1, 1, MOD_TN), lambda l, j: (l, 0, j)),
        ],
        out_specs=pl.BlockSpec((1, 8, MOD_TN), lambda l, j: (l, 0, j)),
        out_shape=jax.ShapeDtypeStruct((DEPTH, 8, n6), F32),
        compiler_params=_cparams(("parallel", "parallel")),
        name="adaln_mod",
    )(st, w_mod, b_mod.reshape(DEPTH, 1, n6))


INPROJ_TN = 512
INPROJ_WIDTHS = (PD_W, 3 * W_B, W_C, 3 * D_MODEL)


def _rms_mod(x, g, scale, shift):
    ms = jnp.mean(x * x, axis=-1, keepdims=True)
    return (x * lax.rsqrt(ms + EPS) * g) * (1.0 + scale) + shift


def _inproj_kernel(first, *refs):
    if first:
        (xc_ref, xl_ref, pos_ref, mod_ref, g_ref, w_ref, pd_ref, ph_ref, pf_ref, pg_ref, x_ref) = refs
        x = jnp.where(pl.program_id(0) < N_CTX_TILES, xc_ref[...], xl_ref[...] + pos_ref[...])
        x_ref[...] = x
    else:
        (xin_ref, mod_ref, g_ref, w_ref, pd_ref, ph_ref, pf_ref, pg_ref) = refs
        x = xin_ref[...]
    m = mod_ref[0]
    h = _rms_mod(x, g_ref[...], m[1:2], m[0:1]).astype(BF16)
    col0 = 0
    for o_ref in (pd_ref, ph_ref, pf_ref, pg_ref):
        n = o_ref.shape[1]
        for c in range(0, n, INPROJ_TN):
            e = min(c + INPROJ_TN, n)
            y = _dot(h, w_ref[0, :, col0 + c:col0 + e])
            if o_ref is pg_ref:
                y = _sigmoid(y)
            o_ref[:, c:e] = y.astype(o_ref.dtype)
        col0 += n


def _resident(shape, index_map):
    return pl.BlockSpec(shape, index_map, pipeline_mode=pl.Buffered(1))


def _inproj_call(xs, mod3, g, w_all, layer):
    first = len(xs) == 3
    widths = INPROJ_WIDTHS
    row = lambda i: (i, 0)
    const = lambda i: (0, 0)
    if first:
        per_lat = L_LAT // ROW_TILE
        x_specs = list(_ctx_lat_specs(D_MODEL)) + [
            pl.BlockSpec((ROW_TILE, D_MODEL), lambda i: (_lat_tile(i) % per_lat, 0))]
    else:
        x_specs = [pl.BlockSpec((ROW_TILE, D_MODEL), row)]
    out_widths = widths + ((D_MODEL,) if first else ())
    out_dtypes = (F32, F32, F32, BF16) + ((F32,) if first else ())
    return pl.pallas_call(
        functools.partial(_inproj_kernel, first),
        grid=(T_ALL // ROW_TILE,),
        in_specs=x_specs + [
            pl.BlockSpec((1, 6, D_MODEL), lambda i: (_mod_row_block(i), 0, 0)),
            pl.BlockSpec((1, D_MODEL), const),
            _resident((1, D_MODEL, sum(widths)), lambda i: (layer, 0, 0)),
        ],
        out_specs=[pl.BlockSpec((ROW_TILE, w), row) for w in out_widths],
        out_shape=[jax.ShapeDtypeStruct((T_ALL, w), dt) for w, dt in zip(out_widths, out_dtypes)],
        compiler_params=_cparams(("parallel",), VMEM_LIMIT),
        name="inproj_first" if first else "inproj",
    )(*xs, mod3, g.reshape(1, D_MODEL), w_all)


def _conv3_rows(cur, prev_row, next_row, w):
    n = cur.shape[0]
    ridx = lax.broadcasted_iota(jnp.int32, cur.shape, 0)
    up = jnp.where(ridx == 0, prev_row, pltpu.roll(cur, 1, 0))
    dn = jnp.where(ridx == n - 1, next_row, pltpu.roll(cur, n - 1, 0))
    return up * w[0:1] + cur * w[1:2] + dn * w[2:3]


def _halo_specs(width, rows_per_block, n_row_blocks, blk_of):
    per = rows_per_block // 8
    last = n_row_blocks * per - 1
    prev = pl.BlockSpec((8, width), lambda *a: (jnp.maximum(blk_of(*a) * per - 1, 0), 0))
    nxt = pl.BlockSpec((8, width), lambda *a: (jnp.minimum((blk_of(*a) + 1) * per, last), 0))
    return prev, nxt


def _hyprep_kernel(per_lat, ph_ref, prev_ref, next_ref, w_ref, v_ref, x12_ref):
    i = pl.program_id(0)
    lat_pos = (i - N_CTX_TILES) % per_lat
    first = (i < N_CTX_TILES) | (lat_pos == 0)
    last = (i < N_CTX_TILES) | (lat_pos == per_lat - 1)
    prev_row = jnp.where(first, 0.0, prev_ref[7:8, :])
    next_row = jnp.where(last, 0.0, next_ref[0:1, :])
    uc = _conv3_rows(ph_ref[...], prev_row, next_row, w_ref[...])
    x12_ref[0] = uc[:, 0:W_B]
    x12_ref[1] = uc[:, W_B:2 * W_B]
    v_ref[...] = uc[:, 2 * W_B:3 * W_B]


def _hyprep_call(ph, conv_w):
    nblk = T_ALL // ROW_TILE
    prev, nxt = _halo_specs(3 * W_B, ROW_TILE, nblk, lambda i: i)
    assert L_CTX == ROW_TILE
    return pl.pallas_call(
        functools.partial(_hyprep_kernel, L_LAT // ROW_TILE),
        grid=(nblk,),
        in_specs=[pl.BlockSpec((ROW_TILE, 3 * W_B), lambda i: (i, 0)), prev, nxt,
                  pl.BlockSpec((3, 3 * W_B), lambda i: (0, 0))],
        out_specs=[pl.BlockSpec((ROW_TILE, W_B), lambda i: (i, 0)),
                   pl.BlockSpec((2, ROW_TILE, W_B), lambda i: (0, i, 0))],
        out_shape=[jax.ShapeDtypeStruct((T_ALL, W_B), F32),
                   jax.ShapeDtypeStruct((2, T_ALL, W_B), F32)],
        compiler_params=_cparams(("parallel",)),
        name="hyena_prep",
    )(ph, ph, ph, conv_w)


FILT_RT = 256


def _alternating_sum(x):
    t = lax.broadcasted_iota(jnp.int32, x.shape, 0)
    return jnp.sum(jnp.where(t % 2 == 0, x, -x), axis=0, keepdims=True)


def _filter_kernel(L, zpos_ref, w1_ref, b1_ref, fq_ref, w2_ref, b2_ref, w3_ref, win_ref,
                   cos_ref, nsin_ref, k_ref, hs_s, hm_s, krl_s):
    rt = pl.program_id(1)

    @pl.when(rt == 0)
    def _():
        fq = fq_ref[...]
        alt_acc = jnp.zeros((1, W_B), F32)
        for r0 in range(0, L, FILT_RT):
            rows = slice(r0, r0 + FILT_RT)
            h = jnp.sin(fq * (_mm3(zpos_ref[rows, :], w1_ref[...]) + b1_ref[...]))
            h = jnp.sin(fq * (_mm3(h, w2_ref[...]) + b2_ref[...]))
            hf = _mm3(h, w3_ref[...])
            win = win_ref[rows, :]
            fw = hf[:, 0:W_B] * win
            bw = hf[:, W_B:2 * W_B] * win
            hsum = fw + bw
            hs_s[rows, :] = hsum.astype(BF16)
            hm_s[rows, :] = (fw - bw).astype(BF16)
            alt_acc = alt_acc + _alternating_sum(hsum)
        krl_s[...] = jnp.broadcast_to(alt_acc, krl_s.shape)

    p1 = _dot(cos_ref[...], hs_s[...])
    p2 = _dot(nsin_ref[...], hm_s[...])
    first = (rt * FILT_RT + lax.broadcasted_iota(jnp.int32, p1.shape, 0)) == 0
    k_ref[0, 0] = p1
    k_ref[0, 1] = jnp.where(first, krl_s[0:1, :], p1)
    k_ref[0, 2] = jnp.where(first, 0.0, p2)


def _filter_call(L, zpos, w1p, b1, fq, w2, b2, w3, win, cos, nsin):
    nrt = L // FILT_RT
    c2 = lambda o, r: (0, 0)
    return pl.pallas_call(
        functools.partial(_filter_kernel, L),
        grid=(2, nrt),
        in_specs=[
            pl.BlockSpec((L, LANE), c2),
            pl.BlockSpec((LANE, HY_HID), c2),
            pl.BlockSpec((1, HY_HID), c2),
            pl.BlockSpec((1, HY_HID), c2),
            pl.BlockSpec((HY_HID, HY_HID), c2),
            pl.BlockSpec((1, HY_HID), c2),
            pl.BlockSpec((HY_HID, 2 * W_B), lambda o, r: (0, o)),
            pl.BlockSpec((L, W_B), c2),
            pl.BlockSpec((FILT_RT, L), lambda o, r: (r, 0)),
            pl.BlockSpec((FILT_RT, L), lambda o, r: (r, 0)),
        ],
        out_specs=pl.BlockSpec((1, 3, FILT_RT, W_B), lambda o, r: (o, 0, r, 0)),
        out_shape=jax.ShapeDtypeStruct((2, 3, L, W_B), F32),
        scratch_shapes=[pltpu.VMEM((L, W_B), BF16), pltpu.VMEM((L, W_B), BF16),
                        pltpu.VMEM((8, W_B), F32)],
        compiler_params=_cparams(("parallel", "arbitrary"), VMEM_LIMIT),
        name=f"hyena_filter_{L}",
    )(zpos, w1p, b1, fq, w2, b2, w3, win, cos, nsin)


HYENA_FT = 256

def _hyena_kernel(L, ft, nft, v_ref, gate_ref, k_ref, bias_ref, crow_ref, srow_ref, ccol_ref, scol_ref,
                  out_ref, zf_s, zb_s, acc_s, nyq_s):
    o = pl.program_id(1)
    f = pl.program_id(2)

    @pl.when((o == 0) & (f == 0))
    def _():
        v = v_ref[...]
        zf_s[...] = v
        zb_s[...] = v.astype(BF16)

    @pl.when(f == 0)
    def _():
        nyq_s[...] = jnp.broadcast_to(_alternating_sum(zf_s[...]), nyq_s.shape)

    zb = zb_s[...]
    top = _dot(crow_ref[...], zb)
    first = (f * ft + lax.broadcasted_iota(jnp.int32, top.shape, 0)) == 0
    bot = jnp.where(first, nyq_s[0:1, :], _dot(srow_ref[...], zb))
    krt = k_ref[0, 0]
    krb = k_ref[0, 1]
    ki = k_ref[0, 2]
    yt = top * krt - bot * ki
    yb = top * ki + bot * krb
    part = (_dot(ccol_ref[...], jnp.where(first, 0.5 * yt, yt).astype(BF16))
            + _dot(scol_ref[...], yb.astype(BF16)))

    @pl.when(f == 0)
    def _():
        t = lax.broadcasted_iota(jnp.int32, part.shape, 0)
        acc_s[...] = part + jnp.where(t % 2 == 0, 0.5, -0.5) * yb[0:1, :]

    @pl.when(f != 0)
    def _():
        acc_s[...] += part

    @pl.when(f == nft - 1)
    def _():
        znew = gate_ref[0] * (acc_s[...] * (1.0 / L) + bias_ref[0] * zf_s[...])
        zf_s[...] = znew
        zb_s[...] = znew.astype(BF16)
        out_ref[...] = znew


def _hyena_call(v, x12, kspec, bias, cos, nsin, n_seq, L, row_blk0, ft):
    nft = L // ft
    return pl.pallas_call(
        functools.partial(_hyena_kernel, L, ft, nft),
        grid=(n_seq, 2, nft),
        in_specs=[
            pl.BlockSpec((L, W_B), lambda s, o, f: (row_blk0 + s, 0)),
            pl.BlockSpec((1, L, W_B), lambda s, o, f: (o, row_blk0 + s, 0)),
            pl.BlockSpec((1, 3, ft, W_B), lambda s, o, f: (o, 0, f, 0)),
            pl.BlockSpec((1, 1, W_B), lambda s, o, f: (o, 0, 0)),
            pl.BlockSpec((ft, L), lambda s, o, f: (f, 0)),
            pl.BlockSpec((ft, L), lambda s, o, f: (f, 0)),
            pl.BlockSpec((L, ft), lambda s, o, f: (0, f)),
            pl.BlockSpec((L, ft), lambda s, o, f: (0, f)),
        ],
        out_specs=pl.BlockSpec((L, W_B), lambda s, o, f: (s, 0)),
        out_shape=jax.ShapeDtypeStruct((n_seq * L, W_B), F32),
        scratch_shapes=[pltpu.VMEM((L, W_B), F32), pltpu.VMEM((L, W_B), BF16),
                        pltpu.VMEM((L, W_B), F32), pltpu.VMEM((8, W_B), F32)],
        compiler_params=_cparams(("parallel", "arbitrary", "arbitrary"), VMEM_LIMIT),
        name=f"hyena_conv_{L}",
    )(v, x12, kspec, bias, cos, nsin, cos, nsin)


HYENA_SHORT_SB = 4


def _hyena_short_kernel(L, v_ref, gate_ref, k_ref, bias_ref, cos_ref, nsin_ref, out_ref):
    cos = cos_ref[...]
    nsin = nsin_ref[...]
    t = lax.broadcasted_iota(jnp.int32, (L, W_B), 0)
    first = t == 0
    alt_half = jnp.where(t % 2 == 0, 0.5, -0.5)
    rows = [slice(s * L, (s + 1) * L) for s in range(HYENA_SHORT_SB)]
    zs = [v_ref[r, :] for r in rows]
    for o in range(2):
        krt, krb, ki = k_ref[o, 0], k_ref[o, 1], k_ref[o, 2]
        zbs = [z.astype(BF16) for z in zs]
        tops = [_dot(cos, zb) for zb in zbs]
        bots = [jnp.where(first, _alternating_sum(z), _dot(nsin, zb)) for z, zb in zip(zs, zbs)]
        yts = [top * krt - bot * ki for top, bot in zip(tops, bots)]
        ybs = [top * ki + bot * krb for top, bot in zip(tops, bots)]
        accs = [_dot(cos, jnp.where(first, 0.5 * yt, yt).astype(BF16)) + _dot(nsin, yb.astype(BF16))
                + alt_half * yb[0:1, :] for yt, yb in zip(yts, ybs)]
        zs = [gate_ref[o, r, :] * (acc * (1.0 / L) + bias_ref[o] * z) for r, acc, z in zip(rows, accs, zs)]
    for r, z in zip(rows, zs):
        out_ref[r, :] = z


def _hyena_short_call(v, x12, kspec, bias, cos, nsin, n_seq, L):
    rows = HYENA_SHORT_SB * L
    return pl.pallas_call(
        functools.partial(_hyena_short_kernel, L),
        grid=(n_seq // HYENA_SHORT_SB,),
        in_specs=[
            pl.BlockSpec((rows, W_B), lambda i: (i, 0)),
            pl.BlockSpec((2, rows, W_B), lambda i: (0, i, 0)),
            pl.BlockSpec((2, 3, L, W_B), lambda i: (0, 0, 0, 0)),
            pl.BlockSpec((2, 1, W_B), lambda i: (0, 0, 0)),
            pl.BlockSpec((L, L), lambda i: (0, 0)),
            pl.BlockSpec((L, L), lambda i: (0, 0)),
        ],
        out_specs=pl.BlockSpec((rows, W_B), lambda i: (i, 0)),
        out_shape=jax.ShapeDtypeStruct((n_seq * L, W_B), F32),
        compiler_params=_cparams(("parallel",), VMEM_LIMIT),
        name=f"hyena_conv_{L}",
    )(v, x12, kspec, bias, cos, nsin)


FNET_RT = 256


def _fnet_kernel(L, x_ref, cos_ref, nsin_ref, bdc_ref, bds_ref, out_ref, xc_s, xs_s):
    r = pl.program_id(1)

    @pl.when(r == 0)
    def _():
        for r0 in range(0, L, FNET_RT):
            xb = x_ref[r0:r0 + FNET_RT, :].astype(BF16)
            xc_s[r0:r0 + FNET_RT, :] = _dot(xb, bdc_ref[...]).astype(BF16)
            xs_s[r0:r0 + FNET_RT, :] = _dot(xb, bds_ref[...]).astype(BF16)

    y = _dot(cos_ref[...], xc_s[...]) + _dot(nsin_ref[...], xs_s[...])
    out_ref[...] = y * (1.0 / math.sqrt(DC * L))


def _fnet_call(pf, cos, nsin, bdc, bds, n_seq, L, row_blk0):
    nrt = L // FNET_RT
    return pl.pallas_call(
        functools.partial(_fnet_kernel, L),
        grid=(n_seq, nrt),
        in_specs=[
            pl.BlockSpec((L, W_C), lambda s, r: (row_blk0 + s, 0)),
            pl.BlockSpec((FNET_RT, L), lambda s, r: (r, 0)),
            pl.BlockSpec((FNET_RT, L), lambda s, r: (r, 0)),
            pl.BlockSpec((W_C, W_C), lambda s, r: (0, 0)),
            pl.BlockSpec((W_C, W_C), lambda s, r: (0, 0)),
        ],
        out_specs=pl.BlockSpec((FNET_RT, W_C), lambda s, r: (s * nrt + r, 0)),
        out_shape=jax.ShapeDtypeStruct((n_seq * L, W_C), F32),
        scratch_shapes=[pltpu.VMEM((L, W_C), BF16), pltpu.VMEM((L, W_C), BF16)],
        compiler_params=_cparams(("parallel", "arbitrary"), VMEM_LIMIT),
        name=f"fnet_{L}",
    )(pf, cos, nsin, bdc, bds)


DELTA_RB = 256
CHUNKS_PER_RB = DELTA_RB // CHUNK


def _block_diag(y, head_mask):
    return jnp.concatenate([y.astype(BF16)] * HEAD_GROUP, axis=0) * head_mask


def _delta_block_prep(pd_ref, prev_ref, next_ref, has_prev, has_next, cw, a_neg, dtb, ones_bd, head_mask):
    chunks = []
    for r in range(CHUNKS_PER_RB):
        rows = slice(r * CHUNK, (r + 1) * CHUNK)
        cur = pd_ref[rows, 0:QKV_W]
        if r == 0:
            prev_row = jnp.where(has_prev, prev_ref[7:8, :], 0.0)
        else:
            prev_row = pd_ref[r * CHUNK - 1:r * CHUNK, 0:QKV_W]
        if r == CHUNKS_PER_RB - 1:
            next_row = jnp.where(has_next, next_ref[0:1, :], 0.0)
        else:
            next_row = pd_ref[(r + 1) * CHUNK:(r + 1) * CHUNK + 1, 0:QKV_W]
        qkv = _silu(_conv3_rows(cur, prev_row, next_row, cw))
        q = qkv[:, 0:H_A * DK]
        k = qkv[:, H_A * DK:2 * H_A * DK]
        qn = q * lax.rsqrt(_mm_const_rhs(q * q, ones_bd, 2) + EPS) * (DK ** -0.5)
        kn = k * lax.rsqrt(_mm_const_rhs(k * k, ones_bd, 2) + EPS)
        ba = pd_ref[rows, OFF_B:OFF_B + LANE]
        gram = []
        for g in range(N_GROUPS):
            lanes = slice(g * GROUP_W, (g + 1) * GROUP_W)
            lhs = jnp.concatenate([kn[:, lanes], qn[:, lanes]], axis=0).astype(BF16)
            gram.append(_dot_nt(lhs, _block_diag(kn[:, lanes], head_mask)))
        chunks.append(dict(qn=qn, kn=kn, v=qkv[:, 2 * H_A * DK:], sig=_sigmoid(ba),
                           glog=a_neg * _softplus(ba + dtb), gram=gram))
    return chunks


def _delta_kernel(nb, zero_init, *refs):
    shared = nb == 1
    it = iter(refs)
    blocks = [(next(it), next(it), next(it))]
    if not shared:
        blocks.append((next(it), next(it), next(it)))
    cw_ref, par_ref, esel_ref, ones_ref = next(it), next(it), next(it), next(it)
    s0_ref = None if zero_init else next(it)
    o_refs = (next(it), next(it))
    sfin_ref = next(it)
    u_s, w_s, p_s, qg_s, kg_s, gl_s, st_s = (next(it) for _ in range(7))
    j = pl.program_id(1)

    ri = lax.broadcasted_iota(jnp.int32, (CHUNK, GROUP_W), 0)
    cj = lax.broadcasted_iota(jnp.int32, (CHUNK, GROUP_W), 1) % CHUNK
    ixj = ri ^ cj
    eye = ixj == 0
    br = lax.broadcasted_iota(jnp.int32, (GROUP_W, GROUP_W), 0) // CHUNK
    bc = lax.broadcasted_iota(jnp.int32, (GROUP_W, GROUP_W), 1) // CHUNK
    same_head = br == bc
    head_mask = jnp.where(same_head, 1.0, 0.0).astype(BF16)
    ti = lax.broadcasted_iota(jnp.int32, (CHUNK, CHUNK), 0)
    tm = lax.broadcasted_iota(jnp.int32, (CHUNK, CHUNK), 1)
    ri8 = lax.broadcasted_iota(jnp.int32, (CHUNK, H_A * DK), 0)
    cj8 = lax.broadcasted_iota(jnp.int32, (CHUNK, H_A * DK), 1) % CHUNK
    incl = (ri >= cj, ri <= cj)
    strict = (ri > cj, ri < cj)
    tri = tuple(jnp.where(m, 1.0, 0.0).astype(BF16) for m in (tm <= ti, tm >= ti))
    tt = (ri8 <= cj8, ri8 >= cj8)
    last_row = (CHUNK - 1, 0)

    @pl.when(j == 0)
    def _():
        for d in range(2):
            for g in range(N_GROUPS):
                if zero_init:
                    st_s[d, g] = jnp.zeros((GROUP_W, GROUP_W), F32)
                else:
                    nat = s0_ref[0, d, g * GROUP_W:(g + 1) * GROUP_W, :]
                    st_s[d, g] = jnp.where(same_head, jnp.concatenate([nat] * HEAD_GROUP, axis=1), 0.0)

    cw = cw_ref[...]
    a_neg = -jnp.exp(par_ref[0:1, :])
    dtb = par_ref[1:2, :]
    pos = (j, nb - 1 - j)
    prepped = [_delta_block_prep(*blocks[b], pos[b] > 0, pos[b] < nb - 1, cw, a_neg, dtb,
                                 ones_ref[...], head_mask) for b in range(len(blocks))]
    units = []
    for d in range(2):
        for r, ch in enumerate(prepped[0 if shared else d]):
            rows = slice(r * CHUNK, (r + 1) * CHUNK)
            beta = _mm_const_rhs(ch["sig"], esel_ref[d, 0], 3)
            gexp = _mm_const_rhs(ch["glog"], esel_ref[d, 1], 3)
            gcc8 = _mm_const_lhs(tri[d], gexp, 3)
            gcr8 = jnp.sum(jnp.where(tt[d], gexp, 0.0), axis=0, keepdims=True)
            for g in range(N_GROUPS):
                lanes = slice(g * GROUP_W, (g + 1) * GROUP_W)
                qn, kn, be, gcc = ch["qn"][:, lanes], ch["kn"][:, lanes], beta[:, lanes], gcc8[:, lanes]
                kq = ch["gram"][g]
                dec = jnp.exp(jnp.where(incl[d], gcc - gcr8[:, lanes], -1e30))
                a = jnp.where(strict[d], kq[0:CHUNK] * be * dec, 0.0)
                eg = jnp.exp(gcc)
                gcl = gcc[last_row[d]:last_row[d] + 1, :]
                units.append((d, rows, lanes, a, ch["v"][:, lanes] * be, kn * be * eg))
                p_s[d, rows, lanes] = kq[CHUNK:2 * CHUNK] * dec
                qg_s[d, rows, lanes] = qn * eg
                kg_s[d, rows, lanes] = kn * jnp.exp(gcl - gcc)
                gl_s[d, r * 8:(r + 1) * 8, lanes] = jnp.broadcast_to(jnp.exp(gcl), (8, GROUP_W))

    xs = [jnp.where(eye, 1.0, 0.0) - jnp.where(ixj == 1, un[3], 0.0) for un in units]
    for lvl in range(1, 6):
        t1s = [_dot(x.astype(BF16), _block_diag(jnp.where((ixj >> lvl) == 1, un[3], 0.0), head_mask))
               for x, un in zip(xs, units)]
        xs = [x - _dot(t1.astype(BF16), _block_diag(x, head_mask)) for x, t1 in zip(xs, t1s)]
    for x, (d, rows, lanes, _, vb, kbe) in zip(xs, units):
        rhs = jnp.concatenate([_block_diag(vb, head_mask), _block_diag(kbe, head_mask)], axis=1)
        uw = _dot(x.astype(BF16), rhs)
        u_s[d, rows, lanes] = uw[:, 0:GROUP_W]
        w_s[d, rows, lanes] = uw[:, GROUP_W:2 * GROUP_W]

    for c in range(CHUNKS_PER_RB):
        for d in range(2):
            r = c if d == 0 else CHUNKS_PER_RB - 1 - c
            rows = slice(r * CHUNK, (r + 1) * CHUNK)
            for g in range(N_GROUPS):
                lanes = slice(g * GROUP_W, (g + 1) * GROUP_W)
                s = st_s[d, g]
                wq = jnp.concatenate([w_s[d, rows, lanes], qg_s[d, rows, lanes]], axis=0)
                ws_qs = _dot(wq.astype(BF16), s.astype(BF16))
                v_new = u_s[d, rows, lanes] - ws_qs[0:CHUNK]
                o = ws_qs[CHUNK:2 * CHUNK] + _dot(p_s[d, rows, lanes].astype(BF16),
                                                  _block_diag(v_new, head_mask))
                upd = _dot_tn(kg_s[d, rows, lanes].astype(BF16), v_new.astype(BF16))
                st_s[d, g] = s * gl_s[d, r * 8:r * 8 + 1, lanes] + jnp.where(same_head, upd, 0.0)
                o_refs[d][rows, lanes] = o

    @pl.when(j == nb - 1)
    def _():
        for d in range(2):
            for g in range(N_GROUPS):
                s = st_s[d, g]
                nat = s[:, 0:DV]
                for hh in range(1, HEAD_GROUP):
                    nat = nat + s[:, hh * DV:(hh + 1) * DV]
                sfin_ref[0, d, g * GROUP_W:(g + 1) * GROUP_W, :] = nat


def _delta_call(pd, conv_w, par, esel, ones_bd, s0, n_seq, nb, blk0):
    zero_init = s0 is None
    n_blocks_all = T_ALL // DELTA_RB
    blk_of = (lambda s, j: blk0 + s * nb + j, lambda s, j: blk0 + s * nb + nb - 1 - j)
    in_specs, args = [], []
    for d in range(1 if nb == 1 else 2):
        prev, nxt = _halo_specs(QKV_W, DELTA_RB, n_blocks_all, blk_of[d])
        in_specs += [pl.BlockSpec((DELTA_RB, PD_W), lambda s, j, d=d: (blk_of[d](s, j), 0)), prev, nxt]
        args += [pd, pd, pd]
    in_specs += [
        pl.BlockSpec((3, QKV_W), lambda s, j: (0, 0)),
        pl.BlockSpec((8, LANE), lambda s, j: (0, 0)),
        pl.BlockSpec((2, 2, LANE, H_A * DK), lambda s, j: (0, 0, 0, 0)),
        pl.BlockSpec((H_A * DK, H_A * DK), lambda s, j: (0, 0)),
    ]
    args += [conv_w, par, esel, ones_bd]
    if not zero_init:
        in_specs.append(pl.BlockSpec((1, 2, H_A * DK, DV), lambda s, j: (s, 0, 0, 0)))
        args.append(s0)
    rows = n_seq * nb * DELTA_RB
    dir_buf = lambda n: pltpu.VMEM((2, n, H_A * DK), F32)
    return pl.pallas_call(
        functools.partial(_delta_kernel, nb, zero_init),
        grid=(n_seq, nb),
        in_specs=in_specs,
        out_specs=[pl.BlockSpec((DELTA_RB, W_A), lambda s, j: (s * nb + j, 0)),
                   pl.BlockSpec((DELTA_RB, W_A), lambda s, j: (s * nb + nb - 1 - j, 0)),
                   pl.BlockSpec((1, 2, H_A * DK, DV), lambda s, j: (s, 0, 0, 0))],
        out_shape=[jax.ShapeDtypeStruct((rows, W_A), F32),
                   jax.ShapeDtypeStruct((rows, W_A), F32),
                   jax.ShapeDtypeStruct((n_seq, 2, H_A * DK, DV), F32)],
        scratch_shapes=[dir_buf(DELTA_RB) for _ in range(5)]
        + [dir_buf(CHUNKS_PER_RB * 8), pltpu.VMEM((2, N_GROUPS, GROUP_W, GROUP_W), F32)],
        compiler_params=_cparams(("parallel", "arbitrary"), VMEM_LIMIT),
        name=f"deltanet_nb{nb}",
    )(*args)


FFN_TN = D_FF // 2


def _postmix_kernel(final_norm, ofc_ref, ofl_ref, obc_ref, obl_ref, z_ref, ybc_ref, ybl_ref, ycc_ref, ycl_ref,
                    pg_ref, x_ref, mod_ref, na_ref, ones_ref, wpa_ref, wpb_ref, wpc_ref, wo_ref,
                    g2_ref, wgu_ref, wdn_ref, nf_ref, out_ref):
    is_ctx = pl.program_id(0) < N_CTX_TILES
    m = mod_ref[0]
    o = jnp.where(is_ctx, ofc_ref[...] + obc_ref[...], ofl_ref[...] + obl_ref[...])
    yb = jnp.where(is_ctx, ybc_ref[...], ybl_ref[...])
    yc = jnp.where(is_ctx, ycc_ref[...], ycl_ref[...])
    ms = _mm_const_rhs(o * o, ones_ref[...], 2) * (1.0 / DV)
    ya = (o * lax.rsqrt(ms + EPS) * na_ref[...]) * _silu(z_ref[...])
    merged = (pg_ref[:, 0:D_MODEL].astype(F32) * _dot(ya.astype(BF16), wpa_ref[0])
              + pg_ref[:, D_MODEL:2 * D_MODEL].astype(F32) * _dot(yb.astype(BF16), wpb_ref[0])
              + pg_ref[:, 2 * D_MODEL:3 * D_MODEL].astype(F32) * _dot(yc.astype(BF16), wpc_ref[0]))
    x = x_ref[...] + m[2:3] * _dot(merged.astype(BF16), wo_ref[0])

    h = _rms_mod(x, g2_ref[...], m[4:5], m[3:4]).astype(BF16)
    acc = None
    for c in range(0, D_FF, FFN_TN):
        gate = _dot(h, wgu_ref[0, :, c:c + FFN_TN])
        up = _dot(h, wgu_ref[0, :, D_FF + c:D_FF + c + FFN_TN])
        part = _dot((_silu(gate) * up).astype(BF16), wdn_ref[0, c:c + FFN_TN, :])
        acc = part if acc is None else acc + part
    xn = x + m[5:6] * acc
    if final_norm:
        ms = jnp.mean(xn * xn, axis=-1, keepdims=True)
        xn = xn * lax.rsqrt(ms + EPS) * nf_ref[...]
    out_ref[...] = xn


def _postmix_call(o_ctx, o_lat, pd, yb, yc, pg, x, mod3, na512, ones_bd, wpa, wpb, wpc, wo,
                  g2, wgu, wdn, nf, layer, final_norm):
    row = lambda i: (i, 0)
    const = lambda i: (0, 0)
    lyr = lambda i: (layer, 0, 0)
    assert W_A == W_B == W_C
    return pl.pallas_call(
        functools.partial(_postmix_kernel, final_norm),
        grid=(T_ALL // ROW_TILE,),
        in_specs=[
            *_ctx_lat_specs(W_A), *_ctx_lat_specs(W_A),
            pl.BlockSpec((ROW_TILE, W_A), lambda i: (i, OFF_Z // W_A)),
            *_ctx_lat_specs(W_B), *_ctx_lat_specs(W_C),
            pl.BlockSpec((ROW_TILE, 3 * D_MODEL), row),
            pl.BlockSpec((ROW_TILE, D_MODEL), row),
            pl.BlockSpec((1, 6, D_MODEL), lambda i: (_mod_row_block(i), 0, 0)),
            pl.BlockSpec((1, W_A), const),
            _resident((W_A, W_A), const),
            _resident((1, W_A, D_MODEL), lyr),
            _resident((1, W_B, D_MODEL), lyr),
            _resident((1, W_C, D_MODEL), lyr),
            _resident((1, D_MODEL, D_MODEL), lyr),
            pl.BlockSpec((1, D_MODEL), const),
            _resident((1, D_MODEL, 2 * D_FF), lyr),
            _resident((1, D_FF, D_MODEL), lyr),
            pl.BlockSpec((1, D_MODEL), const),
        ],
        out_specs=pl.BlockSpec((ROW_TILE, D_MODEL), row),
        out_shape=jax.ShapeDtypeStruct((T_ALL, D_MODEL), F32),
        compiler_params=_cparams(("parallel",), VMEM_LIMIT),
        name="postmix_final" if final_norm else "postmix",
    )(o_ctx[0], o_lat[0], o_ctx[1], o_lat[1], pd, yb[0], yb[1], yc[0], yc[1], pg, x, mod3, na512,
      ones_bd, wpa, wpb, wpc, wo, g2.reshape(1, D_MODEL), wgu, wdn, nf.reshape(1, D_MODEL))


TABLE_SPLIT = 64


def _grid_pos_embed(n_tokens):
    rows = n_tokens // GRID_W
    r = jnp.repeat(jnp.arange(rows), GRID_W).astype(F32)
    col = jnp.tile(jnp.arange(GRID_W), rows).astype(F32)
    quarter = D_MODEL // 4
    omega = 1.0 / (10000.0 ** (jnp.arange(quarter, dtype=F32) / quarter))

    def emb(pos):
        a = pos[:, None] * omega[None, :]
        return jnp.concatenate([jnp.sin(a), jnp.cos(a)], axis=-1)

    return jnp.concatenate([emb(r), emb(col)], axis=-1)


def _cos_nsin_tables(n, period):
    t = jnp.arange(n, dtype=jnp.int32)[None, :]

    def cs(r):
        ang = ((r * t) % period).astype(F32) * (2.0 * math.pi / period)
        return jnp.cos(ang), jnp.sin(ang)

    ca, sa = cs(jnp.arange(n // TABLE_SPLIT, dtype=jnp.int32)[:, None] * TABLE_SPLIT)
    cb, sb = cs(jnp.arange(TABLE_SPLIT, dtype=jnp.int32)[:, None])
    ca, sa, cb, sb = lax.optimization_barrier((ca, sa, cb, sb))
    ca, sa = ca[:, None, :], sa[:, None, :]
    cos = (ca * cb[None] - sa * sb[None]).reshape(n, n)
    nsin = (-(sa * cb[None] + ca * sb[None])).reshape(n, n)
    return cos.astype(BF16), nsin.astype(BF16)


def _hyena_positions(L):
    bands = (HY_EMB - 1) // 2
    t = jnp.linspace(0.0, 1.0, L, dtype=F32)[:, None]
    wpos = (2.0 * math.pi / L) * jnp.arange(L, dtype=F32)[:, None]
    fr = jnp.linspace(1e-4, bands - 1, bands, dtype=F32)[None, :]
    zpos = jnp.concatenate([t, jnp.cos(fr * wpos), -jnp.sin(fr * wpos)], axis=-1)
    zpos = jnp.pad(zpos, ((0, 0), (0, LANE - HY_EMB)))
    deltas = jnp.abs(jnp.linspace(math.log(HY_DECAY_TARGET) / HY_SLOW_PCT,
                                  math.log(HY_DECAY_TARGET) / HY_FAST_PCT, W_B, dtype=F32))
    window = jnp.exp(-t * deltas[None, :])
    return zpos, window


def _group_tables():
    r = jnp.arange(DC, dtype=jnp.int32)
    ang = ((r[:, None] * r[None, :]) % DC).astype(F32) * (2.0 * math.pi / DC)
    eye = jnp.eye(G_C, dtype=F32)
    return jnp.kron(eye, jnp.cos(ang)).astype(BF16), jnp.kron(eye, jnp.sin(ang)).astype(BF16)


def _head_tables():
    ones_bd = jnp.kron(jnp.eye(H_A, dtype=F32), jnp.ones((DK, DK), F32)).astype(BF16)
    lane = jnp.arange(LANE)[:, None]
    head = (jnp.arange(H_A * DK) // DK)[None, :]
    sel = []
    for d in range(2):
        sel.append(jnp.stack([(lane == d * H_A + head), (lane == 2 * H_A + d * H_A + head)]))
    esel = jnp.stack(sel).astype(BF16)
    return ones_bd, esel


def kernel(x_prompt, x_sample, state_delta, c, c_ctx, w_mod, b_mod, norm1_g, norm2_g, w_in, conv_qkv, a_log, dt_bias, norm_a, conv_hy, hy_w1, hy_b1, hy_freq, hy_w2, hy_b2, hy_w3, hy_bias, w_pa, w_pb, w_pc, w_o, w_gu, w_down, norm_f):
    assert x_prompt.shape == (N_CTX_SEQ, L_CTX, D_MODEL) and x_sample.shape == (N_LAT_SEQ, L_LAT, D_MODEL)
    st = jnp.pad(jnp.concatenate([c_ctx[None], c], axis=0).T, ((0, 0), (0, 8 - 1 - N_LAT_SEQ)))
    mod = _mod_call(st, w_mod, b_mod).reshape(DEPTH, 8, 6, D_MODEL)

    ones_bd, esel = _head_tables()
    bdc, bds = _group_tables()
    seqs = ((L_CTX, N_CTX_SEQ, 0), (L_LAT, N_LAT_SEQ, T_CTX // L_LAT))
    tables = {L: (_cos_nsin_tables(L, 2 * L), _cos_nsin_tables(L, L), _hyena_positions(L))
              for L, _, _ in seqs}

    w_in_b = jnp.concatenate([w_in[:, :, 0:OFF_HY], jnp.zeros((DEPTH, D_MODEL, LANE - 4 * H_A), F32),
                              w_in[:, :, OFF_HY:]], axis=2).astype(BF16)
    w_pa_b, w_pb_b, w_pc_b, w_o_b, w_gu_b, w_down_b = (
        w.astype(BF16) for w in (w_pa, w_pb, w_pc, w_o, w_gu, w_down))

    x = None
    ctx_states = []
    for l in range(DEPTH):
        mod3 = mod[l, 0:3]
        if l == 0:
            xs = (x_prompt.reshape(T_CTX, D_MODEL), x_sample.reshape(T_LAT, D_MODEL), _grid_pos_embed(L_LAT))
            pd, ph, pf, pg, x = _inproj_call(xs, mod3, norm1_g[l], w_in_b, l)
        else:
            pd, ph, pf, pg = _inproj_call((x,), mod3, norm1_g[l], w_in_b, l)

        par = jnp.zeros((8, LANE), F32)
        par = par.at[0, 2 * H_A:4 * H_A].set(a_log[l].reshape(-1))
        par = par.at[1, 2 * H_A:4 * H_A].set(dt_bias[l].reshape(-1))
        *o_ctx, s_ctx = _delta_call(pd, conv_qkv[l], par, esel, ones_bd, None,
                                    N_CTX_SEQ, L_CTX // DELTA_RB, 0)
        s0 = state_delta[:, l].astype(F32).reshape(N_LAT_SEQ, 2, H_A * DK, DV)
        *o_lat, _ = _delta_call(pd, conv_qkv[l], par, esel, ones_bd, s0,
                                N_LAT_SEQ, L_LAT // DELTA_RB, T_CTX // DELTA_RB)
        ctx_states.append(s_ctx.reshape(N_CTX_SEQ, 2, H_A, DK, DV))

        v, x12 = _hyprep_call(ph, conv_hy[l])
        w1p = jnp.pad(hy_w1[l], ((0, LANE - HY_EMB), (0, 0)))
        yb, yc = [], []
        for L, n_seq, blk0 in seqs:
            (cos2, nsin2), (cos1, nsin1), (zpos, window) = tables[L]
            kspec = _filter_call(L, zpos, w1p, hy_b1[l][None], hy_freq[l][None], hy_w2[l],
                                 hy_b2[l][None], hy_w3[l], window, cos2, nsin2)
            bias = hy_bias[l][:, None, :]
            if L == HYENA_FT:
                assert blk0 == 0
                yb.append(_hyena_short_call(v, x12, kspec, bias, cos2, nsin2, n_seq, L))
            else:
                yb.append(_hyena_call(v, x12, kspec, bias, cos2, nsin2, n_seq, L, blk0, HYENA_FT))
            yc.append(_fnet_call(pf, cos1, nsin1, bdc, bds, n_seq, L, blk0))

        na512 = jnp.tile(norm_a[l], H_A)[None]
        x = _postmix_call(o_ctx, o_lat, pd, yb, yc, pg, x, mod3, na512, ones_bd,
                          w_pa_b, w_pb_b, w_pc_b, w_o_b, norm2_g[l], w_gu_b, w_down_b, norm_f,
                          l, l == DEPTH - 1)

    y_prompt = x[:T_CTX].reshape(N_CTX_SEQ, L_CTX, D_MODEL)
    y_sample = x[T_CTX:].reshape(N_LAT_SEQ, L_LAT, D_MODEL)
    new_state = jnp.stack(ctx_states, axis=1).astype(x_prompt.dtype)
    return (y_prompt, y_sample, new_state)
```

```python
import functools
import math

import jax
import jax.numpy as jnp
from jax import lax
from jax.experimental import pallas as pl
from jax.experimental.pallas import tpu as pltpu

F32 = jnp.float32
BF16 = jnp.bfloat16

D_MODEL = 1024
N_CTX_SEQ = 32
L_CTX = 256
DEPTH = 2
N_LAT_SEQ = 2
L_LAT = 2048
GRID_W = 64
EPS = 1e-6
H_A = 8
DK = 64
DV = 64
W_A = H_A * DV
QKV_W = 2 * H_A * DK + H_A * DV
CHUNK = 64
W_B = 512
HY_EMB = 33
HY_HID = 64
HY_DECAY_TARGET = 1e-2
HY_FAST_PCT = 0.3
HY_SLOW_PCT = 1.5
G_C = 8
DC = 64
W_C = G_C * DC
D_FF = ((8 * D_MODEL + 3 * 256 - 1) // (3 * 256)) * 256
OFF_Z = QKV_W
OFF_B = OFF_Z + W_A
OFF_A = OFF_B + 2 * H_A
OFF_HY = OFF_A + 2 * H_A
OFF_FN = OFF_HY + 3 * W_B
OFF_GATE = OFF_FN + W_C

T_CTX = N_CTX_SEQ * L_CTX
T_LAT = N_LAT_SEQ * L_LAT
T_ALL = T_CTX + T_LAT
ROW_TILE = 256
N_CTX_TILES = T_CTX // ROW_TILE
N_LAT_TILES = T_LAT // ROW_TILE
LANE = 128
PD_W = QKV_W + W_A + LANE
HEAD_GROUP = 4
GROUP_W = HEAD_GROUP * DK
N_GROUPS = H_A // HEAD_GROUP
VMEM_LIMIT = 56 * 1024 * 1024


def _cparams(sem, vmem=None):
    return pltpu.CompilerParams(dimension_semantics=sem, vmem_limit_bytes=vmem)


def _dot(a, b):
    return jnp.dot(a, b, preferred_element_type=F32)


def _dot_nt(a, b):
    return lax.dot_general(a, b, (((1,), (1,)), ((), ())), preferred_element_type=F32)


def _dot_tn(a, b):
    return lax.dot_general(a, b, (((0,), (0,)), ((), ())), preferred_element_type=F32)


def _split(a, n):
    parts = []
    rem = a
    for i in range(n):
        p = rem.astype(BF16)
        parts.append(p)
        if i + 1 < n:
            rem = rem - p.astype(F32)
    return parts


def _mm3(a, b):
    ah, al = _split(a, 2)
    bh, bl = _split(b, 2)
    return _dot(ah, bh) + (_dot(ah, bl) + _dot(al, bh))


def _mm_const_rhs(a, c, n):
    out = None
    for p in _split(a, n):
        t = _dot(p, c)
        out = t if out is None else out + t
    return out


def _mm_const_lhs(c, b, n):
    out = None
    for p in _split(b, n):
        t = _dot(c, p)
        out = t if out is None else out + t
    return out


def _sigmoid(x):
    return 1.0 / (1.0 + jnp.exp(-x))


def _silu(x):
    return x * _sigmoid(x)


def _softplus(x):
    return jnp.maximum(x, 0.0) + jnp.log(1.0 + jnp.exp(-jnp.abs(x)))


def _mod_row_block(i):
    per_lat = L_LAT // ROW_TILE
    return jnp.where(i < N_CTX_TILES, 0, 1 + (i - N_CTX_TILES) // per_lat)


def _ctx_tile(i):
    return jnp.minimum(i, N_CTX_TILES - 1)


def _lat_tile(i):
    return jnp.maximum(i - N_CTX_TILES, 0)


def _ctx_lat_specs(width, tile0=0):
    return (pl.BlockSpec((ROW_TILE, width), lambda i: (_ctx_tile(i + tile0), 0)),
            pl.BlockSpec((ROW_TILE, width), lambda i: (_lat_tile(i + tile0), 0)))


MOD_TN = 512


def _mod_kernel(st_ref, w_ref, b_ref, out_ref):
    s = _silu(st_ref[...])
    w = w_ref[0]
    rows = [jnp.sum(s[:, r:r + 1] * w, axis=0, keepdims=True) + b_ref[0] for r in range(3)]
    rows.append(jnp.zeros((5, MOD_TN), F32))
    out_ref[0] = jnp.concatenate(rows, axis=0)


def _mod_call(st, w_mod, b_mod):
    n6 = 6 * D_MODEL
    return pl.pallas_call(
        _mod_kernel,
        grid=(DEPTH, n6 // MOD_TN),
        in_specs=[
            pl.BlockSpec((D_MODEL, 8), lambda l, j: (0, 0)),
            pl.BlockSpec((1, D_MODEL, MOD_TN), lambda l, j: (l, 0, j)),
            pl.BlockSpec((1, 1, MOD_TN), lambda l, j: (l, 0, j)),
        ],
        out_specs=pl.BlockSpec((1, 8, MOD_TN), lambda l, j: (l, 0, j)),
        out_shape=jax.ShapeDtypeStruct((DEPTH, 8, n6), F32),
        compiler_params=_cparams(("parallel", "parallel")),
        name="adaln_mod",
    )(st, w_mod, b_mod.reshape(DEPTH, 1, n6))


PACK_ROWS = 128


def _pack_w_in_kernel(w_ref, out_ref):
    w = w_ref[0]
    pad = jnp.zeros((PACK_ROWS, PD_W - OFF_HY), F32)
    out_ref[0] = jnp.concatenate([w[:, 0:OFF_HY], pad, w[:, OFF_HY:]], axis=1).astype(BF16)


def _pack_w_in_call(w_in):
    d_in = w_in.shape[2]
    d_out = d_in + PD_W - OFF_HY
    return pl.pallas_call(
        _pack_w_in_kernel,
        grid=(DEPTH, D_MODEL // PACK_ROWS),
        in_specs=[pl.BlockSpec((1, PACK_ROWS, d_in), lambda l, r: (l, r, 0))],
        out_specs=pl.BlockSpec((1, PACK_ROWS, d_out), lambda l, r: (l, r, 0)),
        out_shape=jax.ShapeDtypeStruct((DEPTH, D_MODEL, d_out), BF16),
        compiler_params=_cparams(("parallel", "parallel")),
        name="pack_w_in",
    )(w_in)


INPROJ_TN = 512
INPROJ_WIDTHS = (PD_W, 3 * W_B, W_C, 3 * D_MODEL)


def _rms_mod(x, g, scale, shift):
    ms = jnp.mean(x * x, axis=-1, keepdims=True)
    return (x * lax.rsqrt(ms + EPS) * g) * (1.0 + scale) + shift


def _inproj_kernel(first, *refs):
    if first:
        (xc_ref, xl_ref, pos_ref, mod_ref, g_ref, w_ref, pd_ref, ph_ref, pf_ref, pg_ref, x_ref) = refs
        x = jnp.where(pl.program_id(0) < N_CTX_TILES, xc_ref[...], xl_ref[...] + pos_ref[...])
        x_ref[...] = x
    else:
        (xin_ref, mod_ref, g_ref, w_ref, pd_ref, ph_ref, pf_ref, pg_ref) = refs
        x = xin_ref[...]
    m = mod_ref[0]
    h = _rms_mod(x, g_ref[...], m[1:2], m[0:1]).astype(BF16)
    col0 = 0
    for o_ref in (pd_ref, ph_ref, pf_ref, pg_ref):
        n = o_ref.shape[1]
        for c in range(0, n, INPROJ_TN):
            e = min(c + INPROJ_TN, n)
            y = _dot(h, w_ref[0, :, col0 + c:col0 + e])
            if o_ref is pg_ref:
                y = _sigmoid(y)
            o_ref[:, c:e] = y.astype(o_ref.dtype)
        col0 += n


def _resident(shape, index_map):
    return pl.BlockSpec(shape, index_map, pipeline_mode=pl.Buffered(1))


def _inproj_call(xs, mod3, g, w_all, layer):
    first = len(xs) == 3
    widths = INPROJ_WIDTHS
    row = lambda i: (i, 0)
    const = lambda i: (0, 0)
    if first:
        per_lat = L_LAT // ROW_TILE
        x_specs = list(_ctx_lat_specs(D_MODEL)) + [
            pl.BlockSpec((ROW_TILE, D_MODEL), lambda i: (_lat_tile(i) % per_lat, 0))]
    else:
        x_specs = [pl.BlockSpec((ROW_TILE, D_MODEL), row)]
    out_widths = widths + ((D_MODEL,) if first else ())
    out_dtypes = (F32, F32, F32, BF16) + ((F32,) if first else ())
    return pl.pallas_call(
        functools.partial(_inproj_kernel, first),
        grid=(T_ALL // ROW_TILE,),
        in_specs=x_specs + [
            pl.BlockSpec((1, 6, D_MODEL), lambda i: (_mod_row_block(i), 0, 0)),
            pl.BlockSpec((1, D_MODEL), const),
            _resident((1, D_MODEL, sum(widths)), lambda i: (layer, 0, 0)),
        ],
        out_specs=[pl.BlockSpec((ROW_TILE, w), row) for w in out_widths],
        out_shape=[jax.ShapeDtypeStruct((T_ALL, w), dt) for w, dt in zip(out_widths, out_dtypes)],
        compiler_params=_cparams(("parallel",), VMEM_LIMIT),
        name="inproj_first" if first else "inproj",
    )(*xs, mod3, g.reshape(1, D_MODEL), w_all)


def _conv3_rows(cur, prev_row, next_row, w):
    n = cur.shape[0]
    ridx = lax.broadcasted_iota(jnp.int32, cur.shape, 0)
    up = jnp.where(ridx == 0, prev_row, pltpu.roll(cur, 1, 0))
    dn = jnp.where(ridx == n - 1, next_row, pltpu.roll(cur, n - 1, 0))
    return up * w[0:1] + cur * w[1:2] + dn * w[2:3]


def _halo_specs(width, rows_per_block, n_row_blocks, blk_of):
    per = rows_per_block // 8
    last = n_row_blocks * per - 1
    prev = pl.BlockSpec((8, width), lambda *a: (jnp.maximum(blk_of(*a) * per - 1, 0), 0))
    nxt = pl.BlockSpec((8, width), lambda *a: (jnp.minimum((blk_of(*a) + 1) * per, last), 0))
    return prev, nxt


FILT_RT = 256


def _alternating_sum(x):
    t = lax.broadcasted_iota(jnp.int32, x.shape, 0)
    return jnp.sum(jnp.where(t % 2 == 0, x, -x), axis=0, keepdims=True)


def _filter_kernel(L, zpos_ref, w1_ref, b1_ref, fq_ref, w2_ref, b2_ref, w3_ref, win_ref,
                   cos_ref, nsin_ref, k_ref, hs_s, hm_s, krl_s):
    rt = pl.program_id(1)

    @pl.when(rt == 0)
    def _():
        fq = fq_ref[...]
        alt_acc = jnp.zeros((1, W_B), F32)
        for r0 in range(0, L, FILT_RT):
            rows = slice(r0, r0 + FILT_RT)
            h = jnp.sin(fq * (_mm3(zpos_ref[rows, :], w1_ref[...]) + b1_ref[...]))
            h = jnp.sin(fq * (_mm3(h, w2_ref[...]) + b2_ref[...]))
            hf = _mm3(h, w3_ref[...])
            win = win_ref[rows, :]
            fw = hf[:, 0:W_B] * win
            bw = hf[:, W_B:2 * W_B] * win
            hsum = fw + bw
            hs_s[rows, :] = hsum.astype(BF16)
            hm_s[rows, :] = (fw - bw).astype(BF16)
            alt_acc = alt_acc + _alternating_sum(hsum)
        krl_s[...] = jnp.broadcast_to(alt_acc, krl_s.shape)

    p1 = _dot(cos_ref[...], hs_s[...])
    p2 = _dot(nsin_ref[...], hm_s[...])
    first = (rt * FILT_RT + lax.broadcasted_iota(jnp.int32, p1.shape, 0)) == 0
    k_ref[0, 0] = p1
    k_ref[0, 1] = jnp.where(first, krl_s[0:1, :], p1)
    k_ref[0, 2] = jnp.where(first, 0.0, p2)


def _filter_call(L, zpos, w1p, b1, fq, w2, b2, w3, win, cos, nsin):
    nrt = L // FILT_RT
    c2 = lambda o, r: (0, 0)
    return pl.pallas_call(
        functools.partial(_filter_kernel, L),
        grid=(2, nrt),
        in_specs=[
            pl.BlockSpec((L, LANE), c2),
            pl.BlockSpec((LANE, HY_HID), c2),
            pl.BlockSpec((1, HY_HID), c2),
            pl.BlockSpec((1, HY_HID), c2),
            pl.BlockSpec((HY_HID, HY_HID), c2),
            pl.BlockSpec((1, HY_HID), c2),
            pl.BlockSpec((HY_HID, 2 * W_B), lambda o, r: (0, o)),
            pl.BlockSpec((L, W_B), c2),
            pl.BlockSpec((FILT_RT, L), lambda o, r: (r, 0)),
            pl.BlockSpec((FILT_RT, L), lambda o, r: (r, 0)),
        ],
        out_specs=pl.BlockSpec((1, 3, FILT_RT, W_B), lambda o, r: (o, 0, r, 0)),
        out_shape=jax.ShapeDtypeStruct((2, 3, L, W_B), F32),
        scratch_shapes=[pltpu.VMEM((L, W_B), BF16), pltpu.VMEM((L, W_B), BF16),
                        pltpu.VMEM((8, W_B), F32)],
        compiler_params=_cparams(("parallel", "arbitrary"), VMEM_LIMIT),
        name=f"hyena_filter_{L}",
    )(zpos, w1p, b1, fq, w2, b2, w3, win, cos, nsin)


HYENA_FT = 256

HYENA_CONV_ROWS = 256


def _hyena_kernel(L, ft, nft, ph_ref, cw_ref, k_ref, bias_ref, crow_ref, srow_ref, ccol_ref, scol_ref,
                  out_ref, gate_s, zf_s, zb_s, acc_s, nyq_s):
    o = pl.program_id(1)
    f = pl.program_id(2)

    @pl.when((o == 0) & (f == 0))
    def _():
        cw = cw_ref[...]
        for r0 in range(0, L, HYENA_CONV_ROWS):
            r1 = r0 + HYENA_CONV_ROWS
            prev_row = ph_ref[r0 - 1:r0, :] if r0 > 0 else 0.0
            next_row = ph_ref[r1:r1 + 1, :] if r1 < L else 0.0
            uc = _conv3_rows(ph_ref[r0:r1, :], prev_row, next_row, cw)
            gate_s[0, r0:r1, :] = uc[:, 0:W_B]
            gate_s[1, r0:r1, :] = uc[:, W_B:2 * W_B]
            v = uc[:, 2 * W_B:3 * W_B]
            zf_s[r0:r1, :] = v
            zb_s[r0:r1, :] = v.astype(BF16)

    @pl.when(f == 0)
    def _():
        nyq_s[...] = jnp.broadcast_to(_alternating_sum(zf_s[...]), nyq_s.shape)

    zb = zb_s[...]
    top = _dot(crow_ref[...], zb)
    first = (f * ft + lax.broadcasted_iota(jnp.int32, top.shape, 0)) == 0
    bot = jnp.where(first, nyq_s[0:1, :], _dot(srow_ref[...], zb))
    krt = k_ref[0, 0]
    krb = k_ref[0, 1]
    ki = k_ref[0, 2]
    yt = top * krt - bot * ki
    yb = top * ki + bot * krb
    part = (_dot(ccol_ref[...], jnp.where(first, 0.5 * yt, yt).astype(BF16))
            + _dot(scol_ref[...], yb.astype(BF16)))

    @pl.when(f == 0)
    def _():
        t = lax.broadcasted_iota(jnp.int32, part.shape, 0)
        acc_s[...] = part + jnp.where(t % 2 == 0, 0.5, -0.5) * yb[0:1, :]

    @pl.when(f != 0)
    def _():
        acc_s[...] += part

    @pl.when(f == nft - 1)
    def _():
        znew = gate_s[o] * (acc_s[...] * (1.0 / L) + bias_ref[0] * zf_s[...])
        zf_s[...] = znew
        zb_s[...] = znew.astype(BF16)
        out_ref[...] = znew.astype(out_ref.dtype)


def _hyena_call(ph, conv_w, kspec, bias, cos, nsin, n_seq, L, row_blk0, ft):
    nft = L // ft
    return pl.pallas_call(
        functools.partial(_hyena_kernel, L, ft, nft),
        grid=(n_seq, 2, nft),
        in_specs=[
            pl.BlockSpec((L, 3 * W_B), lambda s, o, f: (row_blk0 + s, 0), pipeline_mode=pl.Buffered(1)),
            pl.BlockSpec((3, 3 * W_B), lambda s, o, f: (0, 0)),
            pl.BlockSpec((1, 3, ft, W_B), lambda s, o, f: (o, 0, f, 0)),
            pl.BlockSpec((1, 1, W_B), lambda s, o, f: (o, 0, 0)),
            pl.BlockSpec((ft, L), lambda s, o, f: (f, 0)),
            pl.BlockSpec((ft, L), lambda s, o, f: (f, 0)),
            pl.BlockSpec((L, ft), lambda s, o, f: (0, f)),
            pl.BlockSpec((L, ft), lambda s, o, f: (0, f)),
        ],
        out_specs=pl.BlockSpec((L, W_B), lambda s, o, f: (s, 0)),
        out_shape=jax.ShapeDtypeStruct((n_seq * L, W_B), BF16),
        scratch_shapes=[pltpu.VMEM((2, L, W_B), F32), pltpu.VMEM((L, W_B), F32), pltpu.VMEM((L, W_B), BF16),
                        pltpu.VMEM((L, W_B), F32), pltpu.VMEM((8, W_B), F32)],
        compiler_params=_cparams(("parallel", "arbitrary", "arbitrary"), VMEM_LIMIT),
        name=f"hyena_conv_{L}",
    )(ph, conv_w, kspec, bias, cos, nsin, cos, nsin)


HYENA_SHORT_SB = 4


def _hyena_short_kernel(L, ph_ref, cw_ref, k_ref, bias_ref, cos_ref, nsin_ref, out_ref):
    cos = cos_ref[...]
    nsin = nsin_ref[...]
    cw = cw_ref[...]
    t = lax.broadcasted_iota(jnp.int32, (L, W_B), 0)
    first = t == 0
    alt_half = jnp.where(t % 2 == 0, 0.5, -0.5)
    rows = [slice(s * L, (s + 1) * L) for s in range(HYENA_SHORT_SB)]
    ucs = [_conv3_rows(ph_ref[r, :], 0.0, 0.0, cw) for r in rows]
    zs = [uc[:, 2 * W_B:3 * W_B] for uc in ucs]
    for o in range(2):
        krt, krb, ki = k_ref[o, 0], k_ref[o, 1], k_ref[o, 2]
        zbs = [z.astype(BF16) for z in zs]
        tops = [_dot(cos, zb) for zb in zbs]
        bots = [jnp.where(first, _alternating_sum(z), _dot(nsin, zb)) for z, zb in zip(zs, zbs)]
        yts = [top * krt - bot * ki for top, bot in zip(tops, bots)]
        ybs = [top * ki + bot * krb for top, bot in zip(tops, bots)]
        accs = [_dot(cos, jnp.where(first, 0.5 * yt, yt).astype(BF16)) + _dot(nsin, yb.astype(BF16))
                + alt_half * yb[0:1, :] for yt, yb in zip(yts, ybs)]
        zs = [uc[:, o * W_B:(o + 1) * W_B] * (acc * (1.0 / L) + bias_ref[o] * z)
              for uc, acc, z in zip(ucs, accs, zs)]
    for r, z in zip(rows, zs):
        out_ref[r, :] = z.astype(out_ref.dtype)


def _hyena_short_call(ph, conv_w, kspec, bias, cos, nsin, n_seq, L):
    rows = HYENA_SHORT_SB * L
    return pl.pallas_call(
        functools.partial(_hyena_short_kernel, L),
        grid=(n_seq // HYENA_SHORT_SB,),
        in_specs=[
            pl.BlockSpec((rows, 3 * W_B), lambda i: (i, 0)),
            pl.BlockSpec((3, 3 * W_B), lambda i: (0, 0)),
            pl.BlockSpec((2, 3, L, W_B), lambda i: (0, 0, 0, 0)),
            pl.BlockSpec((2, 1, W_B), lambda i: (0, 0, 0)),
            pl.BlockSpec((L, L), lambda i: (0, 0)),
            pl.BlockSpec((L, L), lambda i: (0, 0)),
        ],
        out_specs=pl.BlockSpec((rows, W_B), lambda i: (i, 0)),
        out_shape=jax.ShapeDtypeStruct((n_seq * L, W_B), BF16),
        compiler_params=_cparams(("parallel",), VMEM_LIMIT),
        name=f"hyena_conv_{L}",
    )(ph, conv_w, kspec, bias, cos, nsin)


FNET_RT = 256


def _fnet_kernel(L, x_ref, cos_ref, nsin_ref, bdc_ref, bds_ref, out_ref, xc_s, xs_s):
    r = pl.program_id(1)

    @pl.when(r == 0)
    def _():
        for r0 in range(0, L, FNET_RT):
            xb = x_ref[r0:r0 + FNET_RT, :].astype(BF16)
            xc_s[r0:r0 + FNET_RT, :] = _dot(xb, bdc_ref[...]).astype(BF16)
            xs_s[r0:r0 + FNET_RT, :] = _dot(xb, bds_ref[...]).astype(BF16)

    y = _dot(cos_ref[...], xc_s[...]) + _dot(nsin_ref[...], xs_s[...])
    out_ref[...] = (y * (1.0 / math.sqrt(DC * L))).astype(out_ref.dtype)


def _fnet_call(pf, cos, nsin, bdc, bds, n_seq, L, row_blk0):
    nrt = L // FNET_RT
    return pl.pallas_call(
        functools.partial(_fnet_kernel, L),
        grid=(n_seq, nrt),
        in_specs=[
            pl.BlockSpec((L, W_C), lambda s, r: (row_blk0 + s, 0)),
            pl.BlockSpec((FNET_RT, L), lambda s, r: (r, 0)),
            pl.BlockSpec((FNET_RT, L), lambda s, r: (r, 0)),
            pl.BlockSpec((W_C, W_C), lambda s, r: (0, 0)),
            pl.BlockSpec((W_C, W_C), lambda s, r: (0, 0)),
        ],
        out_specs=pl.BlockSpec((FNET_RT, W_C), lambda s, r: (s * nrt + r, 0)),
        out_shape=jax.ShapeDtypeStruct((n_seq * L, W_C), BF16),
        scratch_shapes=[pltpu.VMEM((L, W_C), BF16), pltpu.VMEM((L, W_C), BF16)],
        compiler_params=_cparams(("parallel", "arbitrary"), VMEM_LIMIT),
        name=f"fnet_{L}",
    )(pf, cos, nsin, bdc, bds)


DELTA_RB = 256
CHUNKS_PER_RB = DELTA_RB // CHUNK


HEADS_PER_LANE_TILE = LANE // DK


def _block_diag(y, half_masks):
    yb = y.astype(BF16)
    zero = jnp.zeros((CHUNK, LANE), BF16)
    row_blocks = []
    for h in range(HEAD_GROUP):
        tile = h // HEADS_PER_LANE_TILE
        piece = yb[:, tile * LANE:(tile + 1) * LANE] * half_masks[h % HEADS_PER_LANE_TILE]
        row_blocks.append(jnp.concatenate(
            [piece if t == tile else zero for t in range(GROUP_W // LANE)], axis=1))
    return jnp.concatenate(row_blocks, axis=0)


def _stacked_const_rhs(arrs, c, n):
    m = arrs[0].shape[0]
    parts = [p for a in arrs for p in _split(a, n)]
    y = _dot(jnp.concatenate(parts, axis=0), c)
    outs = []
    for i in range(len(arrs)):
        acc = y[i * n * m:(i * n + 1) * m]
        for t in range(1, n):
            acc = acc + y[(i * n + t) * m:(i * n + t + 1) * m]
        outs.append(acc)
    return outs


def _const_lhs_split(c, b, n):
    w = b.shape[1]
    y = _dot(c, jnp.concatenate(_split(b, n), axis=1))
    acc = y[:, 0:w]
    for t in range(1, n):
        acc = acc + y[:, t * w:(t + 1) * w]
    return acc


def _delta_block_prep(pd_ref, prev_ref, next_ref, has_prev, has_next, dirs, cw, a_neg, dtb, ones_bd,
                      esel_ref, tri, half_masks):
    chunks = []
    for r in range(CHUNKS_PER_RB):
        rows = slice(r * CHUNK, (r + 1) * CHUNK)
        cur = pd_ref[rows, 0:QKV_W]
        if r == 0:
            prev_row = jnp.where(has_prev, prev_ref[7:8, :], 0.0)
        else:
            prev_row = pd_ref[r * CHUNK - 1:r * CHUNK, 0:QKV_W]
        if r == CHUNKS_PER_RB - 1:
            next_row = jnp.where(has_next, next_ref[0:1, :], 0.0)
        else:
            next_row = pd_ref[(r + 1) * CHUNK:(r + 1) * CHUNK + 1, 0:QKV_W]
        qkv = _silu(_conv3_rows(cur, prev_row, next_row, cw))
        q = qkv[:, 0:H_A * DK]
        k = qkv[:, H_A * DK:2 * H_A * DK]
        qss, kss = _stacked_const_rhs([q * q, k * k], ones_bd, 2)
        qn = q * lax.rsqrt(qss + EPS) * (DK ** -0.5)
        kn = k * lax.rsqrt(kss + EPS)
        gram = []
        for g in range(N_GROUPS):
            lanes = slice(g * GROUP_W, (g + 1) * GROUP_W)
            lhs = jnp.concatenate([kn[:, lanes], qn[:, lanes]], axis=0).astype(BF16)
            gram.append(_dot_nt(lhs, _block_diag(kn[:, lanes], half_masks)))
        ba = pd_ref[rows, OFF_B:OFF_B + LANE]
        sig = _sigmoid(ba)
        glog = a_neg * _softplus(ba + dtb)
        decay = {}
        for d in dirs:
            gcum = _const_lhs_split(tri[d], glog, 3)
            (beta,) = _stacked_const_rhs([sig], esel_ref[d, 0], 3)
            gexp, gcc8 = _stacked_const_rhs([glog, gcum], esel_ref[d, 1], 3)
            decay[d] = (beta, gexp, gcc8)
        chunks.append(dict(qn=qn, kn=kn, v=qkv[:, 2 * H_A * DK:], gram=gram, decay=decay))
    return chunks


def _delta_kernel(nb, zero_init, *refs):
    shared = nb == 1
    it = iter(refs)
    blocks = [(next(it), next(it), next(it))]
    if not shared:
        blocks.append((next(it), next(it), next(it)))
    cw_ref, par_ref, esel_ref, ones_ref = next(it), next(it), next(it), next(it)
    s0_ref = None if zero_init else next(it)
    o_refs = (next(it), next(it))
    sfin_ref = next(it)
    u_s, w_s, p_s, qg_s, kg_s, gl_s, st_s = (next(it) for _ in range(7))
    j = pl.program_id(1)

    ri = lax.broadcasted_iota(jnp.int32, (CHUNK, GROUP_W), 0)
    cj = lax.broadcasted_iota(jnp.int32, (CHUNK, GROUP_W), 1) % CHUNK
    ixj = ri ^ cj
    eye = ixj == 0
    br = lax.broadcasted_iota(jnp.int32, (GROUP_W, GROUP_W), 0) // CHUNK
    bc = lax.broadcasted_iota(jnp.int32, (GROUP_W, GROUP_W), 1) // CHUNK
    same_head = br == bc
    hl = lax.broadcasted_iota(jnp.int32, (CHUNK, LANE), 1) // DK
    half_masks = tuple(jnp.where(hl == h, 1.0, 0.0).astype(BF16) for h in range(HEADS_PER_LANE_TILE))
    ti = lax.broadcasted_iota(jnp.int32, (CHUNK, CHUNK), 0)
    tm = lax.broadcasted_iota(jnp.int32, (CHUNK, CHUNK), 1)
    ri8 = lax.broadcasted_iota(jnp.int32, (CHUNK, H_A * DK), 0)
    cj8 = lax.broadcasted_iota(jnp.int32, (CHUNK, H_A * DK), 1) % CHUNK
    incl = (ri >= cj, ri <= cj)
    strict = (ri > cj, ri < cj)
    tri = tuple(jnp.where(m, 1.0, 0.0).astype(BF16) for m in (tm <= ti, tm >= ti))
    tt = (ri8 <= cj8, ri8 >= cj8)
    last_row = (CHUNK - 1, 0)

    @pl.when(j == 0)
    def _():
        for d in range(2):
            for g in range(N_GROUPS):
                if zero_init:
                    st_s[d, g] = jnp.zeros((GROUP_W, GROUP_W), F32)
                else:
                    nat = s0_ref[0, d, g * GROUP_W:(g + 1) * GROUP_W, :]
                    st_s[d, g] = jnp.where(same_head, jnp.concatenate([nat] * HEAD_GROUP, axis=1), 0.0)

    cw = cw_ref[...]
    a_neg = -jnp.exp(par_ref[0:1, :])
    dtb = par_ref[1:2, :]
    pos = (j, nb - 1 - j)
    prepped = [_delta_block_prep(*blocks[b], pos[b] > 0, pos[b] < nb - 1, (0, 1) if shared else (b,),
                                 cw, a_neg, dtb, ones_ref[...], esel_ref, tri, half_masks)
               for b in range(len(blocks))]
    units = []
    for d in range(2):
        for r, ch in enumerate(prepped[0 if shared else d]):
            rows = slice(r * CHUNK, (r + 1) * CHUNK)
            beta, gexp, gcc8 = ch["decay"][d]
            gcr8 = jnp.sum(jnp.where(tt[d], gexp, 0.0), axis=0, keepdims=True)
            for g in range(N_GROUPS):
                lanes = slice(g * GROUP_W, (g + 1) * GROUP_W)
                qn, kn, be, gcc = ch["qn"][:, lanes], ch["kn"][:, lanes], beta[:, lanes], gcc8[:, lanes]
                kq = ch["gram"][g]
                dec = jnp.exp(jnp.where(incl[d], gcc - gcr8[:, lanes], -1e30))
                a = jnp.where(strict[d], kq[0:CHUNK] * be * dec, 0.0)
                eg = jnp.exp(gcc)
                gcl = gcc[last_row[d]:last_row[d] + 1, :]
                units.append((d, rows, lanes, a, ch["v"][:, lanes] * be, kn * be * eg))
                p_s[d, rows, lanes] = kq[CHUNK:2 * CHUNK] * dec
                qg_s[d, rows, lanes] = qn * eg
                kg_s[d, rows, lanes] = kn * jnp.exp(gcl - gcc)
                gl_s[d, r * 8:(r + 1) * 8, lanes] = jnp.broadcast_to(jnp.exp(gcl), (8, GROUP_W))

    xs = [jnp.where(eye, 1.0, 0.0) - jnp.where(ixj == 1, un[3], 0.0) for un in units]
    for lvl in range(1, 6):
        t1s = [_dot(x.astype(BF16), _block_diag(jnp.where((ixj >> lvl) == 1, un[3], 0.0), half_masks))
               for x, un in zip(xs, units)]
        xs = [x - _dot(t1.astype(BF16), _block_diag(x, half_masks)) for x, t1 in zip(xs, t1s)]
    for x, (d, rows, lanes, _, vb, kbe) in zip(xs, units):
        rhs = jnp.concatenate([_block_diag(vb, half_masks), _block_diag(kbe, half_masks)], axis=1)
        uw = _dot(x.astype(BF16), rhs)
        u_s[d, rows, lanes] = uw[:, 0:GROUP_W]
        w_s[d, rows, lanes] = uw[:, GROUP_W:2 * GROUP_W]

    for c in range(CHUNKS_PER_RB):
        for d in range(2):
            r = c if d == 0 else CHUNKS_PER_RB - 1 - c
            rows = slice(r * CHUNK, (r + 1) * CHUNK)
            for g in range(N_GROUPS):
                lanes = slice(g * GROUP_W, (g + 1) * GROUP_W)
                s = st_s[d, g]
                wq = jnp.concatenate([w_s[d, rows, lanes], qg_s[d, rows, lanes]], axis=0)
                ws_qs = _dot(wq.astype(BF16), s.astype(BF16))
                v_new = u_s[d, rows, lanes] - ws_qs[0:CHUNK]
                o = ws_qs[CHUNK:2 * CHUNK] + _dot(p_s[d, rows, lanes].astype(BF16),
                                                  _block_diag(v_new, half_masks))
                upd = _dot_tn(kg_s[d, rows, lanes].astype(BF16), v_new.astype(BF16))
                st_s[d, g] = s * gl_s[d, r * 8:r * 8 + 1, lanes] + jnp.where(same_head, upd, 0.0)
                o_refs[d][rows, lanes] = o

    @pl.when(j == nb - 1)
    def _():
        for d in range(2):
            for g in range(N_GROUPS):
                s = st_s[d, g]
                nat = s[:, 0:DV]
                for hh in range(1, HEAD_GROUP):
                    nat = nat + s[:, hh * DV:(hh + 1) * DV]
                sfin_ref[0, d, g * GROUP_W:(g + 1) * GROUP_W, :] = nat


def _delta_call(pd, conv_w, par, esel, ones_bd, s0, n_seq, nb, blk0):
    zero_init = s0 is None
    n_blocks_all = T_ALL // DELTA_RB
    blk_of = (lambda s, j: blk0 + s * nb + j, lambda s, j: blk0 + s * nb + nb - 1 - j)
    in_specs, args = [], []
    for d in range(1 if nb == 1 else 2):
        prev, nxt = _halo_specs(QKV_W, DELTA_RB, n_blocks_all, blk_of[d])
        in_specs += [pl.BlockSpec((DELTA_RB, PD_W), lambda s, j, d=d: (blk_of[d](s, j), 0)), prev, nxt]
        args += [pd, pd, pd]
    in_specs += [
        pl.BlockSpec((3, QKV_W), lambda s, j: (0, 0)),
        pl.BlockSpec((8, LANE), lambda s, j: (0, 0)),
        pl.BlockSpec((2, 2, LANE, H_A * DK), lambda s, j: (0, 0, 0, 0)),
        pl.BlockSpec((H_A * DK, H_A * DK), lambda s, j: (0, 0)),
    ]
    args += [conv_w, par, esel, ones_bd]
    if not zero_init:
        in_specs.append(pl.BlockSpec((1, 2, H_A * DK, DV), lambda s, j: (s, 0, 0, 0)))
        args.append(s0)
    rows = n_seq * nb * DELTA_RB
    dir_buf = lambda n: pltpu.VMEM((2, n, H_A * DK), F32)
    return pl.pallas_call(
        functools.partial(_delta_kernel, nb, zero_init),
        grid=(n_seq, nb),
        in_specs=in_specs,
        out_specs=[pl.BlockSpec((DELTA_RB, W_A), lambda s, j: (s * nb + j, 0)),
                   pl.BlockSpec((DELTA_RB, W_A), lambda s, j: (s * nb + nb - 1 - j, 0)),
                   pl.BlockSpec((1, 2, H_A * DK, DV), lambda s, j: (s, 0, 0, 0))],
        out_shape=[jax.ShapeDtypeStruct((rows, W_A), F32),
                   jax.ShapeDtypeStruct((rows, W_A), F32),
                   jax.ShapeDtypeStruct((n_seq, 2, H_A * DK, DV), F32)],
        scratch_shapes=[dir_buf(DELTA_RB) for _ in range(5)]
        + [dir_buf(CHUNKS_PER_RB * 8), pltpu.VMEM((2, N_GROUPS, GROUP_W, GROUP_W), F32)],
        compiler_params=_cparams(("parallel", "arbitrary"), VMEM_LIMIT),
        name=f"deltanet_nb{nb}",
    )(*args)


FFN_TN = D_FF // 2


def _postmix_kernel(final_norm, tile0, ofc_ref, ofl_ref, obc_ref, obl_ref, z_ref, ybc_ref, ybl_ref,
                    ycc_ref, ycl_ref, pg_ref, x_ref, mod_ref, na_ref, ones_ref, wpa_ref, wpb_ref, wpc_ref,
                    wo_ref, g2_ref, wgu_ref, wdn_ref, nf_ref, out_ref):
    is_ctx = pl.program_id(0) + tile0 < N_CTX_TILES
    m = mod_ref[0]
    o = jnp.where(is_ctx, ofc_ref[...] + obc_ref[...], ofl_ref[...] + obl_ref[...])
    yb = jnp.where(is_ctx, ybc_ref[...], ybl_ref[...])
    yc = jnp.where(is_ctx, ycc_ref[...], ycl_ref[...])
    ms = _mm_const_rhs(o * o, ones_ref[...], 2) * (1.0 / DV)
    ya = (o * lax.rsqrt(ms + EPS) * na_ref[...]) * _silu(z_ref[...])
    merged = (pg_ref[:, 0:D_MODEL].astype(F32) * _dot(ya.astype(BF16), wpa_ref[0])
              + pg_ref[:, D_MODEL:2 * D_MODEL].astype(F32) * _dot(yb.astype(BF16), wpb_ref[0])
              + pg_ref[:, 2 * D_MODEL:3 * D_MODEL].astype(F32) * _dot(yc.astype(BF16), wpc_ref[0]))
    x = x_ref[...] + m[2:3] * _dot(merged.astype(BF16), wo_ref[0])

    h = _rms_mod(x, g2_ref[...], m[4:5], m[3:4]).astype(BF16)
    acc = None
    for c in range(0, D_FF, FFN_TN):
        gate = _dot(h, wgu_ref[0, :, c:c + FFN_TN])
        up = _dot(h, wgu_ref[0, :, D_FF + c:D_FF + c + FFN_TN])
        part = _dot((_silu(gate) * up).astype(BF16), wdn_ref[0, c:c + FFN_TN, :])
        acc = part if acc is None else acc + part
    xn = x + m[5:6] * acc
    if final_norm:
        ms = jnp.mean(xn * xn, axis=-1, keepdims=True)
        xn = xn * lax.rsqrt(ms + EPS) * nf_ref[...]
    out_ref[...] = xn


def _postmix_call(o_ctx, o_lat, pd, yb, yc, pg, x, mod3, na512, ones_bd, wpa, wpb, wpc, wo,
                  g2, wgu, wdn, nf, layer, final_norm, tile0=0, n_tiles=T_ALL // ROW_TILE):
    row = lambda i: (i + tile0, 0)
    const = lambda i: (0, 0)
    lyr = lambda i: (layer, 0, 0)
    assert W_A == W_B == W_C
    return pl.pallas_call(
        functools.partial(_postmix_kernel, final_norm, tile0),
        grid=(n_tiles,),
        in_specs=[
            *_ctx_lat_specs(W_A, tile0), *_ctx_lat_specs(W_A, tile0),
            pl.BlockSpec((ROW_TILE, W_A), lambda i: (i + tile0, OFF_Z // W_A)),
            *_ctx_lat_specs(W_B, tile0), *_ctx_lat_specs(W_C, tile0),
            pl.BlockSpec((ROW_TILE, 3 * D_MODEL), row),
            pl.BlockSpec((ROW_TILE, D_MODEL), row),
            pl.BlockSpec((1, 6, D_MODEL), lambda i: (_mod_row_block(i + tile0), 0, 0)),
            pl.BlockSpec((1, W_A), const),
            _resident((W_A, W_A), const),
            _resident((1, W_A, D_MODEL), lyr),
            _resident((1, W_B, D_MODEL), lyr),
            _resident((1, W_C, D_MODEL), lyr),
            _resident((1, D_MODEL, D_MODEL), lyr),
            pl.BlockSpec((1, D_MODEL), const),
            _resident((1, D_MODEL, 2 * D_FF), lyr),
            _resident((1, D_FF, D_MODEL), lyr),
            pl.BlockSpec((1, D_MODEL), const),
        ],
        out_specs=pl.BlockSpec((ROW_TILE, D_MODEL), lambda i: (i, 0)),
        out_shape=jax.ShapeDtypeStruct((n_tiles * ROW_TILE, D_MODEL), F32),
        compiler_params=_cparams(("parallel",), VMEM_LIMIT),
        name="postmix_final" if final_norm else "postmix",
    )(o_ctx[0], o_lat[0], o_ctx[1], o_lat[1], pd, yb[0], yb[1], yc[0], yc[1], pg, x, mod3, na512,
      ones_bd, wpa, wpb, wpc, wo, g2.reshape(1, D_MODEL), wgu, wdn, nf.reshape(1, D_MODEL))


TABLE_SPLIT = 64


def _grid_pos_embed(n_tokens):
    rows = n_tokens // GRID_W
    r = jnp.repeat(jnp.arange(rows), GRID_W).astype(F32)
    col = jnp.tile(jnp.arange(GRID_W), rows).astype(F32)
    quarter = D_MODEL // 4
    omega = 1.0 / (10000.0 ** (jnp.arange(quarter, dtype=F32) / quarter))

    def emb(pos):
        a = pos[:, None] * omega[None, :]
        return jnp.concatenate([jnp.sin(a), jnp.cos(a)], axis=-1)

    return jnp.concatenate([emb(r), emb(col)], axis=-1)


def _cos_nsin_tables(n, period):
    t = jnp.arange(n, dtype=jnp.int32)[None, :]

    def cs(r):
        ang = ((r * t) % period).astype(F32) * (2.0 * math.pi / period)
        return jnp.cos(ang), jnp.sin(ang)

    ca, sa = cs(jnp.arange(n // TABLE_SPLIT, dtype=jnp.int32)[:, None] * TABLE_SPLIT)
    cb, sb = cs(jnp.arange(TABLE_SPLIT, dtype=jnp.int32)[:, None])
    ca, sa, cb, sb = lax.optimization_barrier((ca, sa, cb, sb))
    ca, sa = ca[:, None, :], sa[:, None, :]
    cos = (ca * cb[None] - sa * sb[None]).reshape(n, n)
    nsin = (-(sa * cb[None] + ca * sb[None])).reshape(n, n)
    return cos.astype(BF16), nsin.astype(BF16)


def _hyena_positions(L):
    bands = (HY_EMB - 1) // 2
    t = jnp.linspace(0.0, 1.0, L, dtype=F32)[:, None]
    wpos = (2.0 * math.pi / L) * jnp.arange(L, dtype=F32)[:, None]
    fr = jnp.linspace(1e-4, bands - 1, bands, dtype=F32)[None, :]
    zpos = jnp.concatenate([t, jnp.cos(fr * wpos), -jnp.sin(fr * wpos)], axis=-1)
    zpos = jnp.pad(zpos, ((0, 0), (0, LANE - HY_EMB)))
    deltas = jnp.abs(jnp.linspace(math.log(HY_DECAY_TARGET) / HY_SLOW_PCT,
                                  math.log(HY_DECAY_TARGET) / HY_FAST_PCT, W_B, dtype=F32))
    window = jnp.exp(-t * deltas[None, :])
    return zpos, window


def _group_tables():
    r = jnp.arange(DC, dtype=jnp.int32)
    ang = ((r[:, None] * r[None, :]) % DC).astype(F32) * (2.0 * math.pi / DC)
    eye = jnp.eye(G_C, dtype=F32)
    return jnp.kron(eye, jnp.cos(ang)).astype(BF16), jnp.kron(eye, jnp.sin(ang)).astype(BF16)


def _head_tables():
    ones_bd = jnp.kron(jnp.eye(H_A, dtype=F32), jnp.ones((DK, DK), F32)).astype(BF16)
    lane = jnp.arange(LANE)[:, None]
    head = (jnp.arange(H_A * DK) // DK)[None, :]
    sel = []
    for d in range(2):
        sel.append(jnp.stack([(lane == d * H_A + head), (lane == 2 * H_A + d * H_A + head)]))
    esel = jnp.stack(sel).astype(BF16)
    return ones_bd, esel


def kernel(x_prompt, x_sample, state_delta, c, c_ctx, w_mod, b_mod, norm1_g, norm2_g, w_in, conv_qkv, a_log, dt_bias, norm_a, conv_hy, hy_w1, hy_b1, hy_freq, hy_w2, hy_b2, hy_w3, hy_bias, w_pa, w_pb, w_pc, w_o, w_gu, w_down, norm_f):
    assert x_prompt.shape == (N_CTX_SEQ, L_CTX, D_MODEL) and x_sample.shape == (N_LAT_SEQ, L_LAT, D_MODEL)
    st = jnp.pad(jnp.concatenate([c_ctx[None], c], axis=0).T, ((0, 0), (0, 8 - 1 - N_LAT_SEQ)))
    mod = _mod_call(st, w_mod, b_mod).reshape(DEPTH, 8, 6, D_MODEL)

    ones_bd, esel = _head_tables()
    bdc, bds = _group_tables()
    seqs = ((L_CTX, N_CTX_SEQ, 0), (L_LAT, N_LAT_SEQ, T_CTX // L_LAT))
    tables = {L: (_cos_nsin_tables(L, 2 * L), _cos_nsin_tables(L, L), _hyena_positions(L))
              for L, _, _ in seqs}

    w_in_b = _pack_w_in_call(w_in)
    w_pa_b, w_pb_b, w_pc_b, w_o_b, w_gu_b, w_down_b = (
        w.astype(BF16) for w in (w_pa, w_pb, w_pc, w_o, w_gu, w_down))

    x = None
    ctx_states = []
    for l in range(DEPTH):
        mod3 = mod[l, 0:3]
        if l == 0:
            xs = (x_prompt.reshape(T_CTX, D_MODEL), x_sample.reshape(T_LAT, D_MODEL), _grid_pos_embed(L_LAT))
            pd, ph, pf, pg, x = _inproj_call(xs, mod3, norm1_g[l], w_in_b, l)
        else:
            pd, ph, pf, pg = _inproj_call((x,), mod3, norm1_g[l], w_in_b, l)

        par = jnp.zeros((8, LANE), F32)
        par = par.at[0, 2 * H_A:4 * H_A].set(a_log[l].reshape(-1))
        par = par.at[1, 2 * H_A:4 * H_A].set(dt_bias[l].reshape(-1))
        *o_ctx, s_ctx = _delta_call(pd, conv_qkv[l], par, esel, ones_bd, None,
                                    N_CTX_SEQ, L_CTX // DELTA_RB, 0)
        s0 = state_delta[:, l].astype(F32).reshape(N_LAT_SEQ, 2, H_A * DK, DV)
        *o_lat, _ = _delta_call(pd, conv_qkv[l], par, esel, ones_bd, s0,
                                N_LAT_SEQ, L_LAT // DELTA_RB, T_CTX // DELTA_RB)
        ctx_states.append(s_ctx.reshape(N_CTX_SEQ, 2, H_A, DK, DV))

        w1p = jnp.pad(hy_w1[l], ((0, LANE - HY_EMB), (0, 0)))
        yb, yc = [], []
        for L, n_seq, blk0 in seqs:
            (cos2, nsin2), (cos1, nsin1), (zpos, window) = tables[L]
            kspec = _filter_call(L, zpos, w1p, hy_b1[l][None], hy_freq[l][None], hy_w2[l],
                                 hy_b2[l][None], hy_w3[l], window, cos2, nsin2)
            bias = hy_bias[l][:, None, :]
            if L == HYENA_FT:
                assert blk0 == 0
                yb.append(_hyena_short_call(ph, conv_hy[l], kspec, bias, cos2, nsin2, n_seq, L))
            else:
                yb.append(_hyena_call(ph, conv_hy[l], kspec, bias, cos2, nsin2, n_seq, L, blk0, HYENA_FT))
            yc.append(_fnet_call(pf, cos1, nsin1, bdc, bds, n_seq, L, blk0))

        na512 = jnp.tile(norm_a[l], H_A)[None]
        post = functools.partial(_postmix_call, o_ctx, o_lat, pd, yb, yc, pg, x, mod3, na512, ones_bd,
                                 w_pa_b, w_pb_b, w_pc_b, w_o_b, norm2_g[l], w_gu_b, w_down_b, norm_f, l)
        if l < DEPTH - 1:
            x = post(False)
        else:
            y_prompt = post(True, 0, N_CTX_TILES).reshape(N_CTX_SEQ, L_CTX, D_MODEL)
            y_sample = post(True, N_CTX_TILES, N_LAT_TILES).reshape(N_LAT_SEQ, L_LAT, D_MODEL)

    new_state = jnp.stack(ctx_states, axis=1).astype(x_prompt.dtype)
    return (y_prompt, y_sample, new_state)
```

```python
import functools
import math

import jax
import jax.numpy as jnp
from jax import lax
from jax.experimental import pallas as pl
from jax.experimental.pallas import tpu as pltpu

F32 = jnp.float32
BF16 = jnp.bfloat16

D_MODEL = 1024
N_CTX_SEQ = 32
L_CTX = 256
DEPTH = 2
N_LAT_SEQ = 2
L_LAT = 2048
GRID_W = 64
EPS = 1e-6
H_A = 8
DK = 64
DV = 64
W_A = H_A * DV
QKV_W = 2 * H_A * DK + H_A * DV
CHUNK = 64
W_B = 512
HY_EMB = 33
HY_HID = 64
HY_DECAY_TARGET = 1e-2
HY_FAST_PCT = 0.3
HY_SLOW_PCT = 1.5
G_C = 8
DC = 64
W_C = G_C * DC
D_FF = ((8 * D_MODEL + 3 * 256 - 1) // (3 * 256)) * 256
OFF_Z = QKV_W
OFF_B = OFF_Z + W_A
OFF_A = OFF_B + 2 * H_A
OFF_HY = OFF_A + 2 * H_A
OFF_FN = OFF_HY + 3 * W_B
OFF_GATE = OFF_FN + W_C

T_CTX = N_CTX_SEQ * L_CTX
T_LAT = N_LAT_SEQ * L_LAT
T_ALL = T_CTX + T_LAT
ROW_TILE = 256
N_CTX_TILES = T_CTX // ROW_TILE
N_LAT_TILES = T_LAT // ROW_TILE
LANE = 128
PD_W = QKV_W + W_A + LANE
HEAD_GROUP = 4
GROUP_W = HEAD_GROUP * DK
N_GROUPS = H_A // HEAD_GROUP
VMEM_LIMIT = 56 * 1024 * 1024


def _cparams(sem, vmem=None):
    return pltpu.CompilerParams(dimension_semantics=sem, vmem_limit_bytes=vmem)


def _dot(a, b):
    return jnp.dot(a, b, preferred_element_type=F32)


def _dot_nt(a, b):
    return lax.dot_general(a, b, (((1,), (1,)), ((), ())), preferred_element_type=F32)


def _dot_tn(a, b):
    return lax.dot_general(a, b, (((0,), (0,)), ((), ())), preferred_element_type=F32)


def _split(a, n):
    parts = []
    rem = a
    for i in range(n):
        p = rem.astype(BF16)
        parts.append(p)
        if i + 1 < n:
            rem = rem - p.astype(F32)
    return parts


def _mm3(a, b):
    ah, al = _split(a, 2)
    bh, bl = _split(b, 2)
    return _dot(ah, bh) + (_dot(ah, bl) + _dot(al, bh))


def _sigmoid(x):
    return 1.0 / (1.0 + jnp.exp(-x))


def _silu(x):
    return x * _sigmoid(x)


def _softplus(x):
    return jnp.maximum(x, 0.0) + jnp.log(1.0 + jnp.exp(-jnp.abs(x)))


def _mod_row_block(i):
    per_lat = L_LAT // ROW_TILE
    return jnp.where(i < N_CTX_TILES, 0, 1 + (i - N_CTX_TILES) // per_lat)


def _ctx_tile(i):
    return jnp.minimum(i, N_CTX_TILES - 1)


def _lat_tile(i):
    return jnp.maximum(i - N_CTX_TILES, 0)


def _ctx_lat_specs(width, tile0=0):
    return (pl.BlockSpec((ROW_TILE, width), lambda i: (_ctx_tile(i + tile0), 0)),
            pl.BlockSpec((ROW_TILE, width), lambda i: (_lat_tile(i + tile0), 0)))


MOD_TN = 512


def _mod_kernel(st_ref, w_ref, b_ref, out_ref):
    s = _silu(st_ref[...])
    w = w_ref[0]
    rows = [jnp.sum(s[:, r:r + 1] * w, axis=0, keepdims=True) + b_ref[0] for r in range(3)]
    rows.append(jnp.zeros((5, MOD_TN), F32))
    out_ref[0] = jnp.concatenate(rows, axis=0)


def _mod_call(st, w_mod, b_mod):
    n6 = 6 * D_MODEL
    return pl.pallas_call(
        _mod_kernel,
        grid=(DEPTH, n6 // MOD_TN),
        in_specs=[
            pl.BlockSpec((D_MODEL, 8), lambda l, j: (0, 0)),
            pl.BlockSpec((1, D_MODEL, MOD_TN), lambda l, j: (l, 0, j)),
            pl.BlockSpec((1, 1, MOD_TN), lambda l, j: (l, 0, j)),
        ],
        out_specs=pl.BlockSpec((1, 8, MOD_TN), lambda l, j: (l, 0, j)),
        out_shape=jax.ShapeDtypeStruct((DEPTH, 8, n6), F32),
        compiler_params=_cparams(("parallel", "parallel")),
        name="adaln_mod",
    )(st, w_mod, b_mod.reshape(DEPTH, 1, n6))


INPROJ_TN = 512
INPROJ_WIDTHS = (PD_W, 3 * W_B, W_C, 3 * D_MODEL)
INPROJ_FEATURES = (OFF_HY, 3 * W_B, W_C, 3 * D_MODEL)


def _rms_mod(x, g, scale, shift):
    ms = jnp.mean(x * x, axis=-1, keepdims=True)
    return (x * lax.rsqrt(ms + EPS) * g) * (1.0 + scale) + shift


def _inproj_kernel(first, *refs):
    if first:
        (xc_ref, xl_ref, pos_ref, mod_ref, g_ref, w_ref, pd_ref, ph_ref, pf_ref, pg_ref, x_ref) = refs
        x = jnp.where(pl.program_id(0) < N_CTX_TILES, xc_ref[...], xl_ref[...] + pos_ref[...])
        x_ref[...] = x
    else:
        (xin_ref, mod_ref, g_ref, w_ref, pd_ref, ph_ref, pf_ref, pg_ref) = refs
        x = xin_ref[...]
    m = mod_ref[0]
    h = _rms_mod(x, g_ref[...], m[1:2], m[0:1]).astype(BF16)
    row0 = 0
    for o_ref, n_feat in zip((pd_ref, ph_ref, pf_ref, pg_ref), INPROJ_FEATURES):
        n = o_ref.shape[1]
        for c in range(0, n, INPROJ_TN):
            e = min(c + INPROJ_TN, n)
            ef = min(e, n_feat)
            y = _dot_nt(h, w_ref[0, row0 + c:row0 + ef, :])
            if ef < e:
                y = jnp.concatenate([y, jnp.zeros((y.shape[0], e - ef), F32)], axis=1)
            if o_ref is pg_ref:
                y = _sigmoid(y)
            o_ref[:, c:e] = y.astype(o_ref.dtype)
        row0 += n_feat


def _resident(shape, index_map):
    return pl.BlockSpec(shape, index_map, pipeline_mode=pl.Buffered(1))


def _inproj_call(xs, mod3, g, w_all, layer):
    first = len(xs) == 3
    widths = INPROJ_WIDTHS
    row = lambda i: (i, 0)
    const = lambda i: (0, 0)
    if first:
        per_lat = L_LAT // ROW_TILE
        x_specs = list(_ctx_lat_specs(D_MODEL)) + [
            pl.BlockSpec((ROW_TILE, D_MODEL), lambda i: (_lat_tile(i) % per_lat, 0))]
    else:
        x_specs = [pl.BlockSpec((ROW_TILE, D_MODEL), row)]
    out_widths = widths + ((D_MODEL,) if first else ())
    out_dtypes = (F32, F32, F32, BF16) + ((F32,) if first else ())
    return pl.pallas_call(
        functools.partial(_inproj_kernel, first),
        grid=(T_ALL // ROW_TILE,),
        in_specs=x_specs + [
            pl.BlockSpec((1, 6, D_MODEL), lambda i: (_mod_row_block(i), 0, 0)),
            pl.BlockSpec((1, D_MODEL), const),
            _resident((1, sum(INPROJ_FEATURES), D_MODEL), lambda i: (layer, 0, 0)),
        ],
        out_specs=[pl.BlockSpec((ROW_TILE, w), row) for w in out_widths],
        out_shape=[jax.ShapeDtypeStruct((T_ALL, w), dt) for w, dt in zip(out_widths, out_dtypes)],
        compiler_params=_cparams(("parallel",), VMEM_LIMIT),
        name="inproj_first" if first else "inproj",
    )(*xs, mod3, g.reshape(1, D_MODEL), w_all)


def _conv3_rows(cur, prev_row, next_row, w):
    n = cur.shape[0]
    ridx = lax.broadcasted_iota(jnp.int32, cur.shape, 0)
    up = jnp.where(ridx == 0, prev_row, pltpu.roll(cur, 1, 0))
    dn = jnp.where(ridx == n - 1, next_row, pltpu.roll(cur, n - 1, 0))
    return up * w[0:1] + cur * w[1:2] + dn * w[2:3]


def _halo_specs(width, rows_per_block, n_row_blocks, blk_of):
    per = rows_per_block // 8
    last = n_row_blocks * per - 1
    prev = pl.BlockSpec((8, width), lambda *a: (jnp.maximum(blk_of(*a) * per - 1, 0), 0))
    nxt = pl.BlockSpec((8, width), lambda *a: (jnp.minimum((blk_of(*a) + 1) * per, last), 0))
    return prev, nxt


FILT_RT = 256


def _alternating_sum(x):
    t = lax.broadcasted_iota(jnp.int32, x.shape, 0)
    return jnp.sum(jnp.where(t % 2 == 0, x, -x), axis=0, keepdims=True)


def _filter_kernel(L, zpos_ref, w1_ref, b1_ref, fq_ref, w2_ref, b2_ref, w3_ref, win_ref,
                   cos_ref, nsin_ref, k_ref, hs_s, hm_s, krl_s):
    rt = pl.program_id(1)

    @pl.when(rt == 0)
    def _():
        fq = fq_ref[...]
        alt_acc = jnp.zeros((1, W_B), F32)
        for r0 in range(0, L, FILT_RT):
            rows = slice(r0, r0 + FILT_RT)
            h = jnp.sin(fq * (_mm3(zpos_ref[rows, :], w1_ref[...]) + b1_ref[...]))
            h = jnp.sin(fq * (_mm3(h, w2_ref[...]) + b2_ref[...]))
            hf = _mm3(h, w3_ref[...])
            win = win_ref[rows, :]
            fw = hf[:, 0:W_B] * win
            bw = hf[:, W_B:2 * W_B] * win
            hsum = fw + bw
            hs_s[rows, :] = hsum.astype(BF16)
            hm_s[rows, :] = (fw - bw).astype(BF16)
            alt_acc = alt_acc + _alternating_sum(hsum)
        krl_s[...] = jnp.broadcast_to(alt_acc, krl_s.shape)

    p1 = _dot(cos_ref[...], hs_s[...])
    p2 = _dot(nsin_ref[...], hm_s[...])
    first = (rt * FILT_RT + lax.broadcasted_iota(jnp.int32, p1.shape, 0)) == 0
    k_ref[0, 0] = p1
    k_ref[0, 1] = jnp.where(first, krl_s[0:1, :], p1)
    k_ref[0, 2] = jnp.where(first, 0.0, p2)


def _filter_call(L, zpos, w1p, b1, fq, w2, b2, w3, win, cos, nsin):
    nrt = L // FILT_RT
    c2 = lambda o, r: (0, 0)
    return pl.pallas_call(
        functools.partial(_filter_kernel, L),
        grid=(2, nrt),
        in_specs=[
            pl.BlockSpec((L, LANE), c2),
            pl.BlockSpec((LANE, HY_HID), c2),
            pl.BlockSpec((1, HY_HID), c2),
            pl.BlockSpec((1, HY_HID), c2),
            pl.BlockSpec((HY_HID, HY_HID), c2),
            pl.BlockSpec((1, HY_HID), c2),
            pl.BlockSpec((HY_HID, 2 * W_B), lambda o, r: (0, o)),
            pl.BlockSpec((L, W_B), c2),
            pl.BlockSpec((FILT_RT, L), lambda o, r: (r, 0)),
            pl.BlockSpec((FILT_RT, L), lambda o, r: (r, 0)),
        ],
        out_specs=pl.BlockSpec((1, 3, FILT_RT, W_B), lambda o, r: (o, 0, r, 0)),
        out_shape=jax.ShapeDtypeStruct((2, 3, L, W_B), F32),
        scratch_shapes=[pltpu.VMEM((L, W_B), BF16), pltpu.VMEM((L, W_B), BF16),
                        pltpu.VMEM((8, W_B), F32)],
        compiler_params=_cparams(("parallel", "arbitrary"), VMEM_LIMIT),
        name=f"hyena_filter_{L}",
    )(zpos, w1p, b1, fq, w2, b2, w3, win, cos, nsin)


HYENA_FT = 256

HYENA_CONV_ROWS = 256


def _hyena_kernel(L, ft, nft, ph_ref, cw_ref, k_ref, bias_ref, crow_ref, srow_ref, ccol_ref, scol_ref,
                  out_ref, gate_s, zf_s, zb_s, acc_s, nyq_s):
    o = pl.program_id(1)
    f = pl.program_id(2)

    @pl.when((o == 0) & (f == 0))
    def _():
        cw = cw_ref[...]
        for r0 in range(0, L, HYENA_CONV_ROWS):
            r1 = r0 + HYENA_CONV_ROWS
            prev_row = ph_ref[r0 - 1:r0, :] if r0 > 0 else 0.0
            next_row = ph_ref[r1:r1 + 1, :] if r1 < L else 0.0
            uc = _conv3_rows(ph_ref[r0:r1, :], prev_row, next_row, cw)
            gate_s[0, r0:r1, :] = uc[:, 0:W_B]
            gate_s[1, r0:r1, :] = uc[:, W_B:2 * W_B]
            v = uc[:, 2 * W_B:3 * W_B]
            zf_s[r0:r1, :] = v
            zb_s[r0:r1, :] = v.astype(BF16)

    @pl.when(f == 0)
    def _():
        nyq_s[...] = jnp.broadcast_to(_alternating_sum(zf_s[...]), nyq_s.shape)

    zb = zb_s[...]
    top = _dot(crow_ref[...], zb)
    first = (f * ft + lax.broadcasted_iota(jnp.int32, top.shape, 0)) == 0
    bot = jnp.where(first, nyq_s[0:1, :], _dot(srow_ref[...], zb))
    krt = k_ref[0, 0]
    krb = k_ref[0, 1]
    ki = k_ref[0, 2]
    yt = top * krt - bot * ki
    yb = top * ki + bot * krb
    part = (_dot(ccol_ref[...], jnp.where(first, 0.5 * yt, yt).astype(BF16))
            + _dot(scol_ref[...], yb.astype(BF16)))

    @pl.when(f == 0)
    def _():
        t = lax.broadcasted_iota(jnp.int32, part.shape, 0)
        acc_s[...] = part + jnp.where(t % 2 == 0, 0.5, -0.5) * yb[0:1, :]

    @pl.when(f != 0)
    def _():
        acc_s[...] += part

    @pl.when(f == nft - 1)
    def _():
        znew = gate_s[o] * (acc_s[...] * (1.0 / L) + bias_ref[0] * zf_s[...])
        zf_s[...] = znew
        zb_s[...] = znew.astype(BF16)
        out_ref[...] = znew.astype(out_ref.dtype)


def _hyena_call(ph, conv_w, kspec, bias, cos, nsin, n_seq, L, row_blk0, ft):
    nft = L // ft
    return pl.pallas_call(
        functools.partial(_hyena_kernel, L, ft, nft),
        grid=(n_seq, 2, nft),
        in_specs=[
            pl.BlockSpec((L, 3 * W_B), lambda s, o, f: (row_blk0 + s, 0), pipeline_mode=pl.Buffered(1)),
            pl.BlockSpec((3, 3 * W_B), lambda s, o, f: (0, 0)),
            pl.BlockSpec((1, 3, ft, W_B), lambda s, o, f: (o, 0, f, 0)),
            pl.BlockSpec((1, 1, W_B), lambda s, o, f: (o, 0, 0)),
            pl.BlockSpec((ft, L), lambda s, o, f: (f, 0)),
            pl.BlockSpec((ft, L), lambda s, o, f: (f, 0)),
            pl.BlockSpec((L, ft), lambda s, o, f: (0, f)),
            pl.BlockSpec((L, ft), lambda s, o, f: (0, f)),
        ],
        out_specs=pl.BlockSpec((L, W_B), lambda s, o, f: (s, 0)),
        out_shape=jax.ShapeDtypeStruct((n_seq * L, W_B), BF16),
        scratch_shapes=[pltpu.VMEM((2, L, W_B), F32), pltpu.VMEM((L, W_B), F32), pltpu.VMEM((L, W_B), BF16),
                        pltpu.VMEM((L, W_B), F32), pltpu.VMEM((8, W_B), F32)],
        compiler_params=_cparams(("parallel", "arbitrary", "arbitrary"), VMEM_LIMIT),
        name=f"hyena_conv_{L}",
    )(ph, conv_w, kspec, bias, cos, nsin, cos, nsin)


HYENA_SHORT_SB = 4


def _hyena_short_kernel(L, ph_ref, cw_ref, k_ref, bias_ref, cos_ref, nsin_ref, out_ref):
    cos = cos_ref[...]
    nsin = nsin_ref[...]
    cw = cw_ref[...]
    t = lax.broadcasted_iota(jnp.int32, (L, W_B), 0)
    first = t == 0
    alt_half = jnp.where(t % 2 == 0, 0.5, -0.5)
    rows = [slice(s * L, (s + 1) * L) for s in range(HYENA_SHORT_SB)]
    ucs = [_conv3_rows(ph_ref[r, :], 0.0, 0.0, cw) for r in rows]
    zs = [uc[:, 2 * W_B:3 * W_B] for uc in ucs]
    for o in range(2):
        krt, krb, ki = k_ref[o, 0], k_ref[o, 1], k_ref[o, 2]
        zbs = [z.astype(BF16) for z in zs]
        tops = [_dot(cos, zb) for zb in zbs]
        bots = [jnp.where(first, _alternating_sum(z), _dot(nsin, zb)) for z, zb in zip(zs, zbs)]
        yts = [top * krt - bot * ki for top, bot in zip(tops, bots)]
        ybs = [top * ki + bot * krb for top, bot in zip(tops, bots)]
        accs = [_dot(cos, jnp.where(first, 0.5 * yt, yt).astype(BF16)) + _dot(nsin, yb.astype(BF16))
                + alt_half * yb[0:1, :] for yt, yb in zip(yts, ybs)]
        zs = [uc[:, o * W_B:(o + 1) * W_B] * (acc * (1.0 / L) + bias_ref[o] * z)
              for uc, acc, z in zip(ucs, accs, zs)]
    for r, z in zip(rows, zs):
        out_ref[r, :] = z.astype(out_ref.dtype)


def _hyena_short_call(ph, conv_w, kspec, bias, cos, nsin, n_seq, L):
    rows = HYENA_SHORT_SB * L
    return pl.pallas_call(
        functools.partial(_hyena_short_kernel, L),
        grid=(n_seq // HYENA_SHORT_SB,),
        in_specs=[
            pl.BlockSpec((rows, 3 * W_B), lambda i: (i, 0)),
            pl.BlockSpec((3, 3 * W_B), lambda i: (0, 0)),
            pl.BlockSpec((2, 3, L, W_B), lambda i: (0, 0, 0, 0)),
            pl.BlockSpec((2, 1, W_B), lambda i: (0, 0, 0)),
            pl.BlockSpec((L, L), lambda i: (0, 0)),
            pl.BlockSpec((L, L), lambda i: (0, 0)),
        ],
        out_specs=pl.BlockSpec((rows, W_B), lambda i: (i, 0)),
        out_shape=jax.ShapeDtypeStruct((n_seq * L, W_B), BF16),
        compiler_params=_cparams(("parallel",), VMEM_LIMIT),
        name=f"hyena_conv_{L}",
    )(ph, conv_w, kspec, bias, cos, nsin)


FNET_RT = 256


def _fnet_kernel(L, x_ref, cos_ref, nsin_ref, bdc_ref, bds_ref, out_ref, xc_s, xs_s):
    r = pl.program_id(1)

    @pl.when(r == 0)
    def _():
        for r0 in range(0, L, FNET_RT):
            xb = x_ref[r0:r0 + FNET_RT, :].astype(BF16)
            xc_s[r0:r0 + FNET_RT, :] = _dot(xb, bdc_ref[...]).astype(BF16)
            xs_s[r0:r0 + FNET_RT, :] = _dot(xb, bds_ref[...]).astype(BF16)

    y = _dot(cos_ref[...], xc_s[...]) + _dot(nsin_ref[...], xs_s[...])
    out_ref[...] = (y * (1.0 / math.sqrt(DC * L))).astype(out_ref.dtype)


def _fnet_call(pf, cos, nsin, bdc, bds, n_seq, L, row_blk0):
    nrt = L // FNET_RT
    return pl.pallas_call(
        functools.partial(_fnet_kernel, L),
        grid=(n_seq, nrt),
        in_specs=[
            pl.BlockSpec((L, W_C), lambda s, r: (row_blk0 + s, 0)),
            pl.BlockSpec((FNET_RT, L), lambda s, r: (r, 0)),
            pl.BlockSpec((FNET_RT, L), lambda s, r: (r, 0)),
            pl.BlockSpec((W_C, W_C), lambda s, r: (0, 0)),
            pl.BlockSpec((W_C, W_C), lambda s, r: (0, 0)),
        ],
        out_specs=pl.BlockSpec((FNET_RT, W_C), lambda s, r: (s * nrt + r, 0)),
        out_shape=jax.ShapeDtypeStruct((n_seq * L, W_C), BF16),
        scratch_shapes=[pltpu.VMEM((L, W_C), BF16), pltpu.VMEM((L, W_C), BF16)],
        compiler_params=_cparams(("parallel", "arbitrary"), VMEM_LIMIT),
        name=f"fnet_{L}",
    )(pf, cos, nsin, bdc, bds)


FNET_SHORT_SB = 4


def _fnet_short_kernel(L, x_ref, cos_ref, nsin_ref, bdc_ref, bds_ref, out_ref):
    xb = x_ref[...].astype(BF16)
    xc = _dot(xb, bdc_ref[...]).astype(BF16)
    xs = _dot(xb, bds_ref[...]).astype(BF16)
    for s in range(FNET_SHORT_SB):
        rows = slice(s * L, (s + 1) * L)
        y = _dot(cos_ref[...], xc[rows]) + _dot(nsin_ref[...], xs[rows])
        out_ref[rows, :] = (y * (1.0 / math.sqrt(DC * L))).astype(out_ref.dtype)


def _fnet_short_call(pf, cos, nsin, bdc, bds, n_seq, L):
    rows = FNET_SHORT_SB * L
    const = lambda i: (0, 0)
    return pl.pallas_call(
        functools.partial(_fnet_short_kernel, L),
        grid=(n_seq // FNET_SHORT_SB,),
        in_specs=[pl.BlockSpec((rows, W_C), lambda i: (i, 0)),
                  pl.BlockSpec((L, L), const), pl.BlockSpec((L, L), const),
                  pl.BlockSpec((W_C, W_C), const), pl.BlockSpec((W_C, W_C), const)],
        out_specs=pl.BlockSpec((rows, W_C), lambda i: (i, 0)),
        out_shape=jax.ShapeDtypeStruct((n_seq * L, W_C), BF16),
        compiler_params=_cparams(("parallel",), VMEM_LIMIT),
        name=f"fnet_{L}",
    )(pf, cos, nsin, bdc, bds)


DELTA_RB = 256
CHUNKS_PER_RB = DELTA_RB // CHUNK


HEADS_PER_LANE_TILE = LANE // DK


def _block_diag(y, half_masks):
    yb = y.astype(BF16)
    zero = jnp.zeros((CHUNK, LANE), BF16)
    row_blocks = []
    for h in range(HEAD_GROUP):
        tile = h // HEADS_PER_LANE_TILE
        piece = yb[:, tile * LANE:(tile + 1) * LANE] * half_masks[h % HEADS_PER_LANE_TILE]
        row_blocks.append(jnp.concatenate(
            [piece if t == tile else zero for t in range(GROUP_W // LANE)], axis=1))
    return jnp.concatenate(row_blocks, axis=0)


def _stacked_const_rhs(arrs, c, n):
    m = arrs[0].shape[0]
    parts = [p for a in arrs for p in _split(a, n)]
    y = _dot(jnp.concatenate(parts, axis=0), c)
    outs = []
    for i in range(len(arrs)):
        acc = y[i * n * m:(i * n + 1) * m]
        for t in range(1, n):
            acc = acc + y[(i * n + t) * m:(i * n + t + 1) * m]
        outs.append(acc)
    return outs


def _head_sums(arrs, ones_group):
    groups = [_stacked_const_rhs([a[:, g * GROUP_W:(g + 1) * GROUP_W] for a in arrs], ones_group, 2)
              for g in range(N_GROUPS)]
    return [jnp.concatenate([groups[g][i] for g in range(N_GROUPS)], axis=1) for i in range(len(arrs))]


def _const_lhs_split(c, b, n):
    w = b.shape[1]
    y = _dot(c, jnp.concatenate(_split(b, n), axis=1))
    acc = y[:, 0:w]
    for t in range(1, n):
        acc = acc + y[:, t * w:(t + 1) * w]
    return acc


def _delta_chunk_stages(pd_ref, prev_ref, next_ref, r, has_prev, has_next, dirs, cw, a_neg, dtb, ones_bd,
                        esel_ref, tri, half_masks):
    ch = {}
    rows = slice(r * CHUNK, (r + 1) * CHUNK)

    def conv():
        cur = pd_ref[rows, 0:QKV_W]
        if r == 0:
            prev_row = jnp.where(has_prev, prev_ref[7:8, :], 0.0)
        else:
            prev_row = pd_ref[r * CHUNK - 1:r * CHUNK, 0:QKV_W]
        if r == CHUNKS_PER_RB - 1:
            next_row = jnp.where(has_next, next_ref[0:1, :], 0.0)
        else:
            next_row = pd_ref[(r + 1) * CHUNK:(r + 1) * CHUNK + 1, 0:QKV_W]
        qkv = _silu(_conv3_rows(cur, prev_row, next_row, cw))
        ch["q"] = qkv[:, 0:H_A * DK]
        ch["k"] = qkv[:, H_A * DK:2 * H_A * DK]
        ch["v"] = qkv[:, 2 * H_A * DK:]

    def norms():
        q, k = ch.pop("q"), ch.pop("k")
        qss, kss = _head_sums([q * q, k * k], ones_bd)
        ch["qn"] = q * lax.rsqrt(qss + EPS) * (DK ** -0.5)
        ch["kn"] = k * lax.rsqrt(kss + EPS)

    def gram():
        ch["gram"] = []
        for g in range(N_GROUPS):
            lanes = slice(g * GROUP_W, (g + 1) * GROUP_W)
            lhs = jnp.concatenate([ch["kn"][:, lanes], ch["qn"][:, lanes]], axis=0).astype(BF16)
            ch["gram"].append(_dot_nt(lhs, _block_diag(ch["kn"][:, lanes], half_masks)))

    def decay():
        ba = pd_ref[rows, OFF_B:OFF_B + LANE]
        sig = _sigmoid(ba)
        glog = a_neg * _softplus(ba + dtb)
        ch["decay"] = {}
        for d in dirs:
            gcum = _const_lhs_split(tri[d], glog, 3)
            (beta,) = _stacked_const_rhs([sig], esel_ref[d, 0], 3)
            (gcc8,) = _stacked_const_rhs([gcum], esel_ref[d, 1], 3)
            ch["decay"][d] = (beta, gcc8)

    return ch, [conv, norms, gram, decay]


def _delta_kernel(nb, zero_init, *refs):
    shared = nb == 1
    it = iter(refs)
    blocks = [(next(it), next(it), next(it))]
    if not shared:
        blocks.append((next(it), next(it), next(it)))
    cw_ref, par_ref, esel_ref, ones_ref = next(it), next(it), next(it), next(it)
    s0_ref = None if zero_init else next(it)
    o_refs = (next(it), next(it))
    sfin_ref = next(it)
    u_s, w_s, p_s, qg_s, kg_s, gl_s, st_s = (next(it) for _ in range(7))
    j = pl.program_id(1)

    ri = lax.broadcasted_iota(jnp.int32, (CHUNK, GROUP_W), 0)
    cj = lax.broadcasted_iota(jnp.int32, (CHUNK, GROUP_W), 1) % CHUNK
    ixj = ri ^ cj
    eye = ixj == 0
    br = lax.broadcasted_iota(jnp.int32, (GROUP_W, GROUP_W), 0) // CHUNK
    bc = lax.broadcasted_iota(jnp.int32, (GROUP_W, GROUP_W), 1) // CHUNK
    same_head = br == bc
    hl = lax.broadcasted_iota(jnp.int32, (CHUNK, LANE), 1) // DK
    half_masks = tuple(jnp.where(hl == h, 1.0, 0.0).astype(BF16) for h in range(HEADS_PER_LANE_TILE))
    ti = lax.broadcasted_iota(jnp.int32, (CHUNK, CHUNK), 0)
    tm = lax.broadcasted_iota(jnp.int32, (CHUNK, CHUNK), 1)
    ri8 = lax.broadcasted_iota(jnp.int32, (CHUNK, H_A * DK), 0)
    cj8 = lax.broadcasted_iota(jnp.int32, (CHUNK, H_A * DK), 1) % CHUNK
    incl = (ri >= cj, ri <= cj)
    strict = (ri > cj, ri < cj)
    tri = tuple(jnp.where(m, 1.0, 0.0).astype(BF16) for m in (tm <= ti, tm >= ti))
    eye8 = ri8 == cj8
    last_row = (CHUNK - 1, 0)

    @pl.when(j == 0)
    def _():
        for d in range(2):
            for g in range(N_GROUPS):
                if zero_init:
                    st_s[d, g] = jnp.zeros((GROUP_W, GROUP_W), F32)
                else:
                    nat = s0_ref[0, d, g * GROUP_W:(g + 1) * GROUP_W, :]
                    st_s[d, g] = jnp.where(same_head, jnp.concatenate([nat] * HEAD_GROUP, axis=1), 0.0)

    cw = cw_ref[...]
    a_neg = -jnp.exp(par_ref[0:1, :])
    dtb = par_ref[1:2, :]
    pos = (j, nb - 1 - j)
    scan_order = (tuple(range(CHUNKS_PER_RB)), tuple(reversed(range(CHUNKS_PER_RB))))

    def unit_thunk(d, r, ch, g, units):
        def run():
            rows = slice(r * CHUNK, (r + 1) * CHUNK)
            lanes = slice(g * GROUP_W, (g + 1) * GROUP_W)
            beta, gcc8 = ch["decay"][d]
            gcr = jnp.sum(jnp.where(eye8, gcc8, 0.0), axis=0, keepdims=True)[:, lanes]
            qn, kn, be, gcc = ch["qn"][:, lanes], ch["kn"][:, lanes], beta[:, lanes], gcc8[:, lanes]
            kq = ch["gram"][g]
            dec = jnp.exp(jnp.where(incl[d], gcc - gcr, -1e30))
            a = jnp.where(strict[d], kq[0:CHUNK] * be * dec, 0.0)
            eg = jnp.exp(gcc)
            gcl = gcc[last_row[d]:last_row[d] + 1, :]
            units.append((d, rows, lanes, a, ch["v"][:, lanes] * be, kn * be * eg))
            p_s[d, rows, lanes] = (kq[CHUNK:2 * CHUNK] * dec).astype(BF16)
            qg_s[d, rows, lanes] = (qn * eg).astype(BF16)
            kg_s[d, rows, lanes] = (kn * jnp.exp(gcl - gcc)).astype(BF16)
            gl_s[d, r * 8:(r + 1) * 8, lanes] = jnp.broadcast_to(jnp.exp(gcl), (8, GROUP_W))
        return run

    def prep_thunks(units):
        thunks = []
        if shared:
            todo = [(0, r, (0, 1)) for r in range(CHUNKS_PER_RB)]
        else:
            todo = [(d, r, (d,)) for d in range(2) for r in scan_order[d]]
        for b, r, dirs in todo:
            ch, stages = _delta_chunk_stages(*blocks[b], r, pos[b] > 0, pos[b] < nb - 1, dirs, cw, a_neg, dtb,
                                             ones_ref[...], esel_ref, tri, half_masks)
            thunks += stages
            thunks += [unit_thunk(d, r, ch, g, units) for d in dirs for g in range(N_GROUPS)]
        return thunks

    def scan_thunk(d, c):
        def run():
            r = scan_order[d][c]
            rows = slice(r * CHUNK, (r + 1) * CHUNK)
            for g in range(N_GROUPS):
                lanes = slice(g * GROUP_W, (g + 1) * GROUP_W)
                s = st_s[d, g]
                wq = jnp.concatenate([w_s[d, rows, lanes], qg_s[d, rows, lanes]], axis=0)
                ws_qs = _dot(wq, s.astype(BF16))
                v_new = u_s[d, rows, lanes] - ws_qs[0:CHUNK]
                o = ws_qs[CHUNK:2 * CHUNK] + _dot(p_s[d, rows, lanes], _block_diag(v_new, half_masks))
                upd = _dot_tn(kg_s[d, rows, lanes], v_new.astype(BF16))
                st_s[d, g] = s * gl_s[d, r * 8:r * 8 + 1, lanes] + jnp.where(same_head, upd, 0.0)
                o_refs[d][rows, lanes] = o
        return run

    def solve(units):
        xs = [jnp.where(eye, 1.0, 0.0) - jnp.where(ixj == 1, un[3], 0.0) for un in units]
        for lvl in range(1, 6):
            t1s = [_dot(x.astype(BF16), _block_diag(jnp.where((ixj >> lvl) == 1, un[3], 0.0), half_masks))
                   for x, un in zip(xs, units)]
            xs = [x - _dot(t1.astype(BF16), _block_diag(x, half_masks)) for x, t1 in zip(xs, t1s)]
        for x, (d, rows, lanes, _, vb, kbe) in zip(xs, units):
            rhs = jnp.concatenate([_block_diag(vb, half_masks), _block_diag(kbe, half_masks)], axis=1)
            uw = _dot(x.astype(BF16), rhs)
            u_s[d, rows, lanes] = uw[:, 0:GROUP_W]
            w_s[d, rows, lanes] = uw[:, GROUP_W:2 * GROUP_W].astype(BF16)

    units = []
    for f in prep_thunks(units):
        f()
    solve(units)
    for c in range(CHUNKS_PER_RB):
        for d in range(2):
            scan_thunk(d, c)()

    @pl.when(j == nb - 1)
    def _():
        for d in range(2):
            for g in range(N_GROUPS):
                s = st_s[d, g]
                nat = s[:, 0:DV]
                for hh in range(1, HEAD_GROUP):
                    nat = nat + s[:, hh * DV:(hh + 1) * DV]
                sfin_ref[0, d, g * GROUP_W:(g + 1) * GROUP_W, :] = nat


def _delta_call(pd, conv_w, par, esel, ones_bd, s0, n_seq, nb, blk0):
    zero_init = s0 is None
    n_blocks_all = T_ALL // DELTA_RB
    blk_of = (lambda s, j: blk0 + s * nb + j, lambda s, j: blk0 + s * nb + nb - 1 - j)
    in_specs, args = [], []
    for d in range(1 if nb == 1 else 2):
        prev, nxt = _halo_specs(QKV_W, DELTA_RB, n_blocks_all, blk_of[d])
        in_specs += [pl.BlockSpec((DELTA_RB, PD_W), lambda s, j, d=d: (blk_of[d](s, j), 0)), prev, nxt]
        args += [pd, pd, pd]
    in_specs += [
        pl.BlockSpec((3, QKV_W), lambda s, j: (0, 0)),
        pl.BlockSpec((8, LANE), lambda s, j: (0, 0)),
        pl.BlockSpec((2, 2, LANE, H_A * DK), lambda s, j: (0, 0, 0, 0)),
        pl.BlockSpec((GROUP_W, GROUP_W), lambda s, j: (0, 0)),
    ]
    args += [conv_w, par, esel, ones_bd]
    if not zero_init:
        in_specs.append(pl.BlockSpec((1, 2, H_A * DK, DV), lambda s, j: (s, 0, 0, 0)))
        args.append(s0)
    rows = n_seq * nb * DELTA_RB
    dir_buf = lambda n, dt=F32: pltpu.VMEM((2, n, H_A * DK), dt)
    return pl.pallas_call(
        functools.partial(_delta_kernel, nb, zero_init),
        grid=(n_seq, nb),
        in_specs=in_specs,
        out_specs=[pl.BlockSpec((DELTA_RB, W_A), lambda s, j: (s * nb + j, 0)),
                   pl.BlockSpec((DELTA_RB, W_A), lambda s, j: (s * nb + nb - 1 - j, 0)),
                   pl.BlockSpec((1, 2, H_A * DK, DV), lambda s, j: (s, 0, 0, 0))],
        out_shape=[jax.ShapeDtypeStruct((rows, W_A), F32),
                   jax.ShapeDtypeStruct((rows, W_A), F32),
                   jax.ShapeDtypeStruct((n_seq, 2, H_A * DK, DV), F32)],
        scratch_shapes=[dir_buf(DELTA_RB)] + [dir_buf(DELTA_RB, BF16) for _ in range(4)]
        + [dir_buf(CHUNKS_PER_RB * 8), pltpu.VMEM((2, N_GROUPS, GROUP_W, GROUP_W), F32)],
        compiler_params=_cparams(("parallel", "arbitrary"), VMEM_LIMIT),
        name=f"deltanet_nb{nb}",
    )(*args)


FFN_TN = D_FF // 2


def _postmix_kernel(final_norm, tile0, ofc_ref, ofl_ref, obc_ref, obl_ref, z_ref, ybc_ref, ybl_ref,
                    ycc_ref, ycl_ref, pg_ref, x_ref, mod_ref, na_ref, ones_ref, wpa_ref, wpb_ref, wpc_ref,
                    wo_ref, g2_ref, wgu_ref, wdn_ref, nf_ref, out_ref):
    is_ctx = pl.program_id(0) + tile0 < N_CTX_TILES
    m = mod_ref[0]
    o = jnp.where(is_ctx, ofc_ref[...] + obc_ref[...], ofl_ref[...] + obl_ref[...])
    yb = jnp.where(is_ctx, ybc_ref[...], ybl_ref[...])
    yc = jnp.where(is_ctx, ycc_ref[...], ycl_ref[...])
    ms = _head_sums([o * o], ones_ref[...])[0] * (1.0 / DV)
    ya = (o * lax.rsqrt(ms + EPS) * na_ref[...]) * _silu(z_ref[...])
    merged = (pg_ref[:, 0:D_MODEL].astype(F32) * _dot(ya.astype(BF16), wpa_ref[0])
              + pg_ref[:, D_MODEL:2 * D_MODEL].astype(F32) * _dot(yb.astype(BF16), wpb_ref[0])
              + pg_ref[:, 2 * D_MODEL:3 * D_MODEL].astype(F32) * _dot(yc.astype(BF16), wpc_ref[0]))
    x = x_ref[...] + m[2:3] * _dot(merged.astype(BF16), wo_ref[0])

    h = _rms_mod(x, g2_ref[...], m[4:5], m[3:4]).astype(BF16)
    acc = None
    for c in range(0, D_FF, FFN_TN):
        gate = _dot(h, wgu_ref[0, :, c:c + FFN_TN])
        up = _dot(h, wgu_ref[0, :, D_FF + c:D_FF + c + FFN_TN])
        part = _dot((_silu(gate) * up).astype(BF16), wdn_ref[0, c:c + FFN_TN, :])
        acc = part if acc is None else acc + part
    xn = x + m[5:6] * acc
    if final_norm:
        ms = jnp.mean(xn * xn, axis=-1, keepdims=True)
        xn = xn * lax.rsqrt(ms + EPS) * nf_ref[...]
    out_ref[...] = xn


def _postmix_call(o_ctx, o_lat, pd, yb, yc, pg, x, mod3, na512, ones_bd, wpa, wpb, wpc, wo,
                  g2, wgu, wdn, nf, layer, final_norm, tile0=0, n_tiles=T_ALL // ROW_TILE):
    row = lambda i: (i + tile0, 0)
    const = lambda i: (0, 0)
    lyr = lambda i: (layer, 0, 0)
    assert W_A == W_B == W_C
    return pl.pallas_call(
        functools.partial(_postmix_kernel, final_norm, tile0),
        grid=(n_tiles,),
        in_specs=[
            *_ctx_lat_specs(W_A, tile0), *_ctx_lat_specs(W_A, tile0),
            pl.BlockSpec((ROW_TILE, W_A), lambda i: (i + tile0, OFF_Z // W_A)),
            *_ctx_lat_specs(W_B, tile0), *_ctx_lat_specs(W_C, tile0),
            pl.BlockSpec((ROW_TILE, 3 * D_MODEL), row),
            pl.BlockSpec((ROW_TILE, D_MODEL), row),
            pl.BlockSpec((1, 6, D_MODEL), lambda i: (_mod_row_block(i + tile0), 0, 0)),
            pl.BlockSpec((1, W_A), const),
            _resident((GROUP_W, GROUP_W), const),
            _resident((1, W_A, D_MODEL), lyr),
            _resident((1, W_B, D_MODEL), lyr),
            _resident((1, W_C, D_MODEL), lyr),
            _resident((1, D_MODEL, D_MODEL), lyr),
            pl.BlockSpec((1, D_MODEL), const),
            _resident((1, D_MODEL, 2 * D_FF), lyr),
            _resident((1, D_FF, D_MODEL), lyr),
            pl.BlockSpec((1, D_MODEL), const),
        ],
        out_specs=pl.BlockSpec((ROW_TILE, D_MODEL), lambda i: (i, 0)),
        out_shape=jax.ShapeDtypeStruct((n_tiles * ROW_TILE, D_MODEL), F32),
        compiler_params=_cparams(("parallel",), VMEM_LIMIT),
        name="postmix_final" if final_norm else "postmix",
    )(o_ctx[0], o_lat[0], o_ctx[1], o_lat[1], pd, yb[0], yb[1], yc[0], yc[1], pg, x, mod3, na512,
      ones_bd, wpa, wpb, wpc, wo, g2.reshape(1, D_MODEL), wgu, wdn, nf.reshape(1, D_MODEL))


TABLE_SPLIT = 64


def _grid_pos_embed(n_tokens):
    rows = n_tokens // GRID_W
    quarter = D_MODEL // 4
    omega = 1.0 / (10000.0 ** (jnp.arange(quarter, dtype=F32) / quarter))

    def emb(pos):
        a = pos[:, None] * omega[None, :]
        return jnp.concatenate([jnp.sin(a), jnp.cos(a)], axis=-1)

    e_row, e_col = lax.optimization_barrier((emb(jnp.arange(rows).astype(F32)),
                                             emb(jnp.arange(GRID_W).astype(F32))))
    return jnp.concatenate([jnp.repeat(e_row, GRID_W, axis=0), jnp.tile(e_col, (rows, 1))], axis=-1)


def _cos_nsin_tables(n, period):
    t = jnp.arange(n, dtype=jnp.int32)[None, :]

    def cs(r):
        ang = ((r * t) % period).astype(F32) * (2.0 * math.pi / period)
        return jnp.cos(ang), jnp.sin(ang)

    ca, sa = cs(jnp.arange(n // TABLE_SPLIT, dtype=jnp.int32)[:, None] * TABLE_SPLIT)
    cb, sb = cs(jnp.arange(TABLE_SPLIT, dtype=jnp.int32)[:, None])
    ca, sa, cb, sb = lax.optimization_barrier((ca, sa, cb, sb))
    ca, sa = ca[:, None, :], sa[:, None, :]
    cos = (ca * cb[None] - sa * sb[None]).reshape(n, n)
    nsin = (-(sa * cb[None] + ca * sb[None])).reshape(n, n)
    return cos.astype(BF16), nsin.astype(BF16)


def _hyena_positions(L):
    bands = (HY_EMB - 1) // 2
    t = jnp.linspace(0.0, 1.0, L, dtype=F32)[:, None]
    wpos = (2.0 * math.pi / L) * jnp.arange(L, dtype=F32)[:, None]
    fr = jnp.linspace(1e-4, bands - 1, bands, dtype=F32)[None, :]
    zpos = jnp.concatenate([t, jnp.cos(fr * wpos), -jnp.sin(fr * wpos)], axis=-1)
    zpos = jnp.pad(zpos, ((0, 0), (0, LANE - HY_EMB)))
    deltas = jnp.abs(jnp.linspace(math.log(HY_DECAY_TARGET) / HY_SLOW_PCT,
                                  math.log(HY_DECAY_TARGET) / HY_FAST_PCT, W_B, dtype=F32))
    window = jnp.exp(-t * deltas[None, :])
    return zpos, window


def _group_tables():
    r = jnp.arange(DC, dtype=jnp.int32)
    ang = ((r[:, None] * r[None, :]) % DC).astype(F32) * (2.0 * math.pi / DC)
    eye = jnp.eye(G_C, dtype=F32)
    return jnp.kron(eye, jnp.cos(ang)).astype(BF16), jnp.kron(eye, jnp.sin(ang)).astype(BF16)


def _head_tables():
    ones_bd = jnp.kron(jnp.eye(HEAD_GROUP, dtype=F32), jnp.ones((DK, DK), F32)).astype(BF16)
    lane = jnp.arange(LANE)[:, None]
    head = (jnp.arange(H_A * DK) // DK)[None, :]
    sel = []
    for d in range(2):
        sel.append(jnp.stack([(lane == d * H_A + head), (lane == 2 * H_A + d * H_A + head)]))
    esel = jnp.stack(sel).astype(BF16)
    return ones_bd, esel


def kernel(x_prompt, x_sample, state_delta, c, c_ctx, w_mod, b_mod, norm1_g, norm2_g, w_in, conv_qkv, a_log, dt_bias, norm_a, conv_hy, hy_w1, hy_b1, hy_freq, hy_w2, hy_b2, hy_w3, hy_bias, w_pa, w_pb, w_pc, w_o, w_gu, w_down, norm_f):
    assert x_prompt.shape == (N_CTX_SEQ, L_CTX, D_MODEL) and x_sample.shape == (N_LAT_SEQ, L_LAT, D_MODEL)
    st = jnp.pad(jnp.concatenate([c_ctx[None], c], axis=0).T, ((0, 0), (0, 8 - 1 - N_LAT_SEQ)))
    mod = _mod_call(st, w_mod, b_mod).reshape(DEPTH, 8, 6, D_MODEL)

    ones_bd, esel = _head_tables()
    bdc, bds = _group_tables()
    seqs = ((L_CTX, N_CTX_SEQ, 0), (L_LAT, N_LAT_SEQ, T_CTX // L_LAT))
    tables = {L: (_cos_nsin_tables(L, 2 * L), _cos_nsin_tables(L, L), _hyena_positions(L))
              for L, _, _ in seqs}

    w_in_b = jnp.swapaxes(w_in, 1, 2).astype(BF16)
    w_pa_b, w_pb_b, w_pc_b, w_o_b, w_gu_b, w_down_b = (
        w.astype(BF16) for w in (w_pa, w_pb, w_pc, w_o, w_gu, w_down))

    x = None
    ctx_states = []
    for l in range(DEPTH):
        mod3 = mod[l, 0:3]
        if l == 0:
            xs = (x_prompt.reshape(T_CTX, D_MODEL), x_sample.reshape(T_LAT, D_MODEL), _grid_pos_embed(L_LAT))
            pd, ph, pf, pg, x = _inproj_call(xs, mod3, norm1_g[l], w_in_b, l)
        else:
            pd, ph, pf, pg = _inproj_call((x,), mod3, norm1_g[l], w_in_b, l)

        par = jnp.zeros((8, LANE), F32)
        par = par.at[0, 2 * H_A:4 * H_A].set(a_log[l].reshape(-1))
        par = par.at[1, 2 * H_A:4 * H_A].set(dt_bias[l].reshape(-1))
        *o_ctx, s_ctx = _delta_call(pd, conv_qkv[l], par, esel, ones_bd, None,
                                    N_CTX_SEQ, L_CTX // DELTA_RB, 0)
        s0 = state_delta[:, l].astype(F32).reshape(N_LAT_SEQ, 2, H_A * DK, DV)
        *o_lat, _ = _delta_call(pd, conv_qkv[l], par, esel, ones_bd, s0,
                                N_LAT_SEQ, L_LAT // DELTA_RB, T_CTX // DELTA_RB)
        ctx_states.append(s_ctx.reshape(N_CTX_SEQ, 2, H_A, DK, DV))

        w1p = jnp.pad(hy_w1[l], ((0, LANE - HY_EMB), (0, 0)))
        yb, yc = [], []
        for L, n_seq, blk0 in seqs:
            (cos2, nsin2), (cos1, nsin1), (zpos, window) = tables[L]
            kspec = _filter_call(L, zpos, w1p, hy_b1[l][None], hy_freq[l][None], hy_w2[l],
                                 hy_b2[l][None], hy_w3[l], window, cos2, nsin2)
            bias = hy_bias[l][:, None, :]
            if L == HYENA_FT:
                assert blk0 == 0
                yb.append(_hyena_short_call(ph, conv_hy[l], kspec, bias, cos2, nsin2, n_seq, L))
                yc.append(_fnet_short_call(pf, cos1, nsin1, bdc, bds, n_seq, L))
            else:
                yb.append(_hyena_call(ph, conv_hy[l], kspec, bias, cos2, nsin2, n_seq, L, blk0, HYENA_FT))
                yc.append(_fnet_call(pf, cos1, nsin1, bdc, bds, n_seq, L, blk0))

        na512 = jnp.tile(norm_a[l], H_A)[None]
        post = functools.partial(_postmix_call, o_ctx, o_lat, pd, yb, yc, pg, x, mod3, na512, ones_bd,
                                 w_pa_b, w_pb_b, w_pc_b, w_o_b, norm2_g[l], w_gu_b, w_down_b, norm_f, l)
        if l < DEPTH - 1:
            x = post(False)
        else:
            y_prompt = post(True, 0, N_CTX_TILES).reshape(N_CTX_SEQ, L_CTX, D_MODEL)
            y_sample = post(True, N_CTX_TILES, N_LAT_TILES).reshape(N_LAT_SEQ, L_LAT, D_MODEL)

    new_state = jnp.stack(ctx_states, axis=1).astype(x_prompt.dtype)
    return (y_prompt, y_sample, new_state)
```

```python
import functools
import math

import jax
import jax.numpy as jnp
from jax import lax
from jax.experimental import pallas as pl
from jax.experimental.pallas import tpu as pltpu

F32 = jnp.float32
BF16 = jnp.bfloat16

D_MODEL = 1024
N_CTX_SEQ = 32
L_CTX = 256
DEPTH = 2
N_LAT_SEQ = 2
L_LAT = 2048
GRID_W = 64
EPS = 1e-6
H_A = 8
DK = 64
DV = 64
W_A = H_A * DV
QKV_W = 2 * H_A * DK + H_A * DV
CHUNK = 64
W_B = 512
HY_EMB = 33
HY_HID = 64
HY_DECAY_TARGET = 1e-2
HY_FAST_PCT = 0.3
HY_SLOW_PCT = 1.5
G_C = 8
DC = 64
W_C = G_C * DC
D_FF = ((8 * D_MODEL + 3 * 256 - 1) // (3 * 256)) * 256
OFF_Z = QKV_W
OFF_B = OFF_Z + W_A
OFF_A = OFF_B + 2 * H_A
OFF_HY = OFF_A + 2 * H_A
OFF_FN = OFF_HY + 3 * W_B
OFF_GATE = OFF_FN + W_C

T_CTX = N_CTX_SEQ * L_CTX
T_LAT = N_LAT_SEQ * L_LAT
T_ALL = T_CTX + T_LAT
ROW_TILE = 256
N_CTX_TILES = T_CTX // ROW_TILE
N_LAT_TILES = T_LAT // ROW_TILE
LANE = 128
PD_W = QKV_W + W_A + LANE
HEAD_GROUP = 4
GROUP_W = HEAD_GROUP * DK
N_GROUPS = H_A // HEAD_GROUP
VMEM_LIMIT = 56 * 1024 * 1024


def _cparams(sem, vmem=None):
    return pltpu.CompilerParams(dimension_semantics=sem, vmem_limit_bytes=vmem)


def _dot(a, b):
    return jnp.dot(a, b, preferred_element_type=F32)


def _dot_nt(a, b):
    return lax.dot_general(a, b, (((1,), (1,)), ((), ())), preferred_element_type=F32)


def _dot_tn(a, b):
    return lax.dot_general(a, b, (((0,), (0,)), ((), ())), preferred_element_type=F32)


def _split(a, n):
    parts = []
    rem = a
    for i in range(n):
        p = rem.astype(BF16)
        parts.append(p)
        if i + 1 < n:
            rem = rem - p.astype(F32)
    return parts


def _mm3(a, b):
    ah, al = _split(a, 2)
    bh, bl = _split(b, 2)
    return _dot(ah, bh) + (_dot(ah, bl) + _dot(al, bh))


def _sigmoid(x):
    return 1.0 / (1.0 + jnp.exp(-x))


def _silu(x):
    return x * _sigmoid(x)


def _softplus(x):
    return jnp.maximum(x, 0.0) + jnp.log(1.0 + jnp.exp(-jnp.abs(x)))


def _mod_row_block(i):
    per_lat = L_LAT // ROW_TILE
    return jnp.where(i < N_CTX_TILES, 0, 1 + (i - N_CTX_TILES) // per_lat)


def _ctx_tile(i):
    return jnp.minimum(i, N_CTX_TILES - 1)


def _lat_tile(i):
    return jnp.maximum(i - N_CTX_TILES, 0)


def _ctx_lat_specs(width, tile0=0):
    return (pl.BlockSpec((ROW_TILE, width), lambda i: (_ctx_tile(i + tile0), 0)),
            pl.BlockSpec((ROW_TILE, width), lambda i: (_lat_tile(i + tile0), 0)))


MOD_TN = 512


def _mod_kernel(st_ref, w_ref, b_ref, out_ref):
    s = _silu(st_ref[...])
    w = w_ref[0]
    rows = [jnp.sum(s[:, r:r + 1] * w, axis=0, keepdims=True) + b_ref[0] for r in range(3)]
    rows.append(jnp.zeros((5, MOD_TN), F32))
    out_ref[0] = jnp.concatenate(rows, axis=0)


def _mod_call(st, w_mod, b_mod):
    n6 = 6 * D_MODEL
    return pl.pallas_call(
        _mod_kernel,
        grid=(DEPTH, n6 // MOD_TN),
        in_specs=[
            pl.BlockSpec((D_MODEL, 8), lambda l, j: (0, 0)),
            pl.BlockSpec((1, D_MODEL, MOD_TN), lambda l, j: (l, 0, j)),
            pl.BlockSpec((1, 1, MOD_TN), lambda l, j: (l, 0, j)),
        ],
        out_specs=pl.BlockSpec((1, 8, MOD_TN), lambda l, j: (l, 0, j)),
        out_shape=jax.ShapeDtypeStruct((DEPTH, 8, n6), F32),
        compiler_params=_cparams(("parallel", "parallel")),
        name="adaln_mod",
    )(st, w_mod, b_mod.reshape(DEPTH, 1, n6))


INPROJ_TN = 512
INPROJ_WIDTHS = (PD_W, 3 * W_B, W_C, 3 * D_MODEL)
INPROJ_FEATURES = (OFF_HY, 3 * W_B, W_C, 3 * D_MODEL)


def _rms_mod(x, g, scale, shift):
    ms = jnp.mean(x * x, axis=-1, keepdims=True)
    return (x * lax.rsqrt(ms + EPS) * g) * (1.0 + scale) + shift


def _inproj_kernel(first, *refs):
    if first:
        (xc_ref, xl_ref, pos_ref, mod_ref, g_ref, w_ref, pd_ref, ph_ref, pf_ref, pg_ref, x_ref) = refs
        x = jnp.where(pl.program_id(0) < N_CTX_TILES, xc_ref[...], xl_ref[...] + pos_ref[...])
        x_ref[...] = x
    else:
        (xin_ref, mod_ref, g_ref, w_ref, pd_ref, ph_ref, pf_ref, pg_ref) = refs
        x = xin_ref[...]
    m = mod_ref[0]
    h = _rms_mod(x, g_ref[...], m[1:2], m[0:1]).astype(BF16)
    row0 = 0
    for o_ref, n_feat in zip((pd_ref, ph_ref, pf_ref, pg_ref), INPROJ_FEATURES):
        n = o_ref.shape[1]
        for c in range(0, n, INPROJ_TN):
            e = min(c + INPROJ_TN, n)
            ef = min(e, n_feat)
            y = _dot_nt(h, w_ref[0, row0 + c:row0 + ef, :])
            if ef < e:
                y = jnp.concatenate([y, jnp.zeros((y.shape[0], e - ef), F32)], axis=1)
            if o_ref is pg_ref:
                y = _sigmoid(y)
            o_ref[:, c:e] = y.astype(o_ref.dtype)
        row0 += n_feat


def _resident(shape, index_map):
    return pl.BlockSpec(shape, index_map, pipeline_mode=pl.Buffered(1))


def _inproj_call(xs, mod3, g, w_all, layer):
    first = len(xs) == 3
    widths = INPROJ_WIDTHS
    row = lambda i: (i, 0)
    const = lambda i: (0, 0)
    if first:
        per_lat = L_LAT // ROW_TILE
        x_specs = list(_ctx_lat_specs(D_MODEL)) + [
            pl.BlockSpec((ROW_TILE, D_MODEL), lambda i: (_lat_tile(i) % per_lat, 0))]
    else:
        x_specs = [pl.BlockSpec((ROW_TILE, D_MODEL), row)]
    out_widths = widths + ((D_MODEL,) if first else ())
    out_dtypes = (F32, F32, F32, BF16) + ((F32,) if first else ())
    return pl.pallas_call(
        functools.partial(_inproj_kernel, first),
        grid=(T_ALL // ROW_TILE,),
        in_specs=x_specs + [
            pl.BlockSpec((1, 6, D_MODEL), lambda i: (_mod_row_block(i), 0, 0)),
            pl.BlockSpec((1, D_MODEL), const),
            _resident((1, sum(INPROJ_FEATURES), D_MODEL), lambda i: (layer, 0, 0)),
        ],
        out_specs=[pl.BlockSpec((ROW_TILE, w), row) for w in out_widths],
        out_shape=[jax.ShapeDtypeStruct((T_ALL, w), dt) for w, dt in zip(out_widths, out_dtypes)],
        compiler_params=_cparams(("parallel",), VMEM_LIMIT),
        name="inproj_first" if first else "inproj",
    )(*xs, mod3, g.reshape(1, D_MODEL), w_all)


def _conv3_rows(cur, prev_row, next_row, w):
    n = cur.shape[0]
    ridx = lax.broadcasted_iota(jnp.int32, cur.shape, 0)
    up = jnp.where(ridx == 0, prev_row, pltpu.roll(cur, 1, 0))
    dn = jnp.where(ridx == n - 1, next_row, pltpu.roll(cur, n - 1, 0))
    return up * w[0:1] + cur * w[1:2] + dn * w[2:3]


def _halo_specs(width, rows_per_block, n_row_blocks, blk_of):
    per = rows_per_block // 8
    last = n_row_blocks * per - 1
    prev = pl.BlockSpec((8, width), lambda *a: (jnp.maximum(blk_of(*a) * per - 1, 0), 0))
    nxt = pl.BlockSpec((8, width), lambda *a: (jnp.minimum((blk_of(*a) + 1) * per, last), 0))
    return prev, nxt


FILT_RT = 256


def _alternating_sum(x):
    t = lax.broadcasted_iota(jnp.int32, x.shape, 0)
    return jnp.sum(jnp.where(t % 2 == 0, x, -x), axis=0, keepdims=True)


def _filter_kernel(L, zpos_ref, w1_ref, b1_ref, fq_ref, w2_ref, b2_ref, w3_ref, win_ref,
                   cos_ref, nsin_ref, k_ref, hs_s, hm_s, krl_s):
    rt = pl.program_id(1)

    @pl.when(rt == 0)
    def _():
        fq = fq_ref[...]
        alt_acc = jnp.zeros((1, W_B), F32)
        for r0 in range(0, L, FILT_RT):
            rows = slice(r0, r0 + FILT_RT)
            h = jnp.sin(fq * (_mm3(zpos_ref[rows, :], w1_ref[...]) + b1_ref[...]))
            h = jnp.sin(fq * (_mm3(h, w2_ref[...]) + b2_ref[...]))
            hf = _mm3(h, w3_ref[...])
            win = win_ref[rows, :]
            fw = hf[:, 0:W_B] * win
            bw = hf[:, W_B:2 * W_B] * win
            hsum = fw + bw
            hs_s[rows, :] = hsum.astype(BF16)
            hm_s[rows, :] = (fw - bw).astype(BF16)
            alt_acc = alt_acc + _alternating_sum(hsum)
        krl_s[...] = jnp.broadcast_to(alt_acc, krl_s.shape)

    p1 = _dot(cos_ref[...], hs_s[...])
    p2 = _dot(nsin_ref[...], hm_s[...])
    first = (rt * FILT_RT + lax.broadcasted_iota(jnp.int32, p1.shape, 0)) == 0
    k_ref[0, 0] = p1
    k_ref[0, 1] = jnp.where(first, krl_s[0:1, :], p1)
    k_ref[0, 2] = jnp.where(first, 0.0, p2)


def _filter_call(L, zpos, w1p, b1, fq, w2, b2, w3, win, cos, nsin):
    nrt = L // FILT_RT
    c2 = lambda o, r: (0, 0)
    return pl.pallas_call(
        functools.partial(_filter_kernel, L),
        grid=(2, nrt),
        in_specs=[
            pl.BlockSpec((L, LANE), c2),
            pl.BlockSpec((LANE, HY_HID), c2),
            pl.BlockSpec((1, HY_HID), c2),
            pl.BlockSpec((1, HY_HID), c2),
            pl.BlockSpec((HY_HID, HY_HID), c2),
            pl.BlockSpec((1, HY_HID), c2),
            pl.BlockSpec((HY_HID, 2 * W_B), lambda o, r: (0, o)),
            pl.BlockSpec((L, W_B), c2),
            pl.BlockSpec((FILT_RT, L), lambda o, r: (r, 0)),
            pl.BlockSpec((FILT_RT, L), lambda o, r: (r, 0)),
        ],
        out_specs=pl.BlockSpec((1, 3, FILT_RT, W_B), lambda o, r: (o, 0, r, 0)),
        out_shape=jax.ShapeDtypeStruct((2, 3, L, W_B), F32),
        scratch_shapes=[pltpu.VMEM((L, W_B), BF16), pltpu.VMEM((L, W_B), BF16),
                        pltpu.VMEM((8, W_B), F32)],
        compiler_params=_cparams(("parallel", "arbitrary"), VMEM_LIMIT),
        name=f"hyena_filter_{L}",
    )(zpos, w1p, b1, fq, w2, b2, w3, win, cos, nsin)


HYENA_FT = 256

HYENA_CONV_ROWS = 256


HYENA_CH = 256


def _hyena_kernel(L, ft, x1_ref, x2_ref, v_ref, cw1_ref, cw2_ref, cwv_ref, k_ref, bias_ref, cos_ref, nsin_ref,
                  out_ref, gate_s, zf_s, zb_s, acc_s):
    o = pl.program_id(2)

    @pl.when(o == 0)
    def _():
        for src, cw_ref, dst in ((x1_ref, cw1_ref, 0), (x2_ref, cw2_ref, 1), (v_ref, cwv_ref, None)):
            cw = cw_ref[...]
            for r0 in range(0, L, HYENA_CONV_ROWS):
                r1 = r0 + HYENA_CONV_ROWS
                prev_row = src[r0 - 1:r0, :] if r0 > 0 else 0.0
                next_row = src[r1:r1 + 1, :] if r1 < L else 0.0
                uc = _conv3_rows(src[r0:r1, :], prev_row, next_row, cw)
                if dst is None:
                    zf_s[r0:r1, :] = uc
                    zb_s[r0:r1, :] = uc.astype(BF16)
                else:
                    gate_s[dst, r0:r1, :] = uc

    zb = zb_s[...]
    nyq = _alternating_sum(zf_s[...])
    for f in range(L // ft):
        rows = slice(f * ft, (f + 1) * ft)
        top = _dot(cos_ref[rows, :], zb)
        bot = _dot(nsin_ref[rows, :], zb)
        if f == 0:
            first = lax.broadcasted_iota(jnp.int32, top.shape, 0) == 0
            bot = jnp.where(first, nyq, bot)
        krt = k_ref[0, 0, rows, :]
        krb = k_ref[0, 1, rows, :]
        ki = k_ref[0, 2, rows, :]
        yt = top * krt - bot * ki
        yb = top * ki + bot * krb
        ytw = jnp.where(first, 0.5 * yt, yt) if f == 0 else yt
        part = _dot(cos_ref[:, rows], ytw.astype(BF16)) + _dot(nsin_ref[:, rows], yb.astype(BF16))
        if f == 0:
            t = lax.broadcasted_iota(jnp.int32, part.shape, 0)
            acc_s[...] = part + jnp.where(t % 2 == 0, 0.5, -0.5) * yb[0:1, :]
        else:
            acc_s[...] += part

    znew = gate_s[o] * (acc_s[...] * (1.0 / L) + bias_ref[0] * zf_s[...])
    zf_s[...] = znew
    zb_s[...] = znew.astype(BF16)
    out_ref[...] = znew.astype(out_ref.dtype)


def _hyena_call(ph, conv_w, kspec, bias, cos, nsin, n_seq, L, row_blk0, ft):
    nch = W_B // HYENA_CH
    once = pl.Buffered(1)
    col = lambda part: (lambda s, c, o: (row_blk0 + s, part * nch + c))
    cwcol = lambda part: (lambda s, c, o: (0, part * nch + c))
    return pl.pallas_call(
        functools.partial(_hyena_kernel, L, ft),
        grid=(n_seq, nch, 2),
        in_specs=[pl.BlockSpec((L, HYENA_CH), col(part), pipeline_mode=once) for part in range(3)]
        + [pl.BlockSpec((3, HYENA_CH), cwcol(part)) for part in range(3)]
        + [
            pl.BlockSpec((1, 3, L, HYENA_CH), lambda s, c, o: (o, 0, 0, c)),
            pl.BlockSpec((1, 1, HYENA_CH), lambda s, c, o: (o, 0, c)),
            pl.BlockSpec((L, L), lambda s, c, o: (0, 0), pipeline_mode=once),
            pl.BlockSpec((L, L), lambda s, c, o: (0, 0), pipeline_mode=once),
        ],
        out_specs=pl.BlockSpec((L, HYENA_CH), lambda s, c, o: (s, c)),
        out_shape=jax.ShapeDtypeStruct((n_seq * L, W_B), BF16),
        scratch_shapes=[pltpu.VMEM((2, L, HYENA_CH), F32), pltpu.VMEM((L, HYENA_CH), F32),
                        pltpu.VMEM((L, HYENA_CH), BF16), pltpu.VMEM((L, HYENA_CH), F32)],
        compiler_params=_cparams(("parallel", "parallel", "arbitrary"), VMEM_LIMIT),
        name=f"hyena_conv_{L}",
    )(ph, ph, ph, conv_w, conv_w, conv_w, kspec, bias, cos, nsin)


HYENA_SHORT_SB = 4


def _hyena_short_kernel(L, ph_ref, cw_ref, k_ref, bias_ref, cos_ref, nsin_ref, out_ref):
    cos = cos_ref[...]
    nsin = nsin_ref[...]
    cw = cw_ref[...]
    t = lax.broadcasted_iota(jnp.int32, (L, W_B), 0)
    first = t == 0
    alt_half = jnp.where(t % 2 == 0, 0.5, -0.5)
    rows = [slice(s * L, (s + 1) * L) for s in range(HYENA_SHORT_SB)]
    ucs = [_conv3_rows(ph_ref[r, :], 0.0, 0.0, cw) for r in rows]
    zs = [uc[:, 2 * W_B:3 * W_B] for uc in ucs]
    for o in range(2):
        krt, krb, ki = k_ref[o, 0], k_ref[o, 1], k_ref[o, 2]
        zbs = [z.astype(BF16) for z in zs]
        tops = [_dot(cos, zb) for zb in zbs]
        bots = [jnp.where(first, _alternating_sum(z), _dot(nsin, zb)) for z, zb in zip(zs, zbs)]
        yts = [top * krt - bot * ki for top, bot in zip(tops, bots)]
        ybs = [top * ki + bot * krb for top, bot in zip(tops, bots)]
        accs = [_dot(cos, jnp.where(first, 0.5 * yt, yt).astype(BF16)) + _dot(nsin, yb.astype(BF16))
                + alt_half * yb[0:1, :] for yt, yb in zip(yts, ybs)]
        zs = [uc[:, o * W_B:(o + 1) * W_B] * (acc * (1.0 / L) + bias_ref[o] * z)
              for uc, acc, z in zip(ucs, accs, zs)]
    for r, z in zip(rows, zs):
        out_ref[r, :] = z.astype(out_ref.dtype)


def _hyena_short_call(ph, conv_w, kspec, bias, cos, nsin, n_seq, L):
    rows = HYENA_SHORT_SB * L
    return pl.pallas_call(
        functools.partial(_hyena_short_kernel, L),
        grid=(n_seq // HYENA_SHORT_SB,),
        in_specs=[
            pl.BlockSpec((rows, 3 * W_B), lambda i: (i, 0)),
            pl.BlockSpec((3, 3 * W_B), lambda i: (0, 0)),
            pl.BlockSpec((2, 3, L, W_B), lambda i: (0, 0, 0, 0)),
            pl.BlockSpec((2, 1, W_B), lambda i: (0, 0, 0)),
            pl.BlockSpec((L, L), lambda i: (0, 0)),
            pl.BlockSpec((L, L), lambda i: (0, 0)),
        ],
        out_specs=pl.BlockSpec((rows, W_B), lambda i: (i, 0)),
        out_shape=jax.ShapeDtypeStruct((n_seq * L, W_B), BF16),
        compiler_params=_cparams(("parallel",), VMEM_LIMIT),
        name=f"hyena_conv_{L}",
    )(ph, conv_w, kspec, bias, cos, nsin)


FNET_RT = 256


def _fnet_kernel(L, x_ref, cos_ref, nsin_ref, bdc_ref, bds_ref, out_ref, xc_s, xs_s):
    r = pl.program_id(1)

    @pl.when(r == 0)
    def _():
        for r0 in range(0, L, FNET_RT):
            xb = x_ref[r0:r0 + FNET_RT, :].astype(BF16)
            xc_s[r0:r0 + FNET_RT, :] = _dot(xb, bdc_ref[...]).astype(BF16)
            xs_s[r0:r0 + FNET_RT, :] = _dot(xb, bds_ref[...]).astype(BF16)

    y = _dot(cos_ref[...], xc_s[...]) + _dot(nsin_ref[...], xs_s[...])
    out_ref[...] = (y * (1.0 / math.sqrt(DC * L))).astype(out_ref.dtype)


def _fnet_call(pf, cos, nsin, bdc, bds, n_seq, L, row_blk0):
    nrt = L // FNET_RT
    return pl.pallas_call(
        functools.partial(_fnet_kernel, L),
        grid=(n_seq, nrt),
        in_specs=[
            pl.BlockSpec((L, W_C), lambda s, r: (row_blk0 + s, 0)),
            pl.BlockSpec((FNET_RT, L), lambda s, r: (r, 0)),
            pl.BlockSpec((FNET_RT, L), lambda s, r: (r, 0)),
            pl.BlockSpec((W_C, W_C), lambda s, r: (0, 0)),
            pl.BlockSpec((W_C, W_C), lambda s, r: (0, 0)),
        ],
        out_specs=pl.BlockSpec((FNET_RT, W_C), lambda s, r: (s * nrt + r, 0)),
        out_shape=jax.ShapeDtypeStruct((n_seq * L, W_C), BF16),
        scratch_shapes=[pltpu.VMEM((L, W_C), BF16), pltpu.VMEM((L, W_C), BF16)],
        compiler_params=_cparams(("parallel", "arbitrary"), VMEM_LIMIT),
        name=f"fnet_{L}",
    )(pf, cos, nsin, bdc, bds)


FNET_SHORT_SB = 4


def _fnet_short_kernel(L, x_ref, cos_ref, nsin_ref, bdc_ref, bds_ref, out_ref):
    xb = x_ref[...].astype(BF16)
    xc = _dot(xb, bdc_ref[...]).astype(BF16)
    xs = _dot(xb, bds_ref[...]).astype(BF16)
    for s in range(FNET_SHORT_SB):
        rows = slice(s * L, (s + 1) * L)
        y = _dot(cos_ref[...], xc[rows]) + _dot(nsin_ref[...], xs[rows])
        out_ref[rows, :] = (y * (1.0 / math.sqrt(DC * L))).astype(out_ref.dtype)


def _fnet_short_call(pf, cos, nsin, bdc, bds, n_seq, L):
    rows = FNET_SHORT_SB * L
    const = lambda i: (0, 0)
    return pl.pallas_call(
        functools.partial(_fnet_short_kernel, L),
        grid=(n_seq // FNET_SHORT_SB,),
        in_specs=[pl.BlockSpec((rows, W_C), lambda i: (i, 0)),
                  pl.BlockSpec((L, L), const), pl.BlockSpec((L, L), const),
                  pl.BlockSpec((W_C, W_C), const), pl.BlockSpec((W_C, W_C), const)],
        out_specs=pl.BlockSpec((rows, W_C), lambda i: (i, 0)),
        out_shape=jax.ShapeDtypeStruct((n_seq * L, W_C), BF16),
        compiler_params=_cparams(("parallel",), VMEM_LIMIT),
        name=f"fnet_{L}",
    )(pf, cos, nsin, bdc, bds)


DELTA_RB = 256
CHUNKS_PER_RB = DELTA_RB // CHUNK


HEADS_PER_LANE_TILE = LANE // DK


def _block_diag(y, half_masks):
    yb = y.astype(BF16)
    zero = jnp.zeros((CHUNK, LANE), BF16)
    row_blocks = []
    for h in range(HEAD_GROUP):
        tile = h // HEADS_PER_LANE_TILE
        piece = yb[:, tile * LANE:(tile + 1) * LANE] * half_masks[h % HEADS_PER_LANE_TILE]
        row_blocks.append(jnp.concatenate(
            [piece if t == tile else zero for t in range(GROUP_W // LANE)], axis=1))
    return jnp.concatenate(row_blocks, axis=0)


def _stacked_const_rhs(arrs, c, n):
    m = arrs[0].shape[0]
    parts = [p for a in arrs for p in _split(a, n)]
    y = _dot(jnp.concatenate(parts, axis=0), c)
    outs = []
    for i in range(len(arrs)):
        acc = y[i * n * m:(i * n + 1) * m]
        for t in range(1, n):
            acc = acc + y[(i * n + t) * m:(i * n + t + 1) * m]
        outs.append(acc)
    return outs


def _head_sums(arrs, ones_group):
    groups = [_stacked_const_rhs([a[:, g * GROUP_W:(g + 1) * GROUP_W] for a in arrs], ones_group, 2)
              for g in range(N_GROUPS)]
    return [jnp.concatenate([groups[g][i] for g in range(N_GROUPS)], axis=1) for i in range(len(arrs))]


def _const_lhs_split(c, b, n):
    w = b.shape[1]
    y = _dot(c, jnp.concatenate(_split(b, n), axis=1))
    acc = y[:, 0:w]
    for t in range(1, n):
        acc = acc + y[:, t * w:(t + 1) * w]
    return acc


def _delta_chunk_stages(pd_ref, prev_ref, next_ref, r, has_prev, has_next, dirs, cw, a_neg, dtb, ones_bd,
                        esel_ref, tri, half_masks):
    ch = {}
    rows = slice(r * CHUNK, (r + 1) * CHUNK)

    def conv():
        cur = pd_ref[rows, 0:QKV_W]
        if r == 0:
            prev_row = jnp.where(has_prev, prev_ref[7:8, :], 0.0)
        else:
            prev_row = pd_ref[r * CHUNK - 1:r * CHUNK, 0:QKV_W]
        if r == CHUNKS_PER_RB - 1:
            next_row = jnp.where(has_next, next_ref[0:1, :], 0.0)
        else:
            next_row = pd_ref[(r + 1) * CHUNK:(r + 1) * CHUNK + 1, 0:QKV_W]
        qkv = _silu(_conv3_rows(cur, prev_row, next_row, cw))
        ch["q"] = qkv[:, 0:H_A * DK]
        ch["k"] = qkv[:, H_A * DK:2 * H_A * DK]
        ch["v"] = qkv[:, 2 * H_A * DK:]

    def norms():
        q, k = ch.pop("q"), ch.pop("k")
        qss, kss = _head_sums([q * q, k * k], ones_bd)
        ch["qn"] = q * lax.rsqrt(qss + EPS) * (DK ** -0.5)
        ch["kn"] = k * lax.rsqrt(kss + EPS)

    def gram():
        ch["gram"] = []
        for g in range(N_GROUPS):
            lanes = slice(g * GROUP_W, (g + 1) * GROUP_W)
            lhs = jnp.concatenate([ch["kn"][:, lanes], ch["qn"][:, lanes]], axis=0).astype(BF16)
            ch["gram"].append(_dot_nt(lhs, _block_diag(ch["kn"][:, lanes], half_masks)))

    def decay():
        ba = pd_ref[rows, OFF_B:OFF_B + LANE]
        sig = _sigmoid(ba)
        glog = a_neg * _softplus(ba + dtb)
        ch["decay"] = {}
        for d in dirs:
            gcum = _const_lhs_split(tri[d], glog, 3)
            (beta,) = _stacked_const_rhs([sig], esel_ref[d, 0], 3)
            (gcc8,) = _stacked_const_rhs([gcum], esel_ref[d, 1], 3)
            ch["decay"][d] = (beta, gcc8)

    return ch, [conv, norms, gram, decay]


def _delta_kernel(nb, zero_init, n_prev, *refs):
    shared = nb == 1
    it = iter(refs)
    blocks = [(next(it), next(it), next(it))]
    if not shared:
        blocks.append((next(it), next(it), next(it)))
    cw_ref, par_ref, esel_ref, ones_ref = next(it), next(it), next(it), next(it)
    s0_ref = None if zero_init else next(it)
    sprev_ref = next(it) if n_prev else None
    o_refs = (next(it), next(it))
    sfin_ref = next(it)
    u_s, w_s, p_s, qg_s, kg_s, gl_s, st_s = (next(it) for _ in range(7))
    j = pl.program_id(1)

    ri = lax.broadcasted_iota(jnp.int32, (CHUNK, GROUP_W), 0)
    cj = lax.broadcasted_iota(jnp.int32, (CHUNK, GROUP_W), 1) % CHUNK
    ixj = ri ^ cj
    eye = ixj == 0
    br = lax.broadcasted_iota(jnp.int32, (GROUP_W, GROUP_W), 0) // CHUNK
    bc = lax.broadcasted_iota(jnp.int32, (GROUP_W, GROUP_W), 1) // CHUNK
    same_head = br == bc
    hl = lax.broadcasted_iota(jnp.int32, (CHUNK, LANE), 1) // DK
    half_masks = tuple(jnp.where(hl == h, 1.0, 0.0).astype(BF16) for h in range(HEADS_PER_LANE_TILE))
    ti = lax.broadcasted_iota(jnp.int32, (CHUNK, CHUNK), 0)
    tm = lax.broadcasted_iota(jnp.int32, (CHUNK, CHUNK), 1)
    ri8 = lax.broadcasted_iota(jnp.int32, (CHUNK, H_A * DK), 0)
    cj8 = lax.broadcasted_iota(jnp.int32, (CHUNK, H_A * DK), 1) % CHUNK
    incl = (ri >= cj, ri <= cj)
    strict = (ri > cj, ri < cj)
    tri = tuple(jnp.where(m, 1.0, 0.0).astype(BF16) for m in (tm <= ti, tm >= ti))
    eye8 = ri8 == cj8
    last_row = (CHUNK - 1, 0)

    @pl.when(j == 0)
    def _():
        for d in range(2):
            for g in range(N_GROUPS):
                if zero_init:
                    st_s[d, g] = jnp.zeros((GROUP_W, GROUP_W), F32)
                else:
                    nat = s0_ref[0, d, g * GROUP_W:(g + 1) * GROUP_W, :]
                    st_s[d, g] = jnp.where(same_head, jnp.concatenate([nat] * HEAD_GROUP, axis=1), 0.0)

    cw = cw_ref[...]
    a_neg = -jnp.exp(par_ref[0:1, :])
    dtb = par_ref[1:2, :]
    pos = (j, nb - 1 - j)
    scan_order = (tuple(range(CHUNKS_PER_RB)), tuple(reversed(range(CHUNKS_PER_RB))))

    def unit_thunk(d, r, ch, g, units):
        def run():
            rows = slice(r * CHUNK, (r + 1) * CHUNK)
            lanes = slice(g * GROUP_W, (g + 1) * GROUP_W)
            beta, gcc8 = ch["decay"][d]
            gcr = jnp.sum(jnp.where(eye8, gcc8, 0.0), axis=0, keepdims=True)[:, lanes]
            qn, kn, be, gcc = ch["qn"][:, lanes], ch["kn"][:, lanes], beta[:, lanes], gcc8[:, lanes]
            kq = ch["gram"][g]
            dec = jnp.exp(jnp.where(incl[d], gcc - gcr, -1e30))
            a = jnp.where(strict[d], kq[0:CHUNK] * be * dec, 0.0)
            eg = jnp.exp(gcc)
            gcl = gcc[last_row[d]:last_row[d] + 1, :]
            units.append((d, rows, lanes, a, ch["v"][:, lanes] * be, kn * be * eg))
            p_s[d, rows, lanes] = (kq[CHUNK:2 * CHUNK] * dec).astype(BF16)
            qg_s[d, rows, lanes] = (qn * eg).astype(BF16)
            kg_s[d, rows, lanes] = (kn * jnp.exp(gcl - gcc)).astype(BF16)
            gl_s[d, r * 8:(r + 1) * 8, lanes] = jnp.broadcast_to(jnp.exp(gcl), (8, GROUP_W))
        return run

    def prep_thunks(units):
        thunks = []
        if shared:
            todo = [(0, r, (0, 1)) for r in range(CHUNKS_PER_RB)]
        else:
            todo = [(d, r, (d,)) for d in range(2) for r in scan_order[d]]
        for b, r, dirs in todo:
            ch, stages = _delta_chunk_stages(*blocks[b], r, pos[b] > 0, pos[b] < nb - 1, dirs, cw, a_neg, dtb,
                                             ones_ref[...], esel_ref, tri, half_masks)
            thunks += stages
            thunks += [unit_thunk(d, r, ch, g, units) for d in dirs for g in range(N_GROUPS)]
        return thunks

    def scan_thunk(d, c):
        def run():
            r = scan_order[d][c]
            rows = slice(r * CHUNK, (r + 1) * CHUNK)
            for g in range(N_GROUPS):
                lanes = slice(g * GROUP_W, (g + 1) * GROUP_W)
                s = st_s[d, g]
                wq = jnp.concatenate([w_s[d, rows, lanes], qg_s[d, rows, lanes]], axis=0)
                ws_qs = _dot(wq, s.astype(BF16))
                v_new = u_s[d, rows, lanes] - ws_qs[0:CHUNK]
                o = ws_qs[CHUNK:2 * CHUNK] + _dot(p_s[d, rows, lanes], _block_diag(v_new, half_masks))
                upd = _dot_tn(kg_s[d, rows, lanes], v_new.astype(BF16))
                st_s[d, g] = s * gl_s[d, r * 8:r * 8 + 1, lanes] + jnp.where(same_head, upd, 0.0)
                o_refs[d][rows, lanes] = o
        return run

    def solve(units):
        xs = [jnp.where(eye, 1.0, 0.0) - jnp.where(ixj == 1, un[3], 0.0) for un in units]
        for lvl in range(1, 6):
            t1s = [_dot(x.astype(BF16), _block_diag(jnp.where((ixj >> lvl) == 1, un[3], 0.0), half_masks))
                   for x, un in zip(xs, units)]
            xs = [x - _dot(t1.astype(BF16), _block_diag(x, half_masks)) for x, t1 in zip(xs, t1s)]
        for x, (d, rows, lanes, _, vb, kbe) in zip(xs, units):
            rhs = jnp.concatenate([_block_diag(vb, half_masks), _block_diag(kbe, half_masks)], axis=1)
            uw = _dot(x.astype(BF16), rhs)
            u_s[d, rows, lanes] = uw[:, 0:GROUP_W]
            w_s[d, rows, lanes] = uw[:, GROUP_W:2 * GROUP_W].astype(BF16)

    units = []
    for f in prep_thunks(units):
        f()
    solve(units)
    for c in range(CHUNKS_PER_RB):
        for d in range(2):
            scan_thunk(d, c)()

    @pl.when(j == nb - 1)
    def _():
        for d in range(2):
            for g in range(N_GROUPS):
                s = st_s[d, g]
                nat = s[:, 0:DV]
                for hh in range(1, HEAD_GROUP):
                    nat = nat + s[:, hh * DV:(hh + 1) * DV]
                sfin_ref[0, n_prev, d, g * GROUP_W:(g + 1) * GROUP_W, :] = nat
        for p in range(n_prev):
            sfin_ref[0, p] = sprev_ref[0, p]


def _delta_call(pd, conv_w, par, esel, ones_bd, s0, n_seq, nb, blk0, prev_states=None):
    zero_init = s0 is None
    n_prev = 0 if prev_states is None else prev_states.shape[1]
    n_blocks_all = T_ALL // DELTA_RB
    blk_of = (lambda s, j: blk0 + s * nb + j, lambda s, j: blk0 + s * nb + nb - 1 - j)
    in_specs, args = [], []
    for d in range(1 if nb == 1 else 2):
        prev, nxt = _halo_specs(QKV_W, DELTA_RB, n_blocks_all, blk_of[d])
        in_specs += [pl.BlockSpec((DELTA_RB, PD_W), lambda s, j, d=d: (blk_of[d](s, j), 0)), prev, nxt]
        args += [pd, pd, pd]
    in_specs += [
        pl.BlockSpec((3, QKV_W), lambda s, j: (0, 0)),
        pl.BlockSpec((8, LANE), lambda s, j: (0, 0)),
        pl.BlockSpec((2, 2, LANE, H_A * DK), lambda s, j: (0, 0, 0, 0)),
        pl.BlockSpec((GROUP_W, GROUP_W), lambda s, j: (0, 0)),
    ]
    args += [conv_w, par, esel, ones_bd]
    if not zero_init:
        in_specs.append(pl.BlockSpec((1, 2, H_A * DK, DV), lambda s, j: (s, 0, 0, 0)))
        args.append(s0)
    if n_prev:
        in_specs.append(pl.BlockSpec((1, n_prev, 2, H_A * DK, DV), lambda s, j: (s, 0, 0, 0, 0)))
        args.append(prev_states)
    rows = n_seq * nb * DELTA_RB
    dir_buf = lambda n, dt=F32: pltpu.VMEM((2, n, H_A * DK), dt)
    return pl.pallas_call(
        functools.partial(_delta_kernel, nb, zero_init, n_prev),
        grid=(n_seq, nb),
        in_specs=in_specs,
        out_specs=[pl.BlockSpec((DELTA_RB, W_A), lambda s, j: (s * nb + j, 0)),
                   pl.BlockSpec((DELTA_RB, W_A), lambda s, j: (s * nb + nb - 1 - j, 0)),
                   pl.BlockSpec((1, n_prev + 1, 2, H_A * DK, DV), lambda s, j: (s, 0, 0, 0, 0))],
        out_shape=[jax.ShapeDtypeStruct((rows, W_A), F32),
                   jax.ShapeDtypeStruct((rows, W_A), F32),
                   jax.ShapeDtypeStruct((n_seq, n_prev + 1, 2, H_A * DK, DV), F32)],
        scratch_shapes=[dir_buf(DELTA_RB)] + [dir_buf(DELTA_RB, BF16) for _ in range(4)]
        + [dir_buf(CHUNKS_PER_RB * 8), pltpu.VMEM((2, N_GROUPS, GROUP_W, GROUP_W), F32)],
        compiler_params=_cparams(("parallel", "arbitrary"), VMEM_LIMIT),
        name=f"deltanet_nb{nb}",
    )(*args)


FFN_TN = D_FF // 2


def _postmix_kernel(final_norm, tile0, ofc_ref, ofl_ref, obc_ref, obl_ref, z_ref, ybc_ref, ybl_ref,
                    ycc_ref, ycl_ref, pg_ref, x_ref, mod_ref, na_ref, ones_ref, wpa_ref, wpb_ref, wpc_ref,
                    wo_ref, g2_ref, wgu_ref, wdn_ref, nf_ref, out_ref):
    is_ctx = pl.program_id(0) + tile0 < N_CTX_TILES
    m = mod_ref[0]
    o = jnp.where(is_ctx, ofc_ref[...] + obc_ref[...], ofl_ref[...] + obl_ref[...])
    yb = jnp.where(is_ctx, ybc_ref[...], ybl_ref[...])
    yc = jnp.where(is_ctx, ycc_ref[...], ycl_ref[...])
    ms = _head_sums([o * o], ones_ref[...])[0] * (1.0 / DV)
    ya = (o * lax.rsqrt(ms + EPS) * na_ref[...]) * _silu(z_ref[...])
    merged = (pg_ref[:, 0:D_MODEL].astype(F32) * _dot(ya.astype(BF16), wpa_ref[0])
              + pg_ref[:, D_MODEL:2 * D_MODEL].astype(F32) * _dot(yb.astype(BF16), wpb_ref[0])
              + pg_ref[:, 2 * D_MODEL:3 * D_MODEL].astype(F32) * _dot(yc.astype(BF16), wpc_ref[0]))
    x = x_ref[...] + m[2:3] * _dot(merged.astype(BF16), wo_ref[0])

    h = _rms_mod(x, g2_ref[...], m[4:5], m[3:4]).astype(BF16)
    acc = None
    for c in range(0, D_FF, FFN_TN):
        gate = _dot(h, wgu_ref[0, :, c:c + FFN_TN])
        up = _dot(h, wgu_ref[0, :, D_FF + c:D_FF + c + FFN_TN])
        part = _dot((_silu(gate) * up).astype(BF16), wdn_ref[0, c:c + FFN_TN, :])
        acc = part if acc is None else acc + part
    xn = x + m[5:6] * acc
    if final_norm:
        ms = jnp.mean(xn * xn, axis=-1, keepdims=True)
        xn = xn * lax.rsqrt(ms + EPS) * nf_ref[...]
    out_ref[...] = xn


def _postmix_call(o_ctx, o_lat, pd, yb, yc, pg, x, mod3, na512, ones_bd, wpa, wpb, wpc, wo,
                  g2, wgu, wdn, nf, layer, final_norm, tile0=0, n_tiles=T_ALL // ROW_TILE):
    row = lambda i: (i + tile0, 0)
    const = lambda i: (0, 0)
    lyr = lambda i: (layer, 0, 0)
    assert W_A == W_B == W_C
    return pl.pallas_call(
        functools.partial(_postmix_kernel, final_norm, tile0),
        grid=(n_tiles,),
        in_specs=[
            *_ctx_lat_specs(W_A, tile0), *_ctx_lat_specs(W_A, tile0),
            pl.BlockSpec((ROW_TILE, W_A), lambda i: (i + tile0, OFF_Z // W_A)),
            *_ctx_lat_specs(W_B, tile0), *_ctx_lat_specs(W_C, tile0),
            pl.BlockSpec((ROW_TILE, 3 * D_MODEL), row),
            pl.BlockSpec((ROW_TILE, D_MODEL), row),
            pl.BlockSpec((1, 6, D_MODEL), lambda i: (_mod_row_block(i + tile0), 0, 0)),
            pl.BlockSpec((1, W_A), const),
            _resident((GROUP_W, GROUP_W), const),
            _resident((1, W_A, D_MODEL), lyr),
            _resident((1, W_B, D_MODEL), lyr),
            _resident((1, W_C, D_MODEL), lyr),
            _resident((1, D_MODEL, D_MODEL), lyr),
            pl.BlockSpec((1, D_MODEL), const),
            _resident((1, D_MODEL, 2 * D_FF), lyr),
            _resident((1, D_FF, D_MODEL), lyr),
            pl.BlockSpec((1, D_MODEL), const),
        ],
        out_specs=pl.BlockSpec((ROW_TILE, D_MODEL), lambda i: (i, 0)),
        out_shape=jax.ShapeDtypeStruct((n_tiles * ROW_TILE, D_MODEL), F32),
        compiler_params=_cparams(("parallel",), VMEM_LIMIT),
        name="postmix_final" if final_norm else "postmix",
    )(o_ctx[0], o_lat[0], o_ctx[1], o_lat[1], pd, yb[0], yb[1], yc[0], yc[1], pg, x, mod3, na512,
      ones_bd, wpa, wpb, wpc, wo, g2.reshape(1, D_MODEL), wgu, wdn, nf.reshape(1, D_MODEL))


TABLE_SPLIT = 64


def _grid_pos_embed(n_tokens):
    rows = n_tokens // GRID_W
    quarter = D_MODEL // 4
    omega = 1.0 / (10000.0 ** (jnp.arange(quarter, dtype=F32) / quarter))

    def emb(pos):
        a = pos[:, None] * omega[None, :]
        return jnp.concatenate([jnp.sin(a), jnp.cos(a)], axis=-1)

    e_row, e_col = lax.optimization_barrier((emb(jnp.arange(rows).astype(F32)),
                                             emb(jnp.arange(GRID_W).astype(F32))))
    return jnp.concatenate([jnp.repeat(e_row, GRID_W, axis=0), jnp.tile(e_col, (rows, 1))], axis=-1)


def _cos_nsin_tables(n, period):
    t = jnp.arange(n, dtype=jnp.int32)[None, :]

    def cs(r):
        ang = ((r * t) % period).astype(F32) * (2.0 * math.pi / period)
        return jnp.cos(ang), jnp.sin(ang)

    ca, sa = cs(jnp.arange(n // TABLE_SPLIT, dtype=jnp.int32)[:, None] * TABLE_SPLIT)
    cb, sb = cs(jnp.arange(TABLE_SPLIT, dtype=jnp.int32)[:, None])
    ca, sa, cb, sb = lax.optimization_barrier((ca, sa, cb, sb))
    ca, sa = ca[:, None, :], sa[:, None, :]
    cos = (ca * cb[None] - sa * sb[None]).reshape(n, n)
    nsin = (-(sa * cb[None] + ca * sb[None])).reshape(n, n)
    return cos.astype(BF16), nsin.astype(BF16)


def _hyena_positions(L):
    bands = (HY_EMB - 1) // 2
    t = jnp.linspace(0.0, 1.0, L, dtype=F32)[:, None]
    wpos = (2.0 * math.pi / L) * jnp.arange(L, dtype=F32)[:, None]
    fr = jnp.linspace(1e-4, bands - 1, bands, dtype=F32)[None, :]
    zpos = jnp.concatenate([t, jnp.cos(fr * wpos), -jnp.sin(fr * wpos)], axis=-1)
    zpos = jnp.pad(zpos, ((0, 0), (0, LANE - HY_EMB)))
    deltas = jnp.abs(jnp.linspace(math.log(HY_DECAY_TARGET) / HY_SLOW_PCT,
                                  math.log(HY_DECAY_TARGET) / HY_FAST_PCT, W_B, dtype=F32))
    window = jnp.exp(-t * deltas[None, :])
    return zpos, window


def _group_tables():
    r = jnp.arange(DC, dtype=jnp.int32)
    ang = ((r[:, None] * r[None, :]) % DC).astype(F32) * (2.0 * math.pi / DC)
    eye = jnp.eye(G_C, dtype=F32)
    return jnp.kron(eye, jnp.cos(ang)).astype(BF16), jnp.kron(eye, jnp.sin(ang)).astype(BF16)


def _head_tables():
    ones_bd = jnp.kron(jnp.eye(HEAD_GROUP, dtype=F32), jnp.ones((DK, DK), F32)).astype(BF16)
    lane = jnp.arange(LANE)[:, None]
    head = (jnp.arange(H_A * DK) // DK)[None, :]
    sel = []
    for d in range(2):
        sel.append(jnp.stack([(lane == d * H_A + head), (lane == 2 * H_A + d * H_A + head)]))
    esel = jnp.stack(sel).astype(BF16)
    return ones_bd, esel


def kernel(x_prompt, x_sample, state_delta, c, c_ctx, w_mod, b_mod, norm1_g, norm2_g, w_in, conv_qkv, a_log, dt_bias, norm_a, conv_hy, hy_w1, hy_b1, hy_freq, hy_w2, hy_b2, hy_w3, hy_bias, w_pa, w_pb, w_pc, w_o, w_gu, w_down, norm_f):
    assert x_prompt.shape == (N_CTX_SEQ, L_CTX, D_MODEL) and x_sample.shape == (N_LAT_SEQ, L_LAT, D_MODEL)
    st = jnp.pad(jnp.concatenate([c_ctx[None], c], axis=0).T, ((0, 0), (0, 8 - 1 - N_LAT_SEQ)))
    mod = _mod_call(st, w_mod, b_mod).reshape(DEPTH, 8, 6, D_MODEL)

    ones_bd, esel = _head_tables()
    bdc, bds = _group_tables()
    seqs = ((L_CTX, N_CTX_SEQ, 0), (L_LAT, N_LAT_SEQ, T_CTX // L_LAT))
    tables = {L: (_cos_nsin_tables(L, 2 * L), _cos_nsin_tables(L, L), _hyena_positions(L))
              for L, _, _ in seqs}

    w_in_b = jnp.swapaxes(w_in, 1, 2).astype(BF16)
    w_pa_b, w_pb_b, w_pc_b, w_o_b, w_gu_b, w_down_b = (
        w.astype(BF16) for w in (w_pa, w_pb, w_pc, w_o, w_gu, w_down))

    x = None
    ctx_states = None
    for l in range(DEPTH):
        mod3 = mod[l, 0:3]
        if l == 0:
            xs = (x_prompt.reshape(T_CTX, D_MODEL), x_sample.reshape(T_LAT, D_MODEL), _grid_pos_embed(L_LAT))
            pd, ph, pf, pg, x = _inproj_call(xs, mod3, norm1_g[l], w_in_b, l)
        else:
            pd, ph, pf, pg = _inproj_call((x,), mod3, norm1_g[l], w_in_b, l)

        par = jnp.zeros((8, LANE), F32)
        par = par.at[0, 2 * H_A:4 * H_A].set(a_log[l].reshape(-1))
        par = par.at[1, 2 * H_A:4 * H_A].set(dt_bias[l].reshape(-1))
        *o_ctx, ctx_states = _delta_call(pd, conv_qkv[l], par, esel, ones_bd, None,
                                         N_CTX_SEQ, L_CTX // DELTA_RB, 0, ctx_states)
        s0 = state_delta[:, l].astype(F32).reshape(N_LAT_SEQ, 2, H_A * DK, DV)
        *o_lat, _ = _delta_call(pd, conv_qkv[l], par, esel, ones_bd, s0,
                                N_LAT_SEQ, L_LAT // DELTA_RB, T_CTX // DELTA_RB)

        w1p = jnp.pad(hy_w1[l], ((0, LANE - HY_EMB), (0, 0)))
        yb, yc = [], []
        for L, n_seq, blk0 in seqs:
            (cos2, nsin2), (cos1, nsin1), (zpos, window) = tables[L]
            kspec = _filter_call(L, zpos, w1p, hy_b1[l][None], hy_freq[l][None], hy_w2[l],
                                 hy_b2[l][None], hy_w3[l], window, cos2, nsin2)
            bias = hy_bias[l][:, None, :]
            if L == HYENA_FT:
                assert blk0 == 0
                yb.append(_hyena_short_call(ph, conv_hy[l], kspec, bias, cos2, nsin2, n_seq, L))
                yc.append(_fnet_short_call(pf, cos1, nsin1, bdc, bds, n_seq, L))
            else:
                yb.append(_hyena_call(ph, conv_hy[l], kspec, bias, cos2, nsin2, n_seq, L, blk0, HYENA_FT))
                yc.append(_fnet_call(pf, cos1, nsin1, bdc, bds, n_seq, L, blk0))

        na512 = jnp.tile(norm_a[l], H_A)[None]
        post = functools.partial(_postmix_call, o_ctx, o_lat, pd, yb, yc, pg, x, mod3, na512, ones_bd,
                                 w_pa_b, w_pb_b, w_pc_b, w_o_b, norm2_g[l], w_gu_b, w_down_b, norm_f, l)
        if l < DEPTH - 1:
            x = post(False)
        else:
            y_prompt = post(True, 0, N_CTX_TILES).reshape(N_CTX_SEQ, L_CTX, D_MODEL)
            y_sample = post(True, N_CTX_TILES, N_LAT_TILES).reshape(N_LAT_SEQ, L_LAT, D_MODEL)

    new_state = ctx_states.reshape(N_CTX_SEQ, DEPTH, 2, H_A, DK, DV).astype(x_prompt.dtype)
    return (y_prompt, y_sample, new_state)
```

```python
import functools
import math

import jax
import jax.numpy as jnp
from jax import lax
from jax.experimental import pallas as pl
from jax.experimental.pallas import tpu as pltpu

F32 = jnp.float32
BF16 = jnp.bfloat16

D_MODEL = 1024
N_CTX_SEQ = 32
L_CTX = 256
DEPTH = 2
N_LAT_SEQ = 2
L_LAT = 2048
GRID_W = 64
EPS = 1e-6
H_A = 8
DK = 64
DV = 64
W_A = H_A * DV
QKV_W = 2 * H_A * DK + H_A * DV
CHUNK = 64
W_B = 512
HY_EMB = 33
HY_HID = 64
HY_DECAY_TARGET = 1e-2
HY_FAST_PCT = 0.3
HY_SLOW_PCT = 1.5
G_C = 8
DC = 64
W_C = G_C * DC
D_FF = ((8 * D_MODEL + 3 * 256 - 1) // (3 * 256)) * 256
OFF_Z = QKV_W
OFF_B = OFF_Z + W_A
OFF_A = OFF_B + 2 * H_A
OFF_HY = OFF_A + 2 * H_A
OFF_FN = OFF_HY + 3 * W_B
OFF_GATE = OFF_FN + W_C

T_CTX = N_CTX_SEQ * L_CTX
T_LAT = N_LAT_SEQ * L_LAT
T_ALL = T_CTX + T_LAT
ROW_TILE = 256
N_CTX_TILES = T_CTX // ROW_TILE
N_LAT_TILES = T_LAT // ROW_TILE
LANE = 128
PD_W = QKV_W + W_A + LANE
HEAD_GROUP = 4
GROUP_W = HEAD_GROUP * DK
N_GROUPS = H_A // HEAD_GROUP
VMEM_LIMIT = 56 * 1024 * 1024


def _cparams(sem, vmem=None):
    return pltpu.CompilerParams(dimension_semantics=sem, vmem_limit_bytes=vmem)


def _dot(a, b):
    return jnp.dot(a, b, preferred_element_type=F32)


def _dot_nt(a, b):
    return lax.dot_general(a, b, (((1,), (1,)), ((), ())), preferred_element_type=F32)


def _dot_tn(a, b):
    return lax.dot_general(a, b, (((0,), (0,)), ((), ())), preferred_element_type=F32)


def _split(a, n):
    parts = []
    rem = a
    for i in range(n):
        p = rem.astype(BF16)
        parts.append(p)
        if i + 1 < n:
            rem = rem - p.astype(F32)
    return parts


def _mm3(a, b):
    ah, al = _split(a, 2)
    bh, bl = _split(b, 2)
    return _dot(ah, bh) + (_dot(ah, bl) + _dot(al, bh))


def _sigmoid(x):
    return 1.0 / (1.0 + jnp.exp(-x))


def _silu(x):
    return x * _sigmoid(x)


def _softplus(x):
    return jnp.maximum(x, 0.0) + jnp.log(1.0 + jnp.exp(-jnp.abs(x)))


def _mod_row_block(i):
    per_lat = L_LAT // ROW_TILE
    return jnp.where(i < N_CTX_TILES, 0, 1 + (i - N_CTX_TILES) // per_lat)


def _ctx_tile(i):
    return jnp.minimum(i, N_CTX_TILES - 1)


def _lat_tile(i):
    return jnp.maximum(i - N_CTX_TILES, 0)


def _ctx_lat_specs(width, tile0=0):
    return (pl.BlockSpec((ROW_TILE, width), lambda i: (_ctx_tile(i + tile0), 0)),
            pl.BlockSpec((ROW_TILE, width), lambda i: (_lat_tile(i + tile0), 0)))


MOD_TN = 512


def _mod_kernel(st_ref, w_ref, b_ref, out_ref):
    s = _silu(st_ref[...])
    w = w_ref[0]
    rows = [jnp.sum(s[:, r:r + 1] * w, axis=0, keepdims=True) + b_ref[0] for r in range(3)]
    rows.append(jnp.zeros((5, MOD_TN), F32))
    out_ref[0] = jnp.concatenate(rows, axis=0)


def _mod_call(st, w_mod, b_mod):
    n6 = 6 * D_MODEL
    return pl.pallas_call(
        _mod_kernel,
        grid=(DEPTH, n6 // MOD_TN),
        in_specs=[
            pl.BlockSpec((D_MODEL, 8), lambda l, j: (0, 0)),
            pl.BlockSpec((1, D_MODEL, MOD_TN), lambda l, j: (l, 0, j)),
            pl.BlockSpec((1, 1, MOD_TN), lambda l, j: (l, 0, j)),
        ],
        out_specs=pl.BlockSpec((1, 8, MOD_TN), lambda l, j: (l, 0, j)),
        out_shape=jax.ShapeDtypeStruct((DEPTH, 8, n6), F32),
        compiler_params=_cparams(("parallel", "parallel")),
        name="adaln_mod",
    )(st, w_mod, b_mod.reshape(DEPTH, 1, n6))


INPROJ_TN = 512
INPROJ_WIDTHS = (PD_W, 3 * W_B, W_C, 3 * D_MODEL)
INPROJ_FEATURES = (OFF_HY, 3 * W_B, W_C, 3 * D_MODEL)


def _rms_mod(x, g, scale, shift):
    ms = jnp.mean(x * x, axis=-1, keepdims=True)
    return (x * lax.rsqrt(ms + EPS) * g) * (1.0 + scale) + shift


def _inproj_kernel(first, *refs):
    if first:
        (xc_ref, xl_ref, pos_ref, mod_ref, g_ref, w_ref, pd_ref, ph_ref, pf_ref, pg_ref, x_ref) = refs
        x = jnp.where(pl.program_id(0) < N_CTX_TILES, xc_ref[...], xl_ref[...] + pos_ref[...])
        x_ref[...] = x
    else:
        (xin_ref, mod_ref, g_ref, w_ref, pd_ref, ph_ref, pf_ref, pg_ref) = refs
        x = xin_ref[...]
    m = mod_ref[0]
    h = _rms_mod(x, g_ref[...], m[1:2], m[0:1]).astype(BF16)
    row0 = 0
    for o_ref, n_feat in zip((pd_ref, ph_ref, pf_ref, pg_ref), INPROJ_FEATURES):
        n = o_ref.shape[1]
        for c in range(0, n, INPROJ_TN):
            e = min(c + INPROJ_TN, n)
            ef = min(e, n_feat)
            y = _dot_nt(h, w_ref[0, row0 + c:row0 + ef, :])
            if ef < e:
                y = jnp.concatenate([y, jnp.zeros((y.shape[0], e - ef), F32)], axis=1)
            if o_ref is pg_ref:
                y = _sigmoid(y)
            o_ref[:, c:e] = y.astype(o_ref.dtype)
        row0 += n_feat


def _resident(shape, index_map):
    return pl.BlockSpec(shape, index_map, pipeline_mode=pl.Buffered(1))


def _inproj_call(xs, mod3, g, w_all, layer):
    first = len(xs) == 3
    widths = INPROJ_WIDTHS
    row = lambda i: (i, 0)
    const = lambda i: (0, 0)
    if first:
        per_lat = L_LAT // ROW_TILE
        x_specs = list(_ctx_lat_specs(D_MODEL)) + [
            pl.BlockSpec((ROW_TILE, D_MODEL), lambda i: (_lat_tile(i) % per_lat, 0))]
    else:
        x_specs = [pl.BlockSpec((ROW_TILE, D_MODEL), row)]
    out_widths = widths + ((D_MODEL,) if first else ())
    out_dtypes = (F32, F32, F32, BF16) + ((F32,) if first else ())
    return pl.pallas_call(
        functools.partial(_inproj_kernel, first),
        grid=(T_ALL // ROW_TILE,),
        in_specs=x_specs + [
            pl.BlockSpec((1, 6, D_MODEL), lambda i: (_mod_row_block(i), 0, 0)),
            pl.BlockSpec((1, D_MODEL), const),
            _resident((1, sum(INPROJ_FEATURES), D_MODEL), lambda i: (layer, 0, 0)),
        ],
        out_specs=[pl.BlockSpec((ROW_TILE, w), row) for w in out_widths],
        out_shape=[jax.ShapeDtypeStruct((T_ALL, w), dt) for w, dt in zip(out_widths, out_dtypes)],
        compiler_params=_cparams(("parallel",), VMEM_LIMIT),
        name="inproj_first" if first else "inproj",
    )(*xs, mod3, g.reshape(1, D_MODEL), w_all)


def _conv3_rows(cur, prev_row, next_row, w):
    n = cur.shape[0]
    ridx = lax.broadcasted_iota(jnp.int32, cur.shape, 0)
    up = jnp.where(ridx == 0, prev_row, pltpu.roll(cur, 1, 0))
    dn = jnp.where(ridx == n - 1, next_row, pltpu.roll(cur, n - 1, 0))
    return up * w[0:1] + cur * w[1:2] + dn * w[2:3]


def _halo_specs(width, rows_per_block, n_row_blocks, blk_of):
    per = rows_per_block // 8
    last = n_row_blocks * per - 1
    prev = pl.BlockSpec((8, width), lambda *a: (jnp.maximum(blk_of(*a) * per - 1, 0), 0))
    nxt = pl.BlockSpec((8, width), lambda *a: (jnp.minimum((blk_of(*a) + 1) * per, last), 0))
    return prev, nxt


FILT_RT = 256


def _alternating_sum(x):
    t = lax.broadcasted_iota(jnp.int32, x.shape, 0)
    return jnp.sum(jnp.where(t % 2 == 0, x, -x), axis=0, keepdims=True)


def _filter_kernel(L, zpos_ref, w1_ref, b1_ref, fq_ref, w2_ref, b2_ref, w3_ref, win_ref,
                   cos_ref, nsin_ref, k_ref, hs_s, hm_s, krl_s):
    rt = pl.program_id(1)

    @pl.when(rt == 0)
    def _():
        fq = fq_ref[...]
        alt_acc = jnp.zeros((1, W_B), F32)
        for r0 in range(0, L, FILT_RT):
            rows = slice(r0, r0 + FILT_RT)
            h = jnp.sin(fq * (_mm3(zpos_ref[rows, :], w1_ref[...]) + b1_ref[...]))
            h = jnp.sin(fq * (_mm3(h, w2_ref[...]) + b2_ref[...]))
            hf = _mm3(h, w3_ref[...])
            win = win_ref[rows, :]
            fw = hf[:, 0:W_B] * win
            bw = hf[:, W_B:2 * W_B] * win
            hsum = fw + bw
            hs_s[rows, :] = hsum.astype(BF16)
            hm_s[rows, :] = (fw - bw).astype(BF16)
            alt_acc = alt_acc + _alternating_sum(hsum)
        krl_s[...] = jnp.broadcast_to(alt_acc, krl_s.shape)

    p1 = _dot(cos_ref[...], hs_s[...])
    p2 = _dot(nsin_ref[...], hm_s[...])
    first = (rt * FILT_RT + lax.broadcasted_iota(jnp.int32, p1.shape, 0)) == 0
    k_ref[0, 0] = p1
    k_ref[0, 1] = jnp.where(first, krl_s[0:1, :], p1)
    k_ref[0, 2] = jnp.where(first, 0.0, p2)


def _filter_call(L, zpos, w1p, b1, fq, w2, b2, w3, win, cos, nsin):
    nrt = L // FILT_RT
    c2 = lambda o, r: (0, 0)
    return pl.pallas_call(
        functools.partial(_filter_kernel, L),
        grid=(2, nrt),
        in_specs=[
            pl.BlockSpec((L, LANE), c2),
            pl.BlockSpec((LANE, HY_HID), c2),
            pl.BlockSpec((1, HY_HID), c2),
            pl.BlockSpec((1, HY_HID), c2),
            pl.BlockSpec((HY_HID, HY_HID), c2),
            pl.BlockSpec((1, HY_HID), c2),
            pl.BlockSpec((HY_HID, 2 * W_B), lambda o, r: (0, o)),
            pl.BlockSpec((L, W_B), c2),
            pl.BlockSpec((FILT_RT, L), lambda o, r: (r, 0)),
            pl.BlockSpec((FILT_RT, L), lambda o, r: (r, 0)),
        ],
        out_specs=pl.BlockSpec((1, 3, FILT_RT, W_B), lambda o, r: (o, 0, r, 0)),
        out_shape=jax.ShapeDtypeStruct((2, 3, L, W_B), F32),
        scratch_shapes=[pltpu.VMEM((L, W_B), BF16), pltpu.VMEM((L, W_B), BF16),
                        pltpu.VMEM((8, W_B), F32)],
        compiler_params=_cparams(("parallel", "arbitrary"), VMEM_LIMIT),
        name=f"hyena_filter_{L}",
    )(zpos, w1p, b1, fq, w2, b2, w3, win, cos, nsin)


HYENA_FT = 256

HYENA_CONV_ROWS = 256


HYENA_CH = 256


def _hyena_kernel(L, ft, x1_ref, x2_ref, v_ref, cw1_ref, cw2_ref, cwv_ref, k_ref, bias_ref, cos_ref, nsin_ref,
                  out_ref, gate_s, zf_s, zb_s, acc_s):
    o = pl.program_id(2)

    @pl.when(o == 0)
    def _():
        for src, cw_ref, dst in ((x1_ref, cw1_ref, 0), (x2_ref, cw2_ref, 1), (v_ref, cwv_ref, None)):
            cw = cw_ref[...]
            for r0 in range(0, L, HYENA_CONV_ROWS):
                r1 = r0 + HYENA_CONV_ROWS
                prev_row = src[r0 - 1:r0, :] if r0 > 0 else 0.0
                next_row = src[r1:r1 + 1, :] if r1 < L else 0.0
                uc = _conv3_rows(src[r0:r1, :], prev_row, next_row, cw)
                if dst is None:
                    zf_s[r0:r1, :] = uc
                    zb_s[r0:r1, :] = uc.astype(BF16)
                else:
                    gate_s[dst, r0:r1, :] = uc

    zb = zb_s[...]
    nyq = _alternating_sum(zf_s[...])
    for f in range(L // ft):
        rows = slice(f * ft, (f + 1) * ft)
        top = _dot(cos_ref[rows, :], zb)
        bot = _dot(nsin_ref[rows, :], zb)
        if f == 0:
            first = lax.broadcasted_iota(jnp.int32, top.shape, 0) == 0
            bot = jnp.where(first, nyq, bot)
        krt = k_ref[0, 0, rows, :]
        krb = k_ref[0, 1, rows, :]
        ki = k_ref[0, 2, rows, :]
        yt = top * krt - bot * ki
        yb = top * ki + bot * krb
        ytw = jnp.where(first, 0.5 * yt, yt) if f == 0 else yt
        part = _dot(cos_ref[:, rows], ytw.astype(BF16)) + _dot(nsin_ref[:, rows], yb.astype(BF16))
        if f == 0:
            t = lax.broadcasted_iota(jnp.int32, part.shape, 0)
            acc_s[...] = part + jnp.where(t % 2 == 0, 0.5, -0.5) * yb[0:1, :]
        else:
            acc_s[...] += part

    znew = gate_s[o] * (acc_s[...] * (1.0 / L) + bias_ref[0] * zf_s[...])
    zf_s[...] = znew
    zb_s[...] = znew.astype(BF16)
    out_ref[...] = znew.astype(out_ref.dtype)


def _hyena_call(ph, conv_w, kspec, bias, cos, nsin, n_seq, L, row_blk0, ft):
    nch = W_B // HYENA_CH
    once = pl.Buffered(1)
    col = lambda part: (lambda s, c, o: (row_blk0 + s, part * nch + c))
    cwcol = lambda part: (lambda s, c, o: (0, part * nch + c))
    return pl.pallas_call(
        functools.partial(_hyena_kernel, L, ft),
        grid=(n_seq, nch, 2),
        in_specs=[pl.BlockSpec((L, HYENA_CH), col(part), pipeline_mode=once) for part in range(3)]
        + [pl.BlockSpec((3, HYENA_CH), cwcol(part)) for part in range(3)]
        + [
            pl.BlockSpec((1, 3, L, HYENA_CH), lambda s, c, o: (o, 0, 0, c)),
            pl.BlockSpec((1, 1, HYENA_CH), lambda s, c, o: (o, 0, c)),
            pl.BlockSpec((L, L), lambda s, c, o: (0, 0), pipeline_mode=once),
            pl.BlockSpec((L, L), lambda s, c, o: (0, 0), pipeline_mode=once),
        ],
        out_specs=pl.BlockSpec((L, HYENA_CH), lambda s, c, o: (s, c)),
        out_shape=jax.ShapeDtypeStruct((n_seq * L, W_B), BF16),
        scratch_shapes=[pltpu.VMEM((2, L, HYENA_CH), F32), pltpu.VMEM((L, HYENA_CH), F32),
                        pltpu.VMEM((L, HYENA_CH), BF16), pltpu.VMEM((L, HYENA_CH), F32)],
        compiler_params=_cparams(("parallel", "parallel", "arbitrary"), VMEM_LIMIT),
        name=f"hyena_conv_{L}",
    )(ph, ph, ph, conv_w, conv_w, conv_w, kspec, bias, cos, nsin)


HYENA_SHORT_SB = 4


def _hyena_short_kernel(L, ph_ref, cw_ref, k_ref, bias_ref, cos_ref, nsin_ref, out_ref):
    cos = cos_ref[...]
    nsin = nsin_ref[...]
    cw = cw_ref[...]
    t = lax.broadcasted_iota(jnp.int32, (L, W_B), 0)
    first = t == 0
    alt_half = jnp.where(t % 2 == 0, 0.5, -0.5)
    rows = [slice(s * L, (s + 1) * L) for s in range(HYENA_SHORT_SB)]
    ucs = [_conv3_rows(ph_ref[r, :], 0.0, 0.0, cw) for r in rows]
    zs = [uc[:, 2 * W_B:3 * W_B] for uc in ucs]
    for o in range(2):
        krt, krb, ki = k_ref[o, 0], k_ref[o, 1], k_ref[o, 2]
        zbs = [z.astype(BF16) for z in zs]
        tops = [_dot(cos, zb) for zb in zbs]
        bots = [jnp.where(first, _alternating_sum(z), _dot(nsin, zb)) for z, zb in zip(zs, zbs)]
        yts = [top * krt - bot * ki for top, bot in zip(tops, bots)]
        ybs = [top * ki + bot * krb for top, bot in zip(tops, bots)]
        accs = [_dot(cos, jnp.where(first, 0.5 * yt, yt).astype(BF16)) + _dot(nsin, yb.astype(BF16))
                + alt_half * yb[0:1, :] for yt, yb in zip(yts, ybs)]
        zs = [uc[:, o * W_B:(o + 1) * W_B] * (acc * (1.0 / L) + bias_ref[o] * z)
              for uc, acc, z in zip(ucs, accs, zs)]
    for r, z in zip(rows, zs):
        out_ref[r, :] = z.astype(out_ref.dtype)


def _hyena_short_call(ph, conv_w, kspec, bias, cos, nsin, n_seq, L):
    rows = HYENA_SHORT_SB * L
    return pl.pallas_call(
        functools.partial(_hyena_short_kernel, L),
        grid=(n_seq // HYENA_SHORT_SB,),
        in_specs=[
            pl.BlockSpec((rows, 3 * W_B), lambda i: (i, 0)),
            pl.BlockSpec((3, 3 * W_B), lambda i: (0, 0)),
            pl.BlockSpec((2, 3, L, W_B), lambda i: (0, 0, 0, 0)),
            pl.BlockSpec((2, 1, W_B), lambda i: (0, 0, 0)),
            pl.BlockSpec((L, L), lambda i: (0, 0)),
            pl.BlockSpec((L, L), lambda i: (0, 0)),
        ],
        out_specs=pl.BlockSpec((rows, W_B), lambda i: (i, 0)),
        out_shape=jax.ShapeDtypeStruct((n_seq * L, W_B), BF16),
        compiler_params=_cparams(("parallel",), VMEM_LIMIT),
        name=f"hyena_conv_{L}",
    )(ph, conv_w, kspec, bias, cos, nsin)


FNET_RT = 256


def _fnet_kernel(L, x_ref, cos_ref, nsin_ref, bdc_ref, bds_ref, out_ref, xc_s, xs_s):
    r = pl.program_id(1)

    @pl.when(r == 0)
    def _():
        for r0 in range(0, L, FNET_RT):
            xb = x_ref[r0:r0 + FNET_RT, :].astype(BF16)
            xc_s[r0:r0 + FNET_RT, :] = _dot(xb, bdc_ref[...]).astype(BF16)
            xs_s[r0:r0 + FNET_RT, :] = _dot(xb, bds_ref[...]).astype(BF16)

    y = _dot(cos_ref[...], xc_s[...]) + _dot(nsin_ref[...], xs_s[...])
    out_ref[...] = (y * (1.0 / math.sqrt(DC * L))).astype(out_ref.dtype)


def _fnet_call(pf, cos, nsin, bdc, bds, n_seq, L, row_blk0):
    nrt = L // FNET_RT
    return pl.pallas_call(
        functools.partial(_fnet_kernel, L),
        grid=(n_seq, nrt),
        in_specs=[
            pl.BlockSpec((L, W_C), lambda s, r: (row_blk0 + s, 0)),
            pl.BlockSpec((FNET_RT, L), lambda s, r: (r, 0)),
            pl.BlockSpec((FNET_RT, L), lambda s, r: (r, 0)),
            pl.BlockSpec((W_C, W_C), lambda s, r: (0, 0)),
            pl.BlockSpec((W_C, W_C), lambda s, r: (0, 0)),
        ],
        out_specs=pl.BlockSpec((FNET_RT, W_C), lambda s, r: (s * nrt + r, 0)),
        out_shape=jax.ShapeDtypeStruct((n_seq * L, W_C), BF16),
        scratch_shapes=[pltpu.VMEM((L, W_C), BF16), pltpu.VMEM((L, W_C), BF16)],
        compiler_params=_cparams(("parallel", "arbitrary"), VMEM_LIMIT),
        name=f"fnet_{L}",
    )(pf, cos, nsin, bdc, bds)


FNET_SHORT_SB = 4


def _fnet_short_kernel(L, x_ref, cos_ref, nsin_ref, bdc_ref, bds_ref, out_ref):
    xb = x_ref[...].astype(BF16)
    xc = _dot(xb, bdc_ref[...]).astype(BF16)
    xs = _dot(xb, bds_ref[...]).astype(BF16)
    for s in range(FNET_SHORT_SB):
        rows = slice(s * L, (s + 1) * L)
        y = _dot(cos_ref[...], xc[rows]) + _dot(nsin_ref[...], xs[rows])
        out_ref[rows, :] = (y * (1.0 / math.sqrt(DC * L))).astype(out_ref.dtype)


def _fnet_short_call(pf, cos, nsin, bdc, bds, n_seq, L):
    rows = FNET_SHORT_SB * L
    const = lambda i: (0, 0)
    return pl.pallas_call(
        functools.partial(_fnet_short_kernel, L),
        grid=(n_seq // FNET_SHORT_SB,),
        in_specs=[pl.BlockSpec((rows, W_C), lambda i: (i, 0)),
                  pl.BlockSpec((L, L), const), pl.BlockSpec((L, L), const),
                  pl.BlockSpec((W_C, W_C), const), pl.BlockSpec((W_C, W_C), const)],
        out_specs=pl.BlockSpec((rows, W_C), lambda i: (i, 0)),
        out_shape=jax.ShapeDtypeStruct((n_seq * L, W_C), BF16),
        compiler_params=_cparams(("parallel",), VMEM_LIMIT),
        name=f"fnet_{L}",
    )(pf, cos, nsin, bdc, bds)


DELTA_RB = 256
CHUNKS_PER_RB = DELTA_RB // CHUNK


HEADS_PER_LANE_TILE = LANE // DK


def _block_diag(y, half_masks):
    yb = y.astype(BF16)
    zero = jnp.zeros((CHUNK, LANE), BF16)
    row_blocks = []
    for h in range(HEAD_GROUP):
        tile = h // HEADS_PER_LANE_TILE
        piece = yb[:, tile * LANE:(tile + 1) * LANE] * half_masks[h % HEADS_PER_LANE_TILE]
        row_blocks.append(jnp.concatenate(
            [piece if t == tile else zero for t in range(GROUP_W // LANE)], axis=1))
    return jnp.concatenate(row_blocks, axis=0)


def _stacked_const_rhs(arrs, c, n):
    m = arrs[0].shape[0]
    parts = [p for a in arrs for p in _split(a, n)]
    y = _dot(jnp.concatenate(parts, axis=0), c)
    outs = []
    for i in range(len(arrs)):
        acc = y[i * n * m:(i * n + 1) * m]
        for t in range(1, n):
            acc = acc + y[(i * n + t) * m:(i * n + t + 1) * m]
        outs.append(acc)
    return outs


def _head_sums(arrs, ones_group):
    groups = [_stacked_const_rhs([a[:, g * GROUP_W:(g + 1) * GROUP_W] for a in arrs], ones_group, 2)
              for g in range(N_GROUPS)]
    return [jnp.concatenate([groups[g][i] for g in range(N_GROUPS)], axis=1) for i in range(len(arrs))]


def _const_lhs_split(c, b, n):
    w = b.shape[1]
    y = _dot(c, jnp.concatenate(_split(b, n), axis=1))
    acc = y[:, 0:w]
    for t in range(1, n):
        acc = acc + y[:, t * w:(t + 1) * w]
    return acc


def _delta_chunk_stages(pd_ref, prev_ref, next_ref, r, has_prev, has_next, dirs, cw, a_neg, dtb, ones_bd,
                        esel_ref, tri, half_masks):
    ch = {}
    rows = slice(r * CHUNK, (r + 1) * CHUNK)

    def conv():
        cur = pd_ref[rows, 0:QKV_W]
        if r == 0:
            prev_row = jnp.where(has_prev, prev_ref[7:8, :], 0.0)
        else:
            prev_row = pd_ref[r * CHUNK - 1:r * CHUNK, 0:QKV_W]
        if r == CHUNKS_PER_RB - 1:
            next_row = jnp.where(has_next, next_ref[0:1, :], 0.0)
        else:
            next_row = pd_ref[(r + 1) * CHUNK:(r + 1) * CHUNK + 1, 0:QKV_W]
        qkv = _silu(_conv3_rows(cur, prev_row, next_row, cw))
        ch["q"] = qkv[:, 0:H_A * DK]
        ch["k"] = qkv[:, H_A * DK:2 * H_A * DK]
        ch["v"] = qkv[:, 2 * H_A * DK:]

    def norms():
        q, k = ch.pop("q"), ch.pop("k")
        qss, kss = _head_sums([q * q, k * k], ones_bd)
        ch["qn"] = q * lax.rsqrt(qss + EPS) * (DK ** -0.5)
        ch["kn"] = k * lax.rsqrt(kss + EPS)

    def gram():
        ch["gram"] = []
        for g in range(N_GROUPS):
            lanes = slice(g * GROUP_W, (g + 1) * GROUP_W)
            lhs = jnp.concatenate([ch["kn"][:, lanes], ch["qn"][:, lanes]], axis=0).astype(BF16)
            ch["gram"].append(_dot_nt(lhs, _block_diag(ch["kn"][:, lanes], half_masks)))

    def decay():
        ba = pd_ref[rows, OFF_B:OFF_B + LANE]
        sig = _sigmoid(ba)
        glog = a_neg * _softplus(ba + dtb)
        ch["decay"] = {}
        for d in dirs:
            gcum = _const_lhs_split(tri[d], glog, 3)
            (beta,) = _stacked_const_rhs([sig], esel_ref[d, 0], 3)
            (gcc8,) = _stacked_const_rhs([gcum], esel_ref[d, 1], 3)
            ch["decay"][d] = (beta, gcc8)

    return ch, [conv, norms, gram, decay]


def _delta_kernel(nb, zero_init, n_prev, *refs):
    shared = nb == 1
    it = iter(refs)
    blocks = [(next(it), next(it), next(it))]
    if not shared:
        blocks.append((next(it), next(it), next(it)))
    cw_ref, par_ref, esel_ref, ones_ref = next(it), next(it), next(it), next(it)
    s0_ref = None if zero_init else next(it)
    sprev_ref = next(it) if n_prev else None
    o_refs = (next(it), next(it))
    sfin_ref = next(it)
    u_s, w_s, p_s, qg_s, kg_s, gl_s, st_s = (next(it) for _ in range(7))
    j = pl.program_id(1)

    ri = lax.broadcasted_iota(jnp.int32, (CHUNK, GROUP_W), 0)
    cj = lax.broadcasted_iota(jnp.int32, (CHUNK, GROUP_W), 1) % CHUNK
    ixj = ri ^ cj
    eye = ixj == 0
    br = lax.broadcasted_iota(jnp.int32, (GROUP_W, GROUP_W), 0) // CHUNK
    bc = lax.broadcasted_iota(jnp.int32, (GROUP_W, GROUP_W), 1) // CHUNK
    same_head = br == bc
    hl = lax.broadcasted_iota(jnp.int32, (CHUNK, LANE), 1) // DK
    half_masks = tuple(jnp.where(hl == h, 1.0, 0.0).astype(BF16) for h in range(HEADS_PER_LANE_TILE))
    ti = lax.broadcasted_iota(jnp.int32, (CHUNK, CHUNK), 0)
    tm = lax.broadcasted_iota(jnp.int32, (CHUNK, CHUNK), 1)
    ri8 = lax.broadcasted_iota(jnp.int32, (CHUNK, H_A * DK), 0)
    cj8 = lax.broadcasted_iota(jnp.int32, (CHUNK, H_A * DK), 1) % CHUNK
    incl = (ri >= cj, ri <= cj)
    strict = (ri > cj, ri < cj)
    tri = tuple(jnp.where(m, 1.0, 0.0).astype(BF16) for m in (tm <= ti, tm >= ti))
    eye8 = ri8 == cj8
    last_row = (CHUNK - 1, 0)

    @pl.when(j == 0)
    def _():
        for d in range(2):
            for g in range(N_GROUPS):
                if zero_init:
                    st_s[d, g] = jnp.zeros((GROUP_W, GROUP_W), F32)
                else:
                    nat = s0_ref[0, d, g * GROUP_W:(g + 1) * GROUP_W, :]
                    st_s[d, g] = jnp.where(same_head, jnp.concatenate([nat] * HEAD_GROUP, axis=1), 0.0)

    cw = cw_ref[...]
    a_neg = -jnp.exp(par_ref[0:1, :])
    dtb = par_ref[1:2, :]
    pos = (j, nb - 1 - j)
    scan_order = (tuple(range(CHUNKS_PER_RB)), tuple(reversed(range(CHUNKS_PER_RB))))

    def unit_thunk(d, r, ch, g, units):
        def run():
            rows = slice(r * CHUNK, (r + 1) * CHUNK)
            lanes = slice(g * GROUP_W, (g + 1) * GROUP_W)
            beta, gcc8 = ch["decay"][d]
            gcr = jnp.sum(jnp.where(eye8, gcc8, 0.0), axis=0, keepdims=True)[:, lanes]
            qn, kn, be, gcc = ch["qn"][:, lanes], ch["kn"][:, lanes], beta[:, lanes], gcc8[:, lanes]
            kq = ch["gram"][g]
            dec = jnp.exp(jnp.where(incl[d], gcc - gcr, -1e30))
            a = jnp.where(strict[d], kq[0:CHUNK] * be * dec, 0.0)
            eg = jnp.exp(gcc)
            gcl = gcc[last_row[d]:last_row[d] + 1, :]
            units.append((d, rows, lanes, a, ch["v"][:, lanes] * be, kn * be * eg))
            p_s[d, rows, lanes] = (kq[CHUNK:2 * CHUNK] * dec).astype(BF16)
            qg_s[d, rows, lanes] = (qn * eg).astype(BF16)
            kg_s[d, rows, lanes] = (kn * jnp.exp(gcl - gcc)).astype(BF16)
            gl_s[d, r * 8:(r + 1) * 8, lanes] = jnp.broadcast_to(jnp.exp(gcl), (8, GROUP_W))
        return run

    def prep_thunks(units):
        thunks = []
        if shared:
            todo = [(0, r, (0, 1)) for r in range(CHUNKS_PER_RB)]
        else:
            todo = [(d, r, (d,)) for d in range(2) for r in scan_order[d]]
        for b, r, dirs in todo:
            ch, stages = _delta_chunk_stages(*blocks[b], r, pos[b] > 0, pos[b] < nb - 1, dirs, cw, a_neg, dtb,
                                             ones_ref[...], esel_ref, tri, half_masks)
            thunks += stages
            thunks += [unit_thunk(d, r, ch, g, units) for d in dirs for g in range(N_GROUPS)]
        return thunks

    def scan_thunk(d, c):
        def run():
            r = scan_order[d][c]
            rows = slice(r * CHUNK, (r + 1) * CHUNK)
            for g in range(N_GROUPS):
                lanes = slice(g * GROUP_W, (g + 1) * GROUP_W)
                s = st_s[d, g]
                wq = jnp.concatenate([w_s[d, rows, lanes], qg_s[d, rows, lanes]], axis=0)
                ws_qs = _dot(wq, s.astype(BF16))
                v_new = u_s[d, rows, lanes] - ws_qs[0:CHUNK]
                o = ws_qs[CHUNK:2 * CHUNK] + _dot(p_s[d, rows, lanes], _block_diag(v_new, half_masks))
                upd = _dot_tn(kg_s[d, rows, lanes], v_new.astype(BF16))
                st_s[d, g] = s * gl_s[d, r * 8:r * 8 + 1, lanes] + jnp.where(same_head, upd, 0.0)
                o_refs[d][rows, lanes] = o
        return run

    def solve(units):
        r0s = [jnp.where(ixj == 1, un[3], 0.0) for un in units]
        zs = [un[3] - _dot(un[3].astype(BF16), _block_diag(r0, half_masks)) for un, r0 in zip(units, r0s)]
        xs = [jnp.where(eye, 1.0, 0.0) - r0 for r0 in r0s]
        for lvl in range(1, 6):
            links = [_block_diag(jnp.where((ixj >> lvl) == 1, z, 0.0), half_masks) for z in zs]
            if lvl < 5:
                prods = [_dot(jnp.concatenate([z, x], axis=0).astype(BF16), lk)
                         for z, x, lk in zip(zs, xs, links)]
                zs = [z - p[0:CHUNK] for z, p in zip(zs, prods)]
                xs = [x - p[CHUNK:2 * CHUNK] for x, p in zip(xs, prods)]
            else:
                xs = [x - _dot(x.astype(BF16), lk) for x, lk in zip(xs, links)]
        for x, (d, rows, lanes, _, vb, kbe) in zip(xs, units):
            rhs = jnp.concatenate([_block_diag(vb, half_masks), _block_diag(kbe, half_masks)], axis=1)
            uw = _dot(x.astype(BF16), rhs)
            u_s[d, rows, lanes] = uw[:, 0:GROUP_W]
            w_s[d, rows, lanes] = uw[:, GROUP_W:2 * GROUP_W].astype(BF16)

    units = []
    for f in prep_thunks(units):
        f()
    solve(units)
    for c in range(CHUNKS_PER_RB):
        for d in range(2):
            scan_thunk(d, c)()

    @pl.when(j == nb - 1)
    def _():
        for d in range(2):
            for g in range(N_GROUPS):
                s = st_s[d, g]
                nat = s[:, 0:DV]
                for hh in range(1, HEAD_GROUP):
                    nat = nat + s[:, hh * DV:(hh + 1) * DV]
                sfin_ref[0, n_prev, d, g * GROUP_W:(g + 1) * GROUP_W, :] = nat
        for p in range(n_prev):
            sfin_ref[0, p] = sprev_ref[0, p]


def _delta_call(pd, conv_w, par, esel, ones_bd, s0, n_seq, nb, blk0, prev_states=None):
    zero_init = s0 is None
    n_prev = 0 if prev_states is None else prev_states.shape[1]
    n_blocks_all = T_ALL // DELTA_RB
    blk_of = (lambda s, j: blk0 + s * nb + j, lambda s, j: blk0 + s * nb + nb - 1 - j)
    in_specs, args = [], []
    for d in range(1 if nb == 1 else 2):
        prev, nxt = _halo_specs(QKV_W, DELTA_RB, n_blocks_all, blk_of[d])
        in_specs += [pl.BlockSpec((DELTA_RB, PD_W), lambda s, j, d=d: (blk_of[d](s, j), 0)), prev, nxt]
        args += [pd, pd, pd]
    in_specs += [
        pl.BlockSpec((3, QKV_W), lambda s, j: (0, 0)),
        pl.BlockSpec((8, LANE), lambda s, j: (0, 0)),
        pl.BlockSpec((2, 2, LANE, H_A * DK), lambda s, j: (0, 0, 0, 0)),
        pl.BlockSpec((GROUP_W, GROUP_W), lambda s, j: (0, 0)),
    ]
    args += [conv_w, par, esel, ones_bd]
    if not zero_init:
        in_specs.append(pl.BlockSpec((1, 2, H_A * DK, DV), lambda s, j: (s, 0, 0, 0)))
        args.append(s0)
    if n_prev:
        in_specs.append(pl.BlockSpec((1, n_prev, 2, H_A * DK, DV), lambda s, j: (s, 0, 0, 0, 0)))
        args.append(prev_states)
    rows = n_seq * nb * DELTA_RB
    dir_buf = lambda n, dt=F32: pltpu.VMEM((2, n, H_A * DK), dt)
    return pl.pallas_call(
        functools.partial(_delta_kernel, nb, zero_init, n_prev),
        grid=(n_seq, nb),
        in_specs=in_specs,
        out_specs=[pl.BlockSpec((DELTA_RB, W_A), lambda s, j: (s * nb + j, 0)),
                   pl.BlockSpec((DELTA_RB, W_A), lambda s, j: (s * nb + nb - 1 - j, 0)),
                   pl.BlockSpec((1, n_prev + 1, 2, H_A * DK, DV), lambda s, j: (s, 0, 0, 0, 0))],
        out_shape=[jax.ShapeDtypeStruct((rows, W_A), F32),
                   jax.ShapeDtypeStruct((rows, W_A), F32),
                   jax.ShapeDtypeStruct((n_seq, n_prev + 1, 2, H_A * DK, DV), F32)],
        scratch_shapes=[dir_buf(DELTA_RB)] + [dir_buf(DELTA_RB, BF16) for _ in range(4)]
        + [dir_buf(CHUNKS_PER_RB * 8), pltpu.VMEM((2, N_GROUPS, GROUP_W, GROUP_W), F32)],
        compiler_params=_cparams(("parallel", "arbitrary"), VMEM_LIMIT),
        name=f"deltanet_nb{nb}",
    )(*args)


FFN_TN = D_FF // 2


def _postmix_kernel(final_norm, tile0, ofc_ref, ofl_ref, obc_ref, obl_ref, z_ref, ybc_ref, ybl_ref,
                    ycc_ref, ycl_ref, pg_ref, x_ref, mod_ref, na_ref, ones_ref, wpa_ref, wpb_ref, wpc_ref,
                    wo_ref, g2_ref, wgu_ref, wdn_ref, nf_ref, out_ref):
    is_ctx = pl.program_id(0) + tile0 < N_CTX_TILES
    m = mod_ref[0]
    o = jnp.where(is_ctx, ofc_ref[...] + obc_ref[...], ofl_ref[...] + obl_ref[...])
    yb = jnp.where(is_ctx, ybc_ref[...], ybl_ref[...])
    yc = jnp.where(is_ctx, ycc_ref[...], ycl_ref[...])
    ms = _head_sums([o * o], ones_ref[...])[0] * (1.0 / DV)
    ya = (o * lax.rsqrt(ms + EPS) * na_ref[...]) * _silu(z_ref[...])
    merged = (pg_ref[:, 0:D_MODEL].astype(F32) * _dot(ya.astype(BF16), wpa_ref[0])
              + pg_ref[:, D_MODEL:2 * D_MODEL].astype(F32) * _dot(yb.astype(BF16), wpb_ref[0])
              + pg_ref[:, 2 * D_MODEL:3 * D_MODEL].astype(F32) * _dot(yc.astype(BF16), wpc_ref[0]))
    x = x_ref[...] + m[2:3] * _dot(merged.astype(BF16), wo_ref[0])

    h = _rms_mod(x, g2_ref[...], m[4:5], m[3:4]).astype(BF16)
    acc = None
    for c in range(0, D_FF, FFN_TN):
        gate = _dot(h, wgu_ref[0, :, c:c + FFN_TN])
        up = _dot(h, wgu_ref[0, :, D_FF + c:D_FF + c + FFN_TN])
        part = _dot((_silu(gate) * up).astype(BF16), wdn_ref[0, c:c + FFN_TN, :])
        acc = part if acc is None else acc + part
    xn = x + m[5:6] * acc
    if final_norm:
        ms = jnp.mean(xn * xn, axis=-1, keepdims=True)
        xn = xn * lax.rsqrt(ms + EPS) * nf_ref[...]
    out_ref[...] = xn


def _postmix_call(o_ctx, o_lat, pd, yb, yc, pg, x, mod3, na512, ones_bd, wpa, wpb, wpc, wo,
                  g2, wgu, wdn, nf, layer, final_norm, tile0=0, n_tiles=T_ALL // ROW_TILE):
    row = lambda i: (i + tile0, 0)
    const = lambda i: (0, 0)
    lyr = lambda i: (layer, 0, 0)
    assert W_A == W_B == W_C
    return pl.pallas_call(
        functools.partial(_postmix_kernel, final_norm, tile0),
        grid=(n_tiles,),
        in_specs=[
            *_ctx_lat_specs(W_A, tile0), *_ctx_lat_specs(W_A, tile0),
            pl.BlockSpec((ROW_TILE, W_A), lambda i: (i + tile0, OFF_Z // W_A)),
            *_ctx_lat_specs(W_B, tile0), *_ctx_lat_specs(W_C, tile0),
            pl.BlockSpec((ROW_TILE, 3 * D_MODEL), row),
            pl.BlockSpec((ROW_TILE, D_MODEL), row),
            pl.BlockSpec((1, 6, D_MODEL), lambda i: (_mod_row_block(i + tile0), 0, 0)),
            pl.BlockSpec((1, W_A), const),
            _resident((GROUP_W, GROUP_W), const),
            _resident((1, W_A, D_MODEL), lyr),
            _resident((1, W_B, D_MODEL), lyr),
            _resident((1, W_C, D_MODEL), lyr),
            _resident((1, D_MODEL, D_MODEL), lyr),
            pl.BlockSpec((1, D_MODEL), const),
            _resident((1, D_MODEL, 2 * D_FF), lyr),
            _resident((1, D_FF, D_MODEL), lyr),
            pl.BlockSpec((1, D_MODEL), const),
        ],
        out_specs=pl.BlockSpec((ROW_TILE, D_MODEL), lambda i: (i, 0)),
        out_shape=jax.ShapeDtypeStruct((n_tiles * ROW_TILE, D_MODEL), F32),
        compiler_params=_cparams(("parallel",), VMEM_LIMIT),
        name="postmix_final" if final_norm else "postmix",
    )(o_ctx[0], o_lat[0], o_ctx[1], o_lat[1], pd, yb[0], yb[1], yc[0], yc[1], pg, x, mod3, na512,
      ones_bd, wpa, wpb, wpc, wo, g2.reshape(1, D_MODEL), wgu, wdn, nf.reshape(1, D_MODEL))


TABLE_SPLIT = 64


def _grid_pos_embed(n_tokens):
    rows = n_tokens // GRID_W
    quarter = D_MODEL // 4
    omega = 1.0 / (10000.0 ** (jnp.arange(quarter, dtype=F32) / quarter))

    def emb(pos):
        a = pos[:, None] * omega[None, :]
        return jnp.concatenate([jnp.sin(a), jnp.cos(a)], axis=-1)

    e_row, e_col = lax.optimization_barrier((emb(jnp.arange(rows).astype(F32)),
                                             emb(jnp.arange(GRID_W).astype(F32))))
    return jnp.concatenate([jnp.repeat(e_row, GRID_W, axis=0), jnp.tile(e_col, (rows, 1))], axis=-1)


def _cos_nsin_tables(n, period):
    t = jnp.arange(n, dtype=jnp.int32)[None, :]

    def cs(r):
        ang = ((r * t) % period).astype(F32) * (2.0 * math.pi / period)
        return jnp.cos(ang), jnp.sin(ang)

    ca, sa = cs(jnp.arange(n // TABLE_SPLIT, dtype=jnp.int32)[:, None] * TABLE_SPLIT)
    cb, sb = cs(jnp.arange(TABLE_SPLIT, dtype=jnp.int32)[:, None])
    ca, sa, cb, sb = lax.optimization_barrier((ca, sa, cb, sb))
    ca, sa = ca[:, None, :], sa[:, None, :]
    cos = (ca * cb[None] - sa * sb[None]).reshape(n, n)
    nsin = (-(sa * cb[None] + ca * sb[None])).reshape(n, n)
    return cos.astype(BF16), nsin.astype(BF16)


def _hyena_positions(L):
    bands = (HY_EMB - 1) // 2
    t = jnp.linspace(0.0, 1.0, L, dtype=F32)[:, None]
    wpos = (2.0 * math.pi / L) * jnp.arange(L, dtype=F32)[:, None]
    fr = jnp.linspace(1e-4, bands - 1, bands, dtype=F32)[None, :]
    zpos = jnp.concatenate([t, jnp.cos(fr * wpos), -jnp.sin(fr * wpos)], axis=-1)
    zpos = jnp.pad(zpos, ((0, 0), (0, LANE - HY_EMB)))
    deltas = jnp.abs(jnp.linspace(math.log(HY_DECAY_TARGET) / HY_SLOW_PCT,
                                  math.log(HY_DECAY_TARGET) / HY_FAST_PCT, W_B, dtype=F32))
    window = jnp.exp(-t * deltas[None, :])
    return zpos, window


def _group_tables():
    r = jnp.arange(DC, dtype=jnp.int32)
    ang = ((r[:, None] * r[None, :]) % DC).astype(F32) * (2.0 * math.pi / DC)
    eye = jnp.eye(G_C, dtype=F32)
    return jnp.kron(eye, jnp.cos(ang)).astype(BF16), jnp.kron(eye, jnp.sin(ang)).astype(BF16)


def _head_tables():
    ones_bd = jnp.kron(jnp.eye(HEAD_GROUP, dtype=F32), jnp.ones((DK, DK), F32)).astype(BF16)
    lane = jnp.arange(LANE)[:, None]
    head = (jnp.arange(H_A * DK) // DK)[None, :]
    sel = []
    for d in range(2):
        sel.append(jnp.stack([(lane == d * H_A + head), (lane == 2 * H_A + d * H_A + head)]))
    esel = jnp.stack(sel).astype(BF16)
    return ones_bd, esel


def kernel(x_prompt, x_sample, state_delta, c, c_ctx, w_mod, b_mod, norm1_g, norm2_g, w_in, conv_qkv, a_log, dt_bias, norm_a, conv_hy, hy_w1, hy_b1, hy_freq, hy_w2, hy_b2, hy_w3, hy_bias, w_pa, w_pb, w_pc, w_o, w_gu, w_down, norm_f):
    assert x_prompt.shape == (N_CTX_SEQ, L_CTX, D_MODEL) and x_sample.shape == (N_LAT_SEQ, L_LAT, D_MODEL)
    st = jnp.pad(jnp.concatenate([c_ctx[None], c], axis=0).T, ((0, 0), (0, 8 - 1 - N_LAT_SEQ)))
    mod = _mod_call(st, w_mod, b_mod).reshape(DEPTH, 8, 6, D_MODEL)

    ones_bd, esel = _head_tables()
    bdc, bds = _group_tables()
    seqs = ((L_CTX, N_CTX_SEQ, 0), (L_LAT, N_LAT_SEQ, T_CTX // L_LAT))
    tables = {L: (_cos_nsin_tables(L, 2 * L), _cos_nsin_tables(L, L), _hyena_positions(L))
              for L, _, _ in seqs}

    w_in_b = jnp.swapaxes(w_in, 1, 2).astype(BF16)
    w_pa_b, w_pb_b, w_pc_b, w_o_b, w_gu_b, w_down_b = (
        w.astype(BF16) for w in (w_pa, w_pb, w_pc, w_o, w_gu, w_down))

    x = None
    ctx_states = None
    for l in range(DEPTH):
        mod3 = mod[l, 0:3]
        if l == 0:
            xs = (x_prompt.reshape(T_CTX, D_MODEL), x_sample.reshape(T_LAT, D_MODEL), _grid_pos_embed(L_LAT))
            pd, ph, pf, pg, x = _inproj_call(xs, mod3, norm1_g[l], w_in_b, l)
        else:
            pd, ph, pf, pg = _inproj_call((x,), mod3, norm1_g[l], w_in_b, l)

        par = jnp.zeros((8, LANE), F32)
        par = par.at[0, 2 * H_A:4 * H_A].set(a_log[l].reshape(-1))
        par = par.at[1, 2 * H_A:4 * H_A].set(dt_bias[l].reshape(-1))
        *o_ctx, ctx_states = _delta_call(pd, conv_qkv[l], par, esel, ones_bd, None,
                                         N_CTX_SEQ, L_CTX // DELTA_RB, 0, ctx_states)
        s0 = state_delta[:, l].astype(F32).reshape(N_LAT_SEQ, 2, H_A * DK, DV)
        *o_lat, _ = _delta_call(pd, conv_qkv[l], par, esel, ones_bd, s0,
                                N_LAT_SEQ, L_LAT // DELTA_RB, T_CTX // DELTA_RB)

        w1p = jnp.pad(hy_w1[l], ((0, LANE - HY_EMB), (0, 0)))
        yb, yc = [], []
        for L, n_seq, blk0 in seqs:
            (cos2, nsin2), (cos1, nsin1), (zpos, window) = tables[L]
            kspec = _filter_call(L, zpos, w1p, hy_b1[l][None], hy_freq[l][None], hy_w2[l],
                                 hy_b2[l][None], hy_w3[l], window, cos2, nsin2)
            bias = hy_bias[l][:, None, :]
            if L == HYENA_FT:
                assert blk0 == 0
                yb.append(_hyena_short_call(ph, conv_hy[l], kspec, bias, cos2, nsin2, n_seq, L))
                yc.append(_fnet_short_call(pf, cos1, nsin1, bdc, bds, n_seq, L))
            else:
                yb.append(_hyena_call(ph, conv_hy[l], kspec, bias, cos2, nsin2, n_seq, L, blk0, HYENA_FT))
                yc.append(_fnet_call(pf, cos1, nsin1, bdc, bds, n_seq, L, blk0))

        na512 = jnp.tile(norm_a[l], H_A)[None]
        post = functools.partial(_postmix_call, o_ctx, o_lat, pd, yb, yc, pg, x, mod3, na512, ones_bd,
                                 w_pa_b, w_pb_b, w_pc_b, w_o_b, norm2_g[l], w_gu_b, w_down_b, norm_f, l)
        if l < DEPTH - 1:
            x = post(False)
        else:
            y_prompt = post(True, 0, N_CTX_TILES).reshape(N_CTX_SEQ, L_CTX, D_MODEL)
            y_sample = post(True, N_CTX_TILES, N_LAT_TILES).reshape(N_LAT_SEQ, L_LAT, D_MODEL)

    new_state = ctx_states.reshape(N_CTX_SEQ, DEPTH, 2, H_A, DK, DV).astype(x_prompt.dtype)
    return (y_prompt, y_sample, new_state)
```

```python
import functools
import math

import jax
import jax.numpy as jnp
from jax import lax
from jax.experimental import pallas as pl
from jax.experimental.pallas import tpu as pltpu

F32 = jnp.float32
BF16 = jnp.bfloat16

D_MODEL = 1024
N_CTX_SEQ = 32
L_CTX = 256
DEPTH = 2
N_LAT_SEQ = 2
L_LAT = 2048
GRID_W = 64
EPS = 1e-6
H_A = 8
DK = 64
DV = 64
W_A = H_A * DV
QKV_W = 2 * H_A * DK + H_A * DV
CHUNK = 64
W_B = 512
HY_EMB = 33
HY_HID = 64
HY_DECAY_TARGET = 1e-2
HY_FAST_PCT = 0.3
HY_SLOW_PCT = 1.5
G_C = 8
DC = 64
W_C = G_C * DC
D_FF = ((8 * D_MODEL + 3 * 256 - 1) // (3 * 256)) * 256
OFF_Z = QKV_W
OFF_B = OFF_Z + W_A
OFF_A = OFF_B + 2 * H_A
OFF_HY = OFF_A + 2 * H_A
OFF_FN = OFF_HY + 3 * W_B
OFF_GATE = OFF_FN + W_C

T_CTX = N_CTX_SEQ * L_CTX
T_LAT = N_LAT_SEQ * L_LAT
T_ALL = T_CTX + T_LAT
ROW_TILE = 256
N_CTX_TILES = T_CTX // ROW_TILE
N_LAT_TILES = T_LAT // ROW_TILE
LANE = 128
PD_W = QKV_W + W_A + LANE
HEAD_GROUP = 4
GROUP_W = HEAD_GROUP * DK
N_GROUPS = H_A // HEAD_GROUP
VMEM_LIMIT = 56 * 1024 * 1024


def _cparams(sem, vmem=None):
    return pltpu.CompilerParams(dimension_semantics=sem, vmem_limit_bytes=vmem)


def _dot(a, b):
    return jnp.dot(a, b, preferred_element_type=F32)


def _dot_nt(a, b):
    return lax.dot_general(a, b, (((1,), (1,)), ((), ())), preferred_element_type=F32)


def _dot_tn(a, b):
    return lax.dot_general(a, b, (((0,), (0,)), ((), ())), preferred_element_type=F32)


def _split(a, n):
    parts = []
    rem = a
    for i in range(n):
        p = rem.astype(BF16)
        parts.append(p)
        if i + 1 < n:
            rem = rem - p.astype(F32)
    return parts


def _mm3(a, b):
    ah, al = _split(a, 2)
    bh, bl = _split(b, 2)
    return _dot(ah, bh) + (_dot(ah, bl) + _dot(al, bh))


def _sigmoid(x):
    return 1.0 / (1.0 + jnp.exp(-x))


def _silu(x):
    return x * _sigmoid(x)


def _softplus(x):
    return jnp.maximum(x, 0.0) + jnp.log(1.0 + jnp.exp(-jnp.abs(x)))


def _mod_row_block(i):
    per_lat = L_LAT // ROW_TILE
    return jnp.where(i < N_CTX_TILES, 0, 1 + (i - N_CTX_TILES) // per_lat)


def _ctx_tile(i):
    return jnp.minimum(i, N_CTX_TILES - 1)


def _lat_tile(i):
    return jnp.maximum(i - N_CTX_TILES, 0)


def _ctx_lat_specs(width, tile0=0):
    return (pl.BlockSpec((ROW_TILE, width), lambda i: (_ctx_tile(i + tile0), 0)),
            pl.BlockSpec((ROW_TILE, width), lambda i: (_lat_tile(i + tile0), 0)))


MOD_TN = 512


def _mod_kernel(st_ref, w_ref, b_ref, out_ref):
    s = _silu(st_ref[...])
    w = w_ref[0]
    rows = [jnp.sum(s[:, r:r + 1] * w, axis=0, keepdims=True) + b_ref[0] for r in range(3)]
    rows.append(jnp.zeros((5, MOD_TN), F32))
    out_ref[0] = jnp.concatenate(rows, axis=0)


def _mod_call(st, w_mod, b_mod):
    n6 = 6 * D_MODEL
    return pl.pallas_call(
        _mod_kernel,
        grid=(DEPTH, n6 // MOD_TN),
        in_specs=[
            pl.BlockSpec((D_MODEL, 8), lambda l, j: (0, 0)),
            pl.BlockSpec((1, D_MODEL, MOD_TN), lambda l, j: (l, 0, j)),
            pl.BlockSpec((1, 1, MOD_TN), lambda l, j: (l, 0, j)),
        ],
        out_specs=pl.BlockSpec((1, 8, MOD_TN), lambda l, j: (l, 0, j)),
        out_shape=jax.ShapeDtypeStruct((DEPTH, 8, n6), F32),
        compiler_params=_cparams(("parallel", "parallel")),
        name="adaln_mod",
    )(st, w_mod, b_mod.reshape(DEPTH, 1, n6))


INPROJ_TN = 512
INPROJ_WIDTHS = (PD_W, 3 * W_B, W_C, 3 * D_MODEL)
INPROJ_FEATURES = (OFF_HY, 3 * W_B, W_C, 3 * D_MODEL)


def _rms_mod(x, g, scale, shift):
    ms = jnp.mean(x * x, axis=-1, keepdims=True)
    return (x * lax.rsqrt(ms + EPS) * g) * (1.0 + scale) + shift


def _inproj_kernel(first, *refs):
    if first:
        (xc_ref, xl_ref, pos_ref, mod_ref, g_ref, w_ref, pd_ref, ph_ref, pf_ref, pg_ref, x_ref) = refs
        x = jnp.where(pl.program_id(0) < N_CTX_TILES, xc_ref[...], xl_ref[...] + pos_ref[...])
        x_ref[...] = x
    else:
        (xin_ref, mod_ref, g_ref, w_ref, pd_ref, ph_ref, pf_ref, pg_ref) = refs
        x = xin_ref[...]
    m = mod_ref[0]
    h = _rms_mod(x, g_ref[...], m[1:2], m[0:1]).astype(BF16)
    row0 = 0
    for o_ref, n_feat in zip((pd_ref, ph_ref, pf_ref, pg_ref), INPROJ_FEATURES):
        n = o_ref.shape[1]
        for c in range(0, n, INPROJ_TN):
            e = min(c + INPROJ_TN, n)
            ef = min(e, n_feat)
            y = _dot_nt(h, w_ref[0, row0 + c:row0 + ef, :])
            if ef < e:
                y = jnp.concatenate([y, jnp.zeros((y.shape[0], e - ef), F32)], axis=1)
            if o_ref is pg_ref:
                y = _sigmoid(y)
            o_ref[:, c:e] = y.astype(o_ref.dtype)
        row0 += n_feat


def _resident(shape, index_map):
    return pl.BlockSpec(shape, index_map, pipeline_mode=pl.Buffered(1))


def _inproj_call(xs, mod3, g, w_all, layer):
    first = len(xs) == 3
    widths = INPROJ_WIDTHS
    row = lambda i: (i, 0)
    const = lambda i: (0, 0)
    if first:
        per_lat = L_LAT // ROW_TILE
        x_specs = list(_ctx_lat_specs(D_MODEL)) + [
            pl.BlockSpec((ROW_TILE, D_MODEL), lambda i: (_lat_tile(i) % per_lat, 0))]
    else:
        x_specs = [pl.BlockSpec((ROW_TILE, D_MODEL), row)]
    out_widths = widths + ((D_MODEL,) if first else ())
    out_dtypes = (F32, F32, F32, BF16) + ((F32,) if first else ())
    return pl.pallas_call(
        functools.partial(_inproj_kernel, first),
        grid=(T_ALL // ROW_TILE,),
        in_specs=x_specs + [
            pl.BlockSpec((1, 6, D_MODEL), lambda i: (_mod_row_block(i), 0, 0)),
            pl.BlockSpec((1, D_MODEL), const),
            _resident((1, sum(INPROJ_FEATURES), D_MODEL), lambda i: (layer, 0, 0)),
        ],
        out_specs=[pl.BlockSpec((ROW_TILE, w), row) for w in out_widths],
        out_shape=[jax.ShapeDtypeStruct((T_ALL, w), dt) for w, dt in zip(out_widths, out_dtypes)],
        compiler_params=_cparams(("parallel",), VMEM_LIMIT),
        name="inproj_first" if first else "inproj",
    )(*xs, mod3, g.reshape(1, D_MODEL), w_all)


def _conv3_rows(cur, prev_row, next_row, w):
    n = cur.shape[0]
    ridx = lax.broadcasted_iota(jnp.int32, cur.shape, 0)
    up = jnp.where(ridx == 0, prev_row, pltpu.roll(cur, 1, 0))
    dn = jnp.where(ridx == n - 1, next_row, pltpu.roll(cur, n - 1, 0))
    return up * w[0:1] + cur * w[1:2] + dn * w[2:3]


def _halo_specs(width, rows_per_block, n_row_blocks, blk_of):
    per = rows_per_block // 8
    last = n_row_blocks * per - 1
    prev = pl.BlockSpec((8, width), lambda *a: (jnp.maximum(blk_of(*a) * per - 1, 0), 0))
    nxt = pl.BlockSpec((8, width), lambda *a: (jnp.minimum((blk_of(*a) + 1) * per, last), 0))
    return prev, nxt


FILT_RT = 256


def _alternating_sum(x):
    t = lax.broadcasted_iota(jnp.int32, x.shape, 0)
    return jnp.sum(jnp.where(t % 2 == 0, x, -x), axis=0, keepdims=True)


def _filter_kernel(L, zpos_ref, w1_ref, b1_ref, fq_ref, w2_ref, b2_ref, w3_ref, win_ref,
                   cos_ref, nsin_ref, k_ref, hs_s, hm_s, krl_s):
    rt = pl.program_id(1)

    @pl.when(rt == 0)
    def _():
        fq = fq_ref[...]
        alt_acc = jnp.zeros((1, W_B), F32)
        for r0 in range(0, L, FILT_RT):
            rows = slice(r0, r0 + FILT_RT)
            h = jnp.sin(fq * (_mm3(zpos_ref[rows, :], w1_ref[...]) + b1_ref[...]))
            h = jnp.sin(fq * (_mm3(h, w2_ref[...]) + b2_ref[...]))
            hf = _mm3(h, w3_ref[...])
            win = win_ref[rows, :]
            fw = hf[:, 0:W_B] * win
            bw = hf[:, W_B:2 * W_B] * win
            hsum = fw + bw
            hs_s[rows, :] = hsum.astype(BF16)
            hm_s[rows, :] = (fw - bw).astype(BF16)
            alt_acc = alt_acc + _alternating_sum(hsum)
        krl_s[...] = jnp.broadcast_to(alt_acc, krl_s.shape)

    p1 = _dot(cos_ref[...], hs_s[...])
    p2 = _dot(nsin_ref[...], hm_s[...])
    first = (rt * FILT_RT + lax.broadcasted_iota(jnp.int32, p1.shape, 0)) == 0
    k_ref[0, 0] = p1
    k_ref[0, 1] = jnp.where(first, krl_s[0:1, :], p1)
    k_ref[0, 2] = jnp.where(first, 0.0, p2)


def _filter_call(L, zpos, w1p, b1, fq, w2, b2, w3, win, cos, nsin):
    nrt = L // FILT_RT
    c2 = lambda o, r: (0, 0)
    return pl.pallas_call(
        functools.partial(_filter_kernel, L),
        grid=(2, nrt),
        in_specs=[
            pl.BlockSpec((L, LANE), c2),
            pl.BlockSpec((LANE, HY_HID), c2),
            pl.BlockSpec((1, HY_HID), c2),
            pl.BlockSpec((1, HY_HID), c2),
            pl.BlockSpec((HY_HID, HY_HID), c2),
            pl.BlockSpec((1, HY_HID), c2),
            pl.BlockSpec((HY_HID, 2 * W_B), lambda o, r: (0, o)),
            pl.BlockSpec((L, W_B), c2),
            pl.BlockSpec((FILT_RT, L), lambda o, r: (r, 0)),
            pl.BlockSpec((FILT_RT, L), lambda o, r: (r, 0)),
        ],
        out_specs=pl.BlockSpec((1, 3, FILT_RT, W_B), lambda o, r: (o, 0, r, 0)),
        out_shape=jax.ShapeDtypeStruct((2, 3, L, W_B), F32),
        scratch_shapes=[pltpu.VMEM((L, W_B), BF16), pltpu.VMEM((L, W_B), BF16),
                        pltpu.VMEM((8, W_B), F32)],
        compiler_params=_cparams(("parallel", "arbitrary"), VMEM_LIMIT),
        name=f"hyena_filter_{L}",
    )(zpos, w1p, b1, fq, w2, b2, w3, win, cos, nsin)


HYENA_FT = 256

HYENA_CONV_ROWS = 256


HYENA_CH = 256


def _hyena_kernel(L, ft, x1_ref, x2_ref, v_ref, cw1_ref, cw2_ref, cwv_ref, k_ref, bias_ref, cos_ref, nsin_ref,
                  out_ref, gate_s, zf_s, zb_s, acc_s):
    o = pl.program_id(2)

    @pl.when(o == 0)
    def _():
        for src, cw_ref, dst in ((x1_ref, cw1_ref, 0), (x2_ref, cw2_ref, 1), (v_ref, cwv_ref, None)):
            cw = cw_ref[...]
            for r0 in range(0, L, HYENA_CONV_ROWS):
                r1 = r0 + HYENA_CONV_ROWS
                prev_row = src[r0 - 1:r0, :] if r0 > 0 else 0.0
                next_row = src[r1:r1 + 1, :] if r1 < L else 0.0
                uc = _conv3_rows(src[r0:r1, :], prev_row, next_row, cw)
                if dst is None:
                    zf_s[r0:r1, :] = uc
                    zb_s[r0:r1, :] = uc.astype(BF16)
                else:
                    gate_s[dst, r0:r1, :] = uc

    zb = zb_s[...]
    nyq = _alternating_sum(zf_s[...])
    for f in range(L // ft):
        rows = slice(f * ft, (f + 1) * ft)
        top = _dot(cos_ref[rows, :], zb)
        bot = _dot(nsin_ref[rows, :], zb)
        if f == 0:
            first = lax.broadcasted_iota(jnp.int32, top.shape, 0) == 0
            bot = jnp.where(first, nyq, bot)
        krt = k_ref[0, 0, rows, :]
        krb = k_ref[0, 1, rows, :]
        ki = k_ref[0, 2, rows, :]
        yt = top * krt - bot * ki
        yb = top * ki + bot * krb
        ytw = jnp.where(first, 0.5 * yt, yt) if f == 0 else yt
        part = _dot(cos_ref[:, rows], ytw.astype(BF16)) + _dot(nsin_ref[:, rows], yb.astype(BF16))
        if f == 0:
            t = lax.broadcasted_iota(jnp.int32, part.shape, 0)
            acc_s[...] = part + jnp.where(t % 2 == 0, 0.5, -0.5) * yb[0:1, :]
        else:
            acc_s[...] += part

    znew = gate_s[o] * (acc_s[...] * (1.0 / L) + bias_ref[0] * zf_s[...])
    zf_s[...] = znew
    zb_s[...] = znew.astype(BF16)
    out_ref[...] = znew.astype(out_ref.dtype)


def _hyena_call(ph, conv_w, kspec, bias, cos, nsin, n_seq, L, row_blk0, ft):
    nch = W_B // HYENA_CH
    once = pl.Buffered(1)
    col = lambda part: (lambda s, c, o: (row_blk0 + s, part * nch + c))
    cwcol = lambda part: (lambda s, c, o: (0, part * nch + c))
    return pl.pallas_call(
        functools.partial(_hyena_kernel, L, ft),
        grid=(n_seq, nch, 2),
        in_specs=[pl.BlockSpec((L, HYENA_CH), col(part), pipeline_mode=once) for part in range(3)]
        + [pl.BlockSpec((3, HYENA_CH), cwcol(part)) for part in range(3)]
        + [
            pl.BlockSpec((1, 3, L, HYENA_CH), lambda s, c, o: (o, 0, 0, c)),
            pl.BlockSpec((1, 1, HYENA_CH), lambda s, c, o: (o, 0, c)),
            pl.BlockSpec((L, L), lambda s, c, o: (0, 0), pipeline_mode=once),
            pl.BlockSpec((L, L), lambda s, c, o: (0, 0), pipeline_mode=once),
        ],
        out_specs=pl.BlockSpec((L, HYENA_CH), lambda s, c, o: (s, c)),
        out_shape=jax.ShapeDtypeStruct((n_seq * L, W_B), BF16),
        scratch_shapes=[pltpu.VMEM((2, L, HYENA_CH), F32), pltpu.VMEM((L, HYENA_CH), F32),
                        pltpu.VMEM((L, HYENA_CH), BF16), pltpu.VMEM((L, HYENA_CH), F32)],
        compiler_params=_cparams(("parallel", "parallel", "arbitrary"), VMEM_LIMIT),
        name=f"hyena_conv_{L}",
    )(ph, ph, ph, conv_w, conv_w, conv_w, kspec, bias, cos, nsin)


HYENA_SHORT_SB = 4


def _hyena_short_kernel(L, ph_ref, cw_ref, k_ref, bias_ref, cos_ref, nsin_ref, out_ref):
    cos = cos_ref[...]
    nsin = nsin_ref[...]
    cw = cw_ref[...]
    t = lax.broadcasted_iota(jnp.int32, (L, W_B), 0)
    first = t == 0
    alt_half = jnp.where(t % 2 == 0, 0.5, -0.5)
    rows = [slice(s * L, (s + 1) * L) for s in range(HYENA_SHORT_SB)]
    ucs = [_conv3_rows(ph_ref[r, :], 0.0, 0.0, cw) for r in rows]
    zs = [uc[:, 2 * W_B:3 * W_B] for uc in ucs]
    for o in range(2):
        krt, krb, ki = k_ref[o, 0], k_ref[o, 1], k_ref[o, 2]
        zbs = [z.astype(BF16) for z in zs]
        tops = [_dot(cos, zb) for zb in zbs]
        bots = [jnp.where(first, _alternating_sum(z), _dot(nsin, zb)) for z, zb in zip(zs, zbs)]
        yts = [top * krt - bot * ki for top, bot in zip(tops, bots)]
        ybs = [top * ki + bot * krb for top, bot in zip(tops, bots)]
        accs = [_dot(cos, jnp.where(first, 0.5 * yt, yt).astype(BF16)) + _dot(nsin, yb.astype(BF16))
                + alt_half * yb[0:1, :] for yt, yb in zip(yts, ybs)]
        zs = [uc[:, o * W_B:(o + 1) * W_B] * (acc * (1.0 / L) + bias_ref[o] * z)
              for uc, acc, z in zip(ucs, accs, zs)]
    for r, z in zip(rows, zs):
        out_ref[r, :] = z.astype(out_ref.dtype)


def _hyena_short_call(ph, conv_w, kspec, bias, cos, nsin, n_seq, L):
    rows = HYENA_SHORT_SB * L
    return pl.pallas_call(
        functools.partial(_hyena_short_kernel, L),
        grid=(n_seq // HYENA_SHORT_SB,),
        in_specs=[
            pl.BlockSpec((rows, 3 * W_B), lambda i: (i, 0)),
            pl.BlockSpec((3, 3 * W_B), lambda i: (0, 0)),
            pl.BlockSpec((2, 3, L, W_B), lambda i: (0, 0, 0, 0)),
            pl.BlockSpec((2, 1, W_B), lambda i: (0, 0, 0)),
            pl.BlockSpec((L, L), lambda i: (0, 0)),
            pl.BlockSpec((L, L), lambda i: (0, 0)),
        ],
        out_specs=pl.BlockSpec((rows, W_B), lambda i: (i, 0)),
        out_shape=jax.ShapeDtypeStruct((n_seq * L, W_B), BF16),
        compiler_params=_cparams(("parallel",), VMEM_LIMIT),
        name=f"hyena_conv_{L}",
    )(ph, conv_w, kspec, bias, cos, nsin)


FNET_RT = 256


def _fnet_kernel(L, x_ref, cos_ref, nsin_ref, bdc_ref, bds_ref, out_ref, xc_s, xs_s):
    r = pl.program_id(1)

    @pl.when(r == 0)
    def _():
        for r0 in range(0, L, FNET_RT):
            xb = x_ref[r0:r0 + FNET_RT, :].astype(BF16)
            xc_s[r0:r0 + FNET_RT, :] = _dot(xb, bdc_ref[...]).astype(BF16)
            xs_s[r0:r0 + FNET_RT, :] = _dot(xb, bds_ref[...]).astype(BF16)

    y = _dot(cos_ref[...], xc_s[...]) + _dot(nsin_ref[...], xs_s[...])
    out_ref[...] = (y * (1.0 / math.sqrt(DC * L))).astype(out_ref.dtype)


def _fnet_call(pf, cos, nsin, bdc, bds, n_seq, L, row_blk0):
    nrt = L // FNET_RT
    return pl.pallas_call(
        functools.partial(_fnet_kernel, L),
        grid=(n_seq, nrt),
        in_specs=[
            pl.BlockSpec((L, W_C), lambda s, r: (row_blk0 + s, 0)),
            pl.BlockSpec((FNET_RT, L), lambda s, r: (r, 0)),
            pl.BlockSpec((FNET_RT, L), lambda s, r: (r, 0)),
            pl.BlockSpec((W_C, W_C), lambda s, r: (0, 0)),
            pl.BlockSpec((W_C, W_C), lambda s, r: (0, 0)),
        ],
        out_specs=pl.BlockSpec((FNET_RT, W_C), lambda s, r: (s * nrt + r, 0)),
        out_shape=jax.ShapeDtypeStruct((n_seq * L, W_C), BF16),
        scratch_shapes=[pltpu.VMEM((L, W_C), BF16), pltpu.VMEM((L, W_C), BF16)],
        compiler_params=_cparams(("parallel", "arbitrary"), VMEM_LIMIT),
        name=f"fnet_{L}",
    )(pf, cos, nsin, bdc, bds)


FNET_SHORT_SB = 4


def _fnet_short_kernel(L, x_ref, cos_ref, nsin_ref, bdc_ref, bds_ref, out_ref):
    xb = x_ref[...].astype(BF16)
    xc = _dot(xb, bdc_ref[...]).astype(BF16)
    xs = _dot(xb, bds_ref[...]).astype(BF16)
    for s in range(FNET_SHORT_SB):
        rows = slice(s * L, (s + 1) * L)
        y = _dot(cos_ref[...], xc[rows]) + _dot(nsin_ref[...], xs[rows])
        out_ref[rows, :] = (y * (1.0 / math.sqrt(DC * L))).astype(out_ref.dtype)


def _fnet_short_call(pf, cos, nsin, bdc, bds, n_seq, L):
    rows = FNET_SHORT_SB * L
    const = lambda i: (0, 0)
    return pl.pallas_call(
        functools.partial(_fnet_short_kernel, L),
        grid=(n_seq // FNET_SHORT_SB,),
        in_specs=[pl.BlockSpec((rows, W_C), lambda i: (i, 0)),
                  pl.BlockSpec((L, L), const), pl.BlockSpec((L, L), const),
                  pl.BlockSpec((W_C, W_C), const), pl.BlockSpec((W_C, W_C), const)],
        out_specs=pl.BlockSpec((rows, W_C), lambda i: (i, 0)),
        out_shape=jax.ShapeDtypeStruct((n_seq * L, W_C), BF16),
        compiler_params=_cparams(("parallel",), VMEM_LIMIT),
        name=f"fnet_{L}",
    )(pf, cos, nsin, bdc, bds)


DELTA_RB = 256
CHUNKS_PER_RB = DELTA_RB // CHUNK


HEADS_PER_LANE_TILE = LANE // DK


def _block_diag(y, half_masks):
    yb = y.astype(BF16)
    zero = jnp.zeros((CHUNK, LANE), BF16)
    row_blocks = []
    for h in range(HEAD_GROUP):
        tile = h // HEADS_PER_LANE_TILE
        piece = yb[:, tile * LANE:(tile + 1) * LANE] * half_masks[h % HEADS_PER_LANE_TILE]
        row_blocks.append(jnp.concatenate(
            [piece if t == tile else zero for t in range(GROUP_W // LANE)], axis=1))
    return jnp.concatenate(row_blocks, axis=0)


def _stacked_const_rhs(arrs, c, n):
    m = arrs[0].shape[0]
    parts = [p for a in arrs for p in _split(a, n)]
    y = _dot(jnp.concatenate(parts, axis=0), c)
    outs = []
    for i in range(len(arrs)):
        acc = y[i * n * m:(i * n + 1) * m]
        for t in range(1, n):
            acc = acc + y[(i * n + t) * m:(i * n + t + 1) * m]
        outs.append(acc)
    return outs


def _head_sums(arrs, ones_group):
    groups = [_stacked_const_rhs([a[:, g * GROUP_W:(g + 1) * GROUP_W] for a in arrs], ones_group, 2)
              for g in range(N_GROUPS)]
    return [jnp.concatenate([groups[g][i] for g in range(N_GROUPS)], axis=1) for i in range(len(arrs))]


def _expand_heads(x, lane0):
    cols = [jnp.broadcast_to(x[:, lane0 + h:lane0 + h + 1], (x.shape[0], DK)) for h in range(H_A)]
    return jnp.concatenate(cols, axis=1)


def _const_lhs_split(c, b, n):
    w = b.shape[1]
    y = _dot(c, jnp.concatenate(_split(b, n), axis=1))
    acc = y[:, 0:w]
    for t in range(1, n):
        acc = acc + y[:, t * w:(t + 1) * w]
    return acc


def _delta_chunk_stages(pd_ref, prev_ref, next_ref, r, has_prev, has_next, dirs, cw, a_neg, dtb, ones_bd,
                        tri, half_masks):
    ch = {}
    rows = slice(r * CHUNK, (r + 1) * CHUNK)

    def conv():
        cur = pd_ref[rows, 0:QKV_W]
        if r == 0:
            prev_row = jnp.where(has_prev, prev_ref[7:8, :], 0.0)
        else:
            prev_row = pd_ref[r * CHUNK - 1:r * CHUNK, 0:QKV_W]
        if r == CHUNKS_PER_RB - 1:
            next_row = jnp.where(has_next, next_ref[0:1, :], 0.0)
        else:
            next_row = pd_ref[(r + 1) * CHUNK:(r + 1) * CHUNK + 1, 0:QKV_W]
        qkv = _silu(_conv3_rows(cur, prev_row, next_row, cw))
        ch["q"] = qkv[:, 0:H_A * DK]
        ch["k"] = qkv[:, H_A * DK:2 * H_A * DK]
        ch["v"] = qkv[:, 2 * H_A * DK:]

    def norms():
        q, k = ch.pop("q"), ch.pop("k")
        qss, kss = _head_sums([q * q, k * k], ones_bd)
        ch["qn"] = q * lax.rsqrt(qss + EPS) * (DK ** -0.5)
        ch["kn"] = k * lax.rsqrt(kss + EPS)

    def gram():
        ch["gram"] = []
        for g in range(N_GROUPS):
            lanes = slice(g * GROUP_W, (g + 1) * GROUP_W)
            lhs = jnp.concatenate([ch["kn"][:, lanes], ch["qn"][:, lanes]], axis=0).astype(BF16)
            ch["gram"].append(_dot_nt(lhs, _block_diag(ch["kn"][:, lanes], half_masks)))

    def decay():
        ba = pd_ref[rows, OFF_B:OFF_B + LANE]
        sig = _sigmoid(ba)
        glog = a_neg * _softplus(ba + dtb)
        ch["decay"] = {}
        for d in dirs:
            gcum = _const_lhs_split(tri[d], glog, 3)
            beta = _expand_heads(sig, d * H_A)
            gcc8 = _expand_heads(gcum, (2 + d) * H_A)
            ch["decay"][d] = (beta, gcc8)

    return ch, [conv, norms, gram, decay]


def _delta_kernel(nb, zero_init, n_prev, *refs):
    shared = nb == 1
    it = iter(refs)
    blocks = [(next(it), next(it), next(it))]
    if not shared:
        blocks.append((next(it), next(it), next(it)))
    cw_ref, par_ref, ones_ref = next(it), next(it), next(it)
    s0_ref = None if zero_init else next(it)
    sprev_ref = next(it) if n_prev else None
    o_refs = (next(it), next(it))
    sfin_ref = next(it)
    u_s, w_s, p_s, qg_s, kg_s, gl_s, st_s = (next(it) for _ in range(7))
    j = pl.program_id(1)

    ri = lax.broadcasted_iota(jnp.int32, (CHUNK, GROUP_W), 0)
    cj = lax.broadcasted_iota(jnp.int32, (CHUNK, GROUP_W), 1) % CHUNK
    ixj = ri ^ cj
    eye = ixj == 0
    br = lax.broadcasted_iota(jnp.int32, (GROUP_W, GROUP_W), 0) // CHUNK
    bc = lax.broadcasted_iota(jnp.int32, (GROUP_W, GROUP_W), 1) // CHUNK
    same_head = br == bc
    hl = lax.broadcasted_iota(jnp.int32, (CHUNK, LANE), 1) // DK
    half_masks = tuple(jnp.where(hl == h, 1.0, 0.0).astype(BF16) for h in range(HEADS_PER_LANE_TILE))
    ti = lax.broadcasted_iota(jnp.int32, (CHUNK, CHUNK), 0)
    tm = lax.broadcasted_iota(jnp.int32, (CHUNK, CHUNK), 1)
    ri8 = lax.broadcasted_iota(jnp.int32, (CHUNK, H_A * DK), 0)
    cj8 = lax.broadcasted_iota(jnp.int32, (CHUNK, H_A * DK), 1) % CHUNK
    incl = (ri >= cj, ri <= cj)
    strict = (ri > cj, ri < cj)
    tri = tuple(jnp.where(m, 1.0, 0.0).astype(BF16) for m in (tm <= ti, tm >= ti))
    eye8 = ri8 == cj8
    last_row = (CHUNK - 1, 0)

    @pl.when(j == 0)
    def _():
        for d in range(2):
            for g in range(N_GROUPS):
                if zero_init:
                    st_s[d, g] = jnp.zeros((GROUP_W, GROUP_W), F32)
                else:
                    nat = s0_ref[0, d, g * GROUP_W:(g + 1) * GROUP_W, :]
                    st_s[d, g] = jnp.where(same_head, jnp.concatenate([nat] * HEAD_GROUP, axis=1), 0.0)

    cw = cw_ref[...]
    a_neg = -jnp.exp(par_ref[0:1, :])
    dtb = par_ref[1:2, :]
    pos = (j, nb - 1 - j)
    scan_order = (tuple(range(CHUNKS_PER_RB)), tuple(reversed(range(CHUNKS_PER_RB))))

    def unit_thunk(d, r, ch, g, units):
        def run():
            rows = slice(r * CHUNK, (r + 1) * CHUNK)
            lanes = slice(g * GROUP_W, (g + 1) * GROUP_W)
            beta, gcc8 = ch["decay"][d]
            gcr = jnp.sum(jnp.where(eye8, gcc8, 0.0), axis=0, keepdims=True)[:, lanes]
            qn, kn, be, gcc = ch["qn"][:, lanes], ch["kn"][:, lanes], beta[:, lanes], gcc8[:, lanes]
            kq = ch["gram"][g]
            dec = jnp.exp(jnp.where(incl[d], gcc - gcr, -1e30))
            a = jnp.where(strict[d], kq[0:CHUNK] * be * dec, 0.0)
            eg = jnp.exp(gcc)
            gcl = gcc[last_row[d]:last_row[d] + 1, :]
            units.append((d, rows, lanes, a, ch["v"][:, lanes] * be, kn * be * eg))
            p_s[d, rows, lanes] = (kq[CHUNK:2 * CHUNK] * dec).astype(BF16)
            qg_s[d, rows, lanes] = (qn * eg).astype(BF16)
            kg_s[d, rows, lanes] = (kn * jnp.exp(gcl - gcc)).astype(BF16)
            gl_s[d, r * 8:(r + 1) * 8, lanes] = jnp.broadcast_to(jnp.exp(gcl), (8, GROUP_W))
        return run

    def prep_thunks(units):
        thunks = []
        if shared:
            todo = [(0, r, (0, 1)) for r in range(CHUNKS_PER_RB)]
        else:
            todo = [(d, r, (d,)) for d in range(2) for r in scan_order[d]]
        for b, r, dirs in todo:
            ch, stages = _delta_chunk_stages(*blocks[b], r, pos[b] > 0, pos[b] < nb - 1, dirs, cw, a_neg, dtb,
                                             ones_ref[...], tri, half_masks)
            thunks += stages
            thunks += [unit_thunk(d, r, ch, g, units) for d in dirs for g in range(N_GROUPS)]
        return thunks

    def scan_thunk(d, c):
        def run():
            r = scan_order[d][c]
            rows = slice(r * CHUNK, (r + 1) * CHUNK)
            for g in range(N_GROUPS):
                lanes = slice(g * GROUP_W, (g + 1) * GROUP_W)
                s = st_s[d, g]
                wq = jnp.concatenate([w_s[d, rows, lanes], qg_s[d, rows, lanes]], axis=0)
                ws_qs = _dot(wq, s.astype(BF16))
                v_new = u_s[d, rows, lanes] - ws_qs[0:CHUNK]
                o = ws_qs[CHUNK:2 * CHUNK] + _dot(p_s[d, rows, lanes], _block_diag(v_new, half_masks))
                upd = _dot_tn(kg_s[d, rows, lanes], v_new.astype(BF16))
                st_s[d, g] = s * gl_s[d, r * 8:r * 8 + 1, lanes] + jnp.where(same_head, upd, 0.0)
                o_refs[d][rows, lanes] = o
        return run

    def solve(units):
        r0s = [jnp.where(ixj == 1, un[3], 0.0) for un in units]
        zs = [un[3] - _dot(un[3].astype(BF16), _block_diag(r0, half_masks)) for un, r0 in zip(units, r0s)]
        xs = [jnp.where(eye, 1.0, 0.0) - r0 for r0 in r0s]
        for lvl in range(1, 6):
            links = [_block_diag(jnp.where((ixj >> lvl) == 1, z, 0.0), half_masks) for z in zs]
            if lvl < 5:
                prods = [_dot(jnp.concatenate([z, x], axis=0).astype(BF16), lk)
                         for z, x, lk in zip(zs, xs, links)]
                zs = [z - p[0:CHUNK] for z, p in zip(zs, prods)]
                xs = [x - p[CHUNK:2 * CHUNK] for x, p in zip(xs, prods)]
            else:
                xs = [x - _dot(x.astype(BF16), lk) for x, lk in zip(xs, links)]
        for x, (d, rows, lanes, _, vb, kbe) in zip(xs, units):
            rhs = jnp.concatenate([_block_diag(vb, half_masks), _block_diag(kbe, half_masks)], axis=1)
            uw = _dot(x.astype(BF16), rhs)
            u_s[d, rows, lanes] = uw[:, 0:GROUP_W]
            w_s[d, rows, lanes] = uw[:, GROUP_W:2 * GROUP_W].astype(BF16)

    units = []
    for f in prep_thunks(units):
        f()
    solve(units)
    for c in range(CHUNKS_PER_RB):
        for d in range(2):
            scan_thunk(d, c)()

    @pl.when(j == nb - 1)
    def _():
        for d in range(2):
            for g in range(N_GROUPS):
                s = st_s[d, g]
                nat = s[:, 0:DV]
                for hh in range(1, HEAD_GROUP):
                    nat = nat + s[:, hh * DV:(hh + 1) * DV]
                sfin_ref[0, n_prev, d, g * GROUP_W:(g + 1) * GROUP_W, :] = nat
        for p in range(n_prev):
            sfin_ref[0, p] = sprev_ref[0, p]


def _delta_call(pd, conv_w, par, ones_bd, s0, n_seq, nb, blk0, prev_states=None):
    zero_init = s0 is None
    n_prev = 0 if prev_states is None else prev_states.shape[1]
    n_blocks_all = T_ALL // DELTA_RB
    blk_of = (lambda s, j: blk0 + s * nb + j, lambda s, j: blk0 + s * nb + nb - 1 - j)
    in_specs, args = [], []
    for d in range(1 if nb == 1 else 2):
        prev, nxt = _halo_specs(QKV_W, DELTA_RB, n_blocks_all, blk_of[d])
        in_specs += [pl.BlockSpec((DELTA_RB, PD_W), lambda s, j, d=d: (blk_of[d](s, j), 0)), prev, nxt]
        args += [pd, pd, pd]
    in_specs += [
        pl.BlockSpec((3, QKV_W), lambda s, j: (0, 0)),
        pl.BlockSpec((8, LANE), lambda s, j: (0, 0)),
        pl.BlockSpec((GROUP_W, GROUP_W), lambda s, j: (0, 0)),
    ]
    args += [conv_w, par, ones_bd]
    if not zero_init:
        in_specs.append(pl.BlockSpec((1, 2, H_A * DK, DV), lambda s, j: (s, 0, 0, 0)))
        args.append(s0)
    if n_prev:
        in_specs.append(pl.BlockSpec((1, n_prev, 2, H_A * DK, DV), lambda s, j: (s, 0, 0, 0, 0)))
        args.append(prev_states)
    rows = n_seq * nb * DELTA_RB
    dir_buf = lambda n, dt=F32: pltpu.VMEM((2, n, H_A * DK), dt)
    return pl.pallas_call(
        functools.partial(_delta_kernel, nb, zero_init, n_prev),
        grid=(n_seq, nb),
        in_specs=in_specs,
        out_specs=[pl.BlockSpec((DELTA_RB, W_A), lambda s, j: (s * nb + j, 0)),
                   pl.BlockSpec((DELTA_RB, W_A), lambda s, j: (s * nb + nb - 1 - j, 0)),
                   pl.BlockSpec((1, n_prev + 1, 2, H_A * DK, DV), lambda s, j: (s, 0, 0, 0, 0))],
        out_shape=[jax.ShapeDtypeStruct((rows, W_A), F32),
                   jax.ShapeDtypeStruct((rows, W_A), F32),
                   jax.ShapeDtypeStruct((n_seq, n_prev + 1, 2, H_A * DK, DV), F32)],
        scratch_shapes=[dir_buf(DELTA_RB)] + [dir_buf(DELTA_RB, BF16) for _ in range(4)]
        + [dir_buf(CHUNKS_PER_RB * 8), pltpu.VMEM((2, N_GROUPS, GROUP_W, GROUP_W), F32)],
        compiler_params=_cparams(("parallel", "arbitrary"), VMEM_LIMIT),
        name=f"deltanet_nb{nb}",
    )(*args)


FFN_TN = D_FF // 2


def _postmix_kernel(final_norm, tile0, ofc_ref, ofl_ref, obc_ref, obl_ref, z_ref, ybc_ref, ybl_ref,
                    ycc_ref, ycl_ref, pg_ref, x_ref, mod_ref, na_ref, ones_ref, wpa_ref, wpb_ref, wpc_ref,
                    wo_ref, g2_ref, wgu_ref, wdn_ref, nf_ref, out_ref):
    is_ctx = pl.program_id(0) + tile0 < N_CTX_TILES
    m = mod_ref[0]
    o = jnp.where(is_ctx, ofc_ref[...] + obc_ref[...], ofl_ref[...] + obl_ref[...])
    yb = jnp.where(is_ctx, ybc_ref[...], ybl_ref[...])
    yc = jnp.where(is_ctx, ycc_ref[...], ycl_ref[...])
    ms = _head_sums([o * o], ones_ref[...])[0] * (1.0 / DV)
    ya = (o * lax.rsqrt(ms + EPS) * na_ref[...]) * _silu(z_ref[...])
    merged = (pg_ref[:, 0:D_MODEL].astype(F32) * _dot(ya.astype(BF16), wpa_ref[0])
              + pg_ref[:, D_MODEL:2 * D_MODEL].astype(F32) * _dot(yb.astype(BF16), wpb_ref[0])
              + pg_ref[:, 2 * D_MODEL:3 * D_MODEL].astype(F32) * _dot(yc.astype(BF16), wpc_ref[0]))
    x = x_ref[...] + m[2:3] * _dot(merged.astype(BF16), wo_ref[0])

    h = _rms_mod(x, g2_ref[...], m[4:5], m[3:4]).astype(BF16)
    acc = None
    for c in range(0, D_FF, FFN_TN):
        gate = _dot(h, wgu_ref[0, :, c:c + FFN_TN])
        up = _dot(h, wgu_ref[0, :, D_FF + c:D_FF + c + FFN_TN])
        part = _dot((_silu(gate) * up).astype(BF16), wdn_ref[0, c:c + FFN_TN, :])
        acc = part if acc is None else acc + part
    xn = x + m[5:6] * acc
    if final_norm:
        ms = jnp.mean(xn * xn, axis=-1, keepdims=True)
        xn = xn * lax.rsqrt(ms + EPS) * nf_ref[...]
    out_ref[...] = xn


def _postmix_call(o_ctx, o_lat, pd, yb, yc, pg, x, mod3, na512, ones_bd, wpa, wpb, wpc, wo,
                  g2, wgu, wdn, nf, layer, final_norm, tile0=0, n_tiles=T_ALL // ROW_TILE):
    row = lambda i: (i + tile0, 0)
    const = lambda i: (0, 0)
    lyr = lambda i: (layer, 0, 0)
    assert W_A == W_B == W_C
    return pl.pallas_call(
        functools.partial(_postmix_kernel, final_norm, tile0),
        grid=(n_tiles,),
        in_specs=[
            *_ctx_lat_specs(W_A, tile0), *_ctx_lat_specs(W_A, tile0),
            pl.BlockSpec((ROW_TILE, W_A), lambda i: (i + tile0, OFF_Z // W_A)),
            *_ctx_lat_specs(W_B, tile0), *_ctx_lat_specs(W_C, tile0),
            pl.BlockSpec((ROW_TILE, 3 * D_MODEL), row),
            pl.BlockSpec((ROW_TILE, D_MODEL), row),
            pl.BlockSpec((1, 6, D_MODEL), lambda i: (_mod_row_block(i + tile0), 0, 0)),
            pl.BlockSpec((1, W_A), const),
            _resident((GROUP_W, GROUP_W), const),
            _resident((1, W_A, D_MODEL), lyr),
            _resident((1, W_B, D_MODEL), lyr),
            _resident((1, W_C, D_MODEL), lyr),
            _resident((1, D_MODEL, D_MODEL), lyr),
            pl.BlockSpec((1, D_MODEL), const),
            _resident((1, D_MODEL, 2 * D_FF), lyr),
            _resident((1, D_FF, D_MODEL), lyr),
            pl.BlockSpec((1, D_MODEL), const),
        ],
        out_specs=pl.BlockSpec((ROW_TILE, D_MODEL), lambda i: (i, 0)),
        out_shape=jax.ShapeDtypeStruct((n_tiles * ROW_TILE, D_MODEL), F32),
        compiler_params=_cparams(("parallel",), VMEM_LIMIT),
        name="postmix_final" if final_norm else "postmix",
    )(o_ctx[0], o_lat[0], o_ctx[1], o_lat[1], pd, yb[0], yb[1], yc[0], yc[1], pg, x, mod3, na512,
      ones_bd, wpa, wpb, wpc, wo, g2.reshape(1, D_MODEL), wgu, wdn, nf.reshape(1, D_MODEL))


TABLE_SPLIT = 64


def _grid_pos_embed(n_tokens):
    rows = n_tokens // GRID_W
    quarter = D_MODEL // 4
    omega = 1.0 / (10000.0 ** (jnp.arange(quarter, dtype=F32) / quarter))

    def emb(pos):
        a = pos[:, None] * omega[None, :]
        return jnp.concatenate([jnp.sin(a), jnp.cos(a)], axis=-1)

    e_row, e_col = lax.optimization_barrier((emb(jnp.arange(rows).astype(F32)),
                                             emb(jnp.arange(GRID_W).astype(F32))))
    return jnp.concatenate([jnp.repeat(e_row, GRID_W, axis=0), jnp.tile(e_col, (rows, 1))], axis=-1)


def _cos_nsin_tables(n, period):
    t = jnp.arange(n, dtype=jnp.int32)[None, :]

    def cs(r):
        ang = ((r * t) % period).astype(F32) * (2.0 * math.pi / period)
        return jnp.cos(ang), jnp.sin(ang)

    ca, sa = cs(jnp.arange(n // TABLE_SPLIT, dtype=jnp.int32)[:, None] * TABLE_SPLIT)
    cb, sb = cs(jnp.arange(TABLE_SPLIT, dtype=jnp.int32)[:, None])
    ca, sa, cb, sb = lax.optimization_barrier((ca, sa, cb, sb))
    ca, sa = ca[:, None, :], sa[:, None, :]
    cos = (ca * cb[None] - sa * sb[None]).reshape(n, n)
    nsin = (-(sa * cb[None] + ca * sb[None])).reshape(n, n)
    return cos.astype(BF16), nsin.astype(BF16)


def _hyena_positions(L):
    bands = (HY_EMB - 1) // 2
    t = jnp.linspace(0.0, 1.0, L, dtype=F32)[:, None]
    wpos = (2.0 * math.pi / L) * jnp.arange(L, dtype=F32)[:, None]
    fr = jnp.linspace(1e-4, bands - 1, bands, dtype=F32)[None, :]
    zpos = jnp.concatenate([t, jnp.cos(fr * wpos), -jnp.sin(fr * wpos)], axis=-1)
    zpos = jnp.pad(zpos, ((0, 0), (0, LANE - HY_EMB)))
    deltas = jnp.abs(jnp.linspace(math.log(HY_DECAY_TARGET) / HY_SLOW_PCT,
                                  math.log(HY_DECAY_TARGET) / HY_FAST_PCT, W_B, dtype=F32))
    window = jnp.exp(-t * deltas[None, :])
    return zpos, window


def _group_tables():
    r = jnp.arange(DC, dtype=jnp.int32)
    ang = ((r[:, None] * r[None, :]) % DC).astype(F32) * (2.0 * math.pi / DC)
    eye = jnp.eye(G_C, dtype=F32)
    return jnp.kron(eye, jnp.cos(ang)).astype(BF16), jnp.kron(eye, jnp.sin(ang)).astype(BF16)


def _head_ones():
    return jnp.kron(jnp.eye(HEAD_GROUP, dtype=F32), jnp.ones((DK, DK), F32)).astype(BF16)


def kernel(x_prompt, x_sample, state_delta, c, c_ctx, w_mod, b_mod, norm1_g, norm2_g, w_in, conv_qkv, a_log, dt_bias, norm_a, conv_hy, hy_w1, hy_b1, hy_freq, hy_w2, hy_b2, hy_w3, hy_bias, w_pa, w_pb, w_pc, w_o, w_gu, w_down, norm_f):
    assert x_prompt.shape == (N_CTX_SEQ, L_CTX, D_MODEL) and x_sample.shape == (N_LAT_SEQ, L_LAT, D_MODEL)
    st = jnp.pad(jnp.concatenate([c_ctx[None], c], axis=0).T, ((0, 0), (0, 8 - 1 - N_LAT_SEQ)))
    mod = _mod_call(st, w_mod, b_mod).reshape(DEPTH, 8, 6, D_MODEL)

    ones_bd = _head_ones()
    bdc, bds = _group_tables()
    seqs = ((L_CTX, N_CTX_SEQ, 0), (L_LAT, N_LAT_SEQ, T_CTX // L_LAT))
    tables = {L: (_cos_nsin_tables(L, 2 * L), _cos_nsin_tables(L, L), _hyena_positions(L))
              for L, _, _ in seqs}

    w_in_b = jnp.swapaxes(w_in, 1, 2).astype(BF16)
    w_pa_b, w_pb_b, w_pc_b, w_o_b, w_gu_b, w_down_b = (
        w.astype(BF16) for w in (w_pa, w_pb, w_pc, w_o, w_gu, w_down))

    x = None
    ctx_states = None
    for l in range(DEPTH):
        mod3 = mod[l, 0:3]
        if l == 0:
            xs = (x_prompt.reshape(T_CTX, D_MODEL), x_sample.reshape(T_LAT, D_MODEL), _grid_pos_embed(L_LAT))
            pd, ph, pf, pg, x = _inproj_call(xs, mod3, norm1_g[l], w_in_b, l)
        else:
            pd, ph, pf, pg = _inproj_call((x,), mod3, norm1_g[l], w_in_b, l)

        par = jnp.zeros((8, LANE), F32)
        par = par.at[0, 2 * H_A:4 * H_A].set(a_log[l].reshape(-1))
        par = par.at[1, 2 * H_A:4 * H_A].set(dt_bias[l].reshape(-1))
        *o_ctx, ctx_states = _delta_call(pd, conv_qkv[l], par, ones_bd, None,
                                         N_CTX_SEQ, L_CTX // DELTA_RB, 0, ctx_states)
        s0 = state_delta[:, l].astype(F32).reshape(N_LAT_SEQ, 2, H_A * DK, DV)
        *o_lat, _ = _delta_call(pd, conv_qkv[l], par, ones_bd, s0,
                                N_LAT_SEQ, L_LAT // DELTA_RB, T_CTX // DELTA_RB)

        w1p = jnp.pad(hy_w1[l], ((0, LANE - HY_EMB), (0, 0)))
        yb, yc = [], []
        for L, n_seq, blk0 in seqs:
            (cos2, nsin2), (cos1, nsin1), (zpos, window) = tables[L]
            kspec = _filter_call(L, zpos, w1p, hy_b1[l][None], hy_freq[l][None], hy_w2[l],
                                 hy_b2[l][None], hy_w3[l], window, cos2, nsin2)
            bias = hy_bias[l][:, None, :]
            if L == HYENA_FT:
                assert blk0 == 0
                yb.append(_hyena_short_call(ph, conv_hy[l], kspec, bias, cos2, nsin2, n_seq, L))
                yc.append(_fnet_short_call(pf, cos1, nsin1, bdc, bds, n_seq, L))
            else:
                yb.append(_hyena_call(ph, conv_hy[l], kspec, bias, cos2, nsin2, n_seq, L, blk0, HYENA_FT))
                yc.append(_fnet_call(pf, cos1, nsin1, bdc, bds, n_seq, L, blk0))

        na512 = jnp.tile(norm_a[l], H_A)[None]
        post = functools.partial(_postmix_call, o_ctx, o_lat, pd, yb, yc, pg, x, mod3, na512, ones_bd,
                                 w_pa_b, w_pb_b, w_pc_b, w_o_b, norm2_g[l], w_gu_b, w_down_b, norm_f, l)
        if l < DEPTH - 1:
            x = post(False)
        else:
            y_prompt = post(True, 0, N_CTX_TILES).reshape(N_CTX_SEQ, L_CTX, D_MODEL)
            y_sample = post(True, N_CTX_TILES, N_LAT_TILES).reshape(N_LAT_SEQ, L_LAT, D_MODEL)

    new_state = ctx_states.reshape(N_CTX_SEQ, DEPTH, 2, H_A, DK, DV).astype(x_prompt.dtype)
    return (y_prompt, y_sample, new_state)
```

```python
import functools
import math

import jax
import jax.numpy as jnp
from jax import lax
from jax.experimental import pallas as pl
from jax.experimental.pallas import tpu as pltpu

F32 = jnp.float32
BF16 = jnp.bfloat16

D_MODEL = 1024
N_CTX_SEQ = 32
L_CTX = 256
DEPTH = 2
N_LAT_SEQ = 2
L_LAT = 2048
GRID_W = 64
EPS = 1e-6
H_A = 8
DK = 64
DV = 64
W_A = H_A * DV
QKV_W = 2 * H_A * DK + H_A * DV
CHUNK = 64
W_B = 512
HY_EMB = 33
HY_HID = 64
HY_DECAY_TARGET = 1e-2
HY_FAST_PCT = 0.3
HY_SLOW_PCT = 1.5
G_C = 8
DC = 64
W_C = G_C * DC
D_FF = ((8 * D_MODEL + 3 * 256 - 1) // (3 * 256)) * 256
OFF_Z = QKV_W
OFF_B = OFF_Z + W_A
OFF_A = OFF_B + 2 * H_A
OFF_HY = OFF_A + 2 * H_A
OFF_FN = OFF_HY + 3 * W_B
OFF_GATE = OFF_FN + W_C

T_CTX = N_CTX_SEQ * L_CTX
T_LAT = N_LAT_SEQ * L_LAT
T_ALL = T_CTX + T_LAT
ROW_TILE = 256
N_CTX_TILES = T_CTX // ROW_TILE
N_LAT_TILES = T_LAT // ROW_TILE
LANE = 128
PD_W = QKV_W + W_A + LANE
HEAD_GROUP = 4
GROUP_W = HEAD_GROUP * DK
N_GROUPS = H_A // HEAD_GROUP
VMEM_LIMIT = 56 * 1024 * 1024


def _cparams(sem, vmem=None):
    return pltpu.CompilerParams(dimension_semantics=sem, vmem_limit_bytes=vmem)


def _dot(a, b):
    return jnp.dot(a, b, preferred_element_type=F32)


def _dot_nt(a, b):
    return lax.dot_general(a, b, (((1,), (1,)), ((), ())), preferred_element_type=F32)


def _split(a, n):
    parts = []
    rem = a
    for i in range(n):
        p = rem.astype(BF16)
        parts.append(p)
        if i + 1 < n:
            rem = rem - p.astype(F32)
    return parts


def _mm3(a, b):
    ah, al = _split(a, 2)
    bh, bl = _split(b, 2)
    return _dot(ah, bh) + (_dot(ah, bl) + _dot(al, bh))


def _sigmoid(x):
    return 1.0 / (1.0 + jnp.exp(-x))


def _silu(x):
    return x * _sigmoid(x)


def _softplus(x):
    return jnp.maximum(x, 0.0) + jnp.log(1.0 + jnp.exp(-jnp.abs(x)))


def _mod_row_block(i):
    per_lat = L_LAT // ROW_TILE
    return jnp.where(i < N_CTX_TILES, 0, 1 + (i - N_CTX_TILES) // per_lat)


def _ctx_tile(i):
    return jnp.minimum(i, N_CTX_TILES - 1)


def _lat_tile(i):
    return jnp.maximum(i - N_CTX_TILES, 0)


def _ctx_lat_specs(width, tile0=0):
    return (pl.BlockSpec((ROW_TILE, width), lambda i: (_ctx_tile(i + tile0), 0)),
            pl.BlockSpec((ROW_TILE, width), lambda i: (_lat_tile(i + tile0), 0)))


MOD_TN = 512


def _mod_kernel(st_ref, w_ref, b_ref, out_ref):
    s = _silu(st_ref[...])
    w = w_ref[0]
    rows = [jnp.sum(s[:, r:r + 1] * w, axis=0, keepdims=True) + b_ref[0] for r in range(3)]
    rows.append(jnp.zeros((5, MOD_TN), F32))
    out_ref[0] = jnp.concatenate(rows, axis=0)


def _mod_call(st, w_mod, b_mod):
    n6 = 6 * D_MODEL
    return pl.pallas_call(
        _mod_kernel,
        grid=(DEPTH, n6 // MOD_TN),
        in_specs=[
            pl.BlockSpec((D_MODEL, 8), lambda l, j: (0, 0)),
            pl.BlockSpec((1, D_MODEL, MOD_TN), lambda l, j: (l, 0, j)),
            pl.BlockSpec((1, 1, MOD_TN), lambda l, j: (l, 0, j)),
        ],
        out_specs=pl.BlockSpec((1, 8, MOD_TN), lambda l, j: (l, 0, j)),
        out_shape=jax.ShapeDtypeStruct((DEPTH, 8, n6), F32),
        compiler_params=_cparams(("parallel", "parallel")),
        name="adaln_mod",
    )(st, w_mod, b_mod.reshape(DEPTH, 1, n6))


INPROJ_TN = 512
INPROJ_WIDTHS = (PD_W, 3 * W_B, W_C, 3 * D_MODEL)
INPROJ_FEATURES = (OFF_HY, 3 * W_B, W_C, 3 * D_MODEL)


def _rms_mod(x, g, scale, shift):
    ms = jnp.mean(x * x, axis=-1, keepdims=True)
    return (x * lax.rsqrt(ms + EPS) * g) * (1.0 + scale) + shift


def _inproj_kernel(first, *refs):
    if first:
        (xc_ref, xl_ref, pos_ref, mod_ref, g_ref, w_ref, pd_ref, ph_ref, pf_ref, pg_ref, x_ref) = refs
        x = jnp.where(pl.program_id(0) < N_CTX_TILES, xc_ref[...], xl_ref[...] + pos_ref[...])
        x_ref[...] = x
    else:
        (xin_ref, mod_ref, g_ref, w_ref, pd_ref, ph_ref, pf_ref, pg_ref) = refs
        x = xin_ref[...]
    m = mod_ref[0]
    h = _rms_mod(x, g_ref[...], m[1:2], m[0:1]).astype(BF16)
    row0 = 0
    for o_ref, n_feat in zip((pd_ref, ph_ref, pf_ref, pg_ref), INPROJ_FEATURES):
        n = o_ref.shape[1]
        for c in range(0, n, INPROJ_TN):
            e = min(c + INPROJ_TN, n)
            ef = min(e, n_feat)
            y = _dot_nt(h, w_ref[0, row0 + c:row0 + ef, :])
            if ef < e:
                y = jnp.concatenate([y, jnp.zeros((y.shape[0], e - ef), F32)], axis=1)
            if o_ref is pg_ref:
                y = _sigmoid(y)
            o_ref[:, c:e] = y.astype(o_ref.dtype)
        row0 += n_feat


def _resident(shape, index_map):
    return pl.BlockSpec(shape, index_map, pipeline_mode=pl.Buffered(1))


def _inproj_call(xs, mod3, g, w_all, layer):
    first = len(xs) == 3
    widths = INPROJ_WIDTHS
    row = lambda i: (i, 0)
    const = lambda i: (0, 0)
    if first:
        per_lat = L_LAT // ROW_TILE
        x_specs = list(_ctx_lat_specs(D_MODEL)) + [
            pl.BlockSpec((ROW_TILE, D_MODEL), lambda i: (_lat_tile(i) % per_lat, 0))]
    else:
        x_specs = [pl.BlockSpec((ROW_TILE, D_MODEL), row)]
    out_widths = widths + ((D_MODEL,) if first else ())
    out_dtypes = (F32, F32, F32, BF16) + ((F32,) if first else ())
    return pl.pallas_call(
        functools.partial(_inproj_kernel, first),
        grid=(T_ALL // ROW_TILE,),
        in_specs=x_specs + [
            pl.BlockSpec((1, 6, D_MODEL), lambda i: (_mod_row_block(i), 0, 0)),
            pl.BlockSpec((1, D_MODEL), const),
            _resident((1, sum(INPROJ_FEATURES), D_MODEL), lambda i: (layer, 0, 0)),
        ],
        out_specs=[pl.BlockSpec((ROW_TILE, w), row) for w in out_widths],
        out_shape=[jax.ShapeDtypeStruct((T_ALL, w), dt) for w, dt in zip(out_widths, out_dtypes)],
        compiler_params=_cparams(("parallel",), VMEM_LIMIT),
        name="inproj_first" if first else "inproj",
    )(*xs, mod3, g.reshape(1, D_MODEL), w_all)


def _conv3_rows(cur, prev_row, next_row, w):
    n = cur.shape[0]
    ridx = lax.broadcasted_iota(jnp.int32, cur.shape, 0)
    up = jnp.where(ridx == 0, prev_row, pltpu.roll(cur, 1, 0))
    dn = jnp.where(ridx == n - 1, next_row, pltpu.roll(cur, n - 1, 0))
    return up * w[0:1] + cur * w[1:2] + dn * w[2:3]


def _halo_specs(width, rows_per_block, n_row_blocks, blk_of):
    per = rows_per_block // 8
    last = n_row_blocks * per - 1
    prev = pl.BlockSpec((8, width), lambda *a: (jnp.maximum(blk_of(*a) * per - 1, 0), 0))
    nxt = pl.BlockSpec((8, width), lambda *a: (jnp.minimum((blk_of(*a) + 1) * per, last), 0))
    return prev, nxt


FILT_RT = 256


def _alternating_sum(x):
    t = lax.broadcasted_iota(jnp.int32, x.shape, 0)
    return jnp.sum(jnp.where(t % 2 == 0, x, -x), axis=0, keepdims=True)


def _filter_kernel(L, zpos_ref, w1_ref, b1_ref, fq_ref, w2_ref, b2_ref, w3_ref, win_ref,
                   cos_ref, nsin_ref, k_ref, hs_s, hm_s, krl_s):
    rt = pl.program_id(1)

    @pl.when(rt == 0)
    def _():
        fq = fq_ref[...]
        alt_acc = jnp.zeros((1, W_B), F32)
        for r0 in range(0, L, FILT_RT):
            rows = slice(r0, r0 + FILT_RT)
            h = jnp.sin(fq * (_mm3(zpos_ref[rows, :], w1_ref[...]) + b1_ref[...]))
            h = jnp.sin(fq * (_mm3(h, w2_ref[...]) + b2_ref[...]))
            hf = _mm3(h, w3_ref[...])
            win = win_ref[rows, :]
            fw = hf[:, 0:W_B] * win
            bw = hf[:, W_B:2 * W_B] * win
            hsum = fw + bw
            hs_s[rows, :] = hsum.astype(BF16)
            hm_s[rows, :] = (fw - bw).astype(BF16)
            alt_acc = alt_acc + _alternating_sum(hsum)
        krl_s[...] = jnp.broadcast_to(alt_acc, krl_s.shape)

    p1 = _dot(cos_ref[...], hs_s[...])
    p2 = _dot(nsin_ref[...], hm_s[...])
    first = (rt * FILT_RT + lax.broadcasted_iota(jnp.int32, p1.shape, 0)) == 0
    k_ref[0, 0] = p1
    k_ref[0, 1] = jnp.where(first, krl_s[0:1, :], p1)
    k_ref[0, 2] = jnp.where(first, 0.0, p2)


def _filter_call(L, zpos, w1p, b1, fq, w2, b2, w3, win, cos, nsin):
    nrt = L // FILT_RT
    c2 = lambda o, r: (0, 0)
    return pl.pallas_call(
        functools.partial(_filter_kernel, L),
        grid=(2, nrt),
        in_specs=[
            pl.BlockSpec((L, LANE), c2),
            pl.BlockSpec((LANE, HY_HID), c2),
            pl.BlockSpec((1, HY_HID), c2),
            pl.BlockSpec((1, HY_HID), c2),
            pl.BlockSpec((HY_HID, HY_HID), c2),
            pl.BlockSpec((1, HY_HID), c2),
            pl.BlockSpec((HY_HID, 2 * W_B), lambda o, r: (0, o)),
            pl.BlockSpec((L, W_B), c2),
            pl.BlockSpec((FILT_RT, L), lambda o, r: (r, 0)),
            pl.BlockSpec((FILT_RT, L), lambda o, r: (r, 0)),
        ],
        out_specs=pl.BlockSpec((1, 3, FILT_RT, W_B), lambda o, r: (o, 0, r, 0)),
        out_shape=jax.ShapeDtypeStruct((2, 3, L, W_B), F32),
        scratch_shapes=[pltpu.VMEM((L, W_B), BF16), pltpu.VMEM((L, W_B), BF16),
                        pltpu.VMEM((8, W_B), F32)],
        compiler_params=_cparams(("parallel", "arbitrary"), VMEM_LIMIT),
        name=f"hyena_filter_{L}",
    )(zpos, w1p, b1, fq, w2, b2, w3, win, cos, nsin)


HYENA_FT = 256

HYENA_CONV_ROWS = 256


HYENA_CH = 256
HYENA_OUT_ROWS = 512


def _hyena_kernel(L, ft, x1_ref, x2_ref, v_ref, cw1_ref, cw2_ref, cwv_ref, k_ref, bias_ref, cos_ref, nsin_ref,
                  out_ref, gate_s, zf_s, zb_s, yt_s, yb_s):
    o = pl.program_id(2)

    @pl.when(o == 0)
    def _():
        for src, cw_ref, dst in ((x1_ref, cw1_ref, 0), (x2_ref, cw2_ref, 1), (v_ref, cwv_ref, None)):
            cw = cw_ref[...]
            for r0 in range(0, L, HYENA_CONV_ROWS):
                r1 = r0 + HYENA_CONV_ROWS
                prev_row = src[r0 - 1:r0, :] if r0 > 0 else 0.0
                next_row = src[r1:r1 + 1, :] if r1 < L else 0.0
                uc = _conv3_rows(src[r0:r1, :], prev_row, next_row, cw)
                if dst is None:
                    zf_s[r0:r1, :] = uc
                    zb_s[r0:r1, :] = uc.astype(BF16)
                else:
                    gate_s[dst, r0:r1, :] = uc

    zb = zb_s[...]
    nyq = _alternating_sum(zf_s[...])
    for f in range(L // ft):
        rows = slice(f * ft, (f + 1) * ft)
        top = _dot(cos_ref[rows, :], zb)
        bot = _dot(nsin_ref[rows, :], zb)
        if f == 0:
            first = lax.broadcasted_iota(jnp.int32, top.shape, 0) == 0
            bot = jnp.where(first, nyq, bot)
        krt = k_ref[0, 0, rows, :]
        krb = k_ref[0, 1, rows, :]
        ki = k_ref[0, 2, rows, :]
        yt = top * krt - bot * ki
        yb = top * ki + bot * krb
        if f == 0:
            yt = jnp.where(first, 0.5 * yt, yt)
            y_nyq = yb[0:1, :]
        yt_s[rows, :] = yt.astype(BF16)
        yb_s[rows, :] = yb.astype(BF16)

    t = lax.broadcasted_iota(jnp.int32, (HYENA_OUT_ROWS, HYENA_CH), 0)
    alt_half = jnp.where(t % 2 == 0, 0.5, -0.5)
    for r0 in range(0, L, HYENA_OUT_ROWS):
        rows = slice(r0, r0 + HYENA_OUT_ROWS)
        acc = _dot(cos_ref[rows, :], yt_s[...]) + _dot(nsin_ref[rows, :], yb_s[...]) + alt_half * y_nyq
        znew = gate_s[o, rows, :] * (acc * (1.0 / L) + bias_ref[0] * zf_s[rows, :])
        zf_s[rows, :] = znew
        zb_s[rows, :] = znew.astype(BF16)
        out_ref[rows, :] = znew.astype(out_ref.dtype)


def _hyena_call(ph, conv_w, kspec, bias, cos, nsin, n_seq, L, row_blk0, ft):
    nch = W_B // HYENA_CH
    once = pl.Buffered(1)
    col = lambda part: (lambda s, c, o: (row_blk0 + s, part * nch + c))
    cwcol = lambda part: (lambda s, c, o: (0, part * nch + c))
    return pl.pallas_call(
        functools.partial(_hyena_kernel, L, ft),
        grid=(n_seq, nch, 2),
        in_specs=[pl.BlockSpec((L, HYENA_CH), col(part), pipeline_mode=once) for part in range(3)]
        + [pl.BlockSpec((3, HYENA_CH), cwcol(part)) for part in range(3)]
        + [
            pl.BlockSpec((1, 3, L, HYENA_CH), lambda s, c, o: (o, 0, 0, c)),
            pl.BlockSpec((1, 1, HYENA_CH), lambda s, c, o: (o, 0, c)),
            pl.BlockSpec((L, L), lambda s, c, o: (0, 0), pipeline_mode=once),
            pl.BlockSpec((L, L), lambda s, c, o: (0, 0), pipeline_mode=once),
        ],
        out_specs=pl.BlockSpec((L, HYENA_CH), lambda s, c, o: (s, c)),
        out_shape=jax.ShapeDtypeStruct((n_seq * L, W_B), BF16),
        scratch_shapes=[pltpu.VMEM((2, L, HYENA_CH), F32), pltpu.VMEM((L, HYENA_CH), F32),
                        pltpu.VMEM((L, HYENA_CH), BF16), pltpu.VMEM((L, HYENA_CH), BF16),
                        pltpu.VMEM((L, HYENA_CH), BF16)],
        compiler_params=_cparams(("parallel", "parallel", "arbitrary"), VMEM_LIMIT),
        name=f"hyena_conv_{L}",
    )(ph, ph, ph, conv_w, conv_w, conv_w, kspec, bias, cos, nsin)


HYENA_SHORT_SB = 4


def _hyena_short_kernel(L, ph_ref, cw_ref, k_ref, bias_ref, cos_ref, nsin_ref, out_ref):
    cos = cos_ref[...]
    nsin = nsin_ref[...]
    cw = cw_ref[...]
    t = lax.broadcasted_iota(jnp.int32, (L, W_B), 0)
    first = t == 0
    alt_half = jnp.where(t % 2 == 0, 0.5, -0.5)
    rows = [slice(s * L, (s + 1) * L) for s in range(HYENA_SHORT_SB)]
    ucs = [_conv3_rows(ph_ref[r, :], 0.0, 0.0, cw) for r in rows]
    zs = [uc[:, 2 * W_B:3 * W_B] for uc in ucs]
    for o in range(2):
        krt, krb, ki = k_ref[o, 0], k_ref[o, 1], k_ref[o, 2]
        zbs = [z.astype(BF16) for z in zs]
        tops = [_dot(cos, zb) for zb in zbs]
        bots = [jnp.where(first, _alternating_sum(z), _dot(nsin, zb)) for z, zb in zip(zs, zbs)]
        yts = [top * krt - bot * ki for top, bot in zip(tops, bots)]
        ybs = [top * ki + bot * krb for top, bot in zip(tops, bots)]
        accs = [_dot(cos, jnp.where(first, 0.5 * yt, yt).astype(BF16)) + _dot(nsin, yb.astype(BF16))
                + alt_half * yb[0:1, :] for yt, yb in zip(yts, ybs)]
        zs = [uc[:, o * W_B:(o + 1) * W_B] * (acc * (1.0 / L) + bias_ref[o] * z)
              for uc, acc, z in zip(ucs, accs, zs)]
    for r, z in zip(rows, zs):
        out_ref[r, :] = z.astype(out_ref.dtype)


def _hyena_short_call(ph, conv_w, kspec, bias, cos, nsin, n_seq, L):
    rows = HYENA_SHORT_SB * L
    return pl.pallas_call(
        functools.partial(_hyena_short_kernel, L),
        grid=(n_seq // HYENA_SHORT_SB,),
        in_specs=[
            pl.BlockSpec((rows, 3 * W_B), lambda i: (i, 0)),
            pl.BlockSpec((3, 3 * W_B), lambda i: (0, 0)),
            pl.BlockSpec((2, 3, L, W_B), lambda i: (0, 0, 0, 0)),
            pl.BlockSpec((2, 1, W_B), lambda i: (0, 0, 0)),
            pl.BlockSpec((L, L), lambda i: (0, 0)),
            pl.BlockSpec((L, L), lambda i: (0, 0)),
        ],
        out_specs=pl.BlockSpec((rows, W_B), lambda i: (i, 0)),
        out_shape=jax.ShapeDtypeStruct((n_seq * L, W_B), BF16),
        compiler_params=_cparams(("parallel",), VMEM_LIMIT),
        name=f"hyena_conv_{L}",
    )(ph, conv_w, kspec, bias, cos, nsin)


FNET_RT = 256


def _fnet_kernel(L, x_ref, cos_ref, nsin_ref, bdc_ref, bds_ref, out_ref, xc_s, xs_s):
    r = pl.program_id(1)

    @pl.when(r == 0)
    def _():
        for r0 in range(0, L, FNET_RT):
            xb = x_ref[r0:r0 + FNET_RT, :].astype(BF16)
            xc_s[r0:r0 + FNET_RT, :] = _dot(xb, bdc_ref[...]).astype(BF16)
            xs_s[r0:r0 + FNET_RT, :] = _dot(xb, bds_ref[...]).astype(BF16)

    y = _dot(cos_ref[...], xc_s[...]) + _dot(nsin_ref[...], xs_s[...])
    out_ref[...] = (y * (1.0 / math.sqrt(DC * L))).astype(out_ref.dtype)


def _fnet_call(pf, cos, nsin, bdc, bds, n_seq, L, row_blk0):
    nrt = L // FNET_RT
    return pl.pallas_call(
        functools.partial(_fnet_kernel, L),
        grid=(n_seq, nrt),
        in_specs=[
            pl.BlockSpec((L, W_C), lambda s, r: (row_blk0 + s, 0)),
            pl.BlockSpec((FNET_RT, L), lambda s, r: (r, 0)),
            pl.BlockSpec((FNET_RT, L), lambda s, r: (r, 0)),
            pl.BlockSpec((W_C, W_C), lambda s, r: (0, 0)),
            pl.BlockSpec((W_C, W_C), lambda s, r: (0, 0)),
        ],
        out_specs=pl.BlockSpec((FNET_RT, W_C), lambda s, r: (s * nrt + r, 0)),
        out_shape=jax.ShapeDtypeStruct((n_seq * L, W_C), BF16),
        scratch_shapes=[pltpu.VMEM((L, W_C), BF16), pltpu.VMEM((L, W_C), BF16)],
        compiler_params=_cparams(("parallel", "arbitrary"), VMEM_LIMIT),
        name=f"fnet_{L}",
    )(pf, cos, nsin, bdc, bds)


FNET_SHORT_SB = 4


def _fnet_short_kernel(L, x_ref, cos_ref, nsin_ref, bdc_ref, bds_ref, out_ref):
    xb = x_ref[...].astype(BF16)
    xc = _dot(xb, bdc_ref[...]).astype(BF16)
    xs = _dot(xb, bds_ref[...]).astype(BF16)
    for s in range(FNET_SHORT_SB):
        rows = slice(s * L, (s + 1) * L)
        y = _dot(cos_ref[...], xc[rows]) + _dot(nsin_ref[...], xs[rows])
        out_ref[rows, :] = (y * (1.0 / math.sqrt(DC * L))).astype(out_ref.dtype)


def _fnet_short_call(pf, cos, nsin, bdc, bds, n_seq, L):
    rows = FNET_SHORT_SB * L
    const = lambda i: (0, 0)
    return pl.pallas_call(
        functools.partial(_fnet_short_kernel, L),
        grid=(n_seq // FNET_SHORT_SB,),
        in_specs=[pl.BlockSpec((rows, W_C), lambda i: (i, 0)),
                  pl.BlockSpec((L, L), const), pl.BlockSpec((L, L), const),
                  pl.BlockSpec((W_C, W_C), const), pl.BlockSpec((W_C, W_C), const)],
        out_specs=pl.BlockSpec((rows, W_C), lambda i: (i, 0)),
        out_shape=jax.ShapeDtypeStruct((n_seq * L, W_C), BF16),
        compiler_params=_cparams(("parallel",), VMEM_LIMIT),
        name=f"fnet_{L}",
    )(pf, cos, nsin, bdc, bds)


DELTA_RB = 256
CHUNKS_PER_RB = DELTA_RB // CHUNK


HEADS_PER_LANE_TILE = LANE // DK


def _block_diag(y, half_masks):
    yb = y.astype(BF16)
    zero = jnp.zeros((CHUNK, LANE), BF16)
    row_blocks = []
    for h in range(HEAD_GROUP):
        tile = h // HEADS_PER_LANE_TILE
        piece = yb[:, tile * LANE:(tile + 1) * LANE] * half_masks[h % HEADS_PER_LANE_TILE]
        row_blocks.append(jnp.concatenate(
            [piece if t == tile else zero for t in range(GROUP_W // LANE)], axis=1))
    return jnp.concatenate(row_blocks, axis=0)


def _stacked_const_rhs(arrs, c, n):
    m = arrs[0].shape[0]
    parts = [p for a in arrs for p in _split(a, n)]
    y = _dot(jnp.concatenate(parts, axis=0), c)
    outs = []
    for i in range(len(arrs)):
        acc = y[i * n * m:(i * n + 1) * m]
        for t in range(1, n):
            acc = acc + y[(i * n + t) * m:(i * n + t + 1) * m]
        outs.append(acc)
    return outs


def _head_sums(arrs, ones_group):
    groups = [_stacked_const_rhs([a[:, g * GROUP_W:(g + 1) * GROUP_W] for a in arrs], ones_group, 2)
              for g in range(N_GROUPS)]
    return [jnp.concatenate([groups[g][i] for g in range(N_GROUPS)], axis=1) for i in range(len(arrs))]


def _expand_heads(x, lane0):
    cols = [jnp.broadcast_to(x[:, lane0 + h:lane0 + h + 1], (x.shape[0], DK)) for h in range(H_A)]
    return jnp.concatenate(cols, axis=1)


def _const_lhs_split(c, b, n):
    w = b.shape[1]
    y = _dot(c, jnp.concatenate(_split(b, n), axis=1))
    acc = y[:, 0:w]
    for t in range(1, n):
        acc = acc + y[:, t * w:(t + 1) * w]
    return acc


def _delta_chunk_stages(pd_ref, prev_ref, next_ref, r, has_prev, has_next, dirs, cw, a_neg, dtb, ones_bd,
                        tri, half_masks, eye_tile):
    ch = {}
    rows = slice(r * CHUNK, (r + 1) * CHUNK)

    def conv():
        cur = pd_ref[rows, 0:QKV_W]
        if r == 0:
            prev_row = jnp.where(has_prev, prev_ref[7:8, :], 0.0)
        else:
            prev_row = pd_ref[r * CHUNK - 1:r * CHUNK, 0:QKV_W]
        if r == CHUNKS_PER_RB - 1:
            next_row = jnp.where(has_next, next_ref[0:1, :], 0.0)
        else:
            next_row = pd_ref[(r + 1) * CHUNK:(r + 1) * CHUNK + 1, 0:QKV_W]
        qkv = _silu(_conv3_rows(cur, prev_row, next_row, cw))
        ch["q"] = qkv[:, 0:H_A * DK]
        ch["k"] = qkv[:, H_A * DK:2 * H_A * DK]
        ch["v"] = qkv[:, 2 * H_A * DK:]

    def norms():
        q, k = ch.pop("q"), ch.pop("k")
        qss, kss = _head_sums([q * q, k * k], ones_bd)
        ch["qn"] = q * lax.rsqrt(qss + EPS) * (DK ** -0.5)
        ch["kn"] = k * lax.rsqrt(kss + EPS)

    def gram():
        ch["gram"], ch["knT"] = [], []
        for g in range(N_GROUPS):
            lanes = slice(g * GROUP_W, (g + 1) * GROUP_W)
            lhs = jnp.concatenate([ch["kn"][:, lanes].astype(BF16), ch["qn"][:, lanes].astype(BF16), eye_tile],
                                  axis=0)
            res = _dot_nt(lhs, _block_diag(ch["kn"][:, lanes], half_masks))
            ch["gram"].append(res[0:2 * CHUNK])
            ch["knT"].append(res[2 * CHUNK:3 * CHUNK])

    def decay():
        ba = pd_ref[rows, OFF_B:OFF_B + LANE]
        sig = _sigmoid(ba)
        glog = a_neg * _softplus(ba + dtb)
        ch["decay"] = {}
        for d in dirs:
            gcum = _const_lhs_split(tri[d], glog, 3)
            beta = _expand_heads(sig, d * H_A)
            gcc8 = _expand_heads(gcum, (2 + d) * H_A)
            ch["decay"][d] = (beta, gcc8)

    return ch, [conv, norms, gram, decay]


def _delta_kernel(nb, zero_init, n_prev, *refs):
    shared = nb == 1
    it = iter(refs)
    blocks = [(next(it), next(it), next(it))]
    if not shared:
        blocks.append((next(it), next(it), next(it)))
    cw_ref, par_ref, ones_ref = next(it), next(it), next(it)
    s0_ref = None if zero_init else next(it)
    sprev_ref = next(it) if n_prev else None
    o_refs = (next(it), next(it))
    sfin_ref = next(it)
    u_s, w_s, p_s, qg_s, kg_s, gl_s, st_s = (next(it) for _ in range(7))
    j = pl.program_id(1)

    ri = lax.broadcasted_iota(jnp.int32, (CHUNK, GROUP_W), 0)
    cj = lax.broadcasted_iota(jnp.int32, (CHUNK, GROUP_W), 1) % CHUNK
    ixj = ri ^ cj
    eye = ixj == 0
    eye_tile = jnp.where(eye, 1.0, 0.0).astype(BF16)
    hl =lax.broadcasted_iota(jnp.int32, (CHUNK, LANE), 1) // DK
    half_masks = tuple(jnp.where(hl == h, 1.0, 0.0).astype(BF16) for h in range(HEADS_PER_LANE_TILE))
    ti = lax.broadcasted_iota(jnp.int32, (CHUNK, CHUNK), 0)
    tm = lax.broadcasted_iota(jnp.int32, (CHUNK, CHUNK), 1)
    ri8 = lax.broadcasted_iota(jnp.int32, (CHUNK, H_A * DK), 0)
    cj8 = lax.broadcasted_iota(jnp.int32, (CHUNK, H_A * DK), 1) % CHUNK
    incl = (ri >= cj, ri <= cj)
    strict = (ri > cj, ri < cj)
    tri = tuple(jnp.where(m, 1.0, 0.0).astype(BF16) for m in (tm <= ti, tm >= ti))
    eye8 = ri8 == cj8
    last_row = (CHUNK - 1, 0)

    @pl.when(j == 0)
    def _():
        for d in range(2):
            for g in range(N_GROUPS):
                if zero_init:
                    st_s[d, g] = jnp.zeros((DK, GROUP_W), F32)
                else:
                    r0 = g * GROUP_W
                    st_s[d, g] = jnp.concatenate(
                        [s0_ref[0, d, r0 + hh * DK:r0 + (hh + 1) * DK, :] for hh in range(HEAD_GROUP)], axis=1)

    cw = cw_ref[...]
    a_neg = -jnp.exp(par_ref[0:1, :])
    dtb = par_ref[1:2, :]
    pos = (j, nb - 1 - j)
    scan_order = (tuple(range(CHUNKS_PER_RB)), tuple(reversed(range(CHUNKS_PER_RB))))

    def unit_thunk(d, r, ch, g, units):
        def run():
            rows = slice(r * CHUNK, (r + 1) * CHUNK)
            lanes = slice(g * GROUP_W, (g + 1) * GROUP_W)
            beta, gcc8 = ch["decay"][d]
            gcr = jnp.sum(jnp.where(eye8, gcc8, 0.0), axis=0, keepdims=True)[:, lanes]
            qn, kn, be, gcc = ch["qn"][:, lanes], ch["kn"][:, lanes], beta[:, lanes], gcc8[:, lanes]
            kq = ch["gram"][g]
            dec = jnp.exp(jnp.where(incl[d], gcc - gcr, -1e30))
            a = jnp.where(strict[d], kq[0:CHUNK] * be * dec, 0.0)
            eg = jnp.exp(gcc)
            gcl = gcc[last_row[d]:last_row[d] + 1, :]
            units.append((d, rows, lanes, a, ch["v"][:, lanes] * be, kn * be * eg))
            p_s[d, rows, lanes] = (kq[CHUNK:2 * CHUNK] * dec).astype(BF16)
            qg_s[d, rows, lanes] = (qn * eg).astype(BF16)
            kg_s[d, rows, lanes] = (ch["knT"][g] * jnp.exp(gcl - gcr)).astype(BF16)
            gl_s[d, r * 8:(r + 1) * 8, lanes] = jnp.broadcast_to(jnp.exp(gcl), (8, GROUP_W))
        return run

    def prep_thunks(units):
        thunks = []
        if shared:
            todo = [(0, r, (0, 1)) for r in range(CHUNKS_PER_RB)]
        else:
            todo = [(d, r, (d,)) for d in range(2) for r in scan_order[d]]
        for b, r, dirs in todo:
            ch, stages = _delta_chunk_stages(*blocks[b], r, pos[b] > 0, pos[b] < nb - 1, dirs, cw, a_neg, dtb,
                                             ones_ref[...], tri, half_masks, eye_tile)
            thunks += stages
            thunks += [unit_thunk(d, r, ch, g, units) for d in dirs for g in range(N_GROUPS)]
        return thunks

    def scan_thunk(d, c):
        def run():
            r = scan_order[d][c]
            rows = slice(r * CHUNK, (r + 1) * CHUNK)
            for g in range(N_GROUPS):
                lanes = slice(g * GROUP_W, (g + 1) * GROUP_W)
                s = st_s[d, g]
                wq = jnp.concatenate([w_s[d, rows, lanes], qg_s[d, rows, lanes]], axis=0)
                ws_qs = _dot(wq, _block_diag(s, half_masks))
                v_new = u_s[d, rows, lanes] - ws_qs[0:CHUNK]
                pk = jnp.concatenate([p_s[d, rows, lanes], kg_s[d, rows, lanes]], axis=0)
                po = _dot(pk, _block_diag(v_new, half_masks))
                st_s[d, g] = s * gl_s[d, r * 8:r * 8 + 1, lanes] + po[CHUNK:2 * CHUNK]
                o_refs[d][rows, lanes] = ws_qs[CHUNK:2 * CHUNK] + po[0:CHUNK]
        return run

    def solve(units):
        r0s = [jnp.where(ixj == 1, un[3], 0.0) for un in units]
        zs = [un[3] - _dot(un[3].astype(BF16), _block_diag(r0, half_masks)) for un, r0 in zip(units, r0s)]
        xs = [jnp.where(eye, 1.0, 0.0) - r0 for r0 in r0s]
        for lvl in range(1, 6):
            links = [_block_diag(jnp.where((ixj >> lvl) == 1, z, 0.0), half_masks) for z in zs]
            if lvl < 5:
                prods = [_dot(jnp.concatenate([z, x], axis=0).astype(BF16), lk)
                         for z, x, lk in zip(zs, xs, links)]
                zs = [z - p[0:CHUNK] for z, p in zip(zs, prods)]
                xs = [x - p[CHUNK:2 * CHUNK] for x, p in zip(xs, prods)]
            else:
                xs = [x - _dot(x.astype(BF16), lk) for x, lk in zip(xs, links)]
        for x, (d, rows, lanes, _, vb, kbe) in zip(xs, units):
            rhs = jnp.concatenate([_block_diag(vb, half_masks), _block_diag(kbe, half_masks)], axis=1)
            uw = _dot(x.astype(BF16), rhs)
            u_s[d, rows, lanes] = uw[:, 0:GROUP_W]
            w_s[d, rows, lanes] = uw[:, GROUP_W:2 * GROUP_W].astype(BF16)

    units = []
    for f in prep_thunks(units):
        f()
    solve(units)
    for c in range(CHUNKS_PER_RB):
        for d in range(2):
            scan_thunk(d, c)()

    @pl.when(j == nb - 1)
    def _():
        for d in range(2):
            for g in range(N_GROUPS):
                s = st_s[d, g]
                for hh in range(HEAD_GROUP):
                    r0 = g * GROUP_W + hh * DK
                    sfin_ref[0, n_prev, d, r0:r0 + DK, :] = s[:, hh * DV:(hh + 1) * DV]
        for p in range(n_prev):
            sfin_ref[0, p] = sprev_ref[0, p]


def _delta_call(pd, conv_w, par, ones_bd, s0, n_seq, nb, blk0, prev_states=None):
    zero_init = s0 is None
    n_prev = 0 if prev_states is None else prev_states.shape[1]
    n_blocks_all = T_ALL // DELTA_RB
    blk_of = (lambda s, j: blk0 + s * nb + j, lambda s, j: blk0 + s * nb + nb - 1 - j)
    in_specs, args = [], []
    for d in range(1 if nb == 1 else 2):
        prev, nxt = _halo_specs(QKV_W, DELTA_RB, n_blocks_all, blk_of[d])
        in_specs += [pl.BlockSpec((DELTA_RB, PD_W), lambda s, j, d=d: (blk_of[d](s, j), 0)), prev, nxt]
        args += [pd, pd, pd]
    in_specs += [
        pl.BlockSpec((3, QKV_W), lambda s, j: (0, 0)),
        pl.BlockSpec((8, LANE), lambda s, j: (0, 0)),
        pl.BlockSpec((GROUP_W, GROUP_W), lambda s, j: (0, 0)),
    ]
    args += [conv_w, par, ones_bd]
    if not zero_init:
        in_specs.append(pl.BlockSpec((1, 2, H_A * DK, DV), lambda s, j: (s, 0, 0, 0)))
        args.append(s0)
    if n_prev:
        in_specs.append(pl.BlockSpec((1, n_prev, 2, H_A * DK, DV), lambda s, j: (s, 0, 0, 0, 0)))
        args.append(prev_states)
    rows = n_seq * nb * DELTA_RB
    dir_buf = lambda n, dt=F32: pltpu.VMEM((2, n, H_A * DK), dt)
    return pl.pallas_call(
        functools.partial(_delta_kernel, nb, zero_init, n_prev),
        grid=(n_seq, nb),
        in_specs=in_specs,
        out_specs=[pl.BlockSpec((DELTA_RB, W_A), lambda s, j: (s * nb + j, 0)),
                   pl.BlockSpec((DELTA_RB, W_A), lambda s, j: (s * nb + nb - 1 - j, 0)),
                   pl.BlockSpec((1, n_prev + 1, 2, H_A * DK, DV), lambda s, j: (s, 0, 0, 0, 0))],
        out_shape=[jax.ShapeDtypeStruct((rows, W_A), F32),
                   jax.ShapeDtypeStruct((rows, W_A), F32),
                   jax.ShapeDtypeStruct((n_seq, n_prev + 1, 2, H_A * DK, DV), F32)],
        scratch_shapes=[dir_buf(DELTA_RB)] + [dir_buf(DELTA_RB, BF16) for _ in range(4)]
        + [dir_buf(CHUNKS_PER_RB * 8), pltpu.VMEM((2, N_GROUPS, DK, GROUP_W), F32)],
        compiler_params=_cparams(("parallel", "arbitrary"), VMEM_LIMIT),
        name=f"deltanet_nb{nb}",
    )(*args)


FFN_TN = D_FF // 2


def _postmix_kernel(final_norm, tile0, ofc_ref, ofl_ref, obc_ref, obl_ref, z_ref, ybc_ref, ybl_ref,
                    ycc_ref, ycl_ref, pg_ref, x_ref, mod_ref, na_ref, ones_ref, wpa_ref, wpb_ref, wpc_ref,
                    wo_ref, g2_ref, wgu_ref, wdn_ref, nf_ref, out_ref):
    is_ctx = pl.program_id(0) + tile0 < N_CTX_TILES
    m = mod_ref[0]
    o = jnp.where(is_ctx, ofc_ref[...] + obc_ref[...], ofl_ref[...] + obl_ref[...])
    yb = jnp.where(is_ctx, ybc_ref[...], ybl_ref[...])
    yc = jnp.where(is_ctx, ycc_ref[...], ycl_ref[...])
    ms = _head_sums([o * o], ones_ref[...])[0] * (1.0 / DV)
    ya = (o * lax.rsqrt(ms + EPS) * na_ref[...]) * _silu(z_ref[...])
    merged = (pg_ref[:, 0:D_MODEL].astype(F32) * _dot(ya.astype(BF16), wpa_ref[0])
              + pg_ref[:, D_MODEL:2 * D_MODEL].astype(F32) * _dot(yb.astype(BF16), wpb_ref[0])
              + pg_ref[:, 2 * D_MODEL:3 * D_MODEL].astype(F32) * _dot(yc.astype(BF16), wpc_ref[0]))
    x = x_ref[...] + m[2:3] * _dot(merged.astype(BF16), wo_ref[0])

    h = _rms_mod(x, g2_ref[...], m[4:5], m[3:4]).astype(BF16)
    acc = None
    for c in range(0, D_FF, FFN_TN):
        gate = _dot(h, wgu_ref[0, :, c:c + FFN_TN])
        up = _dot(h, wgu_ref[0, :, D_FF + c:D_FF + c + FFN_TN])
        part = _dot((_silu(gate) * up).astype(BF16), wdn_ref[0, c:c + FFN_TN, :])
        acc = part if acc is None else acc + part
    xn = x + m[5:6] * acc
    if final_norm:
        ms = jnp.mean(xn * xn, axis=-1, keepdims=True)
        xn = xn * lax.rsqrt(ms + EPS) * nf_ref[...]
    out_ref[...] = xn


def _postmix_call(o_ctx, o_lat, pd, yb, yc, pg, x, mod3, na512, ones_bd, wpa, wpb, wpc, wo,
                  g2, wgu, wdn, nf, layer, final_norm, tile0=0, n_tiles=T_ALL // ROW_TILE):
    row = lambda i: (i + tile0, 0)
    const = lambda i: (0, 0)
    lyr = lambda i: (layer, 0, 0)
    assert W_A == W_B == W_C
    return pl.pallas_call(
        functools.partial(_postmix_kernel, final_norm, tile0),
        grid=(n_tiles,),
        in_specs=[
            *_ctx_lat_specs(W_A, tile0), *_ctx_lat_specs(W_A, tile0),
            pl.BlockSpec((ROW_TILE, W_A), lambda i: (i + tile0, OFF_Z // W_A)),
            *_ctx_lat_specs(W_B, tile0), *_ctx_lat_specs(W_C, tile0),
            pl.BlockSpec((ROW_TILE, 3 * D_MODEL), row),
            pl.BlockSpec((ROW_TILE, D_MODEL), row),
            pl.BlockSpec((1, 6, D_MODEL), lambda i: (_mod_row_block(i + tile0), 0, 0)),
            pl.BlockSpec((1, W_A), const),
            _resident((GROUP_W, GROUP_W), const),
            _resident((1, W_A, D_MODEL), lyr),
            _resident((1, W_B, D_MODEL), lyr),
            _resident((1, W_C, D_MODEL), lyr),
            _resident((1, D_MODEL, D_MODEL), lyr),
            pl.BlockSpec((1, D_MODEL), const),
            _resident((1, D_MODEL, 2 * D_FF), lyr),
            _resident((1, D_FF, D_MODEL), lyr),
            pl.BlockSpec((1, D_MODEL), const),
        ],
        out_specs=pl.BlockSpec((ROW_TILE, D_MODEL), lambda i: (i, 0)),
        out_shape=jax.ShapeDtypeStruct((n_tiles * ROW_TILE, D_MODEL), F32),
        compiler_params=_cparams(("parallel",), VMEM_LIMIT),
        name="postmix_final" if final_norm else "postmix",
    )(o_ctx[0], o_lat[0], o_ctx[1], o_lat[1], pd, yb[0], yb[1], yc[0], yc[1], pg, x, mod3, na512,
      ones_bd, wpa, wpb, wpc, wo, g2.reshape(1, D_MODEL), wgu, wdn, nf.reshape(1, D_MODEL))


TABLE_SPLIT = 64


def _grid_pos_embed(n_tokens):
    rows = n_tokens // GRID_W
    quarter = D_MODEL // 4
    omega = 1.0 / (10000.0 ** (jnp.arange(quarter, dtype=F32) / quarter))

    def emb(pos):
        a = pos[:, None] * omega[None, :]
        return jnp.concatenate([jnp.sin(a), jnp.cos(a)], axis=-1)

    e_row, e_col = lax.optimization_barrier((emb(jnp.arange(rows).astype(F32)),
                                             emb(jnp.arange(GRID_W).astype(F32))))
    return jnp.concatenate([jnp.repeat(e_row, GRID_W, axis=0), jnp.tile(e_col, (rows, 1))], axis=-1)


def _cos_nsin_tables(n, period):
    t = jnp.arange(n, dtype=jnp.int32)[None, :]

    def cs(r):
        ang = ((r * t) % period).astype(F32) * (2.0 * math.pi / period)
        return jnp.cos(ang), jnp.sin(ang)

    ca, sa = cs(jnp.arange(n // TABLE_SPLIT, dtype=jnp.int32)[:, None] * TABLE_SPLIT)
    cb, sb = cs(jnp.arange(TABLE_SPLIT, dtype=jnp.int32)[:, None])
    ca, sa, cb, sb = lax.optimization_barrier((ca, sa, cb, sb))
    ca, sa = ca[:, None, :], sa[:, None, :]
    cos = (ca * cb[None] - sa * sb[None]).reshape(n, n)
    nsin = (-(sa * cb[None] + ca * sb[None])).reshape(n, n)
    return cos.astype(BF16), nsin.astype(BF16)


def _hyena_positions(L):
    bands = (HY_EMB - 1) // 2
    t = jnp.linspace(0.0, 1.0, L, dtype=F32)[:, None]
    wpos = (2.0 * math.pi / L) * jnp.arange(L, dtype=F32)[:, None]
    fr = jnp.linspace(1e-4, bands - 1, bands, dtype=F32)[None, :]
    zpos = jnp.concatenate([t, jnp.cos(fr * wpos), -jnp.sin(fr * wpos)], axis=-1)
    zpos = jnp.pad(zpos, ((0, 0), (0, LANE - HY_EMB)))
    deltas = jnp.abs(jnp.linspace(math.log(HY_DECAY_TARGET) / HY_SLOW_PCT,
                                  math.log(HY_DECAY_TARGET) / HY_FAST_PCT, W_B, dtype=F32))
    window = jnp.exp(-t * deltas[None, :])
    return zpos, window


def _group_tables():
    r = jnp.arange(DC, dtype=jnp.int32)
    ang = ((r[:, None] * r[None, :]) % DC).astype(F32) * (2.0 * math.pi / DC)
    eye = jnp.eye(G_C, dtype=F32)
    return jnp.kron(eye, jnp.cos(ang)).astype(BF16), jnp.kron(eye, jnp.sin(ang)).astype(BF16)


def _head_ones():
    return jnp.kron(jnp.eye(HEAD_GROUP, dtype=F32), jnp.ones((DK, DK), F32)).astype(BF16)


def kernel(x_prompt, x_sample, state_delta, c, c_ctx, w_mod, b_mod, norm1_g, norm2_g, w_in, conv_qkv, a_log, dt_bias, norm_a, conv_hy, hy_w1, hy_b1, hy_freq, hy_w2, hy_b2, hy_w3, hy_bias, w_pa, w_pb, w_pc, w_o, w_gu, w_down, norm_f):
    assert x_prompt.shape == (N_CTX_SEQ, L_CTX, D_MODEL) and x_sample.shape == (N_LAT_SEQ, L_LAT, D_MODEL)
    st = jnp.pad(jnp.concatenate([c_ctx[None], c], axis=0).T, ((0, 0), (0, 8 - 1 - N_LAT_SEQ)))
    mod = _mod_call(st, w_mod, b_mod).reshape(DEPTH, 8, 6, D_MODEL)

    ones_bd = _head_ones()
    bdc, bds = _group_tables()
    seqs = ((L_CTX, N_CTX_SEQ, 0), (L_LAT, N_LAT_SEQ, T_CTX // L_LAT))
    tables = {L: (_cos_nsin_tables(L, 2 * L), _cos_nsin_tables(L, L), _hyena_positions(L))
              for L, _, _ in seqs}

    w_in_b = jnp.swapaxes(w_in, 1, 2).astype(BF16)
    w_pa_b, w_pb_b, w_pc_b, w_o_b, w_gu_b, w_down_b = (
        w.astype(BF16) for w in (w_pa, w_pb, w_pc, w_o, w_gu, w_down))

    x = None
    ctx_states = None
    for l in range(DEPTH):
        mod3 = mod[l, 0:3]
        if l == 0:
            xs = (x_prompt.reshape(T_CTX, D_MODEL), x_sample.reshape(T_LAT, D_MODEL), _grid_pos_embed(L_LAT))
            pd, ph, pf, pg, x = _inproj_call(xs, mod3, norm1_g[l], w_in_b, l)
        else:
            pd, ph, pf, pg = _inproj_call((x,), mod3, norm1_g[l], w_in_b, l)

        par = jnp.zeros((8, LANE), F32)
        par = par.at[0, 2 * H_A:4 * H_A].set(a_log[l].reshape(-1))
        par = par.at[1, 2 * H_A:4 * H_A].set(dt_bias[l].reshape(-1))
        *o_ctx, ctx_states = _delta_call(pd, conv_qkv[l], par, ones_bd, None,
                                         N_CTX_SEQ, L_CTX // DELTA_RB, 0, ctx_states)
        s0 = state_delta[:, l].astype(F32).reshape(N_LAT_SEQ, 2, H_A * DK, DV)
        *o_lat, _ = _delta_call(pd, conv_qkv[l], par, ones_bd, s0,
                                N_LAT_SEQ, L_LAT // DELTA_RB, T_CTX // DELTA_RB)

        w1p = jnp.pad(hy_w1[l], ((0, LANE - HY_EMB), (0, 0)))
        yb, yc = [], []
        for L, n_seq, blk0 in seqs:
            (cos2, nsin2), (cos1, nsin1), (zpos, window) = tables[L]
            kspec = _filter_call(L, zpos, w1p, hy_b1[l][None], hy_freq[l][None], hy_w2[l],
                                 hy_b2[l][None], hy_w3[l], window, cos2, nsin2)
            bias = hy_bias[l][:, None, :]
            if L == HYENA_FT:
                assert blk0 == 0
                yb.append(_hyena_short_call(ph, conv_hy[l], kspec, bias, cos2, nsin2, n_seq, L))
                yc.append(_fnet_short_call(pf, cos1, nsin1, bdc, bds, n_seq, L))
            else:
                yb.append(_hyena_call(ph, conv_hy[l], kspec, bias, cos2, nsin2, n_seq, L, blk0, HYENA_FT))
                yc.append(_fnet_call(pf, cos1, nsin1, bdc, bds, n_seq, L, blk0))

        na512 = jnp.tile(norm_a[l], H_A)[None]
        post = functools.partial(_postmix_call, o_ctx, o_lat, pd, yb, yc, pg, x, mod3, na512, ones_bd,
                                 w_pa_b, w_pb_b, w_pc_b, w_o_b, norm2_g[l], w_gu_b, w_down_b, norm_f, l)
        if l < DEPTH - 1:
            x = post(False)
        else:
            y_prompt = post(True, 0, N_CTX_TILES).reshape(N_CTX_SEQ, L_CTX, D_MODEL)
            y_sample = post(True, N_CTX_TILES, N_LAT_TILES).reshape(N_LAT_SEQ, L_LAT, D_MODEL)

    new_state = ctx_states.reshape(N_CTX_SEQ, DEPTH, 2, H_A, DK, DV).astype(x_prompt.dtype)
    return (y_prompt, y_sample, new_state)
```

```python
import functools
import math

import jax
import jax.numpy as jnp
from jax import lax
from jax.experimental import pallas as pl
from jax.experimental.pallas import tpu as pltpu

F32 = jnp.float32
BF16 = jnp.bfloat16

D_MODEL = 1024
N_CTX_SEQ = 32
L_CTX = 256
DEPTH = 2
N_LAT_SEQ = 2
L_LAT = 2048
GRID_W = 64
EPS = 1e-6
H_A = 8
DK = 64
DV = 64
W_A = H_A * DV
QKV_W = 2 * H_A * DK + H_A * DV
CHUNK = 64
W_B = 512
HY_EMB = 33
HY_HID = 64
HY_DECAY_TARGET = 1e-2
HY_FAST_PCT = 0.3
HY_SLOW_PCT = 1.5
G_C = 8
DC = 64
W_C = G_C * DC
D_FF = ((8 * D_MODEL + 3 * 256 - 1) // (3 * 256)) * 256
OFF_Z = QKV_W
OFF_B = OFF_Z + W_A
OFF_A = OFF_B + 2 * H_A
OFF_HY = OFF_A + 2 * H_A
OFF_FN = OFF_HY + 3 * W_B
OFF_GATE = OFF_FN + W_C

T_CTX = N_CTX_SEQ * L_CTX
T_LAT = N_LAT_SEQ * L_LAT
T_ALL = T_CTX + T_LAT
ROW_TILE = 256
N_CTX_TILES = T_CTX // ROW_TILE
N_LAT_TILES = T_LAT // ROW_TILE
LANE = 128
PD_W = QKV_W + W_A + LANE
HEAD_GROUP = 4
GROUP_W = HEAD_GROUP * DK
N_GROUPS = H_A // HEAD_GROUP
VMEM_LIMIT = 56 * 1024 * 1024


def _cparams(sem, vmem=None):
    return pltpu.CompilerParams(dimension_semantics=sem, vmem_limit_bytes=vmem)


def _dot(a, b):
    return jnp.dot(a, b, preferred_element_type=F32)


def _dot_nt(a, b):
    return lax.dot_general(a, b, (((1,), (1,)), ((), ())), preferred_element_type=F32)


def _split(a, n):
    parts = []
    rem = a
    for i in range(n):
        p = rem.astype(BF16)
        parts.append(p)
        if i + 1 < n:
            rem = rem - p.astype(F32)
    return parts


def _mm3(a, b):
    ah, al = _split(a, 2)
    bh, bl = _split(b, 2)
    return _dot(ah, bh) + (_dot(ah, bl) + _dot(al, bh))


def _sigmoid(x):
    return 1.0 / (1.0 + jnp.exp(-x))


def _silu(x):
    return x * _sigmoid(x)


def _softplus(x):
    return jnp.maximum(x, 0.0) + jnp.log(1.0 + jnp.exp(-jnp.abs(x)))


def _mod_row_block(i):
    per_lat = L_LAT // ROW_TILE
    return jnp.where(i < N_CTX_TILES, 0, 1 + (i - N_CTX_TILES) // per_lat)


def _ctx_tile(i):
    return jnp.minimum(i, N_CTX_TILES - 1)


def _lat_tile(i):
    return jnp.maximum(i - N_CTX_TILES, 0)


def _ctx_lat_specs(width, tile0=0):
    return (pl.BlockSpec((ROW_TILE, width), lambda i: (_ctx_tile(i + tile0), 0)),
            pl.BlockSpec((ROW_TILE, width), lambda i: (_lat_tile(i + tile0), 0)))


MOD_TN = 1536


def _mod_kernel(st_ref, w_ref, b_ref, out_ref):
    s = _silu(st_ref[...])
    w = w_ref[0]
    rows = [jnp.sum(s[:, r:r + 1] * w, axis=0, keepdims=True) + b_ref[0] for r in range(3)]
    rows.append(jnp.zeros((5, MOD_TN), F32))
    out_ref[0] = jnp.concatenate(rows, axis=0)


def _mod_call(st, w_mod, b_mod):
    n6 = 6 * D_MODEL
    return pl.pallas_call(
        _mod_kernel,
        grid=(DEPTH, n6 // MOD_TN),
        in_specs=[
            pl.BlockSpec((D_MODEL, 8), lambda l, j: (0, 0)),
            pl.BlockSpec((1, D_MODEL, MOD_TN), lambda l, j: (l, 0, j)),
            pl.BlockSpec((1, 1, MOD_TN), lambda l, j: (l, 0, j)),
        ],
        out_specs=pl.BlockSpec((1, 8, MOD_TN), lambda l, j: (l, 0, j)),
        out_shape=jax.ShapeDtypeStruct((DEPTH, 8, n6), F32),
        compiler_params=_cparams(("parallel", "parallel")),
        name="adaln_mod",
    )(st, w_mod, b_mod.reshape(DEPTH, 1, n6))


INPROJ_TN = 512
INPROJ_WIDTHS = (PD_W, 3 * W_B, W_C, 3 * D_MODEL)
INPROJ_FEATURES = (OFF_HY, 3 * W_B, W_C, 3 * D_MODEL)


def _rms_mod(x, g, scale, shift):
    ms = jnp.mean(x * x, axis=-1, keepdims=True)
    return (x * lax.rsqrt(ms + EPS) * g) * (1.0 + scale) + shift


def _inproj_kernel(first, *refs):
    if first:
        (xc_ref, xl_ref, pos_ref, mod_ref, g_ref, w_ref, pd_ref, ph_ref, pf_ref, pg_ref, x_ref) = refs
        x = jnp.where(pl.program_id(0) < N_CTX_TILES, xc_ref[...], xl_ref[...] + pos_ref[...])
        x_ref[...] = x
    else:
        (xin_ref, mod_ref, g_ref, w_ref, pd_ref, ph_ref, pf_ref, pg_ref) = refs
        x = xin_ref[...]
    m = mod_ref[0]
    h = _rms_mod(x, g_ref[...], m[1:2], m[0:1]).astype(BF16)
    row0 = 0
    for o_ref, n_feat in zip((pd_ref, ph_ref, pf_ref, pg_ref), INPROJ_FEATURES):
        n = o_ref.shape[1]
        for c in range(0, n, INPROJ_TN):
            e = min(c + INPROJ_TN, n)
            ef = min(e, n_feat)
            y = _dot_nt(h, w_ref[0, row0 + c:row0 + ef, :])
            if ef < e:
                y = jnp.concatenate([y, jnp.zeros((y.shape[0], e - ef), F32)], axis=1)
            if o_ref is pg_ref:
                y = _sigmoid(y)
            o_ref[:, c:e] = y.astype(o_ref.dtype)
        row0 += n_feat


def _resident(shape, index_map):
    return pl.BlockSpec(shape, index_map, pipeline_mode=pl.Buffered(1))


def _inproj_call(xs, mod3, g, w_all, layer):
    first = len(xs) == 3
    widths = INPROJ_WIDTHS
    row = lambda i: (i, 0)
    const = lambda i: (0, 0)
    if first:
        per_lat = L_LAT // ROW_TILE
        x_specs = list(_ctx_lat_specs(D_MODEL)) + [
            pl.BlockSpec((ROW_TILE, D_MODEL), lambda i: (_lat_tile(i) % per_lat, 0))]
    else:
        x_specs = [pl.BlockSpec((ROW_TILE, D_MODEL), row)]
    out_widths = widths + ((D_MODEL,) if first else ())
    out_dtypes = (F32, F32, F32, BF16) + ((F32,) if first else ())
    return pl.pallas_call(
        functools.partial(_inproj_kernel, first),
        grid=(T_ALL // ROW_TILE,),
        in_specs=x_specs + [
            pl.BlockSpec((1, 6, D_MODEL), lambda i: (_mod_row_block(i), 0, 0)),
            pl.BlockSpec((1, D_MODEL), const),
            _resident((1, sum(INPROJ_FEATURES), D_MODEL), lambda i: (layer, 0, 0)),
        ],
        out_specs=[pl.BlockSpec((ROW_TILE, w), row) for w in out_widths],
        out_shape=[jax.ShapeDtypeStruct((T_ALL, w), dt) for w, dt in zip(out_widths, out_dtypes)],
        compiler_params=_cparams(("parallel",), VMEM_LIMIT),
        name="inproj_first" if first else "inproj",
    )(*xs, mod3, g.reshape(1, D_MODEL), w_all)


def _conv3_rows(cur, prev_row, next_row, w):
    n = cur.shape[0]
    sub = lax.broadcasted_iota(jnp.int32, (8, cur.shape[1]), 0)
    up = pltpu.roll(cur, 1, 0)
    up = jnp.concatenate([jnp.where(sub == 0, prev_row, up[0:8]), up[8:]], axis=0)
    dn = pltpu.roll(cur, n - 1, 0)
    dn = jnp.concatenate([dn[0:n - 8], jnp.where(sub == 7, next_row, dn[n - 8:])], axis=0)
    return up * w[0:1] + cur * w[1:2] + dn * w[2:3]


def _halo_specs(width, rows_per_block, n_row_blocks, blk_of):
    per = rows_per_block // 8
    last = n_row_blocks * per - 1
    prev = pl.BlockSpec((8, width), lambda *a: (jnp.maximum(blk_of(*a) * per - 1, 0), 0))
    nxt = pl.BlockSpec((8, width), lambda *a: (jnp.minimum((blk_of(*a) + 1) * per, last), 0))
    return prev, nxt


FILT_RT = 256


def _alternating_sum(x):
    t = lax.broadcasted_iota(jnp.int32, x.shape, 0)
    return jnp.sum(jnp.where(t % 2 == 0, x, -x), axis=0, keepdims=True)


def _filter_kernel(L, zpos_ref, w1_ref, b1_ref, fq_ref, w2_ref, b2_ref, w3_ref, win_ref,
                   cos_ref, nsin_ref, k_ref, hs_s, hm_s, krl_s):
    rt = pl.program_id(1)

    @pl.when(rt == 0)
    def _():
        fq = fq_ref[...]
        alt_acc = jnp.zeros((1, W_B), F32)
        for r0 in range(0, L, FILT_RT):
            rows = slice(r0, r0 + FILT_RT)
            h = jnp.sin(fq * (_mm3(zpos_ref[rows, :], w1_ref[...]) + b1_ref[...]))
            h = jnp.sin(fq * (_mm3(h, w2_ref[...]) + b2_ref[...]))
            hf = _mm3(h, w3_ref[...])
            win = win_ref[rows, :]
            fw = hf[:, 0:W_B] * win
            bw = hf[:, W_B:2 * W_B] * win
            hsum = fw + bw
            hs_s[rows, :] = hsum.astype(BF16)
            hm_s[rows, :] = (fw - bw).astype(BF16)
            alt_acc = alt_acc + _alternating_sum(hsum)
        krl_s[...] = jnp.broadcast_to(alt_acc, krl_s.shape)

    p1 = _dot(cos_ref[...], hs_s[...])
    p2 = _dot(nsin_ref[...], hm_s[...])
    first = (rt * FILT_RT + lax.broadcasted_iota(jnp.int32, p1.shape, 0)) == 0
    k_ref[0, 0] = p1
    k_ref[0, 1] = jnp.where(first, krl_s[0:1, :], p1)
    k_ref[0, 2] = jnp.where(first, 0.0, p2)


def _filter_call(L, zpos, w1p, b1, fq, w2, b2, w3, win, cos, nsin):
    nrt = L // FILT_RT
    c2 = lambda o, r: (0, 0)
    return pl.pallas_call(
        functools.partial(_filter_kernel, L),
        grid=(2, nrt),
        in_specs=[
            pl.BlockSpec((L, LANE), c2),
            pl.BlockSpec((LANE, HY_HID), c2),
            pl.BlockSpec((1, HY_HID), c2),
            pl.BlockSpec((1, HY_HID), c2),
            pl.BlockSpec((HY_HID, HY_HID), c2),
            pl.BlockSpec((1, HY_HID), c2),
            pl.BlockSpec((HY_HID, 2 * W_B), lambda o, r: (0, o)),
            pl.BlockSpec((L, W_B), c2),
            pl.BlockSpec((FILT_RT, L), lambda o, r: (r, 0)),
            pl.BlockSpec((FILT_RT, L), lambda o, r: (r, 0)),
        ],
        out_specs=pl.BlockSpec((1, 3, FILT_RT, W_B), lambda o, r: (o, 0, r, 0)),
        out_shape=jax.ShapeDtypeStruct((2, 3, L, W_B), F32),
        scratch_shapes=[pltpu.VMEM((L, W_B), BF16), pltpu.VMEM((L, W_B), BF16),
                        pltpu.VMEM((8, W_B), F32)],
        compiler_params=_cparams(("parallel", "arbitrary"), VMEM_LIMIT),
        name=f"hyena_filter_{L}",
    )(zpos, w1p, b1, fq, w2, b2, w3, win, cos, nsin)


HYENA_FT = 256

HYENA_CONV_ROWS = 256


HYENA_CH = 256
HYENA_OUT_ROWS = 512


def _hyena_kernel(L, ft, x1_ref, x2_ref, v_ref, cw1_ref, cw2_ref, cwv_ref, k_ref, bias_ref, cos_ref, nsin_ref,
                  out_ref, gate_s, zf_s, zb_s, yt_s, yb_s):
    o = pl.program_id(2)

    @pl.when(o == 0)
    def _():
        for src, cw_ref, dst in ((x1_ref, cw1_ref, 0), (x2_ref, cw2_ref, 1), (v_ref, cwv_ref, None)):
            cw = cw_ref[...]
            for r0 in range(0, L, HYENA_CONV_ROWS):
                r1 = r0 + HYENA_CONV_ROWS
                prev_row = src[r0 - 1:r0, :] if r0 > 0 else 0.0
                next_row = src[r1:r1 + 1, :] if r1 < L else 0.0
                uc = _conv3_rows(src[r0:r1, :], prev_row, next_row, cw)
                if dst is None:
                    zf_s[r0:r1, :] = uc
                    zb_s[r0:r1, :] = uc.astype(BF16)
                else:
                    gate_s[dst, r0:r1, :] = uc

    zb = zb_s[...]
    nyq = _alternating_sum(zf_s[...])
    for f in range(L // ft):
        rows = slice(f * ft, (f + 1) * ft)
        top = _dot(cos_ref[rows, :], zb)
        bot = _dot(nsin_ref[rows, :], zb)
        if f == 0:
            first = lax.broadcasted_iota(jnp.int32, top.shape, 0) == 0
            bot = jnp.where(first, nyq, bot)
        krt = k_ref[0, 0, rows, :]
        krb = k_ref[0, 1, rows, :]
        ki = k_ref[0, 2, rows, :]
        yt = top * krt - bot * ki
        yb = top * ki + bot * krb
        if f == 0:
            yt = jnp.where(first, 0.5 * yt, yt)
            y_nyq = yb[0:1, :]
        yt_s[rows, :] = yt.astype(BF16)
        yb_s[rows, :] = yb.astype(BF16)

    t = lax.broadcasted_iota(jnp.int32, (HYENA_OUT_ROWS, HYENA_CH), 0)
    alt_half = jnp.where(t % 2 == 0, 0.5, -0.5)
    for r0 in range(0, L, HYENA_OUT_ROWS):
        rows = slice(r0, r0 + HYENA_OUT_ROWS)
        acc = _dot(cos_ref[rows, :], yt_s[...]) + _dot(nsin_ref[rows, :], yb_s[...]) + alt_half * y_nyq
        znew = gate_s[o, rows, :] * (acc * (1.0 / L) + bias_ref[0] * zf_s[rows, :])
        zf_s[rows, :] = znew
        zb_s[rows, :] = znew.astype(BF16)
        out_ref[rows, :] = znew.astype(out_ref.dtype)


def _hyena_call(ph, conv_w, kspec, bias, cos, nsin, n_seq, L, row_blk0, ft):
    nch = W_B // HYENA_CH
    once = pl.Buffered(1)
    col = lambda part: (lambda s, c, o: (row_blk0 + s, part * nch + c))
    cwcol = lambda part: (lambda s, c, o: (0, part * nch + c))
    return pl.pallas_call(
        functools.partial(_hyena_kernel, L, ft),
        grid=(n_seq, nch, 2),
        in_specs=[pl.BlockSpec((L, HYENA_CH), col(part), pipeline_mode=once) for part in range(3)]
        + [pl.BlockSpec((3, HYENA_CH), cwcol(part)) for part in range(3)]
        + [
            pl.BlockSpec((1, 3, L, HYENA_CH), lambda s, c, o: (o, 0, 0, c)),
            pl.BlockSpec((1, 1, HYENA_CH), lambda s, c, o: (o, 0, c)),
            pl.BlockSpec((L, L), lambda s, c, o: (0, 0), pipeline_mode=once),
            pl.BlockSpec((L, L), lambda s, c, o: (0, 0), pipeline_mode=once),
        ],
        out_specs=pl.BlockSpec((L, HYENA_CH), lambda s, c, o: (s, c)),
        out_shape=jax.ShapeDtypeStruct((n_seq * L, W_B), BF16),
        scratch_shapes=[pltpu.VMEM((2, L, HYENA_CH), F32), pltpu.VMEM((L, HYENA_CH), F32),
                        pltpu.VMEM((L, HYENA_CH), BF16), pltpu.VMEM((L, HYENA_CH), BF16),
                        pltpu.VMEM((L, HYENA_CH), BF16)],
        compiler_params=_cparams(("parallel", "parallel", "arbitrary"), VMEM_LIMIT),
        name=f"hyena_conv_{L}",
    )(ph, ph, ph, conv_w, conv_w, conv_w, kspec, bias, cos, nsin)


HYENA_SHORT_SB = 4


def _hyena_short_kernel(L, ph_ref, cw_ref, k_ref, bias_ref, cos_ref, nsin_f_ref, cos_i_ref, nsin_i_ref, out_ref):
    cos, nsin_f, cos_i, nsin_i = cos_ref[...], nsin_f_ref[...], cos_i_ref[...], nsin_i_ref[...]
    cw = cw_ref[...]
    rows = [slice(s * L, (s + 1) * L) for s in range(HYENA_SHORT_SB)]
    ucs = [_conv3_rows(ph_ref[r, :], 0.0, 0.0, cw) for r in rows]
    zs = [uc[:, 2 * W_B:3 * W_B] for uc in ucs]
    for o in range(2):
        krt, krb, ki = k_ref[o, 0], k_ref[o, 1], k_ref[o, 2]
        zbs = [z.astype(BF16) for z in zs]
        tops = [_dot(cos, zb) for zb in zbs]
        bots = [_dot(nsin_f, zb) for zb in zbs]
        yts = [top * krt - bot * ki for top, bot in zip(tops, bots)]
        ybs = [top * ki + bot * krb for top, bot in zip(tops, bots)]
        accs = [_dot(cos_i, yt.astype(BF16)) + _dot(nsin_i, yb.astype(BF16)) for yt, yb in zip(yts, ybs)]
        zs = [uc[:, o * W_B:(o + 1) * W_B] * (acc * (1.0 / L) + bias_ref[o] * z)
              for uc, acc, z in zip(ucs, accs, zs)]
    for r, z in zip(rows, zs):
        out_ref[r, :] = z.astype(out_ref.dtype)


def _hyena_short_call(ph, conv_w, kspec, bias, cos, nsin, n_seq, L):
    rows = HYENA_SHORT_SB * L
    alt = jnp.where(jnp.arange(L) % 2 == 0, 1.0, -1.0).astype(BF16)
    nsin_f = nsin.at[0, :].set(alt)
    cos_i = cos.at[:, 0].set(0.5)
    nsin_i = nsin.at[:, 0].set(0.5 * alt)
    return pl.pallas_call(
        functools.partial(_hyena_short_kernel, L),
        grid=(n_seq // HYENA_SHORT_SB,),
        in_specs=[
            pl.BlockSpec((rows, 3 * W_B), lambda i: (i, 0)),
            pl.BlockSpec((3, 3 * W_B), lambda i: (0, 0)),
            pl.BlockSpec((2, 3, L, W_B), lambda i: (0, 0, 0, 0)),
            pl.BlockSpec((2, 1, W_B), lambda i: (0, 0, 0)),
        ] + [pl.BlockSpec((L, L), lambda i: (0, 0)) for _ in range(4)],
        out_specs=pl.BlockSpec((rows, W_B), lambda i: (i, 0)),
        out_shape=jax.ShapeDtypeStruct((n_seq * L, W_B), BF16),
        compiler_params=_cparams(("parallel",), VMEM_LIMIT),
        name=f"hyena_conv_{L}",
    )(ph, conv_w, kspec, bias, cos, nsin_f, cos_i, nsin_i)


FNET_RT = 256


def _fnet_kernel(L, x_ref, cos_ref, nsin_ref, bdc_ref, bds_ref, out_ref, xc_s, xs_s):
    r = pl.program_id(1)

    @pl.when(r == 0)
    def _():
        for r0 in range(0, L, FNET_RT):
            xb = x_ref[r0:r0 + FNET_RT, :].astype(BF16)
            xc_s[r0:r0 + FNET_RT, :] = _dot(xb, bdc_ref[...]).astype(BF16)
            xs_s[r0:r0 + FNET_RT, :] = _dot(xb, bds_ref[...]).astype(BF16)

    y = _dot(cos_ref[...], xc_s[...]) + _dot(nsin_ref[...], xs_s[...])
    out_ref[...] = (y * (1.0 / math.sqrt(DC * L))).astype(out_ref.dtype)


def _fnet_call(pf, cos, nsin, bdc, bds, n_seq, L, row_blk0):
    nrt = L // FNET_RT
    return pl.pallas_call(
        functools.partial(_fnet_kernel, L),
        grid=(n_seq, nrt),
        in_specs=[
            pl.BlockSpec((L, W_C), lambda s, r: (row_blk0 + s, 0)),
            pl.BlockSpec((FNET_RT, L), lambda s, r: (r, 0)),
            pl.BlockSpec((FNET_RT, L), lambda s, r: (r, 0)),
            pl.BlockSpec((W_C, W_C), lambda s, r: (0, 0)),
            pl.BlockSpec((W_C, W_C), lambda s, r: (0, 0)),
        ],
        out_specs=pl.BlockSpec((FNET_RT, W_C), lambda s, r: (s * nrt + r, 0)),
        out_shape=jax.ShapeDtypeStruct((n_seq * L, W_C), BF16),
        scratch_shapes=[pltpu.VMEM((L, W_C), BF16), pltpu.VMEM((L, W_C), BF16)],
        compiler_params=_cparams(("parallel", "arbitrary"), VMEM_LIMIT),
        name=f"fnet_{L}",
    )(pf, cos, nsin, bdc, bds)


FNET_SHORT_SB = 4


def _fnet_short_kernel(L, x_ref, cos_ref, nsin_ref, bdc_ref, bds_ref, out_ref):
    xb = x_ref[...].astype(BF16)
    xc = _dot(xb, bdc_ref[...]).astype(BF16)
    xs = _dot(xb, bds_ref[...]).astype(BF16)
    for s in range(FNET_SHORT_SB):
        rows = slice(s * L, (s + 1) * L)
        y = _dot(cos_ref[...], xc[rows]) + _dot(nsin_ref[...], xs[rows])
        out_ref[rows, :] = (y * (1.0 / math.sqrt(DC * L))).astype(out_ref.dtype)


def _fnet_short_call(pf, cos, nsin, bdc, bds, n_seq, L):
    rows = FNET_SHORT_SB * L
    const = lambda i: (0, 0)
    return pl.pallas_call(
        functools.partial(_fnet_short_kernel, L),
        grid=(n_seq // FNET_SHORT_SB,),
        in_specs=[pl.BlockSpec((rows, W_C), lambda i: (i, 0)),
                  pl.BlockSpec((L, L), const), pl.BlockSpec((L, L), const),
                  pl.BlockSpec((W_C, W_C), const), pl.BlockSpec((W_C, W_C), const)],
        out_specs=pl.BlockSpec((rows, W_C), lambda i: (i, 0)),
        out_shape=jax.ShapeDtypeStruct((n_seq * L, W_C), BF16),
        compiler_params=_cparams(("parallel",), VMEM_LIMIT),
        name=f"fnet_{L}",
    )(pf, cos, nsin, bdc, bds)


DELTA_RB = 256
CHUNKS_PER_RB = DELTA_RB // CHUNK


HEADS_PER_LANE_TILE = LANE // DK


def _block_diag(y, half_masks):
    yb = y.astype(BF16)
    zero = jnp.zeros((CHUNK, LANE), BF16)
    row_blocks = []
    for h in range(HEAD_GROUP):
        tile = h // HEADS_PER_LANE_TILE
        piece = yb[:, tile * LANE:(tile + 1) * LANE] * half_masks[h % HEADS_PER_LANE_TILE]
        row_blocks.append(jnp.concatenate(
            [piece if t == tile else zero for t in range(GROUP_W // LANE)], axis=1))
    return jnp.concatenate(row_blocks, axis=0)


def _stacked_const_rhs(arrs, c, n):
    m = arrs[0].shape[0]
    parts = [p for a in arrs for p in _split(a, n)]
    y = _dot(jnp.concatenate(parts, axis=0), c)
    outs = []
    for i in range(len(arrs)):
        acc = y[i * n * m:(i * n + 1) * m]
        for t in range(1, n):
            acc = acc + y[(i * n + t) * m:(i * n + t + 1) * m]
        outs.append(acc)
    return outs


def _head_sums(arrs, ones_group):
    groups = [_stacked_const_rhs([a[:, g * GROUP_W:(g + 1) * GROUP_W] for a in arrs], ones_group, 2)
              for g in range(N_GROUPS)]
    return [jnp.concatenate([groups[g][i] for g in range(N_GROUPS)], axis=1) for i in range(len(arrs))]


def _expand_heads(x, lane0):
    cols = [jnp.broadcast_to(x[:, lane0 + h:lane0 + h + 1], (x.shape[0], DK)) for h in range(H_A)]
    return jnp.concatenate(cols, axis=1)


def _const_lhs_split(c, b, n):
    w = b.shape[1]
    y = _dot(c, jnp.concatenate(_split(b, n), axis=1))
    acc = y[:, 0:w]
    for t in range(1, n):
        acc = acc + y[:, t * w:(t + 1) * w]
    return acc


def _delta_chunk_stages(pd_ref, prev_ref, next_ref, r, has_prev, has_next, dirs, cw, a_neg, dtb, ones_bd,
                        tri, half_masks, eye_tile):
    ch = {}
    rows = slice(r * CHUNK, (r + 1) * CHUNK)

    def conv():
        cur = pd_ref[rows, 0:QKV_W]
        if r == 0:
            prev_row = jnp.where(has_prev, prev_ref[7:8, :], 0.0)
        else:
            prev_row = pd_ref[r * CHUNK - 1:r * CHUNK, 0:QKV_W]
        if r == CHUNKS_PER_RB - 1:
            next_row = jnp.where(has_next, next_ref[0:1, :], 0.0)
        else:
            next_row = pd_ref[(r + 1) * CHUNK:(r + 1) * CHUNK + 1, 0:QKV_W]
        qkv = _silu(_conv3_rows(cur, prev_row, next_row, cw))
        ch["q"] = qkv[:, 0:H_A * DK]
        ch["k"] = qkv[:, H_A * DK:2 * H_A * DK]
        ch["v"] = qkv[:, 2 * H_A * DK:]

    def norms():
        q, k = ch.pop("q"), ch.pop("k")
        qss, kss = _head_sums([q * q, k * k], ones_bd)
        ch["qn"] = q * lax.rsqrt(qss + EPS) * (DK ** -0.5)
        ch["kn"] = k * lax.rsqrt(kss + EPS)

    def gram():
        ch["gram"], ch["knT"] = [], []
        for g in range(N_GROUPS):
            lanes = slice(g * GROUP_W, (g + 1) * GROUP_W)
            lhs = jnp.concatenate([ch["kn"][:, lanes].astype(BF16), ch["qn"][:, lanes].astype(BF16), eye_tile],
                                  axis=0)
            res = _dot_nt(lhs, _block_diag(ch["kn"][:, lanes], half_masks))
            ch["gram"].append(res[0:2 * CHUNK])
            ch["knT"].append(res[2 * CHUNK:3 * CHUNK])

    def decay():
        ba = pd_ref[rows, OFF_B:OFF_B + LANE]
        sig = _sigmoid(ba)
        glog = a_neg * _softplus(ba + dtb)
        ch["decay"] = {}
        for d in dirs:
            gcum = _const_lhs_split(tri[d], glog, 3)
            beta = _expand_heads(sig, d * H_A)
            gcc8 = _expand_heads(gcum, (2 + d) * H_A)
            ch["decay"][d] = (beta, gcc8)

    return ch, [conv, norms, gram, decay]


def _delta_kernel(nb, zero_init, n_prev, *refs):
    shared = nb == 1
    it = iter(refs)
    blocks = [(next(it), next(it), next(it))]
    if not shared:
        blocks.append((next(it), next(it), next(it)))
    cw_ref, par_ref, ones_ref = next(it), next(it), next(it)
    s0_ref = None if zero_init else next(it)
    sprev_ref = next(it) if n_prev else None
    o_refs = (next(it), next(it))
    sfin_ref = next(it)
    u_s, w_s, p_s, qg_s, kg_s, gl_s, st_s = (next(it) for _ in range(7))
    j = pl.program_id(1)

    ri = lax.broadcasted_iota(jnp.int32, (CHUNK, GROUP_W), 0)
    cj = lax.broadcasted_iota(jnp.int32, (CHUNK, GROUP_W), 1) % CHUNK
    ixj = ri ^ cj
    eye = ixj == 0
    eye_tile = jnp.where(eye, 1.0, 0.0).astype(BF16)
    hl =lax.broadcasted_iota(jnp.int32, (CHUNK, LANE), 1) // DK
    half_masks = tuple(jnp.where(hl == h, 1.0, 0.0).astype(BF16) for h in range(HEADS_PER_LANE_TILE))
    ti = lax.broadcasted_iota(jnp.int32, (CHUNK, CHUNK), 0)
    tm = lax.broadcasted_iota(jnp.int32, (CHUNK, CHUNK), 1)
    ri8 = lax.broadcasted_iota(jnp.int32, (CHUNK, H_A * DK), 0)
    cj8 = lax.broadcasted_iota(jnp.int32, (CHUNK, H_A * DK), 1) % CHUNK
    incl = (ri >= cj, ri <= cj)
    strict = (ri > cj, ri < cj)
    tri = tuple(jnp.where(m, 1.0, 0.0).astype(BF16) for m in (tm <= ti, tm >= ti))
    eye8 = ri8 == cj8
    last_row = (CHUNK - 1, 0)

    @pl.when(j == 0)
    def _():
        for d in range(2):
            for g in range(N_GROUPS):
                if zero_init:
                    st_s[d, g] = jnp.zeros((DK, GROUP_W), F32)
                else:
                    r0 = g * GROUP_W
                    st_s[d, g] = jnp.concatenate(
                        [s0_ref[0, d, r0 + hh * DK:r0 + (hh + 1) * DK, :] for hh in range(HEAD_GROUP)], axis=1)

    cw = cw_ref[...]
    a_neg = -jnp.exp(par_ref[0:1, :])
    dtb = par_ref[1:2, :]
    pos = (j, nb - 1 - j)
    scan_order = (tuple(range(CHUNKS_PER_RB)), tuple(reversed(range(CHUNKS_PER_RB))))

    def unit_thunk(d, r, ch, g, units):
        def run():
            rows = slice(r * CHUNK, (r + 1) * CHUNK)
            lanes = slice(g * GROUP_W, (g + 1) * GROUP_W)
            beta, gcc8 = ch["decay"][d]
            gcr = jnp.sum(jnp.where(eye8, gcc8, 0.0), axis=0, keepdims=True)[:, lanes]
            qn, kn, be, gcc = ch["qn"][:, lanes], ch["kn"][:, lanes], beta[:, lanes], gcc8[:, lanes]
            kq = ch["gram"][g]
            dec = jnp.exp(jnp.where(incl[d], gcc - gcr, -1e30))
            a = jnp.where(strict[d], kq[0:CHUNK] * be * dec, 0.0)
            eg = jnp.exp(gcc)
            gcl = gcc[last_row[d]:last_row[d] + 1, :]
            units.append((d, rows, lanes, a, ch["v"][:, lanes] * be, kn * be * eg))
            p_s[d, rows, lanes] = (kq[CHUNK:2 * CHUNK] * dec).astype(BF16)
            qg_s[d, rows, lanes] = (qn * eg).astype(BF16)
            kg_s[d, rows, lanes] = (ch["knT"][g] * jnp.exp(gcl - gcr)).astype(BF16)
            gl_s[d, r * 8:(r + 1) * 8, lanes] = jnp.broadcast_to(jnp.exp(gcl), (8, GROUP_W))
        return run

    def prep_thunks(units):
        thunks = []
        if shared:
            todo = [(0, r, (0, 1)) for r in range(CHUNKS_PER_RB)]
        else:
            todo = [(d, r, (d,)) for d in range(2) for r in scan_order[d]]
        for b, r, dirs in todo:
            ch, stages = _delta_chunk_stages(*blocks[b], r, pos[b] > 0, pos[b] < nb - 1, dirs, cw, a_neg, dtb,
                                             ones_ref[...], tri, half_masks, eye_tile)
            thunks += stages
            thunks += [unit_thunk(d, r, ch, g, units) for d in dirs for g in range(N_GROUPS)]
        return thunks

    def scan_thunk(d, c):
        def run():
            r = scan_order[d][c]
            rows = slice(r * CHUNK, (r + 1) * CHUNK)
            for g in range(N_GROUPS):
                lanes = slice(g * GROUP_W, (g + 1) * GROUP_W)
                s = st_s[d, g]
                wq = jnp.concatenate([w_s[d, rows, lanes], qg_s[d, rows, lanes]], axis=0)
                ws_qs = _dot(wq, _block_diag(s, half_masks))
                v_new = u_s[d, rows, lanes] - ws_qs[0:CHUNK]
                pk = jnp.concatenate([p_s[d, rows, lanes], kg_s[d, rows, lanes]], axis=0)
                po = _dot(pk, _block_diag(v_new, half_masks))
                st_s[d, g] = s * gl_s[d, r * 8:r * 8 + 1, lanes] + po[CHUNK:2 * CHUNK]
                o_refs[d][rows, lanes] = ws_qs[CHUNK:2 * CHUNK] + po[0:CHUNK]
        return run

    def solve(units):
        r0s = [jnp.where(ixj == 1, un[3], 0.0) for un in units]
        zs = [un[3] - _dot(un[3].astype(BF16), _block_diag(r0, half_masks)) for un, r0 in zip(units, r0s)]
        xs = [jnp.where(eye, 1.0, 0.0) - r0 for r0 in r0s]
        for lvl in range(1, 6):
            links = [_block_diag(jnp.where((ixj >> lvl) == 1, z, 0.0), half_masks) for z in zs]
            if lvl < 5:
                prods = [_dot(jnp.concatenate([z, x], axis=0).astype(BF16), lk)
                         for z, x, lk in zip(zs, xs, links)]
                zs = [z - p[0:CHUNK] for z, p in zip(zs, prods)]
                xs = [x - p[CHUNK:2 * CHUNK] for x, p in zip(xs, prods)]
            else:
                xs = [x - _dot(x.astype(BF16), lk) for x, lk in zip(xs, links)]
        for x, (d, rows, lanes, _, vb, kbe) in zip(xs, units):
            rhs = jnp.concatenate([_block_diag(vb, half_masks), _block_diag(kbe, half_masks)], axis=1)
            uw = _dot(x.astype(BF16), rhs)
            u_s[d, rows, lanes] = uw[:, 0:GROUP_W]
            w_s[d, rows, lanes] = uw[:, GROUP_W:2 * GROUP_W].astype(BF16)

    units = []
    for f in prep_thunks(units):
        f()
    solve(units)
    for c in range(CHUNKS_PER_RB):
        for d in range(2):
            scan_thunk(d, c)()

    @pl.when(j == nb - 1)
    def _():
        for d in range(2):
            for g in range(N_GROUPS):
                s = st_s[d, g]
                for hh in range(HEAD_GROUP):
                    r0 = g * GROUP_W + hh * DK
                    sfin_ref[0, n_prev, d, r0:r0 + DK, :] = s[:, hh * DV:(hh + 1) * DV]
        for p in range(n_prev):
            sfin_ref[0, p] = sprev_ref[0, p]


def _delta_call(pd, conv_w, par, ones_bd, s0, n_seq, nb, blk0, prev_states=None):
    zero_init = s0 is None
    n_prev = 0 if prev_states is None else prev_states.shape[1]
    n_blocks_all = T_ALL // DELTA_RB
    blk_of = (lambda s, j: blk0 + s * nb + j, lambda s, j: blk0 + s * nb + nb - 1 - j)
    in_specs, args = [], []
    for d in range(1 if nb == 1 else 2):
        prev, nxt = _halo_specs(QKV_W, DELTA_RB, n_blocks_all, blk_of[d])
        in_specs += [pl.BlockSpec((DELTA_RB, PD_W), lambda s, j, d=d: (blk_of[d](s, j), 0)), prev, nxt]
        args += [pd, pd, pd]
    in_specs += [
        pl.BlockSpec((3, QKV_W), lambda s, j: (0, 0)),
        pl.BlockSpec((8, LANE), lambda s, j: (0, 0)),
        pl.BlockSpec((GROUP_W, GROUP_W), lambda s, j: (0, 0)),
    ]
    args += [conv_w, par, ones_bd]
    if not zero_init:
        in_specs.append(pl.BlockSpec((1, 2, H_A * DK, DV), lambda s, j: (s, 0, 0, 0)))
        args.append(s0)
    if n_prev:
        in_specs.append(pl.BlockSpec((1, n_prev, 2, H_A * DK, DV), lambda s, j: (s, 0, 0, 0, 0)))
        args.append(prev_states)
    rows = n_seq * nb * DELTA_RB
    dir_buf = lambda n, dt=F32: pltpu.VMEM((2, n, H_A * DK), dt)
    return pl.pallas_call(
        functools.partial(_delta_kernel, nb, zero_init, n_prev),
        grid=(n_seq, nb),
        in_specs=in_specs,
        out_specs=[pl.BlockSpec((DELTA_RB, W_A), lambda s, j: (s * nb + j, 0)),
                   pl.BlockSpec((DELTA_RB, W_A), lambda s, j: (s * nb + nb - 1 - j, 0)),
                   pl.BlockSpec((1, n_prev + 1, 2, H_A * DK, DV), lambda s, j: (s, 0, 0, 0, 0))],
        out_shape=[jax.ShapeDtypeStruct((rows, W_A), F32),
                   jax.ShapeDtypeStruct((rows, W_A), F32),
                   jax.ShapeDtypeStruct((n_seq, n_prev + 1, 2, H_A * DK, DV), F32)],
        scratch_shapes=[dir_buf(DELTA_RB)] + [dir_buf(DELTA_RB, BF16) for _ in range(4)]
        + [dir_buf(CHUNKS_PER_RB * 8), pltpu.VMEM((2, N_GROUPS, DK, GROUP_W), F32)],
        compiler_params=_cparams(("parallel", "arbitrary"), VMEM_LIMIT),
        name=f"deltanet_nb{nb}",
    )(*args)


FFN_TN = D_FF // 2


def _postmix_kernel(final_norm, tile0, ofc_ref, ofl_ref, obc_ref, obl_ref, z_ref, ybc_ref, ybl_ref,
                    ycc_ref, ycl_ref, pg_ref, x_ref, mod_ref, na_ref, ones_ref, wpa_ref, wpb_ref, wpc_ref,
                    wo_ref, g2_ref, wgu_ref, wdn_ref, nf_ref, out_ref):
    is_ctx = pl.program_id(0) + tile0 < N_CTX_TILES
    m = mod_ref[0]
    o = jnp.where(is_ctx, ofc_ref[...] + obc_ref[...], ofl_ref[...] + obl_ref[...])
    yb = jnp.where(is_ctx, ybc_ref[...], ybl_ref[...])
    yc = jnp.where(is_ctx, ycc_ref[...], ycl_ref[...])
    ms = _head_sums([o * o], ones_ref[...])[0] * (1.0 / DV)
    ya = (o * lax.rsqrt(ms + EPS) * na_ref[...]) * _silu(z_ref[...])
    merged = (pg_ref[:, 0:D_MODEL].astype(F32) * _dot(ya.astype(BF16), wpa_ref[0])
              + pg_ref[:, D_MODEL:2 * D_MODEL].astype(F32) * _dot(yb.astype(BF16), wpb_ref[0])
              + pg_ref[:, 2 * D_MODEL:3 * D_MODEL].astype(F32) * _dot(yc.astype(BF16), wpc_ref[0]))
    x = x_ref[...] + m[2:3] * _dot(merged.astype(BF16), wo_ref[0])

    h = _rms_mod(x, g2_ref[...], m[4:5], m[3:4]).astype(BF16)
    acc = None
    for c in range(0, D_FF, FFN_TN):
        gate = _dot(h, wgu_ref[0, :, c:c + FFN_TN])
        up = _dot(h, wgu_ref[0, :, D_FF + c:D_FF + c + FFN_TN])
        part = _dot((_silu(gate) * up).astype(BF16), wdn_ref[0, c:c + FFN_TN, :])
        acc = part if acc is None else acc + part
    xn = x + m[5:6] * acc
    if final_norm:
        ms = jnp.mean(xn * xn, axis=-1, keepdims=True)
        xn = xn * lax.rsqrt(ms + EPS) * nf_ref[...]
    out_ref[...] = xn


def _postmix_call(o_ctx, o_lat, pd, yb, yc, pg, x, mod3, na512, ones_bd, wpa, wpb, wpc, wo,
                  g2, wgu, wdn, nf, layer, final_norm, tile0=0, n_tiles=T_ALL // ROW_TILE):
    row = lambda i: (i + tile0, 0)
    const = lambda i: (0, 0)
    lyr = lambda i: (layer, 0, 0)
    assert W_A == W_B == W_C
    return pl.pallas_call(
        functools.partial(_postmix_kernel, final_norm, tile0),
        grid=(n_tiles,),
        in_specs=[
            *_ctx_lat_specs(W_A, tile0), *_ctx_lat_specs(W_A, tile0),
            pl.BlockSpec((ROW_TILE, W_A), lambda i: (i + tile0, OFF_Z // W_A)),
            *_ctx_lat_specs(W_B, tile0), *_ctx_lat_specs(W_C, tile0),
            pl.BlockSpec((ROW_TILE, 3 * D_MODEL), row),
            pl.BlockSpec((ROW_TILE, D_MODEL), row),
            pl.BlockSpec((1, 6, D_MODEL), lambda i: (_mod_row_block(i + tile0), 0, 0)),
            pl.BlockSpec((1, W_A), const),
            _resident((GROUP_W, GROUP_W), const),
            _resident((1, W_A, D_MODEL), lyr),
            _resident((1, W_B, D_MODEL), lyr),
            _resident((1, W_C, D_MODEL), lyr),
            _resident((1, D_MODEL, D_MODEL), lyr),
            pl.BlockSpec((1, D_MODEL), const),
            _resident((1, D_MODEL, 2 * D_FF), lyr),
            _resident((1, D_FF, D_MODEL), lyr),
            pl.BlockSpec((1, D_MODEL), const),
        ],
        out_specs=pl.BlockSpec((ROW_TILE, D_MODEL), lambda i: (i, 0)),
        out_shape=jax.ShapeDtypeStruct((n_tiles * ROW_TILE, D_MODEL), F32),
        compiler_params=_cparams(("parallel",), VMEM_LIMIT),
        name="postmix_final" if final_norm else "postmix",
    )(o_ctx[0], o_lat[0], o_ctx[1], o_lat[1], pd, yb[0], yb[1], yc[0], yc[1], pg, x, mod3, na512,
      ones_bd, wpa, wpb, wpc, wo, g2.reshape(1, D_MODEL), wgu, wdn, nf.reshape(1, D_MODEL))


TABLE_SPLIT = 64


def _grid_pos_embed(n_tokens):
    rows = n_tokens // GRID_W
    quarter = D_MODEL // 4
    omega = 1.0 / (10000.0 ** (jnp.arange(quarter, dtype=F32) / quarter))

    def emb(pos):
        a = pos[:, None] * omega[None, :]
        return jnp.concatenate([jnp.sin(a), jnp.cos(a)], axis=-1)

    e_row, e_col = lax.optimization_barrier((emb(jnp.arange(rows).astype(F32)),
                                             emb(jnp.arange(GRID_W).astype(F32))))
    return jnp.concatenate([jnp.repeat(e_row, GRID_W, axis=0), jnp.tile(e_col, (rows, 1))], axis=-1)


def _cos_nsin_tables(n, period):
    t = jnp.arange(n, dtype=jnp.int32)[None, :]

    def cs(r):
        ang = ((r * t) % period).astype(F32) * (2.0 * math.pi / period)
        return jnp.cos(ang), jnp.sin(ang)

    ca, sa = cs(jnp.arange(n // TABLE_SPLIT, dtype=jnp.int32)[:, None] * TABLE_SPLIT)
    cb, sb = cs(jnp.arange(TABLE_SPLIT, dtype=jnp.int32)[:, None])
    ca, sa, cb, sb = lax.optimization_barrier((ca, sa, cb, sb))
    ca, sa = ca[:, None, :], sa[:, None, :]
    cos = (ca * cb[None] - sa * sb[None]).reshape(n, n)
    nsin = (-(sa * cb[None] + ca * sb[None])).reshape(n, n)
    return cos.astype(BF16), nsin.astype(BF16)


def _hyena_positions(L):
    bands = (HY_EMB - 1) // 2
    t = jnp.linspace(0.0, 1.0, L, dtype=F32)[:, None]
    wpos = (2.0 * math.pi / L) * jnp.arange(L, dtype=F32)[:, None]
    fr = jnp.linspace(1e-4, bands - 1, bands, dtype=F32)[None, :]
    zpos = jnp.concatenate([t, jnp.cos(fr * wpos), -jnp.sin(fr * wpos)], axis=-1)
    zpos = jnp.pad(zpos, ((0, 0), (0, LANE - HY_EMB)))
    deltas = jnp.abs(jnp.linspace(math.log(HY_DECAY_TARGET) / HY_SLOW_PCT,
                                  math.log(HY_DECAY_TARGET) / HY_FAST_PCT, W_B, dtype=F32))
    window = jnp.exp(-t * deltas[None, :])
    return zpos, window


def _group_tables():
    r = jnp.arange(DC, dtype=jnp.int32)
    ang = ((r[:, None] * r[None, :]) % DC).astype(F32) * (2.0 * math.pi / DC)
    eye = jnp.eye(G_C, dtype=F32)
    return jnp.kron(eye, jnp.cos(ang)).astype(BF16), jnp.kron(eye, jnp.sin(ang)).astype(BF16)


def _head_ones():
    return jnp.kron(jnp.eye(HEAD_GROUP, dtype=F32), jnp.ones((DK, DK), F32)).astype(BF16)


def kernel(x_prompt, x_sample, state_delta, c, c_ctx, w_mod, b_mod, norm1_g, norm2_g, w_in, conv_qkv, a_log, dt_bias, norm_a, conv_hy, hy_w1, hy_b1, hy_freq, hy_w2, hy_b2, hy_w3, hy_bias, w_pa, w_pb, w_pc, w_o, w_gu, w_down, norm_f):
    assert x_prompt.shape == (N_CTX_SEQ, L_CTX, D_MODEL) and x_sample.shape == (N_LAT_SEQ, L_LAT, D_MODEL)
    st = jnp.pad(jnp.concatenate([c_ctx[None], c], axis=0).T, ((0, 0), (0, 8 - 1 - N_LAT_SEQ)))
    mod = _mod_call(st, w_mod, b_mod).reshape(DEPTH, 8, 6, D_MODEL)

    ones_bd = _head_ones()
    bdc, bds = _group_tables()
    seqs = ((L_CTX, N_CTX_SEQ, 0), (L_LAT, N_LAT_SEQ, T_CTX // L_LAT))
    tables = {L: (_cos_nsin_tables(L, 2 * L), _cos_nsin_tables(L, L), _hyena_positions(L))
              for L, _, _ in seqs}

    w_in_b = jnp.swapaxes(w_in, 1, 2).astype(BF16)
    w_pa_b, w_pb_b, w_pc_b, w_o_b, w_gu_b, w_down_b = (
        w.astype(BF16) for w in (w_pa, w_pb, w_pc, w_o, w_gu, w_down))

    x = None
    ctx_states = None
    for l in range(DEPTH):
        mod3 = mod[l, 0:3]
        if l == 0:
            xs = (x_prompt.reshape(T_CTX, D_MODEL), x_sample.reshape(T_LAT, D_MODEL), _grid_pos_embed(L_LAT))
            pd, ph, pf, pg, x = _inproj_call(xs, mod3, norm1_g[l], w_in_b, l)
        else:
            pd, ph, pf, pg = _inproj_call((x,), mod3, norm1_g[l], w_in_b, l)

        par = jnp.zeros((8, LANE), F32)
        par = par.at[0, 2 * H_A:4 * H_A].set(a_log[l].reshape(-1))
        par = par.at[1, 2 * H_A:4 * H_A].set(dt_bias[l].reshape(-1))
        *o_ctx, ctx_states = _delta_call(pd, conv_qkv[l], par, ones_bd, None,
                                         N_CTX_SEQ, L_CTX // DELTA_RB, 0, ctx_states)
        s0 = state_delta[:, l].astype(F32).reshape(N_LAT_SEQ, 2, H_A * DK, DV)
        *o_lat, _ = _delta_call(pd, conv_qkv[l], par, ones_bd, s0,
                                N_LAT_SEQ, L_LAT // DELTA_RB, T_CTX // DELTA_RB)

        w1p = jnp.pad(hy_w1[l], ((0, LANE - HY_EMB), (0, 0)))
        yb, yc = [], []
        for L, n_seq, blk0 in seqs:
            (cos2, nsin2), (cos1, nsin1), (zpos, window) = tables[L]
            kspec = _filter_call(L, zpos, w1p, hy_b1[l][None], hy_freq[l][None], hy_w2[l],
                                 hy_b2[l][None], hy_w3[l], window, cos2, nsin2)
            bias = hy_bias[l][:, None, :]
            if L == HYENA_FT:
                assert blk0 == 0
                yb.append(_hyena_short_call(ph, conv_hy[l], kspec, bias, cos2, nsin2, n_seq, L))
                yc.append(_fnet_short_call(pf, cos1, nsin1, bdc, bds, n_seq, L))
            else:
                yb.append(_hyena_call(ph, conv_hy[l], kspec, bias, cos2, nsin2, n_seq, L, blk0, HYENA_FT))
                yc.append(_fnet_call(pf, cos1, nsin1, bdc, bds, n_seq, L, blk0))

        na512 = jnp.tile(norm_a[l], H_A)[None]
        post = functools.partial(_postmix_call, o_ctx, o_lat, pd, yb, yc, pg, x, mod3, na512, ones_bd,
                                 w_pa_b, w_pb_b, w_pc_b, w_o_b, norm2_g[l], w_gu_b, w_down_b, norm_f, l)
        if l < DEPTH - 1:
            x = post(False)
        else:
            y_prompt = post(True, 0, N_CTX_TILES).reshape(N_CTX_SEQ, L_CTX, D_MODEL)
            y_sample = post(True, N_CTX_TILES, N_LAT_TILES).reshape(N_LAT_SEQ, L_LAT, D_MODEL)

    new_state = ctx_states.reshape(N_CTX_SEQ, DEPTH, 2, H_A, DK, DV).astype(x_prompt.dtype)
    return (y_prompt, y_sample, new_state)
```

```python
import functools
import math

import jax
import jax.numpy as jnp
from jax import lax
from jax.experimental import pallas as pl
from jax.experimental.pallas import tpu as pltpu

F32 = jnp.float32
BF16 = jnp.bfloat16

D_MODEL = 1024
N_CTX_SEQ = 32
L_CTX = 256
DEPTH = 2
N_LAT_SEQ = 2
L_LAT = 2048
GRID_W = 64
EPS = 1e-6
H_A = 8
DK = 64
DV = 64
W_A = H_A * DV
QKV_W = 2 * H_A * DK + H_A * DV
CHUNK = 64
W_B = 512
HY_EMB = 33
HY_HID = 64
HY_DECAY_TARGET = 1e-2
HY_FAST_PCT = 0.3
HY_SLOW_PCT = 1.5
G_C = 8
DC = 64
W_C = G_C * DC
D_FF = ((8 * D_MODEL + 3 * 256 - 1) // (3 * 256)) * 256
OFF_Z = QKV_W
OFF_B = OFF_Z + W_A
OFF_A = OFF_B + 2 * H_A
OFF_HY = OFF_A + 2 * H_A
OFF_FN = OFF_HY + 3 * W_B
OFF_GATE = OFF_FN + W_C

T_CTX = N_CTX_SEQ * L_CTX
T_LAT = N_LAT_SEQ * L_LAT
T_ALL = T_CTX + T_LAT
ROW_TILE = 256
N_CTX_TILES = T_CTX // ROW_TILE
N_LAT_TILES = T_LAT // ROW_TILE
LANE = 128
PD_W = QKV_W + W_A + LANE
HEAD_GROUP = 4
GROUP_W = HEAD_GROUP * DK
N_GROUPS = H_A // HEAD_GROUP
VMEM_LIMIT = 56 * 1024 * 1024


def _cparams(sem, vmem=None):
    return pltpu.CompilerParams(dimension_semantics=sem, vmem_limit_bytes=vmem)


def _dot(a, b):
    return jnp.dot(a, b, preferred_element_type=F32)


def _dot_nt(a, b):
    return lax.dot_general(a, b, (((1,), (1,)), ((), ())), preferred_element_type=F32)


def _split(a, n):
    parts = []
    rem = a
    for i in range(n):
        p = rem.astype(BF16)
        parts.append(p)
        if i + 1 < n:
            rem = rem - p.astype(F32)
    return parts


def _mm3(a, b):
    ah, al = _split(a, 2)
    bh, bl = _split(b, 2)
    return _dot(ah, bh) + (_dot(ah, bl) + _dot(al, bh))


def _sigmoid(x):
    return 1.0 / (1.0 + jnp.exp(-x))


def _silu(x):
    return x * _sigmoid(x)


def _softplus(x):
    return jnp.maximum(x, 0.0) + jnp.log(1.0 + jnp.exp(-jnp.abs(x)))


def _mod_row_block(i):
    per_lat = L_LAT // ROW_TILE
    return jnp.where(i < N_CTX_TILES, 0, 1 + (i - N_CTX_TILES) // per_lat)


def _ctx_tile(i):
    return jnp.minimum(i, N_CTX_TILES - 1)


def _lat_tile(i):
    return jnp.maximum(i - N_CTX_TILES, 0)


def _ctx_lat_specs(width, tile0=0):
    return (pl.BlockSpec((ROW_TILE, width), lambda i: (_ctx_tile(i + tile0), 0)),
            pl.BlockSpec((ROW_TILE, width), lambda i: (_lat_tile(i + tile0), 0)))


MOD_TN = 1536


def _mod_kernel(st_ref, w_ref, b_ref, out_ref):
    s = _silu(st_ref[...])
    w = w_ref[0]
    rows = [jnp.sum(s[:, r:r + 1] * w, axis=0, keepdims=True) + b_ref[0] for r in range(3)]
    rows.append(jnp.zeros((5, MOD_TN), F32))
    out_ref[0] = jnp.concatenate(rows, axis=0)


def _mod_call(st, w_mod, b_mod):
    n6 = 6 * D_MODEL
    return pl.pallas_call(
        _mod_kernel,
        grid=(DEPTH, n6 // MOD_TN),
        in_specs=[
            pl.BlockSpec((D_MODEL, 8), lambda l, j: (0, 0)),
            pl.BlockSpec((1, D_MODEL, MOD_TN), lambda l, j: (l, 0, j)),
            pl.BlockSpec((1, 1, MOD_TN), lambda l, j: (l, 0, j)),
        ],
        out_specs=pl.BlockSpec((1, 8, MOD_TN), lambda l, j: (l, 0, j)),
        out_shape=jax.ShapeDtypeStruct((DEPTH, 8, n6), F32),
        compiler_params=_cparams(("parallel", "parallel")),
        name="adaln_mod",
    )(st, w_mod, b_mod.reshape(DEPTH, 1, n6))


INPROJ_TN = 512
INPROJ_WIDTHS = (PD_W, 3 * W_B, W_C, 3 * D_MODEL)
INPROJ_FEATURES = (OFF_HY, 3 * W_B, W_C, 3 * D_MODEL)


def _rms_mod(x, g, scale, shift):
    ms = jnp.mean(x * x, axis=-1, keepdims=True)
    return (x * lax.rsqrt(ms + EPS) * g) * (1.0 + scale) + shift


def _inproj_kernel(first, *refs):
    if first:
        (xc_ref, xl_ref, pos_ref, mod_ref, g_ref, w_ref, pd_ref, ph_ref, pf_ref, pg_ref, x_ref) = refs
        x = jnp.where(pl.program_id(0) < N_CTX_TILES, xc_ref[...], xl_ref[...] + pos_ref[...])
        x_ref[...] = x
    else:
        (xin_ref, mod_ref, g_ref, w_ref, pd_ref, ph_ref, pf_ref, pg_ref) = refs
        x = xin_ref[...]
    m = mod_ref[0]
    h = _rms_mod(x, g_ref[...], m[1:2], m[0:1]).astype(BF16)
    row0 = 0
    for o_ref, n_feat in zip((pd_ref, ph_ref, pf_ref, pg_ref), INPROJ_FEATURES):
        n = o_ref.shape[1]
        for c in range(0, n, INPROJ_TN):
            e = min(c + INPROJ_TN, n)
            ef = min(e, n_feat)
            y = _dot_nt(h, w_ref[0, row0 + c:row0 + ef, :])
            if ef < e:
                y = jnp.concatenate([y, jnp.zeros((y.shape[0], e - ef), F32)], axis=1)
            if o_ref is pg_ref:
                y = _sigmoid(y)
            o_ref[:, c:e] = y.astype(o_ref.dtype)
        row0 += n_feat


def _resident(shape, index_map):
    return pl.BlockSpec(shape, index_map, pipeline_mode=pl.Buffered(1))


def _inproj_call(xs, mod3, g, w_all, layer):
    first = len(xs) == 3
    widths = INPROJ_WIDTHS
    row = lambda i: (i, 0)
    const = lambda i: (0, 0)
    if first:
        per_lat = L_LAT // ROW_TILE
        x_specs = list(_ctx_lat_specs(D_MODEL)) + [
            pl.BlockSpec((ROW_TILE, D_MODEL), lambda i: (_lat_tile(i) % per_lat, 0))]
    else:
        x_specs = [pl.BlockSpec((ROW_TILE, D_MODEL), row)]
    out_widths = widths + ((D_MODEL,) if first else ())
    out_dtypes = (F32, F32, F32, BF16) + ((F32,) if first else ())
    return pl.pallas_call(
        functools.partial(_inproj_kernel, first),
        grid=(T_ALL // ROW_TILE,),
        in_specs=x_specs + [
            pl.BlockSpec((1, 6, D_MODEL), lambda i: (_mod_row_block(i), 0, 0)),
            pl.BlockSpec((1, D_MODEL), const),
            _resident((1, sum(INPROJ_FEATURES), D_MODEL), lambda i: (layer, 0, 0)),
        ],
        out_specs=[pl.BlockSpec((ROW_TILE, w), row) for w in out_widths],
        out_shape=[jax.ShapeDtypeStruct((T_ALL, w), dt) for w, dt in zip(out_widths, out_dtypes)],
        compiler_params=_cparams(("parallel",), VMEM_LIMIT),
        name="inproj_first" if first else "inproj",
    )(*xs, mod3, g.reshape(1, D_MODEL), w_all)


def _conv3_rows(cur, prev_row, next_row, w):
    n = cur.shape[0]
    sub = lax.broadcasted_iota(jnp.int32, (8, cur.shape[1]), 0)
    up = pltpu.roll(cur, 1, 0)
    up = jnp.concatenate([jnp.where(sub == 0, prev_row, up[0:8]), up[8:]], axis=0)
    dn = pltpu.roll(cur, n - 1, 0)
    dn = jnp.concatenate([dn[0:n - 8], jnp.where(sub == 7, next_row, dn[n - 8:])], axis=0)
    return up * w[0:1] + cur * w[1:2] + dn * w[2:3]


def _halo_specs(width, rows_per_block, n_row_blocks, blk_of):
    per = rows_per_block // 8
    last = n_row_blocks * per - 1
    prev = pl.BlockSpec((8, width), lambda *a: (jnp.maximum(blk_of(*a) * per - 1, 0), 0))
    nxt = pl.BlockSpec((8, width), lambda *a: (jnp.minimum((blk_of(*a) + 1) * per, last), 0))
    return prev, nxt


FILT_RT = 256


def _alternating_sum(x):
    t = lax.broadcasted_iota(jnp.int32, x.shape, 0)
    return jnp.sum(jnp.where(t % 2 == 0, x, -x), axis=0, keepdims=True)


def _filter_kernel(L, zpos_ref, w1_ref, b1_ref, fq_ref, w2_ref, b2_ref, w3_ref, win_ref,
                   cos_ref, nsin_ref, k_ref, hs_s, hm_s, krl_s):
    rt = pl.program_id(1)

    @pl.when(rt == 0)
    def _():
        fq = fq_ref[...]
        alt_acc = jnp.zeros((1, W_B), F32)
        for r0 in range(0, L, FILT_RT):
            rows = slice(r0, r0 + FILT_RT)
            h = jnp.sin(fq * (_mm3(zpos_ref[rows, :], w1_ref[...]) + b1_ref[...]))
            h = jnp.sin(fq * (_mm3(h, w2_ref[...]) + b2_ref[...]))
            hf = _mm3(h, w3_ref[...])
            win = win_ref[rows, :]
            fw = hf[:, 0:W_B] * win
            bw = hf[:, W_B:2 * W_B] * win
            hsum = fw + bw
            hs_s[rows, :] = hsum.astype(BF16)
            hm_s[rows, :] = (fw - bw).astype(BF16)
            alt_acc = alt_acc + _alternating_sum(hsum)
        krl_s[...] = jnp.broadcast_to(alt_acc, krl_s.shape)

    p1 = _dot(cos_ref[...], hs_s[...])
    p2 = _dot(nsin_ref[...], hm_s[...])
    first = (rt * FILT_RT + lax.broadcasted_iota(jnp.int32, p1.shape, 0)) == 0
    k_ref[0, 0] = p1
    k_ref[0, 1] = jnp.where(first, krl_s[0:1, :], p1)
    k_ref[0, 2] = jnp.where(first, 0.0, p2)


def _filter_call(L, zpos, w1p, b1, fq, w2, b2, w3, win, cos, nsin):
    nrt = L // FILT_RT
    c2 = lambda o, r: (0, 0)
    return pl.pallas_call(
        functools.partial(_filter_kernel, L),
        grid=(2, nrt),
        in_specs=[
            pl.BlockSpec((L, LANE), c2),
            pl.BlockSpec((LANE, HY_HID), c2),
            pl.BlockSpec((1, HY_HID), c2),
            pl.BlockSpec((1, HY_HID), c2),
            pl.BlockSpec((HY_HID, HY_HID), c2),
            pl.BlockSpec((1, HY_HID), c2),
            pl.BlockSpec((HY_HID, 2 * W_B), lambda o, r: (0, o)),
            pl.BlockSpec((L, W_B), c2),
            pl.BlockSpec((FILT_RT, L), lambda o, r: (r, 0)),
            pl.BlockSpec((FILT_RT, L), lambda o, r: (r, 0)),
        ],
        out_specs=pl.BlockSpec((1, 3, FILT_RT, W_B), lambda o, r: (o, 0, r, 0)),
        out_shape=jax.ShapeDtypeStruct((2, 3, L, W_B), F32),
        scratch_shapes=[pltpu.VMEM((L, W_B), BF16), pltpu.VMEM((L, W_B), BF16),
                        pltpu.VMEM((8, W_B), F32)],
        compiler_params=_cparams(("parallel", "arbitrary"), VMEM_LIMIT),
        name=f"hyena_filter_{L}",
    )(zpos, w1p, b1, fq, w2, b2, w3, win, cos, nsin)


HYENA_FT = 256

HYENA_CONV_ROWS = 256


HYENA_CH = 256
HYENA_OUT_ROWS = 512


def _hyena_kernel(L, ft, x1_ref, x2_ref, v_ref, cw1_ref, cw2_ref, cwv_ref, k_ref, bias_ref, cos_ref, nsin_ref,
                  out_ref, gate_s, zf_s, zb_s, yt_s, yb_s):
    o = pl.program_id(2)

    @pl.when(o == 0)
    def _():
        for src, cw_ref, dst in ((x1_ref, cw1_ref, 0), (x2_ref, cw2_ref, 1), (v_ref, cwv_ref, None)):
            cw = cw_ref[...]
            for r0 in range(0, L, HYENA_CONV_ROWS):
                r1 = r0 + HYENA_CONV_ROWS
                prev_row = src[r0 - 1:r0, :] if r0 > 0 else 0.0
                next_row = src[r1:r1 + 1, :] if r1 < L else 0.0
                uc = _conv3_rows(src[r0:r1, :], prev_row, next_row, cw)
                if dst is None:
                    zf_s[r0:r1, :] = uc
                    zb_s[r0:r1, :] = uc.astype(BF16)
                else:
                    gate_s[dst, r0:r1, :] = uc

    zb = zb_s[...]
    nyq = _alternating_sum(zf_s[...])
    for f in range(L // ft):
        rows = slice(f * ft, (f + 1) * ft)
        top = _dot(cos_ref[rows, :], zb)
        bot = _dot(nsin_ref[rows, :], zb)
        if f == 0:
            first = lax.broadcasted_iota(jnp.int32, top.shape, 0) == 0
            bot = jnp.where(first, nyq, bot)
        krt = k_ref[0, 0, rows, :]
        krb = k_ref[0, 1, rows, :]
        ki = k_ref[0, 2, rows, :]
        yt = top * krt - bot * ki
        yb = top * ki + bot * krb
        if f == 0:
            yt = jnp.where(first, 0.5 * yt, yt)
            y_nyq = yb[0:1, :]
        yt_s[rows, :] = yt.astype(BF16)
        yb_s[rows, :] = yb.astype(BF16)

    t = lax.broadcasted_iota(jnp.int32, (HYENA_OUT_ROWS, HYENA_CH), 0)
    alt_half = jnp.where(t % 2 == 0, 0.5, -0.5)
    for r0 in range(0, L, HYENA_OUT_ROWS):
        rows = slice(r0, r0 + HYENA_OUT_ROWS)
        acc = _dot(cos_ref[rows, :], yt_s[...]) + _dot(nsin_ref[rows, :], yb_s[...]) + alt_half * y_nyq
        znew = gate_s[o, rows, :] * (acc * (1.0 / L) + bias_ref[0] * zf_s[rows, :])
        zf_s[rows, :] = znew
        zb_s[rows, :] = znew.astype(BF16)
        out_ref[rows, :] = znew.astype(out_ref.dtype)


def _hyena_call(ph, conv_w, kspec, bias, cos, nsin, n_seq, L, row_blk0, ft):
    nch = W_B // HYENA_CH
    once = pl.Buffered(1)
    col = lambda part: (lambda s, c, o: (row_blk0 + s, part * nch + c))
    cwcol = lambda part: (lambda s, c, o: (0, part * nch + c))
    return pl.pallas_call(
        functools.partial(_hyena_kernel, L, ft),
        grid=(n_seq, nch, 2),
        in_specs=[pl.BlockSpec((L, HYENA_CH), col(part), pipeline_mode=once) for part in range(3)]
        + [pl.BlockSpec((3, HYENA_CH), cwcol(part)) for part in range(3)]
        + [
            pl.BlockSpec((1, 3, L, HYENA_CH), lambda s, c, o: (o, 0, 0, c)),
            pl.BlockSpec((1, 1, HYENA_CH), lambda s, c, o: (o, 0, c)),
            pl.BlockSpec((L, L), lambda s, c, o: (0, 0), pipeline_mode=once),
            pl.BlockSpec((L, L), lambda s, c, o: (0, 0), pipeline_mode=once),
        ],
        out_specs=pl.BlockSpec((L, HYENA_CH), lambda s, c, o: (s, c)),
        out_shape=jax.ShapeDtypeStruct((n_seq * L, W_B), BF16),
        scratch_shapes=[pltpu.VMEM((2, L, HYENA_CH), F32), pltpu.VMEM((L, HYENA_CH), F32),
                        pltpu.VMEM((L, HYENA_CH), BF16), pltpu.VMEM((L, HYENA_CH), BF16),
                        pltpu.VMEM((L, HYENA_CH), BF16)],
        compiler_params=_cparams(("parallel", "parallel", "arbitrary"), VMEM_LIMIT),
        name=f"hyena_conv_{L}",
    )(ph, ph, ph, conv_w, conv_w, conv_w, kspec, bias, cos, nsin)


HYENA_SHORT_SB = 4


def _hyena_short_kernel(L, ph_ref, cw_ref, k_ref, bias_ref, cos_ref, nsin_f_ref, cos_i_ref, nsin_i_ref, out_ref):
    cos, nsin_f, cos_i, nsin_i = cos_ref[...], nsin_f_ref[...], cos_i_ref[...], nsin_i_ref[...]
    cw = cw_ref[...]
    rows = [slice(s * L, (s + 1) * L) for s in range(HYENA_SHORT_SB)]
    ucs = [_conv3_rows(ph_ref[r, :], 0.0, 0.0, cw) for r in rows]
    zs = [uc[:, 2 * W_B:3 * W_B] for uc in ucs]
    for o in range(2):
        krt, krb, ki = k_ref[o, 0], k_ref[o, 1], k_ref[o, 2]
        zbs = [z.astype(BF16) for z in zs]
        tops = [_dot(cos, zb) for zb in zbs]
        bots = [_dot(nsin_f, zb) for zb in zbs]
        yts = [top * krt - bot * ki for top, bot in zip(tops, bots)]
        ybs = [top * ki + bot * krb for top, bot in zip(tops, bots)]
        accs = [_dot(cos_i, yt.astype(BF16)) + _dot(nsin_i, yb.astype(BF16)) for yt, yb in zip(yts, ybs)]
        zs = [uc[:, o * W_B:(o + 1) * W_B] * (acc * (1.0 / L) + bias_ref[o] * z)
              for uc, acc, z in zip(ucs, accs, zs)]
    for r, z in zip(rows, zs):
        out_ref[r, :] = z.astype(out_ref.dtype)


def _hyena_short_call(ph, conv_w, kspec, bias, cos, nsin, n_seq, L):
    rows = HYENA_SHORT_SB * L
    alt = jnp.where(jnp.arange(L) % 2 == 0, 1.0, -1.0).astype(BF16)
    nsin_f = nsin.at[0, :].set(alt)
    cos_i = cos.at[:, 0].set(0.5)
    nsin_i = nsin.at[:, 0].set(0.5 * alt)
    return pl.pallas_call(
        functools.partial(_hyena_short_kernel, L),
        grid=(n_seq // HYENA_SHORT_SB,),
        in_specs=[
            pl.BlockSpec((rows, 3 * W_B), lambda i: (i, 0)),
            pl.BlockSpec((3, 3 * W_B), lambda i: (0, 0)),
            pl.BlockSpec((2, 3, L, W_B), lambda i: (0, 0, 0, 0)),
            pl.BlockSpec((2, 1, W_B), lambda i: (0, 0, 0)),
        ] + [pl.BlockSpec((L, L), lambda i: (0, 0)) for _ in range(4)],
        out_specs=pl.BlockSpec((rows, W_B), lambda i: (i, 0)),
        out_shape=jax.ShapeDtypeStruct((n_seq * L, W_B), BF16),
        compiler_params=_cparams(("parallel",), VMEM_LIMIT),
        name=f"hyena_conv_{L}",
    )(ph, conv_w, kspec, bias, cos, nsin_f, cos_i, nsin_i)


FNET_RT = 512


def _fnet_kernel(L, x_ref, cos_ref, nsin_ref, bdc_ref, bds_ref, out_ref, xc_s, xs_s):
    for r0 in range(0, L, FNET_RT):
        rows = slice(r0, r0 + FNET_RT)
        xb = x_ref[rows, :].astype(BF16)
        xc_s[rows, :] = _dot(xb, bdc_ref[...]).astype(BF16)
        xs_s[rows, :] = _dot(xb, bds_ref[...]).astype(BF16)
    for r0 in range(0, L, FNET_RT):
        rows = slice(r0, r0 + FNET_RT)
        y = _dot(cos_ref[rows, :], xc_s[...]) + _dot(nsin_ref[rows, :], xs_s[...])
        out_ref[rows, :] = (y * (1.0 / math.sqrt(DC * L))).astype(out_ref.dtype)


def _fnet_call(pf, cos, nsin, bdc, bds, n_seq, L, row_blk0):
    const = lambda s: (0, 0)
    return pl.pallas_call(
        functools.partial(_fnet_kernel, L),
        grid=(n_seq,),
        in_specs=[
            pl.BlockSpec((L, W_C), lambda s: (row_blk0 + s, 0)),
            _resident((L, L), const),
            _resident((L, L), const),
            pl.BlockSpec((W_C, W_C), const),
            pl.BlockSpec((W_C, W_C), const),
        ],
        out_specs=pl.BlockSpec((L, W_C), lambda s: (s, 0)),
        out_shape=jax.ShapeDtypeStruct((n_seq * L, W_C), BF16),
        scratch_shapes=[pltpu.VMEM((L, W_C), BF16), pltpu.VMEM((L, W_C), BF16)],
        compiler_params=_cparams(("parallel",), VMEM_LIMIT),
        name=f"fnet_{L}",
    )(pf, cos, nsin, bdc, bds)


FNET_SHORT_SB = 4


def _fnet_short_kernel(L, x_ref, cos_ref, nsin_ref, bdc_ref, bds_ref, out_ref):
    xb = x_ref[...].astype(BF16)
    xc = _dot(xb, bdc_ref[...]).astype(BF16)
    xs = _dot(xb, bds_ref[...]).astype(BF16)
    for s in range(FNET_SHORT_SB):
        rows = slice(s * L, (s + 1) * L)
        y = _dot(cos_ref[...], xc[rows]) + _dot(nsin_ref[...], xs[rows])
        out_ref[rows, :] = (y * (1.0 / math.sqrt(DC * L))).astype(out_ref.dtype)


def _fnet_short_call(pf, cos, nsin, bdc, bds, n_seq, L):
    rows = FNET_SHORT_SB * L
    const = lambda i: (0, 0)
    return pl.pallas_call(
        functools.partial(_fnet_short_kernel, L),
        grid=(n_seq // FNET_SHORT_SB,),
        in_specs=[pl.BlockSpec((rows, W_C), lambda i: (i, 0)),
                  pl.BlockSpec((L, L), const), pl.BlockSpec((L, L), const),
                  pl.BlockSpec((W_C, W_C), const), pl.BlockSpec((W_C, W_C), const)],
        out_specs=pl.BlockSpec((rows, W_C), lambda i: (i, 0)),
        out_shape=jax.ShapeDtypeStruct((n_seq * L, W_C), BF16),
        compiler_params=_cparams(("parallel",), VMEM_LIMIT),
        name=f"fnet_{L}",
    )(pf, cos, nsin, bdc, bds)


DELTA_RB = 256
CHUNKS_PER_RB = DELTA_RB // CHUNK


HEADS_PER_LANE_TILE = LANE // DK


def _block_diag(y, half_masks):
    yb = y.astype(BF16)
    zero = jnp.zeros((CHUNK, LANE), BF16)
    row_blocks = []
    for h in range(HEAD_GROUP):
        tile = h // HEADS_PER_LANE_TILE
        piece = yb[:, tile * LANE:(tile + 1) * LANE] * half_masks[h % HEADS_PER_LANE_TILE]
        row_blocks.append(jnp.concatenate(
            [piece if t == tile else zero for t in range(GROUP_W // LANE)], axis=1))
    return jnp.concatenate(row_blocks, axis=0)


def _stacked_const_rhs(arrs, c, n):
    m = arrs[0].shape[0]
    parts = [p for a in arrs for p in _split(a, n)]
    y = _dot(jnp.concatenate(parts, axis=0), c)
    outs = []
    for i in range(len(arrs)):
        acc = y[i * n * m:(i * n + 1) * m]
        for t in range(1, n):
            acc = acc + y[(i * n + t) * m:(i * n + t + 1) * m]
        outs.append(acc)
    return outs


def _head_sums(arrs, ones_group, n_split=2):
    groups = [_stacked_const_rhs([a[:, g * GROUP_W:(g + 1) * GROUP_W] for a in arrs], ones_group, n_split)
              for g in range(N_GROUPS)]
    return [jnp.concatenate([groups[g][i] for g in range(N_GROUPS)], axis=1) for i in range(len(arrs))]


def _expand_heads(x, lane0):
    cols = [jnp.broadcast_to(x[:, lane0 + h:lane0 + h + 1], (x.shape[0], DK)) for h in range(H_A)]
    return jnp.concatenate(cols, axis=1)


def _const_lhs_split(c, b, n):
    w = b.shape[1]
    y = _dot(c, jnp.concatenate(_split(b, n), axis=1))
    acc = y[:, 0:w]
    for t in range(1, n):
        acc = acc + y[:, t * w:(t + 1) * w]
    return acc


def _delta_chunk_stages(pd_ref, prev_ref, next_ref, r, has_prev, has_next, dirs, cw, a_neg, dtb, ones_bd,
                        tri, half_masks, eye_tile):
    ch = {}
    rows = slice(r * CHUNK, (r + 1) * CHUNK)

    def conv():
        cur = pd_ref[rows, 0:QKV_W]
        if r == 0:
            prev_row = jnp.where(has_prev, prev_ref[7:8, :], 0.0)
        else:
            prev_row = pd_ref[r * CHUNK - 1:r * CHUNK, 0:QKV_W]
        if r == CHUNKS_PER_RB - 1:
            next_row = jnp.where(has_next, next_ref[0:1, :], 0.0)
        else:
            next_row = pd_ref[(r + 1) * CHUNK:(r + 1) * CHUNK + 1, 0:QKV_W]
        qkv = _silu(_conv3_rows(cur, prev_row, next_row, cw))
        ch["q"] = qkv[:, 0:H_A * DK]
        ch["k"] = qkv[:, H_A * DK:2 * H_A * DK]
        ch["v"] = qkv[:, 2 * H_A * DK:]

    def norms():
        q, k = ch.pop("q"), ch.pop("k")
        qss, kss = _head_sums([q * q, k * k], ones_bd, 1)
        ch["qn"] = q * lax.rsqrt(qss + EPS) * (DK ** -0.5)
        ch["kn"] = k * lax.rsqrt(kss + EPS)

    def gram():
        ch["gram"], ch["knT"] = [], []
        for g in range(N_GROUPS):
            lanes = slice(g * GROUP_W, (g + 1) * GROUP_W)
            lhs = jnp.concatenate([ch["kn"][:, lanes].astype(BF16), ch["qn"][:, lanes].astype(BF16), eye_tile],
                                  axis=0)
            res = _dot_nt(lhs, _block_diag(ch["kn"][:, lanes], half_masks))
            ch["gram"].append(res[0:2 * CHUNK])
            ch["knT"].append(res[2 * CHUNK:3 * CHUNK])

    def decay():
        ba = pd_ref[rows, OFF_B:OFF_B + LANE]
        sig = _sigmoid(ba)
        glog = a_neg * _softplus(ba + dtb)
        ch["decay"] = {}
        for d in dirs:
            gcum = _const_lhs_split(tri[d], glog, 3)
            beta = _expand_heads(sig, d * H_A)
            gcc8 = _expand_heads(gcum, (2 + d) * H_A)
            ch["decay"][d] = (beta, gcc8)

    return ch, [conv, norms, gram, decay]


def _delta_kernel(nb, zero_init, n_prev, *refs):
    shared = nb == 1
    it = iter(refs)
    blocks = [(next(it), next(it), next(it))]
    if not shared:
        blocks.append((next(it), next(it), next(it)))
    cw_ref, par_ref, ones_ref = next(it), next(it), next(it)
    s0_ref = None if zero_init else next(it)
    sprev_ref = next(it) if n_prev else None
    o_refs = (next(it), next(it))
    sfin_ref = next(it)
    u_s, w_s, p_s, qg_s, kg_s, gl_s, st_s = (next(it) for _ in range(7))
    j = pl.program_id(1)

    ri = lax.broadcasted_iota(jnp.int32, (CHUNK, GROUP_W), 0)
    cj = lax.broadcasted_iota(jnp.int32, (CHUNK, GROUP_W), 1) % CHUNK
    ixj = ri ^ cj
    eye = ixj == 0
    eye_tile = jnp.where(eye, 1.0, 0.0).astype(BF16)
    hl =lax.broadcasted_iota(jnp.int32, (CHUNK, LANE), 1) // DK
    half_masks = tuple(jnp.where(hl == h, 1.0, 0.0).astype(BF16) for h in range(HEADS_PER_LANE_TILE))
    ti = lax.broadcasted_iota(jnp.int32, (CHUNK, CHUNK), 0)
    tm = lax.broadcasted_iota(jnp.int32, (CHUNK, CHUNK), 1)
    ri8 = lax.broadcasted_iota(jnp.int32, (CHUNK, H_A * DK), 0)
    cj8 = lax.broadcasted_iota(jnp.int32, (CHUNK, H_A * DK), 1) % CHUNK
    incl = (ri >= cj, ri <= cj)
    strict = (ri > cj, ri < cj)
    tri = tuple(jnp.where(m, 1.0, 0.0).astype(BF16) for m in (tm <= ti, tm >= ti))
    eye8 = ri8 == cj8
    last_row = (CHUNK - 1, 0)

    @pl.when(j == 0)
    def _():
        for d in range(2):
            for g in range(N_GROUPS):
                if zero_init:
                    st_s[d, g] = jnp.zeros((DK, GROUP_W), F32)
                else:
                    r0 = g * GROUP_W
                    st_s[d, g] = jnp.concatenate(
                        [s0_ref[0, d, r0 + hh * DK:r0 + (hh + 1) * DK, :] for hh in range(HEAD_GROUP)], axis=1)

    cw = cw_ref[...]
    a_neg = -jnp.exp(par_ref[0:1, :])
    dtb = par_ref[1:2, :]
    pos = (j, nb - 1 - j)
    scan_order = (tuple(range(CHUNKS_PER_RB)), tuple(reversed(range(CHUNKS_PER_RB))))

    def unit_thunk(d, r, ch, g, units):
        def run():
            rows = slice(r * CHUNK, (r + 1) * CHUNK)
            lanes = slice(g * GROUP_W, (g + 1) * GROUP_W)
            beta, gcc8 = ch["decay"][d]
            gcr = jnp.sum(jnp.where(eye8, gcc8, 0.0), axis=0, keepdims=True)[:, lanes]
            qn, kn, be, gcc = ch["qn"][:, lanes], ch["kn"][:, lanes], beta[:, lanes], gcc8[:, lanes]
            kq = ch["gram"][g]
            dec = jnp.exp(jnp.where(incl[d], gcc - gcr, -1e30))
            a = jnp.where(strict[d], kq[0:CHUNK] * be * dec, 0.0)
            eg = jnp.exp(gcc)
            gcl = gcc[last_row[d]:last_row[d] + 1, :]
            units.append((d, rows, lanes, a, ch["v"][:, lanes] * be, kn * be * eg))
            p_s[d, rows, lanes] = (kq[CHUNK:2 * CHUNK] * dec).astype(BF16)
            qg_s[d, rows, lanes] = (qn * eg).astype(BF16)
            kg_s[d, rows, lanes] = (ch["knT"][g] * jnp.exp(gcl - gcr)).astype(BF16)
            gl_s[d, r * 8:(r + 1) * 8, lanes] = jnp.broadcast_to(jnp.exp(gcl), (8, GROUP_W))
        return run

    def prep_thunks(units):
        thunks = []
        if shared:
            todo = [(0, r, (0, 1)) for r in range(CHUNKS_PER_RB)]
        else:
            todo = [(d, r, (d,)) for d in range(2) for r in scan_order[d]]
        for b, r, dirs in todo:
            ch, stages = _delta_chunk_stages(*blocks[b], r, pos[b] > 0, pos[b] < nb - 1, dirs, cw, a_neg, dtb,
                                             ones_ref[...], tri, half_masks, eye_tile)
            thunks += stages
            thunks += [unit_thunk(d, r, ch, g, units) for d in dirs for g in range(N_GROUPS)]
        return thunks

    def scan_thunk(d, c):
        def run():
            r = scan_order[d][c]
            rows = slice(r * CHUNK, (r + 1) * CHUNK)
            for g in range(N_GROUPS):
                lanes = slice(g * GROUP_W, (g + 1) * GROUP_W)
                s = st_s[d, g]
                wq = jnp.concatenate([w_s[d, rows, lanes], qg_s[d, rows, lanes]], axis=0)
                ws_qs = _dot(wq, _block_diag(s, half_masks))
                v_new = u_s[d, rows, lanes] - ws_qs[0:CHUNK]
                pk = jnp.concatenate([p_s[d, rows, lanes], kg_s[d, rows, lanes]], axis=0)
                po = _dot(pk, _block_diag(v_new, half_masks))
                st_s[d, g] = s * gl_s[d, r * 8:r * 8 + 1, lanes] + po[CHUNK:2 * CHUNK]
                o_refs[d][rows, lanes] = ws_qs[CHUNK:2 * CHUNK] + po[0:CHUNK]
        return run

    def solve(units):
        r0s = [jnp.where(ixj == 1, un[3], 0.0) for un in units]
        zxs = [jnp.concatenate([un[3] - _dot(un[3].astype(BF16), _block_diag(r0, half_masks)),
                                jnp.where(eye, 1.0, 0.0) - r0], axis=0) for un, r0 in zip(units, r0s)]
        for lvl in range(1, 5):
            links = [_block_diag(jnp.where((ixj >> lvl) == 1, zx[0:CHUNK], 0.0), half_masks) for zx in zxs]
            zxs = [zx - _dot(zx.astype(BF16), lk) for zx, lk in zip(zxs, links)]
        links = [_block_diag(jnp.where((ixj >> 5) == 1, zx[0:CHUNK], 0.0), half_masks) for zx in zxs]
        xs = [zx[CHUNK:2 * CHUNK] - _dot(zx[CHUNK:2 * CHUNK].astype(BF16), lk) for zx, lk in zip(zxs, links)]
        for x, (d, rows, lanes, _, vb, kbe) in zip(xs, units):
            rhs = jnp.concatenate([_block_diag(vb, half_masks), _block_diag(kbe, half_masks)], axis=1)
            uw = _dot(x.astype(BF16), rhs)
            u_s[d, rows, lanes] = uw[:, 0:GROUP_W]
            w_s[d, rows, lanes] = uw[:, GROUP_W:2 * GROUP_W].astype(BF16)

    units = []
    for f in prep_thunks(units):
        f()
    solve(units)
    for c in range(CHUNKS_PER_RB):
        for d in range(2):
            scan_thunk(d, c)()

    @pl.when(j == nb - 1)
    def _():
        for d in range(2):
            for g in range(N_GROUPS):
                s = st_s[d, g]
                for hh in range(HEAD_GROUP):
                    r0 = g * GROUP_W + hh * DK
                    sfin_ref[0, n_prev, d, r0:r0 + DK, :] = s[:, hh * DV:(hh + 1) * DV]
        for p in range(n_prev):
            sfin_ref[0, p] = sprev_ref[0, p]


def _delta_call(pd, conv_w, par, ones_bd, s0, n_seq, nb, blk0, prev_states=None):
    zero_init = s0 is None
    n_prev = 0 if prev_states is None else prev_states.shape[1]
    n_blocks_all = T_ALL // DELTA_RB
    blk_of = (lambda s, j: blk0 + s * nb + j, lambda s, j: blk0 + s * nb + nb - 1 - j)
    in_specs, args = [], []
    for d in range(1 if nb == 1 else 2):
        prev, nxt = _halo_specs(QKV_W, DELTA_RB, n_blocks_all, blk_of[d])
        in_specs += [pl.BlockSpec((DELTA_RB, PD_W), lambda s, j, d=d: (blk_of[d](s, j), 0)), prev, nxt]
        args += [pd, pd, pd]
    in_specs += [
        pl.BlockSpec((3, QKV_W), lambda s, j: (0, 0)),
        pl.BlockSpec((8, LANE), lambda s, j: (0, 0)),
        pl.BlockSpec((GROUP_W, GROUP_W), lambda s, j: (0, 0)),
    ]
    args += [conv_w, par, ones_bd]
    if not zero_init:
        in_specs.append(pl.BlockSpec((1, 2, H_A * DK, DV), lambda s, j: (s, 0, 0, 0)))
        args.append(s0)
    if n_prev:
        in_specs.append(pl.BlockSpec((1, n_prev, 2, H_A * DK, DV), lambda s, j: (s, 0, 0, 0, 0)))
        args.append(prev_states)
    rows = n_seq * nb * DELTA_RB
    dir_buf = lambda n, dt=F32: pltpu.VMEM((2, n, H_A * DK), dt)
    return pl.pallas_call(
        functools.partial(_delta_kernel, nb, zero_init, n_prev),
        grid=(n_seq, nb),
        in_specs=in_specs,
        out_specs=[pl.BlockSpec((DELTA_RB, W_A), lambda s, j: (s * nb + j, 0)),
                   pl.BlockSpec((DELTA_RB, W_A), lambda s, j: (s * nb + nb - 1 - j, 0)),
                   pl.BlockSpec((1, n_prev + 1, 2, H_A * DK, DV), lambda s, j: (s, 0, 0, 0, 0))],
        out_shape=[jax.ShapeDtypeStruct((rows, W_A), F32),
                   jax.ShapeDtypeStruct((rows, W_A), F32),
                   jax.ShapeDtypeStruct((n_seq, n_prev + 1, 2, H_A * DK, DV), F32)],
        scratch_shapes=[dir_buf(DELTA_RB)] + [dir_buf(DELTA_RB, BF16) for _ in range(4)]
        + [dir_buf(CHUNKS_PER_RB * 8), pltpu.VMEM((2, N_GROUPS, DK, GROUP_W), F32)],
        compiler_params=_cparams(("parallel", "arbitrary"), VMEM_LIMIT),
        name=f"deltanet_nb{nb}",
    )(*args)


FFN_TN = D_FF // 2


def _postmix_kernel(final_norm, tile0, ofc_ref, ofl_ref, obc_ref, obl_ref, z_ref, ybc_ref, ybl_ref,
                    ycc_ref, ycl_ref, pg_ref, x_ref, mod_ref, na_ref, ones_ref, wpa_ref, wpb_ref, wpc_ref,
                    wo_ref, g2_ref, wgu_ref, wdn_ref, nf_ref, out_ref):
    is_ctx = pl.program_id(0) + tile0 < N_CTX_TILES
    m = mod_ref[0]
    o = jnp.where(is_ctx, ofc_ref[...] + obc_ref[...], ofl_ref[...] + obl_ref[...])
    yb = jnp.where(is_ctx, ybc_ref[...], ybl_ref[...])
    yc = jnp.where(is_ctx, ycc_ref[...], ycl_ref[...])
    ms = _head_sums([o * o], ones_ref[...])[0] * (1.0 / DV)
    ya = (o * lax.rsqrt(ms + EPS) * na_ref[...]) * _silu(z_ref[...])
    merged = (pg_ref[:, 0:D_MODEL].astype(F32) * _dot(ya.astype(BF16), wpa_ref[0])
              + pg_ref[:, D_MODEL:2 * D_MODEL].astype(F32) * _dot(yb.astype(BF16), wpb_ref[0])
              + pg_ref[:, 2 * D_MODEL:3 * D_MODEL].astype(F32) * _dot(yc.astype(BF16), wpc_ref[0]))
    x = x_ref[...] + m[2:3] * _dot(merged.astype(BF16), wo_ref[0])

    h = _rms_mod(x, g2_ref[...], m[4:5], m[3:4]).astype(BF16)
    acc = None
    for c in range(0, D_FF, FFN_TN):
        gate = _dot(h, wgu_ref[0, :, c:c + FFN_TN])
        up = _dot(h, wgu_ref[0, :, D_FF + c:D_FF + c + FFN_TN])
        part = _dot((_silu(gate) * up).astype(BF16), wdn_ref[0, c:c + FFN_TN, :])
        acc = part if acc is None else acc + part
    xn = x + m[5:6] * acc
    if final_norm:
        ms = jnp.mean(xn * xn, axis=-1, keepdims=True)
        xn = xn * lax.rsqrt(ms + EPS) * nf_ref[...]
    out_ref[...] = xn


def _postmix_call(o_ctx, o_lat, pd, yb, yc, pg, x, mod3, na512, ones_bd, wpa, wpb, wpc, wo,
                  g2, wgu, wdn, nf, layer, final_norm, tile0=0, n_tiles=T_ALL // ROW_TILE):
    row = lambda i: (i + tile0, 0)
    const = lambda i: (0, 0)
    lyr = lambda i: (layer, 0, 0)
    assert W_A == W_B == W_C
    return pl.pallas_call(
        functools.partial(_postmix_kernel, final_norm, tile0),
        grid=(n_tiles,),
        in_specs=[
            *_ctx_lat_specs(W_A, tile0), *_ctx_lat_specs(W_A, tile0),
            pl.BlockSpec((ROW_TILE, W_A), lambda i: (i + tile0, OFF_Z // W_A)),
            *_ctx_lat_specs(W_B, tile0), *_ctx_lat_specs(W_C, tile0),
            pl.BlockSpec((ROW_TILE, 3 * D_MODEL), row),
            pl.BlockSpec((ROW_TILE, D_MODEL), row),
            pl.BlockSpec((1, 6, D_MODEL), lambda i: (_mod_row_block(i + tile0), 0, 0)),
            pl.BlockSpec((1, W_A), const),
            _resident((GROUP_W, GROUP_W), const),
            _resident((1, W_A, D_MODEL), lyr),
            _resident((1, W_B, D_MODEL), lyr),
            _resident((1, W_C, D_MODEL), lyr),
            _resident((1, D_MODEL, D_MODEL), lyr),
            pl.BlockSpec((1, D_MODEL), const),
            _resident((1, D_MODEL, 2 * D_FF), lyr),
            _resident((1, D_FF, D_MODEL), lyr),
            pl.BlockSpec((1, D_MODEL), const),
        ],
        out_specs=pl.BlockSpec((ROW_TILE, D_MODEL), lambda i: (i, 0)),
        out_shape=jax.ShapeDtypeStruct((n_tiles * ROW_TILE, D_MODEL), F32),
        compiler_params=_cparams(("parallel",), VMEM_LIMIT),
        name="postmix_final" if final_norm else "postmix",
    )(o_ctx[0], o_lat[0], o_ctx[1], o_lat[1], pd, yb[0], yb[1], yc[0], yc[1], pg, x, mod3, na512,
      ones_bd, wpa, wpb, wpc, wo, g2.reshape(1, D_MODEL), wgu, wdn, nf.reshape(1, D_MODEL))


TABLE_SPLIT = 64


def _grid_pos_embed(n_tokens):
    rows = n_tokens // GRID_W
    quarter = D_MODEL // 4
    omega = 1.0 / (10000.0 ** (jnp.arange(quarter, dtype=F32) / quarter))

    def emb(pos):
        a = pos[:, None] * omega[None, :]
        return jnp.concatenate([jnp.sin(a), jnp.cos(a)], axis=-1)

    e_row, e_col = lax.optimization_barrier((emb(jnp.arange(rows).astype(F32)),
                                             emb(jnp.arange(GRID_W).astype(F32))))
    return jnp.concatenate([jnp.repeat(e_row, GRID_W, axis=0), jnp.tile(e_col, (rows, 1))], axis=-1)


def _cos_nsin_tables(n, period):
    t = jnp.arange(n, dtype=jnp.int32)[None, :]

    def cs(r):
        ang = ((r * t) % period).astype(F32) * (2.0 * math.pi / period)
        return jnp.cos(ang), jnp.sin(ang)

    ca, sa = cs(jnp.arange(n // TABLE_SPLIT, dtype=jnp.int32)[:, None] * TABLE_SPLIT)
    cb, sb = cs(jnp.arange(TABLE_SPLIT, dtype=jnp.int32)[:, None])
    ca, sa, cb, sb = lax.optimization_barrier((ca, sa, cb, sb))
    ca, sa = ca[:, None, :], sa[:, None, :]
    cos = (ca * cb[None] - sa * sb[None]).reshape(n, n)
    nsin = (-(sa * cb[None] + ca * sb[None])).reshape(n, n)
    return cos.astype(BF16), nsin.astype(BF16)


def _hyena_positions(L):
    bands = (HY_EMB - 1) // 2
    t = jnp.linspace(0.0, 1.0, L, dtype=F32)[:, None]
    wpos = (2.0 * math.pi / L) * jnp.arange(L, dtype=F32)[:, None]
    fr = jnp.linspace(1e-4, bands - 1, bands, dtype=F32)[None, :]
    zpos = jnp.concatenate([t, jnp.cos(fr * wpos), -jnp.sin(fr * wpos)], axis=-1)
    zpos = jnp.pad(zpos, ((0, 0), (0, LANE - HY_EMB)))
    deltas = jnp.abs(jnp.linspace(math.log(HY_DECAY_TARGET) / HY_SLOW_PCT,
                                  math.log(HY_DECAY_TARGET) / HY_FAST_PCT, W_B, dtype=F32))
    window = jnp.exp(-t * deltas[None, :])
    return zpos, window


def _group_tables():
    r = jnp.arange(DC, dtype=jnp.int32)
    ang = ((r[:, None] * r[None, :]) % DC).astype(F32) * (2.0 * math.pi / DC)
    eye = jnp.eye(G_C, dtype=F32)
    return jnp.kron(eye, jnp.cos(ang)).astype(BF16), jnp.kron(eye, jnp.sin(ang)).astype(BF16)


def _head_ones():
    return jnp.kron(jnp.eye(HEAD_GROUP, dtype=F32), jnp.ones((DK, DK), F32)).astype(BF16)


def kernel(x_prompt, x_sample, state_delta, c, c_ctx, w_mod, b_mod, norm1_g, norm2_g, w_in, conv_qkv, a_log, dt_bias, norm_a, conv_hy, hy_w1, hy_b1, hy_freq, hy_w2, hy_b2, hy_w3, hy_bias, w_pa, w_pb, w_pc, w_o, w_gu, w_down, norm_f):
    assert x_prompt.shape == (N_CTX_SEQ, L_CTX, D_MODEL) and x_sample.shape == (N_LAT_SEQ, L_LAT, D_MODEL)
    st = jnp.pad(jnp.concatenate([c_ctx[None], c], axis=0).T, ((0, 0), (0, 8 - 1 - N_LAT_SEQ)))
    mod = _mod_call(st, w_mod, b_mod).reshape(DEPTH, 8, 6, D_MODEL)

    ones_bd = _head_ones()
    bdc, bds = _group_tables()
    seqs = ((L_CTX, N_CTX_SEQ, 0), (L_LAT, N_LAT_SEQ, T_CTX // L_LAT))
    tables = {L: (_cos_nsin_tables(L, 2 * L), _cos_nsin_tables(L, L), _hyena_positions(L))
              for L, _, _ in seqs}

    w_in_b = jnp.swapaxes(w_in, 1, 2).astype(BF16)
    w_pa_b, w_pb_b, w_pc_b, w_o_b, w_gu_b, w_down_b = (
        w.astype(BF16) for w in (w_pa, w_pb, w_pc, w_o, w_gu, w_down))

    x = None
    ctx_states = None
    for l in range(DEPTH):
        mod3 = mod[l, 0:3]
        if l == 0:
            xs = (x_prompt.reshape(T_CTX, D_MODEL), x_sample.reshape(T_LAT, D_MODEL), _grid_pos_embed(L_LAT))
            pd, ph, pf, pg, x = _inproj_call(xs, mod3, norm1_g[l], w_in_b, l)
        else:
            pd, ph, pf, pg = _inproj_call((x,), mod3, norm1_g[l], w_in_b, l)

        par = jnp.zeros((8, LANE), F32)
        par = par.at[0, 2 * H_A:4 * H_A].set(a_log[l].reshape(-1))
        par = par.at[1, 2 * H_A:4 * H_A].set(dt_bias[l].reshape(-1))
        *o_ctx, ctx_states = _delta_call(pd, conv_qkv[l], par, ones_bd, None,
                                         N_CTX_SEQ, L_CTX // DELTA_RB, 0, ctx_states)
        s0 = state_delta[:, l].astype(F32).reshape(N_LAT_SEQ, 2, H_A * DK, DV)
        *o_lat, _ = _delta_call(pd, conv_qkv[l], par, ones_bd, s0,
                                N_LAT_SEQ, L_LAT // DELTA_RB, T_CTX // DELTA_RB)

        w1p = jnp.pad(hy_w1[l], ((0, LANE - HY_EMB), (0, 0)))
        yb, yc = [], []
        for L, n_seq, blk0 in seqs:
            (cos2, nsin2), (cos1, nsin1), (zpos, window) = tables[L]
            kspec = _filter_call(L, zpos, w1p, hy_b1[l][None], hy_freq[l][None], hy_w2[l],
                                 hy_b2[l][None], hy_w3[l], window, cos2, nsin2)
            bias = hy_bias[l][:, None, :]
            if L == HYENA_FT:
                assert blk0 == 0
                yb.append(_hyena_short_call(ph, conv_hy[l], kspec, bias, cos2, nsin2, n_seq, L))
                yc.append(_fnet_short_call(pf, cos1, nsin1, bdc, bds, n_seq, L))
            else:
                yb.append(_hyena_call(ph, conv_hy[l], kspec, bias, cos2, nsin2, n_seq, L, blk0, HYENA_FT))
                yc.append(_fnet_call(pf, cos1, nsin1, bdc, bds, n_seq, L, blk0))

        na512 = jnp.tile(norm_a[l], H_A)[None]
        post = functools.partial(_postmix_call, o_ctx, o_lat, pd, yb, yc, pg, x, mod3, na512, ones_bd,
                                 w_pa_b, w_pb_b, w_pc_b, w_o_b, norm2_g[l], w_gu_b, w_down_b, norm_f, l)
        if l < DEPTH - 1:
            x = post(False)
        else:
            y_prompt = post(True, 0, N_CTX_TILES).reshape(N_CTX_SEQ, L_CTX, D_MODEL)
            y_sample = post(True, N_CTX_TILES, N_LAT_TILES).reshape(N_LAT_SEQ, L_LAT, D_MODEL)

    new_state = ctx_states.reshape(N_CTX_SEQ, DEPTH, 2, H_A, DK, DV).astype(x_prompt.dtype)
    return (y_prompt, y_sample, new_state)
```

```python
import functools
import math

import jax
import jax.numpy as jnp
from jax import lax
from jax.experimental import pallas as pl
from jax.experimental.pallas import tpu as pltpu

F32 = jnp.float32
BF16 = jnp.bfloat16

D_MODEL = 1024
N_CTX_SEQ = 32
L_CTX = 256
DEPTH = 2
N_LAT_SEQ = 2
L_LAT = 2048
GRID_W = 64
EPS = 1e-6
H_A = 8
DK = 64
DV = 64
W_A = H_A * DV
QKV_W = 2 * H_A * DK + H_A * DV
CHUNK = 64
W_B = 512
HY_EMB = 33
HY_HID = 64
HY_DECAY_TARGET = 1e-2
HY_FAST_PCT = 0.3
HY_SLOW_PCT = 1.5
G_C = 8
DC = 64
W_C = G_C * DC
D_FF = ((8 * D_MODEL + 3 * 256 - 1) // (3 * 256)) * 256
OFF_Z = QKV_W
OFF_B = OFF_Z + W_A
OFF_A = OFF_B + 2 * H_A
OFF_HY = OFF_A + 2 * H_A
OFF_FN = OFF_HY + 3 * W_B
OFF_GATE = OFF_FN + W_C

T_CTX = N_CTX_SEQ * L_CTX
T_LAT = N_LAT_SEQ * L_LAT
T_ALL = T_CTX + T_LAT
ROW_TILE = 256
N_CTX_TILES = T_CTX // ROW_TILE
N_LAT_TILES = T_LAT // ROW_TILE
LANE = 128
PD_W = QKV_W + W_A + LANE
HEAD_GROUP = 4
GROUP_W = HEAD_GROUP * DK
N_GROUPS = H_A // HEAD_GROUP
VMEM_LIMIT = 56 * 1024 * 1024


def _cparams(sem, vmem=None):
    return pltpu.CompilerParams(dimension_semantics=sem, vmem_limit_bytes=vmem)


def _dot(a, b):
    return jnp.dot(a, b, preferred_element_type=F32)


def _dot_nt(a, b):
    return lax.dot_general(a, b, (((1,), (1,)), ((), ())), preferred_element_type=F32)


def _split(a, n):
    parts = []
    rem = a
    for i in range(n):
        p = rem.astype(BF16)
        parts.append(p)
        if i + 1 < n:
            rem = rem - p.astype(F32)
    return parts


def _mm3(a, b):
    ah, al = _split(a, 2)
    bh, bl = _split(b, 2)
    return _dot(ah, bh) + (_dot(ah, bl) + _dot(al, bh))


def _sigmoid(x):
    return 1.0 / (1.0 + jnp.exp(-x))


def _silu(x):
    return x * _sigmoid(x)


def _softplus(x):
    return jnp.maximum(x, 0.0) + jnp.log(1.0 + jnp.exp(-jnp.abs(x)))


def _mod_row_block(i):
    per_lat = L_LAT // ROW_TILE
    return jnp.where(i < N_CTX_TILES, 0, 1 + (i - N_CTX_TILES) // per_lat)


def _ctx_tile(i):
    return jnp.minimum(i, N_CTX_TILES - 1)


def _lat_tile(i):
    return jnp.maximum(i - N_CTX_TILES, 0)


def _ctx_lat_specs(width, tile0=0):
    return (pl.BlockSpec((ROW_TILE, width), lambda i: (_ctx_tile(i + tile0), 0)),
            pl.BlockSpec((ROW_TILE, width), lambda i: (_lat_tile(i + tile0), 0)))


MOD_TN = 1536


def _mod_kernel(st_ref, w_ref, b_ref, out_ref):
    s = _silu(st_ref[...])
    w = w_ref[0]
    rows = [jnp.sum(s[:, r:r + 1] * w, axis=0, keepdims=True) + b_ref[0] for r in range(3)]
    rows.append(jnp.zeros((5, MOD_TN), F32))
    out_ref[0] = jnp.concatenate(rows, axis=0)


def _mod_call(st, w_mod, b_mod):
    n6 = 6 * D_MODEL
    return pl.pallas_call(
        _mod_kernel,
        grid=(DEPTH, n6 // MOD_TN),
        in_specs=[
            pl.BlockSpec((D_MODEL, 8), lambda l, j: (0, 0)),
            pl.BlockSpec((1, D_MODEL, MOD_TN), lambda l, j: (l, 0, j)),
            pl.BlockSpec((1, 1, MOD_TN), lambda l, j: (l, 0, j)),
        ],
        out_specs=pl.BlockSpec((1, 8, MOD_TN), lambda l, j: (l, 0, j)),
        out_shape=jax.ShapeDtypeStruct((DEPTH, 8, n6), F32),
        compiler_params=_cparams(("parallel", "parallel")),
        name="adaln_mod",
    )(st, w_mod, b_mod.reshape(DEPTH, 1, n6))


INPROJ_TN = 512
INPROJ_WIDTHS = (PD_W, 3 * W_B, W_C, 3 * D_MODEL)
INPROJ_FEATURES = (OFF_HY, 3 * W_B, W_C, 3 * D_MODEL)


def _rms_mod(x, g, scale, shift):
    ms = jnp.mean(x * x, axis=-1, keepdims=True)
    return (x * lax.rsqrt(ms + EPS) * g) * (1.0 + scale) + shift


def _inproj_kernel(first, *refs):
    if first:
        (xc_ref, xl_ref, pos_ref, mod_ref, g_ref, w_ref, pd_ref, ph_ref, pf_ref, pg_ref, x_ref) = refs
        x = jnp.where(pl.program_id(0) < N_CTX_TILES, xc_ref[...], xl_ref[...] + pos_ref[...])
        x_ref[...] = x
    else:
        (xin_ref, mod_ref, g_ref, w_ref, pd_ref, ph_ref, pf_ref, pg_ref) = refs
        x = xin_ref[...]
    m = mod_ref[0]
    h = _rms_mod(x, g_ref[...], m[1:2], m[0:1]).astype(BF16)
    row0 = 0
    for o_ref, n_feat in zip((pd_ref, ph_ref, pf_ref, pg_ref), INPROJ_FEATURES):
        n = o_ref.shape[1]
        for c in range(0, n, INPROJ_TN):
            e = min(c + INPROJ_TN, n)
            ef = min(e, n_feat)
            y = _dot_nt(h, w_ref[0, row0 + c:row0 + ef, :])
            if ef < e:
                y = jnp.concatenate([y, jnp.zeros((y.shape[0], e - ef), F32)], axis=1)
            if o_ref is pg_ref:
                y = _sigmoid(y)
            o_ref[:, c:e] = y.astype(o_ref.dtype)
        row0 += n_feat


def _resident(shape, index_map):
    return pl.BlockSpec(shape, index_map, pipeline_mode=pl.Buffered(1))


def _inproj_call(xs, mod3, g, w_all, layer):
    first = len(xs) == 3
    widths = INPROJ_WIDTHS
    row = lambda i: (i, 0)
    const = lambda i: (0, 0)
    if first:
        per_lat = L_LAT // ROW_TILE
        x_specs = list(_ctx_lat_specs(D_MODEL)) + [
            pl.BlockSpec((ROW_TILE, D_MODEL), lambda i: (_lat_tile(i) % per_lat, 0))]
    else:
        x_specs = [pl.BlockSpec((ROW_TILE, D_MODEL), row)]
    out_widths = widths + ((D_MODEL,) if first else ())
    out_dtypes = (F32, F32, F32, BF16) + ((F32,) if first else ())
    return pl.pallas_call(
        functools.partial(_inproj_kernel, first),
        grid=(T_ALL // ROW_TILE,),
        in_specs=x_specs + [
            pl.BlockSpec((1, 6, D_MODEL), lambda i: (_mod_row_block(i), 0, 0)),
            pl.BlockSpec((1, D_MODEL), const),
            _resident((1, sum(INPROJ_FEATURES), D_MODEL), lambda i: (layer, 0, 0)),
        ],
        out_specs=[pl.BlockSpec((ROW_TILE, w), row) for w in out_widths],
        out_shape=[jax.ShapeDtypeStruct((T_ALL, w), dt) for w, dt in zip(out_widths, out_dtypes)],
        compiler_params=_cparams(("parallel",), VMEM_LIMIT),
        name="inproj_first" if first else "inproj",
    )(*xs, mod3, g.reshape(1, D_MODEL), w_all)


def _conv3_rows(cur, prev_row, next_row, w):
    n = cur.shape[0]
    sub = lax.broadcasted_iota(jnp.int32, (8, cur.shape[1]), 0)
    up = pltpu.roll(cur, 1, 0)
    up = jnp.concatenate([jnp.where(sub == 0, prev_row, up[0:8]), up[8:]], axis=0)
    dn = pltpu.roll(cur, n - 1, 0)
    dn = jnp.concatenate([dn[0:n - 8], jnp.where(sub == 7, next_row, dn[n - 8:])], axis=0)
    return up * w[0:1] + cur * w[1:2] + dn * w[2:3]


def _halo_specs(width, rows_per_block, n_row_blocks, blk_of):
    per = rows_per_block // 8
    last = n_row_blocks * per - 1
    prev = pl.BlockSpec((8, width), lambda *a: (jnp.maximum(blk_of(*a) * per - 1, 0), 0))
    nxt = pl.BlockSpec((8, width), lambda *a: (jnp.minimum((blk_of(*a) + 1) * per, last), 0))
    return prev, nxt


FILT_RT = 256


def _alternating_sum(x):
    t = lax.broadcasted_iota(jnp.int32, x.shape, 0)
    return jnp.sum(jnp.where(t % 2 == 0, x, -x), axis=0, keepdims=True)


def _filter_kernel(L, zpos_ref, w1_ref, b1_ref, fq_ref, w2_ref, b2_ref, w3_ref, win_ref,
                   cos_ref, nsin_ref, k_ref, hs_s, hm_s, krl_s, h2_s):
    o = pl.program_id(0)
    rt = pl.program_id(1)

    @pl.when((o == 0) & (rt == 0))
    def _():
        fq = fq_ref[...]
        for r0 in range(0, L, FILT_RT):
            rows = slice(r0, r0 + FILT_RT)
            h = jnp.sin(fq * (_mm3(zpos_ref[rows, :], w1_ref[...]) + b1_ref[...]))
            h2_s[rows, :] = jnp.sin(fq * (_mm3(h, w2_ref[...]) + b2_ref[...]))

    @pl.when(rt == 0)
    def _():
        alt_acc = jnp.zeros((1, W_B), F32)
        for r0 in range(0, L, FILT_RT):
            rows = slice(r0, r0 + FILT_RT)
            hf = _mm3(h2_s[rows, :], w3_ref[...])
            win = win_ref[rows, :]
            fw = hf[:, 0:W_B] * win
            bw = hf[:, W_B:2 * W_B] * win
            hsum = fw + bw
            hs_s[rows, :] = hsum.astype(BF16)
            hm_s[rows, :] = (fw - bw).astype(BF16)
            alt_acc = alt_acc + _alternating_sum(hsum)
        krl_s[...] = jnp.broadcast_to(alt_acc, krl_s.shape)

    p1 = _dot(cos_ref[...], hs_s[...])
    p2 = _dot(nsin_ref[...], hm_s[...])
    first = (rt * FILT_RT + lax.broadcasted_iota(jnp.int32, p1.shape, 0)) == 0
    k_ref[0, 0] = p1
    k_ref[0, 1] = jnp.where(first, krl_s[0:1, :], p1)
    k_ref[0, 2] = jnp.where(first, 0.0, p2)


def _filter_call(L, zpos, w1p, b1, fq, w2, b2, w3, win, cos, nsin):
    nrt = L // FILT_RT
    c2 = lambda o, r: (0, 0)
    return pl.pallas_call(
        functools.partial(_filter_kernel, L),
        grid=(2, nrt),
        in_specs=[
            pl.BlockSpec((L, LANE), c2),
            pl.BlockSpec((LANE, HY_HID), c2),
            pl.BlockSpec((1, HY_HID), c2),
            pl.BlockSpec((1, HY_HID), c2),
            pl.BlockSpec((HY_HID, HY_HID), c2),
            pl.BlockSpec((1, HY_HID), c2),
            pl.BlockSpec((HY_HID, 2 * W_B), lambda o, r: (0, o)),
            pl.BlockSpec((L, W_B), c2),
            pl.BlockSpec((FILT_RT, L), lambda o, r: (r, 0)),
            pl.BlockSpec((FILT_RT, L), lambda o, r: (r, 0)),
        ],
        out_specs=pl.BlockSpec((1, 3, FILT_RT, W_B), lambda o, r: (o, 0, r, 0)),
        out_shape=jax.ShapeDtypeStruct((2, 3, L, W_B), F32),
        scratch_shapes=[pltpu.VMEM((L, W_B), BF16), pltpu.VMEM((L, W_B), BF16),
                        pltpu.VMEM((8, W_B), F32), pltpu.VMEM((L, HY_HID), F32)],
        compiler_params=_cparams(("arbitrary", "arbitrary"), VMEM_LIMIT),
        name=f"hyena_filter_{L}",
    )(zpos, w1p, b1, fq, w2, b2, w3, win, cos, nsin)


HYENA_FT = 256

HYENA_CONV_ROWS = 256


HYENA_CH = 256
HYENA_OUT_ROWS = 512


def _hyena_kernel(L, ft, x1_ref, x2_ref, v_ref, cw1_ref, cw2_ref, cwv_ref, k_ref, bias_ref, cos_ref, nsin_ref,
                  out_ref, gate_s, zf_s, zb_s, yt_s, yb_s):
    o = pl.program_id(2)

    @pl.when(o == 0)
    def _():
        for src, cw_ref, dst in ((x1_ref, cw1_ref, 0), (x2_ref, cw2_ref, 1), (v_ref, cwv_ref, None)):
            cw = cw_ref[...]
            for r0 in range(0, L, HYENA_CONV_ROWS):
                r1 = r0 + HYENA_CONV_ROWS
                prev_row = src[r0 - 1:r0, :] if r0 > 0 else 0.0
                next_row = src[r1:r1 + 1, :] if r1 < L else 0.0
                uc = _conv3_rows(src[r0:r1, :], prev_row, next_row, cw)
                if dst is None:
                    zf_s[r0:r1, :] = uc
                    zb_s[r0:r1, :] = uc.astype(BF16)
                else:
                    gate_s[dst, r0:r1, :] = uc

    zb = zb_s[...]
    nyq = _alternating_sum(zf_s[...])
    for f in range(L // ft):
        rows = slice(f * ft, (f + 1) * ft)
        top = _dot(cos_ref[rows, :], zb)
        bot = _dot(nsin_ref[rows, :], zb)
        if f == 0:
            first = lax.broadcasted_iota(jnp.int32, top.shape, 0) == 0
            bot = jnp.where(first, nyq, bot)
        krt = k_ref[0, 0, rows, :]
        krb = k_ref[0, 1, rows, :]
        ki = k_ref[0, 2, rows, :]
        yt = top * krt - bot * ki
        yb = top * ki + bot * krb
        if f == 0:
            yt = jnp.where(first, 0.5 * yt, yt)
            y_nyq = yb[0:1, :]
        yt_s[rows, :] = yt.astype(BF16)
        yb_s[rows, :] = yb.astype(BF16)

    t = lax.broadcasted_iota(jnp.int32, (HYENA_OUT_ROWS, HYENA_CH), 0)
    alt_half = jnp.where(t % 2 == 0, 0.5, -0.5)
    for r0 in range(0, L, HYENA_OUT_ROWS):
        rows = slice(r0, r0 + HYENA_OUT_ROWS)
        acc = _dot(cos_ref[rows, :], yt_s[...]) + _dot(nsin_ref[rows, :], yb_s[...]) + alt_half * y_nyq
        znew = gate_s[o, rows, :] * (acc * (1.0 / L) + bias_ref[0] * zf_s[rows, :])
        zf_s[rows, :] = znew
        zb_s[rows, :] = znew.astype(BF16)
        out_ref[rows, :] = znew.astype(out_ref.dtype)


def _hyena_call(ph, conv_w, kspec, bias, cos, nsin, n_seq, L, row_blk0, ft):
    nch = W_B // HYENA_CH
    once = pl.Buffered(1)
    col = lambda part: (lambda s, c, o: (row_blk0 + s, part * nch + c))
    cwcol = lambda part: (lambda s, c, o: (0, part * nch + c))
    return pl.pallas_call(
        functools.partial(_hyena_kernel, L, ft),
        grid=(n_seq, nch, 2),
        in_specs=[pl.BlockSpec((L, HYENA_CH), col(part), pipeline_mode=once) for part in range(3)]
        + [pl.BlockSpec((3, HYENA_CH), cwcol(part)) for part in range(3)]
        + [
            pl.BlockSpec((1, 3, L, HYENA_CH), lambda s, c, o: (o, 0, 0, c)),
            pl.BlockSpec((1, 1, HYENA_CH), lambda s, c, o: (o, 0, c)),
            pl.BlockSpec((L, L), lambda s, c, o: (0, 0), pipeline_mode=once),
            pl.BlockSpec((L, L), lambda s, c, o: (0, 0), pipeline_mode=once),
        ],
        out_specs=pl.BlockSpec((L, HYENA_CH), lambda s, c, o: (s, c)),
        out_shape=jax.ShapeDtypeStruct((n_seq * L, W_B), BF16),
        scratch_shapes=[pltpu.VMEM((2, L, HYENA_CH), F32), pltpu.VMEM((L, HYENA_CH), F32),
                        pltpu.VMEM((L, HYENA_CH), BF16), pltpu.VMEM((L, HYENA_CH), BF16),
                        pltpu.VMEM((L, HYENA_CH), BF16)],
        compiler_params=_cparams(("parallel", "parallel", "arbitrary"), VMEM_LIMIT),
        name=f"hyena_conv_{L}",
    )(ph, ph, ph, conv_w, conv_w, conv_w, kspec, bias, cos, nsin)


HYENA_SHORT_SB = 4


def _hyena_short_kernel(L, ph_ref, cw_ref, k_ref, bias_ref, cos_ref, nsin_f_ref, cos_i_ref, nsin_i_ref, out_ref):
    cos, nsin_f, cos_i, nsin_i = cos_ref[...], nsin_f_ref[...], cos_i_ref[...], nsin_i_ref[...]
    cw = cw_ref[...]
    rows = [slice(s * L, (s + 1) * L) for s in range(HYENA_SHORT_SB)]
    ucs = [_conv3_rows(ph_ref[r, :], 0.0, 0.0, cw) for r in rows]
    zs = [uc[:, 2 * W_B:3 * W_B] for uc in ucs]
    for o in range(2):
        krt, krb, ki = k_ref[o, 0], k_ref[o, 1], k_ref[o, 2]
        zbs = [z.astype(BF16) for z in zs]
        tops = [_dot(cos, zb) for zb in zbs]
        bots = [_dot(nsin_f, zb) for zb in zbs]
        yts = [top * krt - bot * ki for top, bot in zip(tops, bots)]
        ybs = [top * ki + bot * krb for top, bot in zip(tops, bots)]
        accs = [_dot(cos_i, yt.astype(BF16)) + _dot(nsin_i, yb.astype(BF16)) for yt, yb in zip(yts, ybs)]
        zs = [uc[:, o * W_B:(o + 1) * W_B] * (acc * (1.0 / L) + bias_ref[o] * z)
              for uc, acc, z in zip(ucs, accs, zs)]
    for r, z in zip(rows, zs):
        out_ref[r, :] = z.astype(out_ref.dtype)


def _hyena_short_call(ph, conv_w, kspec, bias, cos, nsin, n_seq, L):
    rows = HYENA_SHORT_SB * L
    alt = jnp.where(jnp.arange(L) % 2 == 0, 1.0, -1.0).astype(BF16)
    nsin_f = nsin.at[0, :].set(alt)
    cos_i = cos.at[:, 0].set(0.5)
    nsin_i = nsin.at[:, 0].set(0.5 * alt)
    return pl.pallas_call(
        functools.partial(_hyena_short_kernel, L),
        grid=(n_seq // HYENA_SHORT_SB,),
        in_specs=[
            pl.BlockSpec((rows, 3 * W_B), lambda i: (i, 0)),
            pl.BlockSpec((3, 3 * W_B), lambda i: (0, 0)),
            pl.BlockSpec((2, 3, L, W_B), lambda i: (0, 0, 0, 0)),
            pl.BlockSpec((2, 1, W_B), lambda i: (0, 0, 0)),
        ] + [pl.BlockSpec((L, L), lambda i: (0, 0)) for _ in range(4)],
        out_specs=pl.BlockSpec((rows, W_B), lambda i: (i, 0)),
        out_shape=jax.ShapeDtypeStruct((n_seq * L, W_B), BF16),
        compiler_params=_cparams(("parallel",), VMEM_LIMIT),
        name=f"hyena_conv_{L}",
    )(ph, conv_w, kspec, bias, cos, nsin_f, cos_i, nsin_i)


FNET_RT = 512


def _fnet_kernel(L, x_ref, cos_ref, nsin_ref, bdc_ref, bds_ref, out_ref, xc_s, xs_s):
    for r0 in range(0, L, FNET_RT):
        rows = slice(r0, r0 + FNET_RT)
        xb = x_ref[rows, :].astype(BF16)
        xc_s[rows, :] = _dot(xb, bdc_ref[...]).astype(BF16)
        xs_s[rows, :] = _dot(xb, bds_ref[...]).astype(BF16)
    for r0 in range(0, L, FNET_RT):
        rows = slice(r0, r0 + FNET_RT)
        y = _dot(cos_ref[rows, :], xc_s[...]) + _dot(nsin_ref[rows, :], xs_s[...])
        out_ref[rows, :] = (y * (1.0 / math.sqrt(DC * L))).astype(out_ref.dtype)


def _fnet_call(pf, cos, nsin, bdc, bds, n_seq, L, row_blk0):
    const = lambda s: (0, 0)
    return pl.pallas_call(
        functools.partial(_fnet_kernel, L),
        grid=(n_seq,),
        in_specs=[
            pl.BlockSpec((L, W_C), lambda s: (row_blk0 + s, 0)),
            _resident((L, L), const),
            _resident((L, L), const),
            pl.BlockSpec((W_C, W_C), const),
            pl.BlockSpec((W_C, W_C), const),
        ],
        out_specs=pl.BlockSpec((L, W_C), lambda s: (s, 0)),
        out_shape=jax.ShapeDtypeStruct((n_seq * L, W_C), BF16),
        scratch_shapes=[pltpu.VMEM((L, W_C), BF16), pltpu.VMEM((L, W_C), BF16)],
        compiler_params=_cparams(("parallel",), VMEM_LIMIT),
        name=f"fnet_{L}",
    )(pf, cos, nsin, bdc, bds)


FNET_SHORT_SB = 4


def _fnet_short_kernel(L, x_ref, cos_ref, nsin_ref, bdc_ref, bds_ref, out_ref):
    xb = x_ref[...].astype(BF16)
    xc = _dot(xb, bdc_ref[...]).astype(BF16)
    xs = _dot(xb, bds_ref[...]).astype(BF16)
    for s in range(FNET_SHORT_SB):
        rows = slice(s * L, (s + 1) * L)
        y = _dot(cos_ref[...], xc[rows]) + _dot(nsin_ref[...], xs[rows])
        out_ref[rows, :] = (y * (1.0 / math.sqrt(DC * L))).astype(out_ref.dtype)


def _fnet_short_call(pf, cos, nsin, bdc, bds, n_seq, L):
    rows = FNET_SHORT_SB * L
    const = lambda i: (0, 0)
    return pl.pallas_call(
        functools.partial(_fnet_short_kernel, L),
        grid=(n_seq // FNET_SHORT_SB,),
        in_specs=[pl.BlockSpec((rows, W_C), lambda i: (i, 0)),
                  pl.BlockSpec((L, L), const), pl.BlockSpec((L, L), const),
                  pl.BlockSpec((W_C, W_C), const), pl.BlockSpec((W_C, W_C), const)],
        out_specs=pl.BlockSpec((rows, W_C), lambda i: (i, 0)),
        out_shape=jax.ShapeDtypeStruct((n_seq * L, W_C), BF16),
        compiler_params=_cparams(("parallel",), VMEM_LIMIT),
        name=f"fnet_{L}",
    )(pf, cos, nsin, bdc, bds)


DELTA_RB = 256
SOLVE_BATCH = 8
CHUNKS_PER_RB = DELTA_RB // CHUNK


HEADS_PER_LANE_TILE = LANE // DK


def _block_diag(y, half_masks):
    yb = y.astype(BF16)
    zero = jnp.zeros((CHUNK, LANE), BF16)
    row_blocks = []
    for h in range(HEAD_GROUP):
        tile = h // HEADS_PER_LANE_TILE
        piece = yb[:, tile * LANE:(tile + 1) * LANE] * half_masks[h % HEADS_PER_LANE_TILE]
        row_blocks.append(jnp.concatenate(
            [piece if t == tile else zero for t in range(GROUP_W // LANE)], axis=1))
    return jnp.concatenate(row_blocks, axis=0)


def _stacked_const_rhs(arrs, c, n):
    m = arrs[0].shape[0]
    parts = [p for a in arrs for p in _split(a, n)]
    y = _dot(jnp.concatenate(parts, axis=0), c)
    outs = []
    for i in range(len(arrs)):
        acc = y[i * n * m:(i * n + 1) * m]
        for t in range(1, n):
            acc = acc + y[(i * n + t) * m:(i * n + t + 1) * m]
        outs.append(acc)
    return outs


def _head_sums(arrs, ones_group, n_split=2):
    groups = [_stacked_const_rhs([a[:, g * GROUP_W:(g + 1) * GROUP_W] for a in arrs], ones_group, n_split)
              for g in range(N_GROUPS)]
    return [jnp.concatenate([groups[g][i] for g in range(N_GROUPS)], axis=1) for i in range(len(arrs))]


def _expand_heads(x, lane0):
    cols = [jnp.broadcast_to(x[:, lane0 + h:lane0 + h + 1], (x.shape[0], DK)) for h in range(H_A)]
    return jnp.concatenate(cols, axis=1)


def _const_lhs_split(c, b, n):
    w = b.shape[1]
    y = _dot(c, jnp.concatenate(_split(b, n), axis=1))
    acc = y[:, 0:w]
    for t in range(1, n):
        acc = acc + y[:, t * w:(t + 1) * w]
    return acc


def _delta_chunk_stages(pd_ref, prev_ref, next_ref, r, has_prev, has_next, dirs, cw, a_neg, dtb, ones_bd,
                        tri, half_masks, eye_tile):
    ch = {}
    rows = slice(r * CHUNK, (r + 1) * CHUNK)

    def conv():
        cur = pd_ref[rows, 0:QKV_W]
        if r == 0:
            prev_row = jnp.where(has_prev, prev_ref[7:8, :], 0.0)
        else:
            prev_row = pd_ref[r * CHUNK - 1:r * CHUNK, 0:QKV_W]
        if r == CHUNKS_PER_RB - 1:
            next_row = jnp.where(has_next, next_ref[0:1, :], 0.0)
        else:
            next_row = pd_ref[(r + 1) * CHUNK:(r + 1) * CHUNK + 1, 0:QKV_W]
        qkv = _silu(_conv3_rows(cur, prev_row, next_row, cw))
        ch["q"] = qkv[:, 0:H_A * DK]
        ch["k"] = qkv[:, H_A * DK:2 * H_A * DK]
        ch["v"] = qkv[:, 2 * H_A * DK:]

    def norms():
        q, k = ch.pop("q"), ch.pop("k")
        qss, kss = _head_sums([q * q, k * k], ones_bd, 1)
        ch["qn"] = q * lax.rsqrt(qss + EPS) * (DK ** -0.5)
        ch["kn"] = k * lax.rsqrt(kss + EPS)

    def gram():
        ch["gram"], ch["knT"] = [], []
        for g in range(N_GROUPS):
            lanes = slice(g * GROUP_W, (g + 1) * GROUP_W)
            lhs = jnp.concatenate([ch["kn"][:, lanes].astype(BF16), ch["qn"][:, lanes].astype(BF16), eye_tile],
                                  axis=0)
            res = _dot_nt(lhs, _block_diag(ch["kn"][:, lanes], half_masks))
            ch["gram"].append(res[0:2 * CHUNK])
            ch["knT"].append(res[2 * CHUNK:3 * CHUNK])

    def decay():
        ba = pd_ref[rows, OFF_B:OFF_B + LANE]
        sig = _sigmoid(ba)
        glog = a_neg * _softplus(ba + dtb)
        ch["decay"] = {}
        for d in dirs:
            gcum = _const_lhs_split(tri[d], glog, 3)
            beta = _expand_heads(sig, d * H_A)
            gcc8 = _expand_heads(gcum, (2 + d) * H_A)
            ch["decay"][d] = (beta, gcc8)

    return ch, [conv, norms, gram, decay]


def _delta_kernel(nb, zero_init, n_prev, *refs):
    shared = nb == 1
    it = iter(refs)
    blocks = [(next(it), next(it), next(it))]
    if not shared:
        blocks.append((next(it), next(it), next(it)))
    cw_ref, par_ref, ones_ref = next(it), next(it), next(it)
    s0_ref = None if zero_init else next(it)
    sprev_ref = next(it) if n_prev else None
    o_refs = (next(it), next(it))
    sfin_ref = next(it)
    u_s, w_s, p_s, qg_s, kg_s, gl_s, st_s = (next(it) for _ in range(7))
    j = pl.program_id(1)

    ri = lax.broadcasted_iota(jnp.int32, (CHUNK, GROUP_W), 0)
    cj = lax.broadcasted_iota(jnp.int32, (CHUNK, GROUP_W), 1) % CHUNK
    ixj = ri ^ cj
    eye = ixj == 0
    eye_tile = jnp.where(eye, 1.0, 0.0).astype(BF16)
    hl =lax.broadcasted_iota(jnp.int32, (CHUNK, LANE), 1) // DK
    half_masks = tuple(jnp.where(hl == h, 1.0, 0.0).astype(BF16) for h in range(HEADS_PER_LANE_TILE))
    ti = lax.broadcasted_iota(jnp.int32, (CHUNK, CHUNK), 0)
    tm = lax.broadcasted_iota(jnp.int32, (CHUNK, CHUNK), 1)
    ri8 = lax.broadcasted_iota(jnp.int32, (CHUNK, H_A * DK), 0)
    cj8 = lax.broadcasted_iota(jnp.int32, (CHUNK, H_A * DK), 1) % CHUNK
    incl = (ri >= cj, ri <= cj)
    strict = (ri > cj, ri < cj)
    tri = tuple(jnp.where(m, 1.0, 0.0).astype(BF16) for m in (tm <= ti, tm >= ti))
    eye8 = ri8 == cj8
    last_row = (CHUNK - 1, 0)

    @pl.when(j == 0)
    def _():
        for d in range(2):
            for g in range(N_GROUPS):
                if zero_init:
                    st_s[d, g] = jnp.zeros((DK, GROUP_W), F32)
                else:
                    r0 = g * GROUP_W
                    st_s[d, g] = jnp.concatenate(
                        [s0_ref[0, d, r0 + hh * DK:r0 + (hh + 1) * DK, :] for hh in range(HEAD_GROUP)], axis=1)

    cw = cw_ref[...]
    a_neg = -jnp.exp(par_ref[0:1, :])
    dtb = par_ref[1:2, :]
    pos = (j, nb - 1 - j)
    scan_order = (tuple(range(CHUNKS_PER_RB)), tuple(reversed(range(CHUNKS_PER_RB))))

    def unit_thunk(d, r, ch, g, units):
        def run():
            rows = slice(r * CHUNK, (r + 1) * CHUNK)
            lanes = slice(g * GROUP_W, (g + 1) * GROUP_W)
            beta, gcc8 = ch["decay"][d]
            gcr = jnp.sum(jnp.where(eye8, gcc8, 0.0), axis=0, keepdims=True)[:, lanes]
            qn, kn, be, gcc = ch["qn"][:, lanes], ch["kn"][:, lanes], beta[:, lanes], gcc8[:, lanes]
            kq = ch["gram"][g]
            dec = jnp.exp(jnp.where(incl[d], gcc - gcr, -1e30))
            a = jnp.where(strict[d], kq[0:CHUNK] * be * dec, 0.0)
            eg = jnp.exp(gcc)
            gcl = gcc[last_row[d]:last_row[d] + 1, :]
            units.append((d, rows, lanes, a, ch["v"][:, lanes] * be, kn * be * eg))
            p_s[d, rows, lanes] = (kq[CHUNK:2 * CHUNK] * dec).astype(BF16)
            qg_s[d, rows, lanes] = (qn * eg).astype(BF16)
            kg_s[d, rows, lanes] = (ch["knT"][g] * jnp.exp(gcl - gcr)).astype(BF16)
            gl_s[d, r * 8:(r + 1) * 8, lanes] = jnp.broadcast_to(jnp.exp(gcl), (8, GROUP_W))
        return run

    def prep_thunks(units):
        thunks = []
        if shared:
            todo = [(0, r, (0, 1)) for r in range(CHUNKS_PER_RB)]
        else:
            todo = [(d, r, (d,)) for d in range(2) for r in scan_order[d]]
        for b, r, dirs in todo:
            ch, stages = _delta_chunk_stages(*blocks[b], r, pos[b] > 0, pos[b] < nb - 1, dirs, cw, a_neg, dtb,
                                             ones_ref[...], tri, half_masks, eye_tile)
            thunks += stages
            thunks += [unit_thunk(d, r, ch, g, units) for d in dirs for g in range(N_GROUPS)]
        return thunks

    def scan_thunk(d, c):
        def run():
            r = scan_order[d][c]
            rows = slice(r * CHUNK, (r + 1) * CHUNK)
            for g in range(N_GROUPS):
                lanes = slice(g * GROUP_W, (g + 1) * GROUP_W)
                s = st_s[d, g]
                wq = jnp.concatenate([w_s[d, rows, lanes], qg_s[d, rows, lanes]], axis=0)
                ws_qs = _dot(wq, _block_diag(s, half_masks))
                v_new = u_s[d, rows, lanes] - ws_qs[0:CHUNK]
                pk = jnp.concatenate([p_s[d, rows, lanes], kg_s[d, rows, lanes]], axis=0)
                po = _dot(pk, _block_diag(v_new, half_masks))
                st_s[d, g] = s * gl_s[d, r * 8:r * 8 + 1, lanes] + po[CHUNK:2 * CHUNK]
                o_refs[d][rows, lanes] = ws_qs[CHUNK:2 * CHUNK] + po[0:CHUNK]
        return run

    def solve(units):
        r0s = [jnp.where(ixj == 1, un[3], 0.0) for un in units]
        zxs = [jnp.concatenate([un[3] - _dot(un[3].astype(BF16), _block_diag(r0, half_masks)),
                                jnp.where(eye, 1.0, 0.0) - r0], axis=0) for un, r0 in zip(units, r0s)]
        for lvl in range(1, 5):
            links = [_block_diag(jnp.where((ixj >> lvl) == 1, zx[0:CHUNK], 0.0), half_masks) for zx in zxs]
            zxs = [zx - _dot(zx.astype(BF16), lk) for zx, lk in zip(zxs, links)]
        links = [_block_diag(jnp.where((ixj >> 5) == 1, zx[0:CHUNK], 0.0), half_masks) for zx in zxs]
        xs = [zx[CHUNK:2 * CHUNK] - _dot(zx[CHUNK:2 * CHUNK].astype(BF16), lk) for zx, lk in zip(zxs, links)]
        for x, (d, rows, lanes, _, vb, kbe) in zip(xs, units):
            rhs = jnp.concatenate([_block_diag(vb, half_masks), _block_diag(kbe, half_masks)], axis=1)
            uw = _dot(x.astype(BF16), rhs)
            u_s[d, rows, lanes] = uw[:, 0:GROUP_W]
            w_s[d, rows, lanes] = uw[:, GROUP_W:2 * GROUP_W].astype(BF16)

    units = []
    for f in prep_thunks(units):
        f()
    for i in range(0, len(units), SOLVE_BATCH):
        solve(units[i:i + SOLVE_BATCH])
    for c in range(CHUNKS_PER_RB):
        for d in range(2):
            scan_thunk(d, c)()

    @pl.when(j == nb - 1)
    def _():
        for d in range(2):
            for g in range(N_GROUPS):
                s = st_s[d, g]
                for hh in range(HEAD_GROUP):
                    r0 = g * GROUP_W + hh * DK
                    sfin_ref[0, n_prev, d, r0:r0 + DK, :] = s[:, hh * DV:(hh + 1) * DV]
        for p in range(n_prev):
            sfin_ref[0, p] = sprev_ref[0, p]


def _delta_call(pd, conv_w, par, ones_bd, s0, n_seq, nb, blk0, prev_states=None):
    zero_init = s0 is None
    n_prev = 0 if prev_states is None else prev_states.shape[1]
    n_blocks_all = T_ALL // DELTA_RB
    blk_of = (lambda s, j: blk0 + s * nb + j, lambda s, j: blk0 + s * nb + nb - 1 - j)
    in_specs, args = [], []
    for d in range(1 if nb == 1 else 2):
        prev, nxt = _halo_specs(QKV_W, DELTA_RB, n_blocks_all, blk_of[d])
        in_specs += [pl.BlockSpec((DELTA_RB, PD_W), lambda s, j, d=d: (blk_of[d](s, j), 0)), prev, nxt]
        args += [pd, pd, pd]
    in_specs += [
        pl.BlockSpec((3, QKV_W), lambda s, j: (0, 0)),
        pl.BlockSpec((8, LANE), lambda s, j: (0, 0)),
        pl.BlockSpec((GROUP_W, GROUP_W), lambda s, j: (0, 0)),
    ]
    args += [conv_w, par, ones_bd]
    if not zero_init:
        in_specs.append(pl.BlockSpec((1, 2, H_A * DK, DV), lambda s, j: (s, 0, 0, 0)))
        args.append(s0)
    if n_prev:
        in_specs.append(pl.BlockSpec((1, n_prev, 2, H_A * DK, DV), lambda s, j: (s, 0, 0, 0, 0)))
        args.append(prev_states)
    rows = n_seq * nb * DELTA_RB
    dir_buf = lambda n, dt=F32: pltpu.VMEM((2, n, H_A * DK), dt)
    return pl.pallas_call(
        functools.partial(_delta_kernel, nb, zero_init, n_prev),
        grid=(n_seq, nb),
        in_specs=in_specs,
        out_specs=[pl.BlockSpec((DELTA_RB, W_A), lambda s, j: (s * nb + j, 0)),
                   pl.BlockSpec((DELTA_RB, W_A), lambda s, j: (s * nb + nb - 1 - j, 0)),
                   pl.BlockSpec((1, n_prev + 1, 2, H_A * DK, DV), lambda s, j: (s, 0, 0, 0, 0))],
        out_shape=[jax.ShapeDtypeStruct((rows, W_A), F32),
                   jax.ShapeDtypeStruct((rows, W_A), F32),
                   jax.ShapeDtypeStruct((n_seq, n_prev + 1, 2, H_A * DK, DV), F32)],
        scratch_shapes=[dir_buf(DELTA_RB)] + [dir_buf(DELTA_RB, BF16) for _ in range(4)]
        + [dir_buf(CHUNKS_PER_RB * 8), pltpu.VMEM((2, N_GROUPS, DK, GROUP_W), F32)],
        compiler_params=_cparams(("parallel", "arbitrary"), VMEM_LIMIT),
        name=f"deltanet_nb{nb}",
    )(*args)


FFN_TN = D_FF // 2


def _postmix_kernel(final_norm, tile0, ofc_ref, ofl_ref, obc_ref, obl_ref, z_ref, ybc_ref, ybl_ref,
                    ycc_ref, ycl_ref, pg_ref, x_ref, mod_ref, na_ref, ones_ref, wpa_ref, wpb_ref, wpc_ref,
                    wo_ref, g2_ref, wgu_ref, wdn_ref, nf_ref, out_ref):
    is_ctx = pl.program_id(0) + tile0 < N_CTX_TILES
    m = mod_ref[0]
    o = jnp.where(is_ctx, ofc_ref[...] + obc_ref[...], ofl_ref[...] + obl_ref[...])
    yb = jnp.where(is_ctx, ybc_ref[...], ybl_ref[...])
    yc = jnp.where(is_ctx, ycc_ref[...], ycl_ref[...])
    ms = _head_sums([o * o], ones_ref[...])[0] * (1.0 / DV)
    ya = (o * lax.rsqrt(ms + EPS) * na_ref[...]) * _silu(z_ref[...])
    merged = (pg_ref[:, 0:D_MODEL].astype(F32) * _dot(ya.astype(BF16), wpa_ref[0])
              + pg_ref[:, D_MODEL:2 * D_MODEL].astype(F32) * _dot(yb.astype(BF16), wpb_ref[0])
              + pg_ref[:, 2 * D_MODEL:3 * D_MODEL].astype(F32) * _dot(yc.astype(BF16), wpc_ref[0]))
    x = x_ref[...] + m[2:3] * _dot(merged.astype(BF16), wo_ref[0])

    h = _rms_mod(x, g2_ref[...], m[4:5], m[3:4]).astype(BF16)
    acc = None
    for c in range(0, D_FF, FFN_TN):
        gate = _dot(h, wgu_ref[0, :, c:c + FFN_TN])
        up = _dot(h, wgu_ref[0, :, D_FF + c:D_FF + c + FFN_TN])
        part = _dot((_silu(gate) * up).astype(BF16), wdn_ref[0, c:c + FFN_TN, :])
        acc = part if acc is None else acc + part
    xn = x + m[5:6] * acc
    if final_norm:
        ms = jnp.mean(xn * xn, axis=-1, keepdims=True)
        xn = xn * lax.rsqrt(ms + EPS) * nf_ref[...]
    out_ref[...] = xn


def _postmix_call(o_ctx, o_lat, pd, yb, yc, pg, x, mod3, na512, ones_bd, wpa, wpb, wpc, wo,
                  g2, wgu, wdn, nf, layer, final_norm, tile0=0, n_tiles=T_ALL // ROW_TILE):
    row = lambda i: (i + tile0, 0)
    const = lambda i: (0, 0)
    lyr = lambda i: (layer, 0, 0)
    assert W_A == W_B == W_C
    return pl.pallas_call(
        functools.partial(_postmix_kernel, final_norm, tile0),
        grid=(n_tiles,),
        in_specs=[
            *_ctx_lat_specs(W_A, tile0), *_ctx_lat_specs(W_A, tile0),
            pl.BlockSpec((ROW_TILE, W_A), lambda i: (i + tile0, OFF_Z // W_A)),
            *_ctx_lat_specs(W_B, tile0), *_ctx_lat_specs(W_C, tile0),
            pl.BlockSpec((ROW_TILE, 3 * D_MODEL), row),
            pl.BlockSpec((ROW_TILE, D_MODEL), row),
            pl.BlockSpec((1, 6, D_MODEL), lambda i: (_mod_row_block(i + tile0), 0, 0)),
            pl.BlockSpec((1, W_A), const),
            _resident((GROUP_W, GROUP_W), const),
            _resident((1, W_A, D_MODEL), lyr),
            _resident((1, W_B, D_MODEL), lyr),
            _resident((1, W_C, D_MODEL), lyr),
            _resident((1, D_MODEL, D_MODEL), lyr),
            pl.BlockSpec((1, D_MODEL), const),
            _resident((1, D_MODEL, 2 * D_FF), lyr),
            _resident((1, D_FF, D_MODEL), lyr),
            pl.BlockSpec((1, D_MODEL), const),
        ],
        out_specs=pl.BlockSpec((ROW_TILE, D_MODEL), lambda i: (i, 0)),
        out_shape=jax.ShapeDtypeStruct((n_tiles * ROW_TILE, D_MODEL), F32),
        compiler_params=_cparams(("parallel",), VMEM_LIMIT),
        name="postmix_final" if final_norm else "postmix",
    )(o_ctx[0], o_lat[0], o_ctx[1], o_lat[1], pd, yb[0], yb[1], yc[0], yc[1], pg, x, mod3, na512,
      ones_bd, wpa, wpb, wpc, wo, g2.reshape(1, D_MODEL), wgu, wdn, nf.reshape(1, D_MODEL))


TABLE_SPLIT = 64


def _grid_pos_embed(n_tokens):
    rows = n_tokens // GRID_W
    quarter = D_MODEL // 4
    omega = 1.0 / (10000.0 ** (jnp.arange(quarter, dtype=F32) / quarter))

    def emb(pos):
        a = pos[:, None] * omega[None, :]
        return jnp.concatenate([jnp.sin(a), jnp.cos(a)], axis=-1)

    e_row, e_col = lax.optimization_barrier((emb(jnp.arange(rows).astype(F32)),
                                             emb(jnp.arange(GRID_W).astype(F32))))
    return jnp.concatenate([jnp.repeat(e_row, GRID_W, axis=0), jnp.tile(e_col, (rows, 1))], axis=-1)


def _cos_nsin_tables(n, period):
    t = jnp.arange(n, dtype=jnp.int32)[None, :]

    def cs(r):
        ang = ((r * t) % period).astype(F32) * (2.0 * math.pi / period)
        return jnp.cos(ang), jnp.sin(ang)

    ca, sa = cs(jnp.arange(n // TABLE_SPLIT, dtype=jnp.int32)[:, None] * TABLE_SPLIT)
    cb, sb = cs(jnp.arange(TABLE_SPLIT, dtype=jnp.int32)[:, None])
    ca, sa, cb, sb = lax.optimization_barrier((ca, sa, cb, sb))
    ca, sa = ca[:, None, :], sa[:, None, :]
    cos = (ca * cb[None] - sa * sb[None]).reshape(n, n)
    nsin = (-(sa * cb[None] + ca * sb[None])).reshape(n, n)
    return cos.astype(BF16), nsin.astype(BF16)


def _hyena_positions(L):
    bands = (HY_EMB - 1) // 2
    t = jnp.linspace(0.0, 1.0, L, dtype=F32)[:, None]
    wpos = (2.0 * math.pi / L) * jnp.arange(L, dtype=F32)[:, None]
    fr = jnp.linspace(1e-4, bands - 1, bands, dtype=F32)[None, :]
    zpos = jnp.concatenate([t, jnp.cos(fr * wpos), -jnp.sin(fr * wpos)], axis=-1)
    zpos = jnp.pad(zpos, ((0, 0), (0, LANE - HY_EMB)))
    deltas = jnp.abs(jnp.linspace(math.log(HY_DECAY_TARGET) / HY_SLOW_PCT,
                                  math.log(HY_DECAY_TARGET) / HY_FAST_PCT, W_B, dtype=F32))
    window = jnp.exp(-t * deltas[None, :])
    return zpos, window


def _group_tables():
    r = jnp.arange(DC, dtype=jnp.int32)
    ang = ((r[:, None] * r[None, :]) % DC).astype(F32) * (2.0 * math.pi / DC)
    eye = jnp.eye(G_C, dtype=F32)
    return jnp.kron(eye, jnp.cos(ang)).astype(BF16), jnp.kron(eye, jnp.sin(ang)).astype(BF16)


def _head_ones():
    return jnp.kron(jnp.eye(HEAD_GROUP, dtype=F32), jnp.ones((DK, DK), F32)).astype(BF16)


def kernel(x_prompt, x_sample, state_delta, c, c_ctx, w_mod, b_mod, norm1_g, norm2_g, w_in, conv_qkv, a_log, dt_bias, norm_a, conv_hy, hy_w1, hy_b1, hy_freq, hy_w2, hy_b2, hy_w3, hy_bias, w_pa, w_pb, w_pc, w_o, w_gu, w_down, norm_f):
    assert x_prompt.shape == (N_CTX_SEQ, L_CTX, D_MODEL) and x_sample.shape == (N_LAT_SEQ, L_LAT, D_MODEL)
    st = jnp.pad(jnp.concatenate([c_ctx[None], c], axis=0).T, ((0, 0), (0, 8 - 1 - N_LAT_SEQ)))
    mod = _mod_call(st, w_mod, b_mod).reshape(DEPTH, 8, 6, D_MODEL)

    ones_bd = _head_ones()
    bdc, bds = _group_tables()
    seqs = ((L_CTX, N_CTX_SEQ, 0), (L_LAT, N_LAT_SEQ, T_CTX // L_LAT))
    tables = {L: (_cos_nsin_tables(L, 2 * L), _cos_nsin_tables(L, L), _hyena_positions(L))
              for L, _, _ in seqs}

    w_in_b = jnp.swapaxes(w_in, 1, 2).astype(BF16)
    w_pa_b, w_pb_b, w_pc_b, w_o_b, w_gu_b, w_down_b = (
        w.astype(BF16) for w in (w_pa, w_pb, w_pc, w_o, w_gu, w_down))

    x = None
    ctx_states = None
    for l in range(DEPTH):
        mod3 = mod[l, 0:3]
        if l == 0:
            xs = (x_prompt.reshape(T_CTX, D_MODEL), x_sample.reshape(T_LAT, D_MODEL), _grid_pos_embed(L_LAT))
            pd, ph, pf, pg, x = _inproj_call(xs, mod3, norm1_g[l], w_in_b, l)
        else:
            pd, ph, pf, pg = _inproj_call((x,), mod3, norm1_g[l], w_in_b, l)

        par = jnp.zeros((8, LANE), F32)
        par = par.at[0, 2 * H_A:4 * H_A].set(a_log[l].reshape(-1))
        par = par.at[1, 2 * H_A:4 * H_A].set(dt_bias[l].reshape(-1))
        *o_ctx, ctx_states = _delta_call(pd, conv_qkv[l], par, ones_bd, None,
                                         N_CTX_SEQ, L_CTX // DELTA_RB, 0, ctx_states)
        s0 = state_delta[:, l].astype(F32).reshape(N_LAT_SEQ, 2, H_A * DK, DV)
        *o_lat, _ = _delta_call(pd, conv_qkv[l], par, ones_bd, s0,
                                N_LAT_SEQ, L_LAT // DELTA_RB, T_CTX // DELTA_RB)

        w1p = jnp.pad(hy_w1[l], ((0, LANE - HY_EMB), (0, 0)))
        yb, yc = [], []
        for L, n_seq, blk0 in seqs:
            (cos2, nsin2), (cos1, nsin1), (zpos, window) = tables[L]
            kspec = _filter_call(L, zpos, w1p, hy_b1[l][None], hy_freq[l][None], hy_w2[l],
                                 hy_b2[l][None], hy_w3[l], window, cos2, nsin2)
            bias = hy_bias[l][:, None, :]
            if L == HYENA_FT:
                assert blk0 == 0
                yb.append(_hyena_short_call(ph, conv_hy[l], kspec, bias, cos2, nsin2, n_seq, L))
                yc.append(_fnet_short_call(pf, cos1, nsin1, bdc, bds, n_seq, L))
            else:
                yb.append(_hyena_call(ph, conv_hy[l], kspec, bias, cos2, nsin2, n_seq, L, blk0, HYENA_FT))
                yc.append(_fnet_call(pf, cos1, nsin1, bdc, bds, n_seq, L, blk0))

        na512 = jnp.tile(norm_a[l], H_A)[None]
        post = functools.partial(_postmix_call, o_ctx, o_lat, pd, yb, yc, pg, x, mod3, na512, ones_bd,
                                 w_pa_b, w_pb_b, w_pc_b, w_o_b, norm2_g[l], w_gu_b, w_down_b, norm_f, l)
        if l < DEPTH - 1:
            x = post(False)
        else:
            y_prompt = post(True, 0, N_CTX_TILES).reshape(N_CTX_SEQ, L_CTX, D_MODEL)
            y_sample = post(True, N_CTX_TILES, N_LAT_TILES).reshape(N_LAT_SEQ, L_LAT, D_MODEL)

    new_state = ctx_states.reshape(N_CTX_SEQ, DEPTH, 2, H_A, DK, DV).astype(x_prompt.dtype)
    return (y_prompt, y_sample, new_state)
```

```python
import functools
import math

import jax
import jax.numpy as jnp
from jax import lax
from jax.experimental import pallas as pl
from jax.experimental.pallas import tpu as pltpu

F32 = jnp.float32
BF16 = jnp.bfloat16

D_MODEL = 1024
N_CTX_SEQ = 32
L_CTX = 256
DEPTH = 2
N_LAT_SEQ = 2
L_LAT = 2048
GRID_W = 64
EPS = 1e-6
H_A = 8
DK = 64
DV = 64
W_A = H_A * DV
QKV_W = 2 * H_A * DK + H_A * DV
CHUNK = 64
W_B = 512
HY_EMB = 33
HY_HID = 64
HY_DECAY_TARGET = 1e-2
HY_FAST_PCT = 0.3
HY_SLOW_PCT = 1.5
G_C = 8
DC = 64
W_C = G_C * DC
D_FF = ((8 * D_MODEL + 3 * 256 - 1) // (3 * 256)) * 256
OFF_Z = QKV_W
OFF_B = OFF_Z + W_A
OFF_A = OFF_B + 2 * H_A
OFF_HY = OFF_A + 2 * H_A
OFF_FN = OFF_HY + 3 * W_B
OFF_GATE = OFF_FN + W_C

T_CTX = N_CTX_SEQ * L_CTX
T_LAT = N_LAT_SEQ * L_LAT
T_ALL = T_CTX + T_LAT
ROW_TILE = 256
N_CTX_TILES = T_CTX // ROW_TILE
N_LAT_TILES = T_LAT // ROW_TILE
LANE = 128
PD_W = QKV_W + W_A + LANE
HEAD_GROUP = 4
GROUP_W = HEAD_GROUP * DK
N_GROUPS = H_A // HEAD_GROUP
VMEM_LIMIT = 56 * 1024 * 1024


def _cparams(sem, vmem=None):
    return pltpu.CompilerParams(dimension_semantics=sem, vmem_limit_bytes=vmem)


def _dot(a, b):
    return jnp.dot(a, b, preferred_element_type=F32)


def _dot_nt(a, b):
    return lax.dot_general(a, b, (((1,), (1,)), ((), ())), preferred_element_type=F32)


def _split(a, n):
    parts = []
    rem = a
    for i in range(n):
        p = rem.astype(BF16)
        parts.append(p)
        if i + 1 < n:
            rem = rem - p.astype(F32)
    return parts


def _mm3(a, b):
    ah, al = _split(a, 2)
    bh, bl = _split(b, 2)
    return _dot(ah, bh) + (_dot(ah, bl) + _dot(al, bh))


def _sigmoid(x):
    return 1.0 / (1.0 + jnp.exp(-x))


def _silu(x):
    return x * _sigmoid(x)


def _softplus(x):
    return jnp.maximum(x, 0.0) + jnp.log(1.0 + jnp.exp(-jnp.abs(x)))


def _mod_row_block(i):
    per_lat = L_LAT // ROW_TILE
    return jnp.where(i < N_CTX_TILES, 0, 1 + (i - N_CTX_TILES) // per_lat)


def _ctx_tile(i):
    return jnp.minimum(i, N_CTX_TILES - 1)


def _lat_tile(i):
    return jnp.maximum(i - N_CTX_TILES, 0)


def _ctx_lat_specs(width, tile0=0):
    return (pl.BlockSpec((ROW_TILE, width), lambda i: (_ctx_tile(i + tile0), 0)),
            pl.BlockSpec((ROW_TILE, width), lambda i: (_lat_tile(i + tile0), 0)))


MOD_TN = 1536


def _mod_kernel(st_ref, w_ref, b_ref, out_ref):
    s = _silu(st_ref[...])
    w = w_ref[0]
    rows = [jnp.sum(s[:, r:r + 1] * w, axis=0, keepdims=True) + b_ref[0] for r in range(3)]
    rows.append(jnp.zeros((5, MOD_TN), F32))
    out_ref[0] = jnp.concatenate(rows, axis=0)


def _mod_call(st, w_mod, b_mod):
    n6 = 6 * D_MODEL
    return pl.pallas_call(
        _mod_kernel,
        grid=(DEPTH, n6 // MOD_TN),
        in_specs=[
            pl.BlockSpec((D_MODEL, 8), lambda l, j: (0, 0)),
            pl.BlockSpec((1, D_MODEL, MOD_TN), lambda l, j: (l, 0, j)),
            pl.BlockSpec((1, 1, MOD_TN), lambda l, j: (l, 0, j)),
        ],
        out_specs=pl.BlockSpec((1, 8, MOD_TN), lambda l, j: (l, 0, j)),
        out_shape=jax.ShapeDtypeStruct((DEPTH, 8, n6), F32),
        compiler_params=_cparams(("parallel", "parallel")),
        name="adaln_mod",
    )(st, w_mod, b_mod.reshape(DEPTH, 1, n6))


INPROJ_TN = 512
INPROJ_WIDTHS = (PD_W, 3 * W_B, W_C, 3 * D_MODEL)
INPROJ_FEATURES = (OFF_HY, 3 * W_B, W_C, 3 * D_MODEL)


def _rms_mod(x, g, scale, shift):
    ms = jnp.mean(x * x, axis=-1, keepdims=True)
    return (x * lax.rsqrt(ms + EPS) * g) * (1.0 + scale) + shift


def _inproj_kernel(first, *refs):
    if first:
        (xc_ref, xl_ref, pos_ref, mod_ref, g_ref, w_ref, pd_ref, ph_ref, pf_ref, pg_ref, x_ref) = refs
        x = jnp.where(pl.program_id(0) < N_CTX_TILES, xc_ref[...], xl_ref[...] + pos_ref[...])
        x_ref[...] = x
    else:
        (xin_ref, mod_ref, g_ref, w_ref, pd_ref, ph_ref, pf_ref, pg_ref) = refs
        x = xin_ref[...]
    m = mod_ref[0]
    h = _rms_mod(x, g_ref[...], m[1:2], m[0:1]).astype(BF16)
    row0 = 0
    for o_ref, n_feat in zip((pd_ref, ph_ref, pf_ref, pg_ref), INPROJ_FEATURES):
        n = o_ref.shape[1]
        for c in range(0, n, INPROJ_TN):
            e = min(c + INPROJ_TN, n)
            ef = min(e, n_feat)
            y = _dot_nt(h, w_ref[0, row0 + c:row0 + ef, :])
            if ef < e:
                y = jnp.concatenate([y, jnp.zeros((y.shape[0], e - ef), F32)], axis=1)
            if o_ref is pg_ref:
                y = _sigmoid(y)
            o_ref[:, c:e] = y.astype(o_ref.dtype)
        row0 += n_feat


def _resident(shape, index_map):
    return pl.BlockSpec(shape, index_map, pipeline_mode=pl.Buffered(1))


def _inproj_call(xs, mod3, g, w_all, layer):
    first = len(xs) == 3
    widths = INPROJ_WIDTHS
    row = lambda i: (i, 0)
    const = lambda i: (0, 0)
    if first:
        per_lat = L_LAT // ROW_TILE
        x_specs = list(_ctx_lat_specs(D_MODEL)) + [
            pl.BlockSpec((ROW_TILE, D_MODEL), lambda i: (_lat_tile(i) % per_lat, 0))]
    else:
        x_specs = [pl.BlockSpec((ROW_TILE, D_MODEL), row)]
    out_widths = widths + ((D_MODEL,) if first else ())
    out_dtypes = (F32, F32, F32, BF16) + ((F32,) if first else ())
    return pl.pallas_call(
        functools.partial(_inproj_kernel, first),
        grid=(T_ALL // ROW_TILE,),
        in_specs=x_specs + [
            pl.BlockSpec((1, 6, D_MODEL), lambda i: (_mod_row_block(i), 0, 0)),
            pl.BlockSpec((1, D_MODEL), const),
            _resident((1, sum(INPROJ_FEATURES), D_MODEL), lambda i: (layer, 0, 0)),
        ],
        out_specs=[pl.BlockSpec((ROW_TILE, w), row) for w in out_widths],
        out_shape=[jax.ShapeDtypeStruct((T_ALL, w), dt) for w, dt in zip(out_widths, out_dtypes)],
        compiler_params=_cparams(("parallel",), VMEM_LIMIT),
        name="inproj_first" if first else "inproj",
    )(*xs, mod3, g.reshape(1, D_MODEL), w_all)


def _conv3_rows(cur, prev_row, next_row, w):
    n = cur.shape[0]
    sub = lax.broadcasted_iota(jnp.int32, (8, cur.shape[1]), 0)
    up = pltpu.roll(cur, 1, 0)
    up = jnp.concatenate([jnp.where(sub == 0, prev_row, up[0:8]), up[8:]], axis=0)
    dn = pltpu.roll(cur, n - 1, 0)
    dn = jnp.concatenate([dn[0:n - 8], jnp.where(sub == 7, next_row, dn[n - 8:])], axis=0)
    return up * w[0:1] + cur * w[1:2] + dn * w[2:3]


def _halo_specs(width, rows_per_block, n_row_blocks, blk_of):
    per = rows_per_block // 8
    last = n_row_blocks * per - 1
    prev = pl.BlockSpec((8, width), lambda *a: (jnp.maximum(blk_of(*a) * per - 1, 0), 0))
    nxt = pl.BlockSpec((8, width), lambda *a: (jnp.minimum((blk_of(*a) + 1) * per, last), 0))
    return prev, nxt


FILT_RT = 256


def _alternating_sum(x):
    t = lax.broadcasted_iota(jnp.int32, x.shape, 0)
    return jnp.sum(jnp.where(t % 2 == 0, x, -x), axis=0, keepdims=True)


def _filter_kernel(L, zpos_ref, w1_ref, b1_ref, fq_ref, w2_ref, b2_ref, w3_ref, win_ref,
                   cos_ref, nsin_ref, k_ref, hs_s, hm_s, krl_s, h2_s):
    o = pl.program_id(0)
    rt = pl.program_id(1)

    @pl.when((o == 0) & (rt == 0))
    def _():
        fq = fq_ref[...]
        for r0 in range(0, L, FILT_RT):
            rows = slice(r0, r0 + FILT_RT)
            h = jnp.sin(fq * (_mm3(zpos_ref[rows, :], w1_ref[...]) + b1_ref[...]))
            h2_s[rows, :] = jnp.sin(fq * (_mm3(h, w2_ref[...]) + b2_ref[...]))

    @pl.when(rt == 0)
    def _():
        alt_acc = jnp.zeros((1, W_B), F32)
        for r0 in range(0, L, FILT_RT):
            rows = slice(r0, r0 + FILT_RT)
            hf = _mm3(h2_s[rows, :], w3_ref[...])
            win = win_ref[rows, :]
            fw = hf[:, 0:W_B] * win
            bw = hf[:, W_B:2 * W_B] * win
            hsum = fw + bw
            hs_s[rows, :] = hsum.astype(BF16)
            hm_s[rows, :] = (fw - bw).astype(BF16)
            alt_acc = alt_acc + _alternating_sum(hsum)
        krl_s[...] = jnp.broadcast_to(alt_acc, krl_s.shape)

    p1 = _dot(cos_ref[...], hs_s[...])
    p2 = _dot(nsin_ref[...], hm_s[...])
    first = (rt * FILT_RT + lax.broadcasted_iota(jnp.int32, p1.shape, 0)) == 0
    k_ref[0, 0] = p1
    k_ref[0, 1] = jnp.where(first, krl_s[0:1, :], p1)
    k_ref[0, 2] = jnp.where(first, 0.0, p2)


def _filter_call(L, zpos, w1p, b1, fq, w2, b2, w3, win, cos, nsin):
    nrt = L // FILT_RT
    c2 = lambda o, r: (0, 0)
    return pl.pallas_call(
        functools.partial(_filter_kernel, L),
        grid=(2, nrt),
        in_specs=[
            pl.BlockSpec((L, LANE), c2),
            pl.BlockSpec((LANE, HY_HID), c2),
            pl.BlockSpec((1, HY_HID), c2),
            pl.BlockSpec((1, HY_HID), c2),
            pl.BlockSpec((HY_HID, HY_HID), c2),
            pl.BlockSpec((1, HY_HID), c2),
            pl.BlockSpec((HY_HID, 2 * W_B), lambda o, r: (0, o)),
            pl.BlockSpec((L, W_B), c2),
            pl.BlockSpec((FILT_RT, L), lambda o, r: (r, 0)),
            pl.BlockSpec((FILT_RT, L), lambda o, r: (r, 0)),
        ],
        out_specs=pl.BlockSpec((1, 3, FILT_RT, W_B), lambda o, r: (o, 0, r, 0)),
        out_shape=jax.ShapeDtypeStruct((2, 3, L, W_B), F32),
        scratch_shapes=[pltpu.VMEM((L, W_B), BF16), pltpu.VMEM((L, W_B), BF16),
                        pltpu.VMEM((8, W_B), F32), pltpu.VMEM((L, HY_HID), F32)],
        compiler_params=_cparams(("arbitrary", "arbitrary"), VMEM_LIMIT),
        name=f"hyena_filter_{L}",
    )(zpos, w1p, b1, fq, w2, b2, w3, win, cos, nsin)


HYENA_FT = 256

HYENA_CONV_ROWS = 256


HYENA_CH = 256
HYENA_OUT_ROWS = 512


def _hyena_kernel(L, ft, x1_ref, x2_ref, v_ref, cw1_ref, cw2_ref, cwv_ref, k_ref, bias_ref, cos_ref, nsin_ref,
                  out_ref, gate_s, zf_s, zb_s, yt_s, yb_s):
    o = pl.program_id(2)

    @pl.when(o == 0)
    def _():
        for src, cw_ref, dst in ((x1_ref, cw1_ref, 0), (x2_ref, cw2_ref, 1), (v_ref, cwv_ref, None)):
            cw = cw_ref[...]
            for r0 in range(0, L, HYENA_CONV_ROWS):
                r1 = r0 + HYENA_CONV_ROWS
                prev_row = src[r0 - 1:r0, :] if r0 > 0 else 0.0
                next_row = src[r1:r1 + 1, :] if r1 < L else 0.0
                uc = _conv3_rows(src[r0:r1, :], prev_row, next_row, cw)
                if dst is None:
                    zf_s[r0:r1, :] = uc
                    zb_s[r0:r1, :] = uc.astype(BF16)
                else:
                    gate_s[dst, r0:r1, :] = uc

    zb = zb_s[...]
    nyq = _alternating_sum(zf_s[...])
    for f in range(L // ft):
        rows = slice(f * ft, (f + 1) * ft)
        top = _dot(cos_ref[rows, :], zb)
        bot = _dot(nsin_ref[rows, :], zb)
        if f == 0:
            first = lax.broadcasted_iota(jnp.int32, top.shape, 0) == 0
            bot = jnp.where(first, nyq, bot)
        krt = k_ref[0, 0, rows, :]
        krb = k_ref[0, 1, rows, :]
        ki = k_ref[0, 2, rows, :]
        yt = top * krt - bot * ki
        yb = top * ki + bot * krb
        if f == 0:
            yt = jnp.where(first, 0.5 * yt, yt)
            y_nyq = yb[0:1, :]
        yt_s[rows, :] = yt.astype(BF16)
        yb_s[rows, :] = yb.astype(BF16)

    t = lax.broadcasted_iota(jnp.int32, (HYENA_OUT_ROWS, HYENA_CH), 0)
    alt_half = jnp.where(t % 2 == 0, 0.5, -0.5)
    for r0 in range(0, L, HYENA_OUT_ROWS):
        rows = slice(r0, r0 + HYENA_OUT_ROWS)
        acc = _dot(cos_ref[rows, :], yt_s[...]) + _dot(nsin_ref[rows, :], yb_s[...]) + alt_half * y_nyq
        znew = gate_s[o, rows, :] * (acc * (1.0 / L) + bias_ref[0] * zf_s[rows, :])
        zf_s[rows, :] = znew
        zb_s[rows, :] = znew.astype(BF16)
        out_ref[rows, :] = znew.astype(out_ref.dtype)


def _hyena_call(ph, conv_w, kspec, bias, cos, nsin, n_seq, L, row_blk0, ft):
    nch = W_B // HYENA_CH
    once = pl.Buffered(1)
    col = lambda part: (lambda s, c, o: (row_blk0 + s, part * nch + c))
    cwcol = lambda part: (lambda s, c, o: (0, part * nch + c))
    return pl.pallas_call(
        functools.partial(_hyena_kernel, L, ft),
        grid=(n_seq, nch, 2),
        in_specs=[pl.BlockSpec((L, HYENA_CH), col(part), pipeline_mode=once) for part in range(3)]
        + [pl.BlockSpec((3, HYENA_CH), cwcol(part)) for part in range(3)]
        + [
            pl.BlockSpec((1, 3, L, HYENA_CH), lambda s, c, o: (o, 0, 0, c)),
            pl.BlockSpec((1, 1, HYENA_CH), lambda s, c, o: (o, 0, c)),
            pl.BlockSpec((L, L), lambda s, c, o: (0, 0), pipeline_mode=once),
            pl.BlockSpec((L, L), lambda s, c, o: (0, 0), pipeline_mode=once),
        ],
        out_specs=pl.BlockSpec((L, HYENA_CH), lambda s, c, o: (s, c)),
        out_shape=jax.ShapeDtypeStruct((n_seq * L, W_B), BF16),
        scratch_shapes=[pltpu.VMEM((2, L, HYENA_CH), F32), pltpu.VMEM((L, HYENA_CH), F32),
                        pltpu.VMEM((L, HYENA_CH), BF16), pltpu.VMEM((L, HYENA_CH), BF16),
                        pltpu.VMEM((L, HYENA_CH), BF16)],
        compiler_params=_cparams(("parallel", "parallel", "arbitrary"), VMEM_LIMIT),
        name=f"hyena_conv_{L}",
    )(ph, ph, ph, conv_w, conv_w, conv_w, kspec, bias, cos, nsin)


HYENA_SHORT_SB = 4


def _hyena_short_kernel(L, ph_ref, cw_ref, k_ref, bias_ref, cos_ref, nsin_f_ref, cos_i_ref, nsin_i_ref, out_ref):
    cos, nsin_f, cos_i, nsin_i = cos_ref[...], nsin_f_ref[...], cos_i_ref[...], nsin_i_ref[...]
    cw = cw_ref[...]
    rows = [slice(s * L, (s + 1) * L) for s in range(HYENA_SHORT_SB)]
    ucs = [_conv3_rows(ph_ref[r, :], 0.0, 0.0, cw) for r in rows]
    zs = [uc[:, 2 * W_B:3 * W_B] for uc in ucs]
    for o in range(2):
        krt, krb, ki = k_ref[o, 0], k_ref[o, 1], k_ref[o, 2]
        zbs = [z.astype(BF16) for z in zs]
        tops = [_dot(cos, zb) for zb in zbs]
        bots = [_dot(nsin_f, zb) for zb in zbs]
        yts = [top * krt - bot * ki for top, bot in zip(tops, bots)]
        ybs = [top * ki + bot * krb for top, bot in zip(tops, bots)]
        accs = [_dot(cos_i, yt.astype(BF16)) + _dot(nsin_i, yb.astype(BF16)) for yt, yb in zip(yts, ybs)]
        zs = [uc[:, o * W_B:(o + 1) * W_B] * (acc * (1.0 / L) + bias_ref[o] * z)
              for uc, acc, z in zip(ucs, accs, zs)]
    for r, z in zip(rows, zs):
        out_ref[r, :] = z.astype(out_ref.dtype)


def _hyena_short_call(ph, conv_w, kspec, bias, cos, nsin, n_seq, L):
    rows = HYENA_SHORT_SB * L
    alt = jnp.where(jnp.arange(L) % 2 == 0, 1.0, -1.0).astype(BF16)
    nsin_f = nsin.at[0, :].set(alt)
    cos_i = cos.at[:, 0].set(0.5)
    nsin_i = nsin.at[:, 0].set(0.5 * alt)
    return pl.pallas_call(
        functools.partial(_hyena_short_kernel, L),
        grid=(n_seq // HYENA_SHORT_SB,),
        in_specs=[
            pl.BlockSpec((rows, 3 * W_B), lambda i: (i, 0)),
            pl.BlockSpec((3, 3 * W_B), lambda i: (0, 0)),
            pl.BlockSpec((2, 3, L, W_B), lambda i: (0, 0, 0, 0)),
            pl.BlockSpec((2, 1, W_B), lambda i: (0, 0, 0)),
        ] + [pl.BlockSpec((L, L), lambda i: (0, 0)) for _ in range(4)],
        out_specs=pl.BlockSpec((rows, W_B), lambda i: (i, 0)),
        out_shape=jax.ShapeDtypeStruct((n_seq * L, W_B), BF16),
        compiler_params=_cparams(("parallel",), VMEM_LIMIT),
        name=f"hyena_conv_{L}",
    )(ph, conv_w, kspec, bias, cos, nsin_f, cos_i, nsin_i)


FNET_RT = 512


def _fnet_kernel(L, x_ref, cos_ref, nsin_ref, bdc_ref, bds_ref, out_ref, xc_s, xs_s):
    for r0 in range(0, L, FNET_RT):
        rows = slice(r0, r0 + FNET_RT)
        xb = x_ref[rows, :].astype(BF16)
        xc_s[rows, :] = _dot(xb, bdc_ref[...]).astype(BF16)
        xs_s[rows, :] = _dot(xb, bds_ref[...]).astype(BF16)
    for r0 in range(0, L, FNET_RT):
        rows = slice(r0, r0 + FNET_RT)
        y = _dot(cos_ref[rows, :], xc_s[...]) + _dot(nsin_ref[rows, :], xs_s[...])
        out_ref[rows, :] = (y * (1.0 / math.sqrt(DC * L))).astype(out_ref.dtype)


def _fnet_call(pf, cos, nsin, bdc, bds, n_seq, L, row_blk0):
    const = lambda s: (0, 0)
    return pl.pallas_call(
        functools.partial(_fnet_kernel, L),
        grid=(n_seq,),
        in_specs=[
            pl.BlockSpec((L, W_C), lambda s: (row_blk0 + s, 0)),
            _resident((L, L), const),
            _resident((L, L), const),
            pl.BlockSpec((W_C, W_C), const),
            pl.BlockSpec((W_C, W_C), const),
        ],
        out_specs=pl.BlockSpec((L, W_C), lambda s: (s, 0)),
        out_shape=jax.ShapeDtypeStruct((n_seq * L, W_C), BF16),
        scratch_shapes=[pltpu.VMEM((L, W_C), BF16), pltpu.VMEM((L, W_C), BF16)],
        compiler_params=_cparams(("parallel",), VMEM_LIMIT),
        name=f"fnet_{L}",
    )(pf, cos, nsin, bdc, bds)


FNET_SHORT_SB = 4


def _fnet_short_kernel(L, x_ref, cos_ref, nsin_ref, bdc_ref, bds_ref, out_ref):
    xb = x_ref[...].astype(BF16)
    xc = _dot(xb, bdc_ref[...]).astype(BF16)
    xs = _dot(xb, bds_ref[...]).astype(BF16)
    for s in range(FNET_SHORT_SB):
        rows = slice(s * L, (s + 1) * L)
        y = _dot(cos_ref[...], xc[rows]) + _dot(nsin_ref[...], xs[rows])
        out_ref[rows, :] = (y * (1.0 / math.sqrt(DC * L))).astype(out_ref.dtype)


def _fnet_short_call(pf, cos, nsin, bdc, bds, n_seq, L):
    rows = FNET_SHORT_SB * L
    const = lambda i: (0, 0)
    return pl.pallas_call(
        functools.partial(_fnet_short_kernel, L),
        grid=(n_seq // FNET_SHORT_SB,),
        in_specs=[pl.BlockSpec((rows, W_C), lambda i: (i, 0)),
                  pl.BlockSpec((L, L), const), pl.BlockSpec((L, L), const),
                  pl.BlockSpec((W_C, W_C), const), pl.BlockSpec((W_C, W_C), const)],
        out_specs=pl.BlockSpec((rows, W_C), lambda i: (i, 0)),
        out_shape=jax.ShapeDtypeStruct((n_seq * L, W_C), BF16),
        compiler_params=_cparams(("parallel",), VMEM_LIMIT),
        name=f"fnet_{L}",
    )(pf, cos, nsin, bdc, bds)


DELTA_RB = 256
SOLVE_BATCH = 8
CHUNKS_PER_RB = DELTA_RB // CHUNK


HEADS_PER_LANE_TILE = LANE // DK


def _block_diag(y, half_masks):
    yb = y.astype(BF16)
    zero = jnp.zeros((CHUNK, LANE), BF16)
    row_blocks = []
    for h in range(HEAD_GROUP):
        tile = h // HEADS_PER_LANE_TILE
        piece = yb[:, tile * LANE:(tile + 1) * LANE] * half_masks[h % HEADS_PER_LANE_TILE]
        row_blocks.append(jnp.concatenate(
            [piece if t == tile else zero for t in range(GROUP_W // LANE)], axis=1))
    return jnp.concatenate(row_blocks, axis=0)


def _stacked_const_rhs(arrs, c, n):
    m = arrs[0].shape[0]
    parts = [p for a in arrs for p in _split(a, n)]
    y = _dot(jnp.concatenate(parts, axis=0), c)
    outs = []
    for i in range(len(arrs)):
        acc = y[i * n * m:(i * n + 1) * m]
        for t in range(1, n):
            acc = acc + y[(i * n + t) * m:(i * n + t + 1) * m]
        outs.append(acc)
    return outs


def _head_sums(arrs, ones_group, n_split=2):
    groups = [_stacked_const_rhs([a[:, g * GROUP_W:(g + 1) * GROUP_W] for a in arrs], ones_group, n_split)
              for g in range(N_GROUPS)]
    return [jnp.concatenate([groups[g][i] for g in range(N_GROUPS)], axis=1) for i in range(len(arrs))]


def _expand_heads(x, lane0):
    cols = [jnp.broadcast_to(x[:, lane0 + h:lane0 + h + 1], (x.shape[0], DK)) for h in range(H_A)]
    return jnp.concatenate(cols, axis=1)


def _const_lhs_split(c, b, n):
    w = b.shape[1]
    y = _dot(c, jnp.concatenate(_split(b, n), axis=1))
    acc = y[:, 0:w]
    for t in range(1, n):
        acc = acc + y[:, t * w:(t + 1) * w]
    return acc


def _delta_chunk_stages(pd_ref, prev_ref, next_ref, r, has_prev, has_next, dirs, cw, a_neg, dtb, ones_bd,
                        tri, half_masks, eye_tile):
    ch = {}
    rows = slice(r * CHUNK, (r + 1) * CHUNK)

    def conv():
        cur = pd_ref[rows, 0:QKV_W]
        if r == 0:
            prev_row = jnp.where(has_prev, prev_ref[7:8, :], 0.0)
        else:
            prev_row = pd_ref[r * CHUNK - 1:r * CHUNK, 0:QKV_W]
        if r == CHUNKS_PER_RB - 1:
            next_row = jnp.where(has_next, next_ref[0:1, :], 0.0)
        else:
            next_row = pd_ref[(r + 1) * CHUNK:(r + 1) * CHUNK + 1, 0:QKV_W]
        qkv = _silu(_conv3_rows(cur, prev_row, next_row, cw))
        ch["q"] = qkv[:, 0:H_A * DK]
        ch["k"] = qkv[:, H_A * DK:2 * H_A * DK]
        ch["v"] = qkv[:, 2 * H_A * DK:]

    def norms():
        q, k = ch.pop("q"), ch.pop("k")
        qss, kss = _head_sums([q * q, k * k], ones_bd, 1)
        ch["qn"] = q * lax.rsqrt(qss + EPS) * (DK ** -0.5)
        ch["kn"] = k * lax.rsqrt(kss + EPS)

    def gram():
        ch["gram"], ch["knT"] = [], []
        for g in range(N_GROUPS):
            lanes = slice(g * GROUP_W, (g + 1) * GROUP_W)
            lhs = jnp.concatenate([ch["kn"][:, lanes].astype(BF16), ch["qn"][:, lanes].astype(BF16), eye_tile],
                                  axis=0)
            res = _dot_nt(lhs, _block_diag(ch["kn"][:, lanes], half_masks))
            ch["gram"].append(res[0:2 * CHUNK])
            ch["knT"].append(res[2 * CHUNK:3 * CHUNK])

    def decay():
        ba = pd_ref[rows, OFF_B:OFF_B + LANE]
        sig = _sigmoid(ba)
        glog = a_neg * _softplus(ba + dtb)
        ch["decay"] = {}
        for d in dirs:
            gcum = _const_lhs_split(tri[d], glog, 3)
            beta = _expand_heads(sig, d * H_A)
            gcc8 = _expand_heads(gcum, (2 + d) * H_A)
            ch["decay"][d] = (beta, gcc8)

    return ch, [conv, norms, gram, decay]


def _delta_kernel(nb, zero_init, n_prev, *refs):
    shared = nb == 1
    it = iter(refs)
    blocks = [(next(it), next(it), next(it))]
    if not shared:
        blocks.append((next(it), next(it), next(it)))
    cw_ref, par_ref, ones_ref = next(it), next(it), next(it)
    s0_ref = None if zero_init else next(it)
    sprev_ref = next(it) if n_prev else None
    o_refs = (next(it), next(it))
    sfin_ref = next(it)
    u_s, w_s, p_s, qg_s, kg_s, gl_s, st_s = (next(it) for _ in range(7))
    j = pl.program_id(1)

    ri = lax.broadcasted_iota(jnp.int32, (CHUNK, GROUP_W), 0)
    cj = lax.broadcasted_iota(jnp.int32, (CHUNK, GROUP_W), 1) % CHUNK
    ixj = ri ^ cj
    eye = ixj == 0
    eye_tile = jnp.where(eye, 1.0, 0.0).astype(BF16)
    hl =lax.broadcasted_iota(jnp.int32, (CHUNK, LANE), 1) // DK
    half_masks = tuple(jnp.where(hl == h, 1.0, 0.0).astype(BF16) for h in range(HEADS_PER_LANE_TILE))
    ti = lax.broadcasted_iota(jnp.int32, (CHUNK, CHUNK), 0)
    tm = lax.broadcasted_iota(jnp.int32, (CHUNK, CHUNK), 1)
    ri8 = lax.broadcasted_iota(jnp.int32, (CHUNK, H_A * DK), 0)
    cj8 = lax.broadcasted_iota(jnp.int32, (CHUNK, H_A * DK), 1) % CHUNK
    incl = (ri >= cj, ri <= cj)
    strict = (ri > cj, ri < cj)
    tri = tuple(jnp.where(m, 1.0, 0.0).astype(BF16) for m in (tm <= ti, tm >= ti))
    eye8 = ri8 == cj8
    last_row = (CHUNK - 1, 0)

    @pl.when(j == 0)
    def _():
        for d in range(2):
            for g in range(N_GROUPS):
                if zero_init:
                    st_s[d, g] = jnp.zeros((DK, GROUP_W), F32)
                else:
                    r0 = g * GROUP_W
                    st_s[d, g] = jnp.concatenate(
                        [s0_ref[0, d, r0 + hh * DK:r0 + (hh + 1) * DK, :] for hh in range(HEAD_GROUP)], axis=1)

    cw = cw_ref[...]
    a_neg = -jnp.exp(par_ref[0:1, :])
    dtb = par_ref[1:2, :]
    pos = (j, nb - 1 - j)
    scan_order = (tuple(range(CHUNKS_PER_RB)), tuple(reversed(range(CHUNKS_PER_RB))))

    def unit_thunk(d, r, ch, g, units):
        def run():
            rows = slice(r * CHUNK, (r + 1) * CHUNK)
            lanes = slice(g * GROUP_W, (g + 1) * GROUP_W)
            beta, gcc8 = ch["decay"][d]
            gcr = jnp.sum(jnp.where(eye8, gcc8, 0.0), axis=0, keepdims=True)[:, lanes]
            qn, kn, be, gcc = ch["qn"][:, lanes], ch["kn"][:, lanes], beta[:, lanes], gcc8[:, lanes]
            kq = ch["gram"][g]
            dec = jnp.exp(jnp.where(incl[d], gcc - gcr, -1e30))
            a = jnp.where(strict[d], kq[0:CHUNK] * be * dec, 0.0)
            eg = jnp.exp(gcc)
            gcl = gcc[last_row[d]:last_row[d] + 1, :]
            units.append((d, rows, lanes, a, ch["v"][:, lanes] * be, kn * be * eg))
            p_s[d, rows, lanes] = (kq[CHUNK:2 * CHUNK] * dec).astype(BF16)
            qg_s[d, rows, lanes] = (qn * eg).astype(BF16)
            kg_s[d, rows, lanes] = (ch["knT"][g] * jnp.exp(gcl - gcr)).astype(BF16)
            gl_s[d, r * 8:(r + 1) * 8, lanes] = jnp.broadcast_to(jnp.exp(gcl), (8, GROUP_W))
        return run

    def prep_thunks(units):
        thunks = []
        if shared:
            todo = [(0, r, (0, 1)) for r in range(CHUNKS_PER_RB)]
        else:
            todo = [(d, r, (d,)) for d in range(2) for r in scan_order[d]]
        for b, r, dirs in todo:
            ch, stages = _delta_chunk_stages(*blocks[b], r, pos[b] > 0, pos[b] < nb - 1, dirs, cw, a_neg, dtb,
                                             ones_ref[...], tri, half_masks, eye_tile)
            thunks += stages
            thunks += [unit_thunk(d, r, ch, g, units) for d in dirs for g in range(N_GROUPS)]
        return thunks

    def scan_thunk(d, c):
        def run():
            r = scan_order[d][c]
            rows = slice(r * CHUNK, (r + 1) * CHUNK)
            for g in range(N_GROUPS):
                lanes = slice(g * GROUP_W, (g + 1) * GROUP_W)
                s = st_s[d, g]
                wq = jnp.concatenate([w_s[d, rows, lanes], qg_s[d, rows, lanes]], axis=0)
                ws_qs = _dot(wq, _block_diag(s, half_masks))
                v_new = u_s[d, rows, lanes] - ws_qs[0:CHUNK]
                pk = jnp.concatenate([p_s[d, rows, lanes], kg_s[d, rows, lanes]], axis=0)
                po = _dot(pk, _block_diag(v_new, half_masks))
                st_s[d, g] = s * gl_s[d, r * 8:r * 8 + 1, lanes] + po[CHUNK:2 * CHUNK]
                o_refs[d][rows, lanes] = ws_qs[CHUNK:2 * CHUNK] + po[0:CHUNK]
        return run

    def solve(units):
        r0s = [jnp.where(ixj == 1, un[3], 0.0) for un in units]
        zxs = [jnp.concatenate([un[3] - _dot(un[3].astype(BF16), _block_diag(r0, half_masks)),
                                jnp.where(eye, 1.0, 0.0) - r0], axis=0) for un, r0 in zip(units, r0s)]
        for lvl in range(1, 5):
            links = [_block_diag(jnp.where((ixj >> lvl) == 1, zx[0:CHUNK], 0.0), half_masks) for zx in zxs]
            zxs = [zx - _dot(zx.astype(BF16), lk) for zx, lk in zip(zxs, links)]
        links = [_block_diag(jnp.where((ixj >> 5) == 1, zx[0:CHUNK], 0.0), half_masks) for zx in zxs]
        xs = [zx[CHUNK:2 * CHUNK] - _dot(zx[CHUNK:2 * CHUNK].astype(BF16), lk) for zx, lk in zip(zxs, links)]
        for x, (d, rows, lanes, _, vb, kbe) in zip(xs, units):
            rhs = jnp.concatenate([_block_diag(vb, half_masks), _block_diag(kbe, half_masks)], axis=1)
            uw = _dot(x.astype(BF16), rhs)
            u_s[d, rows, lanes] = uw[:, 0:GROUP_W]
            w_s[d, rows, lanes] = uw[:, GROUP_W:2 * GROUP_W].astype(BF16)

    units = []
    for f in prep_thunks(units):
        f()
    for i in range(0, len(units), SOLVE_BATCH):
        solve(units[i:i + SOLVE_BATCH])
    for c in range(CHUNKS_PER_RB):
        for d in range(2):
            scan_thunk(d, c)()

    @pl.when(j == nb - 1)
    def _():
        for d in range(2):
            for g in range(N_GROUPS):
                s = st_s[d, g]
                for hh in range(HEAD_GROUP):
                    r0 = g * GROUP_W + hh * DK
                    sfin_ref[0, n_prev, d, r0:r0 + DK, :] = s[:, hh * DV:(hh + 1) * DV]
        for p in range(n_prev):
            sfin_ref[0, p] = sprev_ref[0, p]


def _delta_call(pd, conv_w, par, ones_bd, s0, n_seq, nb, blk0, prev_states=None):
    zero_init = s0 is None
    n_prev = 0 if prev_states is None else prev_states.shape[1]
    n_blocks_all = T_ALL // DELTA_RB
    blk_of = (lambda s, j: blk0 + s * nb + j, lambda s, j: blk0 + s * nb + nb - 1 - j)
    in_specs, args = [], []
    for d in range(1 if nb == 1 else 2):
        prev, nxt = _halo_specs(QKV_W, DELTA_RB, n_blocks_all, blk_of[d])
        in_specs += [pl.BlockSpec((DELTA_RB, PD_W), lambda s, j, d=d: (blk_of[d](s, j), 0)), prev, nxt]
        args += [pd, pd, pd]
    in_specs += [
        pl.BlockSpec((3, QKV_W), lambda s, j: (0, 0)),
        pl.BlockSpec((8, LANE), lambda s, j: (0, 0)),
        pl.BlockSpec((GROUP_W, GROUP_W), lambda s, j: (0, 0)),
    ]
    args += [conv_w, par, ones_bd]
    if not zero_init:
        in_specs.append(pl.BlockSpec((1, 2, H_A * DK, DV), lambda s, j: (s, 0, 0, 0)))
        args.append(s0)
    if n_prev:
        in_specs.append(pl.BlockSpec((1, n_prev, 2, H_A * DK, DV), lambda s, j: (s, 0, 0, 0, 0)))
        args.append(prev_states)
    rows = n_seq * nb * DELTA_RB
    dir_buf = lambda n, dt=F32: pltpu.VMEM((2, n, H_A * DK), dt)
    return pl.pallas_call(
        functools.partial(_delta_kernel, nb, zero_init, n_prev),
        grid=(n_seq, nb),
        in_specs=in_specs,
        out_specs=[pl.BlockSpec((DELTA_RB, W_A), lambda s, j: (s * nb + j, 0)),
                   pl.BlockSpec((DELTA_RB, W_A), lambda s, j: (s * nb + nb - 1 - j, 0)),
                   pl.BlockSpec((1, n_prev + 1, 2, H_A * DK, DV), lambda s, j: (s, 0, 0, 0, 0))],
        out_shape=[jax.ShapeDtypeStruct((rows, W_A), F32),
                   jax.ShapeDtypeStruct((rows, W_A), F32),
                   jax.ShapeDtypeStruct((n_seq, n_prev + 1, 2, H_A * DK, DV), F32)],
        scratch_shapes=[dir_buf(DELTA_RB)] + [dir_buf(DELTA_RB, BF16) for _ in range(4)]
        + [dir_buf(CHUNKS_PER_RB * 8), pltpu.VMEM((2, N_GROUPS, DK, GROUP_W), F32)],
        compiler_params=_cparams(("parallel", "arbitrary"), VMEM_LIMIT),
        name=f"deltanet_nb{nb}",
    )(*args)


FFN_TN = D_FF


def _postmix_kernel(final_norm, tile0, ofc_ref, ofl_ref, obc_ref, obl_ref, z_ref, ybc_ref, ybl_ref,
                    ycc_ref, ycl_ref, pg_ref, x_ref, mod_ref, na_ref, ones_ref, wpa_ref, wpb_ref, wpc_ref,
                    wo_ref, g2_ref, wgu_ref, wdn_ref, nf_ref, out_ref):
    is_ctx = pl.program_id(0) + tile0 < N_CTX_TILES
    m = mod_ref[0]
    o = jnp.where(is_ctx, ofc_ref[...] + obc_ref[...], ofl_ref[...] + obl_ref[...])
    yb = jnp.where(is_ctx, ybc_ref[...], ybl_ref[...])
    yc = jnp.where(is_ctx, ycc_ref[...], ycl_ref[...])
    ms = _head_sums([o * o], ones_ref[...])[0] * (1.0 / DV)
    ya = (o * lax.rsqrt(ms + EPS) * na_ref[...]) * _silu(z_ref[...])
    merged = (pg_ref[:, 0:D_MODEL].astype(F32) * _dot(ya.astype(BF16), wpa_ref[0])
              + pg_ref[:, D_MODEL:2 * D_MODEL].astype(F32) * _dot(yb.astype(BF16), wpb_ref[0])
              + pg_ref[:, 2 * D_MODEL:3 * D_MODEL].astype(F32) * _dot(yc.astype(BF16), wpc_ref[0]))
    x = x_ref[...] + m[2:3] * _dot(merged.astype(BF16), wo_ref[0])

    h = _rms_mod(x, g2_ref[...], m[4:5], m[3:4]).astype(BF16)
    acc = None
    for c in range(0, D_FF, FFN_TN):
        gate = _dot(h, wgu_ref[0, :, c:c + FFN_TN])
        up = _dot(h, wgu_ref[0, :, D_FF + c:D_FF + c + FFN_TN])
        part = _dot((_silu(gate) * up).astype(BF16), wdn_ref[0, c:c + FFN_TN, :])
        acc = part if acc is None else acc + part
    xn = x + m[5:6] * acc
    if final_norm:
        ms = jnp.mean(xn * xn, axis=-1, keepdims=True)
        xn = xn * lax.rsqrt(ms + EPS) * nf_ref[...]
    out_ref[...] = xn


def _postmix_call(o_ctx, o_lat, pd, yb, yc, pg, x, mod3, na512, ones_bd, wpa, wpb, wpc, wo,
                  g2, wgu, wdn, nf, layer, final_norm, tile0=0, n_tiles=T_ALL // ROW_TILE):
    row = lambda i: (i + tile0, 0)
    const = lambda i: (0, 0)
    lyr = lambda i: (layer, 0, 0)
    assert W_A == W_B == W_C
    return pl.pallas_call(
        functools.partial(_postmix_kernel, final_norm, tile0),
        grid=(n_tiles,),
        in_specs=[
            *_ctx_lat_specs(W_A, tile0), *_ctx_lat_specs(W_A, tile0),
            pl.BlockSpec((ROW_TILE, W_A), lambda i: (i + tile0, OFF_Z // W_A)),
            *_ctx_lat_specs(W_B, tile0), *_ctx_lat_specs(W_C, tile0),
            pl.BlockSpec((ROW_TILE, 3 * D_MODEL), row),
            pl.BlockSpec((ROW_TILE, D_MODEL), row),
            pl.BlockSpec((1, 6, D_MODEL), lambda i: (_mod_row_block(i + tile0), 0, 0)),
            pl.BlockSpec((1, W_A), const),
            _resident((GROUP_W, GROUP_W), const),
            _resident((1, W_A, D_MODEL), lyr),
            _resident((1, W_B, D_MODEL), lyr),
            _resident((1, W_C, D_MODEL), lyr),
            _resident((1, D_MODEL, D_MODEL), lyr),
            pl.BlockSpec((1, D_MODEL), const),
            _resident((1, D_MODEL, 2 * D_FF), lyr),
            _resident((1, D_FF, D_MODEL), lyr),
            pl.BlockSpec((1, D_MODEL), const),
        ],
        out_specs=pl.BlockSpec((ROW_TILE, D_MODEL), lambda i: (i, 0)),
        out_shape=jax.ShapeDtypeStruct((n_tiles * ROW_TILE, D_MODEL), F32),
        compiler_params=_cparams(("parallel",), VMEM_LIMIT),
        name="postmix_final" if final_norm else "postmix",
    )(o_ctx[0], o_lat[0], o_ctx[1], o_lat[1], pd, yb[0], yb[1], yc[0], yc[1], pg, x, mod3, na512,
      ones_bd, wpa, wpb, wpc, wo, g2.reshape(1, D_MODEL), wgu, wdn, nf.reshape(1, D_MODEL))


TABLE_SPLIT = 64


def _grid_pos_embed(n_tokens):
    rows = n_tokens // GRID_W
    quarter = D_MODEL // 4
    omega = 1.0 / (10000.0 ** (jnp.arange(quarter, dtype=F32) / quarter))

    def emb(pos):
        a = pos[:, None] * omega[None, :]
        return jnp.concatenate([jnp.sin(a), jnp.cos(a)], axis=-1)

    e_row, e_col = lax.optimization_barrier((emb(jnp.arange(rows).astype(F32)),
                                             emb(jnp.arange(GRID_W).astype(F32))))
    return jnp.concatenate([jnp.repeat(e_row, GRID_W, axis=0), jnp.tile(e_col, (rows, 1))], axis=-1)


def _cos_nsin_tables(n, period):
    t = jnp.arange(n, dtype=jnp.int32)[None, :]

    def cs(r):
        ang = ((r * t) % period).astype(F32) * (2.0 * math.pi / period)
        return jnp.cos(ang), jnp.sin(ang)

    ca, sa = cs(jnp.arange(n // TABLE_SPLIT, dtype=jnp.int32)[:, None] * TABLE_SPLIT)
    cb, sb = cs(jnp.arange(TABLE_SPLIT, dtype=jnp.int32)[:, None])
    ca, sa, cb, sb = lax.optimization_barrier((ca, sa, cb, sb))
    ca, sa = ca[:, None, :], sa[:, None, :]
    cos = (ca * cb[None] - sa * sb[None]).reshape(n, n)
    nsin = (-(sa * cb[None] + ca * sb[None])).reshape(n, n)
    return cos.astype(BF16), nsin.astype(BF16)


def _hyena_positions(L):
    bands = (HY_EMB - 1) // 2
    t = jnp.linspace(0.0, 1.0, L, dtype=F32)[:, None]
    wpos = (2.0 * math.pi / L) * jnp.arange(L, dtype=F32)[:, None]
    fr = jnp.linspace(1e-4, bands - 1, bands, dtype=F32)[None, :]
    zpos = jnp.concatenate([t, jnp.cos(fr * wpos), -jnp.sin(fr * wpos)], axis=-1)
    zpos = jnp.pad(zpos, ((0, 0), (0, LANE - HY_EMB)))
    deltas = jnp.abs(jnp.linspace(math.log(HY_DECAY_TARGET) / HY_SLOW_PCT,
                                  math.log(HY_DECAY_TARGET) / HY_FAST_PCT, W_B, dtype=F32))
    window = jnp.exp(-t * deltas[None, :])
    return zpos, window


def _group_tables():
    r = jnp.arange(DC, dtype=jnp.int32)
    ang = ((r[:, None] * r[None, :]) % DC).astype(F32) * (2.0 * math.pi / DC)
    eye = jnp.eye(G_C, dtype=F32)
    return jnp.kron(eye, jnp.cos(ang)).astype(BF16), jnp.kron(eye, jnp.sin(ang)).astype(BF16)


def _head_ones():
    return jnp.kron(jnp.eye(HEAD_GROUP, dtype=F32), jnp.ones((DK, DK), F32)).astype(BF16)


def kernel(x_prompt, x_sample, state_delta, c, c_ctx, w_mod, b_mod, norm1_g, norm2_g, w_in, conv_qkv, a_log, dt_bias, norm_a, conv_hy, hy_w1, hy_b1, hy_freq, hy_w2, hy_b2, hy_w3, hy_bias, w_pa, w_pb, w_pc, w_o, w_gu, w_down, norm_f):
    assert x_prompt.shape == (N_CTX_SEQ, L_CTX, D_MODEL) and x_sample.shape == (N_LAT_SEQ, L_LAT, D_MODEL)
    st = jnp.pad(jnp.concatenate([c_ctx[None], c], axis=0).T, ((0, 0), (0, 8 - 1 - N_LAT_SEQ)))
    mod = _mod_call(st, w_mod, b_mod).reshape(DEPTH, 8, 6, D_MODEL)

    ones_bd = _head_ones()
    bdc, bds = _group_tables()
    seqs = ((L_CTX, N_CTX_SEQ, 0), (L_LAT, N_LAT_SEQ, T_CTX // L_LAT))
    tables = {L: (_cos_nsin_tables(L, 2 * L), _cos_nsin_tables(L, L), _hyena_positions(L))
              for L, _, _ in seqs}

    w_in_b = jnp.swapaxes(w_in, 1, 2).astype(BF16)
    w_pa_b, w_pb_b, w_pc_b, w_o_b, w_gu_b, w_down_b = (
        w.astype(BF16) for w in (w_pa, w_pb, w_pc, w_o, w_gu, w_down))

    x = None
    ctx_states = None
    for l in range(DEPTH):
        mod3 = mod[l, 0:3]
        if l == 0:
            xs = (x_prompt.reshape(T_CTX, D_MODEL), x_sample.reshape(T_LAT, D_MODEL), _grid_pos_embed(L_LAT))
            pd, ph, pf, pg, x = _inproj_call(xs, mod3, norm1_g[l], w_in_b, l)
        else:
            pd, ph, pf, pg = _inproj_call((x,), mod3, norm1_g[l], w_in_b, l)

        par = jnp.zeros((8, LANE), F32)
        par = par.at[0, 2 * H_A:4 * H_A].set(a_log[l].reshape(-1))
        par = par.at[1, 2 * H_A:4 * H_A].set(dt_bias[l].reshape(-1))
        *o_ctx, ctx_states = _delta_call(pd, conv_qkv[l], par, ones_bd, None,
                                         N_CTX_SEQ, L_CTX // DELTA_RB, 0, ctx_states)
        s0 = state_delta[:, l].astype(F32).reshape(N_LAT_SEQ, 2, H_A * DK, DV)
        *o_lat, _ = _delta_call(pd, conv_qkv[l], par, ones_bd, s0,
                                N_LAT_SEQ, L_LAT // DELTA_RB, T_CTX // DELTA_RB)

        w1p = jnp.pad(hy_w1[l], ((0, LANE - HY_EMB), (0, 0)))
        yb, yc = [], []
        for L, n_seq, blk0 in seqs:
            (cos2, nsin2), (cos1, nsin1), (zpos, window) = tables[L]
            kspec = _filter_call(L, zpos, w1p, hy_b1[l][None], hy_freq[l][None], hy_w2[l],
                                 hy_b2[l][None], hy_w3[l], window, cos2, nsin2)
            bias = hy_bias[l][:, None, :]
            if L == HYENA_FT:
                assert blk0 == 0
                yb.append(_hyena_short_call(ph, conv_hy[l], kspec, bias, cos2, nsin2, n_seq, L))
                yc.append(_fnet_short_call(pf, cos1, nsin1, bdc, bds, n_seq, L))
            else:
                yb.append(_hyena_call(ph, conv_hy[l], kspec, bias, cos2, nsin2, n_seq, L, blk0, HYENA_FT))
                yc.append(_fnet_call(pf, cos1, nsin1, bdc, bds, n_seq, L, blk0))

        na512 = jnp.tile(norm_a[l], H_A)[None]
        post = functools.partial(_postmix_call, o_ctx, o_lat, pd, yb, yc, pg, x, mod3, na512, ones_bd,
                                 w_pa_b, w_pb_b, w_pc_b, w_o_b, norm2_g[l], w_gu_b, w_down_b, norm_f, l)
        if l < DEPTH - 1:
            x = post(False)
        else:
            y_prompt = post(True, 0, N_CTX_TILES).reshape(N_CTX_SEQ, L_CTX, D_MODEL)
            y_sample = post(True, N_CTX_TILES, N_LAT_TILES).reshape(N_LAT_SEQ, L_LAT, D_MODEL)

    new_state = ctx_states.reshape(N_CTX_SEQ, DEPTH, 2, H_A, DK, DV).astype(x_prompt.dtype)
    return (y_prompt, y_sample, new_state)
```

```python
import functools
import math

import jax
import jax.numpy as jnp
from jax import lax
from jax.experimental import pallas as pl
from jax.experimental.pallas import tpu as pltpu

F32 = jnp.float32
BF16 = jnp.bfloat16

D_MODEL = 1024
N_CTX_SEQ = 32
L_CTX = 256
DEPTH = 2
N_LAT_SEQ = 2
L_LAT = 2048
GRID_W = 64
EPS = 1e-6
H_A = 8
DK = 64
DV = 64
W_A = H_A * DV
QKV_W = 2 * H_A * DK + H_A * DV
CHUNK = 64
W_B = 512
HY_EMB = 33
HY_HID = 64
HY_DECAY_TARGET = 1e-2
HY_FAST_PCT = 0.3
HY_SLOW_PCT = 1.5
G_C = 8
DC = 64
W_C = G_C * DC
D_FF = ((8 * D_MODEL + 3 * 256 - 1) // (3 * 256)) * 256
OFF_Z = QKV_W
OFF_B = OFF_Z + W_A
OFF_A = OFF_B + 2 * H_A
OFF_HY = OFF_A + 2 * H_A
OFF_FN = OFF_HY + 3 * W_B
OFF_GATE = OFF_FN + W_C

T_CTX = N_CTX_SEQ * L_CTX
T_LAT = N_LAT_SEQ * L_LAT
T_ALL = T_CTX + T_LAT
ROW_TILE = 256
N_CTX_TILES = T_CTX // ROW_TILE
N_LAT_TILES = T_LAT // ROW_TILE
LANE = 128
PD_W = QKV_W + W_A + LANE
HEAD_GROUP = 4
GROUP_W = HEAD_GROUP * DK
N_GROUPS = H_A // HEAD_GROUP
VMEM_LIMIT = 56 * 1024 * 1024


def _cparams(sem, vmem=None):
    return pltpu.CompilerParams(dimension_semantics=sem, vmem_limit_bytes=vmem)


def _dot(a, b):
    return jnp.dot(a, b, preferred_element_type=F32)


def _dot_nt(a, b):
    return lax.dot_general(a, b, (((1,), (1,)), ((), ())), preferred_element_type=F32)


def _split(a, n):
    parts = []
    rem = a
    for i in range(n):
        p = rem.astype(BF16)
        parts.append(p)
        if i + 1 < n:
            rem = rem - p.astype(F32)
    return parts


def _mm3(a, b):
    ah, al = _split(a, 2)
    bh, bl = _split(b, 2)
    return _dot(ah, bh) + (_dot(ah, bl) + _dot(al, bh))


def _sigmoid(x):
    return 1.0 / (1.0 + jnp.exp(-x))


def _silu(x):
    hx = 0.5 * x
    return hx + hx * jnp.tanh(hx)


def _softplus(x):
    return jnp.maximum(x, 0.0) + jnp.log(1.0 + jnp.exp(-jnp.abs(x)))


def _mod_row_block(i):
    per_lat = L_LAT // ROW_TILE
    return jnp.where(i < N_CTX_TILES, 0, 1 + (i - N_CTX_TILES) // per_lat)


def _ctx_tile(i):
    return jnp.minimum(i, N_CTX_TILES - 1)


def _lat_tile(i):
    return jnp.maximum(i - N_CTX_TILES, 0)


def _ctx_lat_specs(width, tile0=0):
    return (pl.BlockSpec((ROW_TILE, width), lambda i: (_ctx_tile(i + tile0), 0)),
            pl.BlockSpec((ROW_TILE, width), lambda i: (_lat_tile(i + tile0), 0)))


MOD_TN = 1536


def _mod_kernel(st_ref, w_ref, b_ref, out_ref):
    s = _silu(st_ref[...])
    w = w_ref[0]
    rows = [jnp.sum(s[:, r:r + 1] * w, axis=0, keepdims=True) + b_ref[0] for r in range(3)]
    rows.append(jnp.zeros((5, MOD_TN), F32))
    out_ref[0] = jnp.concatenate(rows, axis=0)


def _mod_call(st, w_mod, b_mod):
    n6 = 6 * D_MODEL
    return pl.pallas_call(
        _mod_kernel,
        grid=(DEPTH, n6 // MOD_TN),
        in_specs=[
            pl.BlockSpec((D_MODEL, 8), lambda l, j: (0, 0)),
            pl.BlockSpec((1, D_MODEL, MOD_TN), lambda l, j: (l, 0, j)),
            pl.BlockSpec((1, 1, MOD_TN), lambda l, j: (l, 0, j)),
        ],
        out_specs=pl.BlockSpec((1, 8, MOD_TN), lambda l, j: (l, 0, j)),
        out_shape=jax.ShapeDtypeStruct((DEPTH, 8, n6), F32),
        compiler_params=_cparams(("parallel", "parallel")),
        name="adaln_mod",
    )(st, w_mod, b_mod.reshape(DEPTH, 1, n6))


INPROJ_TN = 512
INPROJ_WIDTHS = (PD_W, 3 * W_B, W_C, 3 * D_MODEL)
INPROJ_FEATURES = (OFF_HY, 3 * W_B, W_C, 3 * D_MODEL)


def _rms_mod(x, g, scale, shift):
    ms = jnp.mean(x * x, axis=-1, keepdims=True)
    return (x * lax.rsqrt(ms + EPS) * g) * (1.0 + scale) + shift


def _inproj_kernel(first, *refs):
    if first:
        (xc_ref, xl_ref, pos_ref, mod_ref, g_ref, w_ref, pd_ref, ph_ref, pf_ref, pg_ref, x_ref) = refs
        x = jnp.where(pl.program_id(0) < N_CTX_TILES, xc_ref[...], xl_ref[...] + pos_ref[...])
        x_ref[...] = x
    else:
        (xin_ref, mod_ref, g_ref, w_ref, pd_ref, ph_ref, pf_ref, pg_ref) = refs
        x = xin_ref[...]
    m = mod_ref[0]
    h = _rms_mod(x, g_ref[...], m[1:2], m[0:1]).astype(BF16)
    row0 = 0
    for o_ref, n_feat in zip((pd_ref, ph_ref, pf_ref, pg_ref), INPROJ_FEATURES):
        n = o_ref.shape[1]
        for c in range(0, n, INPROJ_TN):
            e = min(c + INPROJ_TN, n)
            ef = min(e, n_feat)
            y = _dot_nt(h, w_ref[0, row0 + c:row0 + ef, :])
            if ef < e:
                y = jnp.concatenate([y, jnp.zeros((y.shape[0], e - ef), F32)], axis=1)
            if o_ref is pg_ref:
                y = _sigmoid(y)
            o_ref[:, c:e] = y.astype(o_ref.dtype)
        row0 += n_feat


def _resident(shape, index_map):
    return pl.BlockSpec(shape, index_map, pipeline_mode=pl.Buffered(1))


def _inproj_call(xs, mod3, g, w_all, layer):
    first = len(xs) == 3
    widths = INPROJ_WIDTHS
    row = lambda i: (i, 0)
    const = lambda i: (0, 0)
    if first:
        per_lat = L_LAT // ROW_TILE
        x_specs = list(_ctx_lat_specs(D_MODEL)) + [
            pl.BlockSpec((ROW_TILE, D_MODEL), lambda i: (_lat_tile(i) % per_lat, 0))]
    else:
        x_specs = [pl.BlockSpec((ROW_TILE, D_MODEL), row)]
    out_widths = widths + ((D_MODEL,) if first else ())
    out_dtypes = (F32, F32, F32, BF16) + ((F32,) if first else ())
    return pl.pallas_call(
        functools.partial(_inproj_kernel, first),
        grid=(T_ALL // ROW_TILE,),
        in_specs=x_specs + [
            pl.BlockSpec((1, 6, D_MODEL), lambda i: (_mod_row_block(i), 0, 0)),
            pl.BlockSpec((1, D_MODEL), const),
            _resident((1, sum(INPROJ_FEATURES), D_MODEL), lambda i: (layer, 0, 0)),
        ],
        out_specs=[pl.BlockSpec((ROW_TILE, w), row) for w in out_widths],
        out_shape=[jax.ShapeDtypeStruct((T_ALL, w), dt) for w, dt in zip(out_widths, out_dtypes)],
        compiler_params=_cparams(("parallel",), VMEM_LIMIT),
        name="inproj_first" if first else "inproj",
    )(*xs, mod3, g.reshape(1, D_MODEL), w_all)


def _conv3_rows(cur, prev_row, next_row, w):
    n = cur.shape[0]
    sub = lax.broadcasted_iota(jnp.int32, (8, cur.shape[1]), 0)
    up = pltpu.roll(cur, 1, 0)
    up = jnp.concatenate([jnp.where(sub == 0, prev_row, up[0:8]), up[8:]], axis=0)
    dn = pltpu.roll(cur, n - 1, 0)
    dn = jnp.concatenate([dn[0:n - 8], jnp.where(sub == 7, next_row, dn[n - 8:])], axis=0)
    return up * w[0:1] + cur * w[1:2] + dn * w[2:3]


def _halo_specs(width, rows_per_block, n_row_blocks, blk_of):
    per = rows_per_block // 8
    last = n_row_blocks * per - 1
    prev = pl.BlockSpec((8, width), lambda *a: (jnp.maximum(blk_of(*a) * per - 1, 0), 0))
    nxt = pl.BlockSpec((8, width), lambda *a: (jnp.minimum((blk_of(*a) + 1) * per, last), 0))
    return prev, nxt


FILT_RT = 256


def _alternating_sum(x):
    t = lax.broadcasted_iota(jnp.int32, x.shape, 0)
    return jnp.sum(jnp.where(t % 2 == 0, x, -x), axis=0, keepdims=True)


def _filter_kernel(L, zpos_ref, w1_ref, b1_ref, fq_ref, w2_ref, b2_ref, w3_ref, win_ref,
                   cos_ref, nsin_ref, k_ref, hs_s, hm_s, krl_s, h2_s):
    o = pl.program_id(0)
    rt = pl.program_id(1)

    @pl.when((o == 0) & (rt == 0))
    def _():
        fq = fq_ref[...]
        for r0 in range(0, L, FILT_RT):
            rows = slice(r0, r0 + FILT_RT)
            h = jnp.sin(fq * (_mm3(zpos_ref[rows, :], w1_ref[...]) + b1_ref[...]))
            h2_s[rows, :] = jnp.sin(fq * (_mm3(h, w2_ref[...]) + b2_ref[...]))

    @pl.when(rt == 0)
    def _():
        alt_acc = jnp.zeros((1, W_B), F32)
        for r0 in range(0, L, FILT_RT):
            rows = slice(r0, r0 + FILT_RT)
            hf = _mm3(h2_s[rows, :], w3_ref[...])
            win = win_ref[rows, :]
            fw = hf[:, 0:W_B] * win
            bw = hf[:, W_B:2 * W_B] * win
            hsum = fw + bw
            hs_s[rows, :] = hsum.astype(BF16)
            hm_s[rows, :] = (fw - bw).astype(BF16)
            alt_acc = alt_acc + _alternating_sum(hsum)
        krl_s[...] = jnp.broadcast_to(alt_acc, krl_s.shape)

    p1 = _dot(cos_ref[...], hs_s[...])
    p2 = _dot(nsin_ref[...], hm_s[...])
    first = (rt * FILT_RT + lax.broadcasted_iota(jnp.int32, p1.shape, 0)) == 0
    k_ref[0, 0] = p1
    k_ref[0, 1] = jnp.where(first, krl_s[0:1, :], p1)
    k_ref[0, 2] = jnp.where(first, 0.0, p2)


def _filter_call(L, zpos, w1p, b1, fq, w2, b2, w3, win, cos, nsin):
    nrt = L // FILT_RT
    c2 = lambda o, r: (0, 0)
    return pl.pallas_call(
        functools.partial(_filter_kernel, L),
        grid=(2, nrt),
        in_specs=[
            pl.BlockSpec((L, LANE), c2),
            pl.BlockSpec((LANE, HY_HID), c2),
            pl.BlockSpec((1, HY_HID), c2),
            pl.BlockSpec((1, HY_HID), c2),
            pl.BlockSpec((HY_HID, HY_HID), c2),
            pl.BlockSpec((1, HY_HID), c2),
            pl.BlockSpec((HY_HID, 2 * W_B), lambda o, r: (0, o)),
            pl.BlockSpec((L, W_B), c2),
            pl.BlockSpec((FILT_RT, L), lambda o, r: (r, 0)),
            pl.BlockSpec((FILT_RT, L), lambda o, r: (r, 0)),
        ],
        out_specs=pl.BlockSpec((1, 3, FILT_RT, W_B), lambda o, r: (o, 0, r, 0)),
        out_shape=jax.ShapeDtypeStruct((2, 3, L, W_B), F32),
        scratch_shapes=[pltpu.VMEM((L, W_B), BF16), pltpu.VMEM((L, W_B), BF16),
                        pltpu.VMEM((8, W_B), F32), pltpu.VMEM((L, HY_HID), F32)],
        compiler_params=_cparams(("arbitrary", "arbitrary"), VMEM_LIMIT),
        name=f"hyena_filter_{L}",
    )(zpos, w1p, b1, fq, w2, b2, w3, win, cos, nsin)


HYENA_FT = 256

HYENA_CONV_ROWS = 256


HYENA_CH = 256
HYENA_OUT_ROWS = 512


def _hyena_kernel(L, ft, x1_ref, x2_ref, v_ref, cw1_ref, cw2_ref, cwv_ref, k_ref, bias_ref, cos_ref, nsin_ref,
                  out_ref, gate_s, zf_s, zb_s, yt_s, yb_s):
    o = pl.program_id(2)

    @pl.when(o == 0)
    def _():
        for src, cw_ref, dst in ((x1_ref, cw1_ref, 0), (x2_ref, cw2_ref, 1), (v_ref, cwv_ref, None)):
            cw = cw_ref[...]
            for r0 in range(0, L, HYENA_CONV_ROWS):
                r1 = r0 + HYENA_CONV_ROWS
                prev_row = src[r0 - 1:r0, :] if r0 > 0 else 0.0
                next_row = src[r1:r1 + 1, :] if r1 < L else 0.0
                uc = _conv3_rows(src[r0:r1, :], prev_row, next_row, cw)
                if dst is None:
                    zf_s[r0:r1, :] = uc
                    zb_s[r0:r1, :] = uc.astype(BF16)
                else:
                    gate_s[dst, r0:r1, :] = uc

    zb = zb_s[...]
    nyq = _alternating_sum(zf_s[...])
    for f in range(L // ft):
        rows = slice(f * ft, (f + 1) * ft)
        top = _dot(cos_ref[rows, :], zb)
        bot = _dot(nsin_ref[rows, :], zb)
        if f == 0:
            first = lax.broadcasted_iota(jnp.int32, top.shape, 0) == 0
            bot = jnp.where(first, nyq, bot)
        krt = k_ref[0, 0, rows, :]
        krb = k_ref[0, 1, rows, :]
        ki = k_ref[0, 2, rows, :]
        yt = top * krt - bot * ki
        yb = top * ki + bot * krb
        if f == 0:
            yt = jnp.where(first, 0.5 * yt, yt)
            y_nyq = yb[0:1, :]
        yt_s[rows, :] = yt.astype(BF16)
        yb_s[rows, :] = yb.astype(BF16)

    t = lax.broadcasted_iota(jnp.int32, (HYENA_OUT_ROWS, HYENA_CH), 0)
    alt_half = jnp.where(t % 2 == 0, 0.5, -0.5)
    for r0 in range(0, L, HYENA_OUT_ROWS):
        rows = slice(r0, r0 + HYENA_OUT_ROWS)
        acc = _dot(cos_ref[rows, :], yt_s[...]) + _dot(nsin_ref[rows, :], yb_s[...]) + alt_half * y_nyq
        znew = gate_s[o, rows, :] * (acc * (1.0 / L) + bias_ref[0] * zf_s[rows, :])
        zf_s[rows, :] = znew
        zb_s[rows, :] = znew.astype(BF16)
        out_ref[rows, :] = znew.astype(out_ref.dtype)


def _hyena_call(ph, conv_w, kspec, bias, cos, nsin, n_seq, L, row_blk0, ft):
    nch = W_B // HYENA_CH
    once = pl.Buffered(1)
    col = lambda part: (lambda s, c, o: (row_blk0 + s, part * nch + c))
    cwcol = lambda part: (lambda s, c, o: (0, part * nch + c))
    return pl.pallas_call(
        functools.partial(_hyena_kernel, L, ft),
        grid=(n_seq, nch, 2),
        in_specs=[pl.BlockSpec((L, HYENA_CH), col(part), pipeline_mode=once) for part in range(3)]
        + [pl.BlockSpec((3, HYENA_CH), cwcol(part)) for part in range(3)]
        + [
            pl.BlockSpec((1, 3, L, HYENA_CH), lambda s, c, o: (o, 0, 0, c)),
            pl.BlockSpec((1, 1, HYENA_CH), lambda s, c, o: (o, 0, c)),
            pl.BlockSpec((L, L), lambda s, c, o: (0, 0), pipeline_mode=once),
            pl.BlockSpec((L, L), lambda s, c, o: (0, 0), pipeline_mode=once),
        ],
        out_specs=pl.BlockSpec((L, HYENA_CH), lambda s, c, o: (s, c)),
        out_shape=jax.ShapeDtypeStruct((n_seq * L, W_B), BF16),
        scratch_shapes=[pltpu.VMEM((2, L, HYENA_CH), F32), pltpu.VMEM((L, HYENA_CH), F32),
                        pltpu.VMEM((L, HYENA_CH), BF16), pltpu.VMEM((L, HYENA_CH), BF16),
                        pltpu.VMEM((L, HYENA_CH), BF16)],
        compiler_params=_cparams(("parallel", "parallel", "arbitrary"), VMEM_LIMIT),
        name=f"hyena_conv_{L}",
    )(ph, ph, ph, conv_w, conv_w, conv_w, kspec, bias, cos, nsin)


HYENA_SHORT_SB = 4


def _hyena_short_kernel(L, ph_ref, cw_ref, k_ref, bias_ref, cos_ref, nsin_f_ref, cos_i_ref, nsin_i_ref, out_ref):
    cos, nsin_f, cos_i, nsin_i = cos_ref[...], nsin_f_ref[...], cos_i_ref[...], nsin_i_ref[...]
    cw = cw_ref[...]
    rows = [slice(s * L, (s + 1) * L) for s in range(HYENA_SHORT_SB)]
    ucs = [_conv3_rows(ph_ref[r, :], 0.0, 0.0, cw) for r in rows]
    zs = [uc[:, 2 * W_B:3 * W_B] for uc in ucs]
    for o in range(2):
        krt, krb, ki = k_ref[o, 0], k_ref[o, 1], k_ref[o, 2]
        zbs = [z.astype(BF16) for z in zs]
        tops = [_dot(cos, zb) for zb in zbs]
        bots = [_dot(nsin_f, zb) for zb in zbs]
        yts = [top * krt - bot * ki for top, bot in zip(tops, bots)]
        ybs = [top * ki + bot * krb for top, bot in zip(tops, bots)]
        accs = [_dot(cos_i, yt.astype(BF16)) + _dot(nsin_i, yb.astype(BF16)) for yt, yb in zip(yts, ybs)]
        zs = [uc[:, o * W_B:(o + 1) * W_B] * (acc * (1.0 / L) + bias_ref[o] * z)
              for uc, acc, z in zip(ucs, accs, zs)]
    for r, z in zip(rows, zs):
        out_ref[r, :] = z.astype(out_ref.dtype)


def _hyena_short_call(ph, conv_w, kspec, bias, cos, nsin, n_seq, L):
    rows = HYENA_SHORT_SB * L
    alt = jnp.where(jnp.arange(L) % 2 == 0, 1.0, -1.0).astype(BF16)
    nsin_f = nsin.at[0, :].set(alt)
    cos_i = cos.at[:, 0].set(0.5)
    nsin_i = nsin.at[:, 0].set(0.5 * alt)
    return pl.pallas_call(
        functools.partial(_hyena_short_kernel, L),
        grid=(n_seq // HYENA_SHORT_SB,),
        in_specs=[
            pl.BlockSpec((rows, 3 * W_B), lambda i: (i, 0)),
            pl.BlockSpec((3, 3 * W_B), lambda i: (0, 0)),
            pl.BlockSpec((2, 3, L, W_B), lambda i: (0, 0, 0, 0)),
            pl.BlockSpec((2, 1, W_B), lambda i: (0, 0, 0)),
        ] + [pl.BlockSpec((L, L), lambda i: (0, 0)) for _ in range(4)],
        out_specs=pl.BlockSpec((rows, W_B), lambda i: (i, 0)),
        out_shape=jax.ShapeDtypeStruct((n_seq * L, W_B), BF16),
        compiler_params=_cparams(("parallel",), VMEM_LIMIT),
        name=f"hyena_conv_{L}",
    )(ph, conv_w, kspec, bias, cos, nsin_f, cos_i, nsin_i)


FNET_RT = 512


def _fnet_kernel(L, x_ref, cos_ref, nsin_ref, bdc_ref, bds_ref, out_ref, xc_s, xs_s):
    for r0 in range(0, L, FNET_RT):
        rows = slice(r0, r0 + FNET_RT)
        xb = x_ref[rows, :].astype(BF16)
        xc_s[rows, :] = _dot(xb, bdc_ref[...]).astype(BF16)
        xs_s[rows, :] = _dot(xb, bds_ref[...]).astype(BF16)
    for r0 in range(0, L, FNET_RT):
        rows = slice(r0, r0 + FNET_RT)
        y = _dot(cos_ref[rows, :], xc_s[...]) + _dot(nsin_ref[rows, :], xs_s[...])
        out_ref[rows, :] = (y * (1.0 / math.sqrt(DC * L))).astype(out_ref.dtype)


def _fnet_call(pf, cos, nsin, bdc, bds, n_seq, L, row_blk0):
    const = lambda s: (0, 0)
    return pl.pallas_call(
        functools.partial(_fnet_kernel, L),
        grid=(n_seq,),
        in_specs=[
            pl.BlockSpec((L, W_C), lambda s: (row_blk0 + s, 0)),
            _resident((L, L), const),
            _resident((L, L), const),
            pl.BlockSpec((W_C, W_C), const),
            pl.BlockSpec((W_C, W_C), const),
        ],
        out_specs=pl.BlockSpec((L, W_C), lambda s: (s, 0)),
        out_shape=jax.ShapeDtypeStruct((n_seq * L, W_C), BF16),
        scratch_shapes=[pltpu.VMEM((L, W_C), BF16), pltpu.VMEM((L, W_C), BF16)],
        compiler_params=_cparams(("parallel",), VMEM_LIMIT),
        name=f"fnet_{L}",
    )(pf, cos, nsin, bdc, bds)


FNET_SHORT_SB = 4


def _fnet_short_kernel(L, x_ref, cos_ref, nsin_ref, bdc_ref, bds_ref, out_ref):
    xb = x_ref[...].astype(BF16)
    xc = _dot(xb, bdc_ref[...]).astype(BF16)
    xs = _dot(xb, bds_ref[...]).astype(BF16)
    for s in range(FNET_SHORT_SB):
        rows = slice(s * L, (s + 1) * L)
        y = _dot(cos_ref[...], xc[rows]) + _dot(nsin_ref[...], xs[rows])
        out_ref[rows, :] = (y * (1.0 / math.sqrt(DC * L))).astype(out_ref.dtype)


def _fnet_short_call(pf, cos, nsin, bdc, bds, n_seq, L):
    rows = FNET_SHORT_SB * L
    const = lambda i: (0, 0)
    return pl.pallas_call(
        functools.partial(_fnet_short_kernel, L),
        grid=(n_seq // FNET_SHORT_SB,),
        in_specs=[pl.BlockSpec((rows, W_C), lambda i: (i, 0)),
                  pl.BlockSpec((L, L), const), pl.BlockSpec((L, L), const),
                  pl.BlockSpec((W_C, W_C), const), pl.BlockSpec((W_C, W_C), const)],
        out_specs=pl.BlockSpec((rows, W_C), lambda i: (i, 0)),
        out_shape=jax.ShapeDtypeStruct((n_seq * L, W_C), BF16),
        compiler_params=_cparams(("parallel",), VMEM_LIMIT),
        name=f"fnet_{L}",
    )(pf, cos, nsin, bdc, bds)


DELTA_RB = 256
SOLVE_BATCH = 8
CHUNKS_PER_RB = DELTA_RB // CHUNK


HEADS_PER_LANE_TILE = LANE // DK


def _block_diag(y, half_masks):
    yb = y.astype(BF16)
    zero = jnp.zeros((CHUNK, LANE), BF16)
    row_blocks = []
    for h in range(HEAD_GROUP):
        tile = h // HEADS_PER_LANE_TILE
        piece = yb[:, tile * LANE:(tile + 1) * LANE] * half_masks[h % HEADS_PER_LANE_TILE]
        row_blocks.append(jnp.concatenate(
            [piece if t == tile else zero for t in range(GROUP_W // LANE)], axis=1))
    return jnp.concatenate(row_blocks, axis=0)


def _stacked_const_rhs(arrs, c, n):
    m = arrs[0].shape[0]
    parts = [p for a in arrs for p in _split(a, n)]
    y = _dot(jnp.concatenate(parts, axis=0), c)
    outs = []
    for i in range(len(arrs)):
        acc = y[i * n * m:(i * n + 1) * m]
        for t in range(1, n):
            acc = acc + y[(i * n + t) * m:(i * n + t + 1) * m]
        outs.append(acc)
    return outs


def _head_sums(arrs, ones_group, n_split=2):
    groups = [_stacked_const_rhs([a[:, g * GROUP_W:(g + 1) * GROUP_W] for a in arrs], ones_group, n_split)
              for g in range(N_GROUPS)]
    return [jnp.concatenate([groups[g][i] for g in range(N_GROUPS)], axis=1) for i in range(len(arrs))]


def _expand_heads(x, lane0):
    cols = [jnp.broadcast_to(x[:, lane0 + h:lane0 + h + 1], (x.shape[0], DK)) for h in range(H_A)]
    return jnp.concatenate(cols, axis=1)


def _const_lhs_split(c, b, n):
    w = b.shape[1]
    y = _dot(c, jnp.concatenate(_split(b, n), axis=1))
    acc = y[:, 0:w]
    for t in range(1, n):
        acc = acc + y[:, t * w:(t + 1) * w]
    return acc


def _delta_chunk_stages(pd_ref, prev_ref, next_ref, r, has_prev, has_next, dirs, cw, a_neg, dtb, ones_bd,
                        tri, half_masks, eye_tile):
    ch = {}
    rows = slice(r * CHUNK, (r + 1) * CHUNK)

    def conv():
        cur = pd_ref[rows, 0:QKV_W]
        if r == 0:
            prev_row = jnp.where(has_prev, prev_ref[7:8, :], 0.0)
        else:
            prev_row = pd_ref[r * CHUNK - 1:r * CHUNK, 0:QKV_W]
        if r == CHUNKS_PER_RB - 1:
            next_row = jnp.where(has_next, next_ref[0:1, :], 0.0)
        else:
            next_row = pd_ref[(r + 1) * CHUNK:(r + 1) * CHUNK + 1, 0:QKV_W]
        qkv = _silu(_conv3_rows(cur, prev_row, next_row, cw))
        ch["q"] = qkv[:, 0:H_A * DK]
        ch["k"] = qkv[:, H_A * DK:2 * H_A * DK]
        ch["v"] = qkv[:, 2 * H_A * DK:]

    def norms():
        q, k = ch.pop("q"), ch.pop("k")
        qss, kss = _head_sums([q * q, k * k], ones_bd, 1)
        ch["qn"] = q * lax.rsqrt(qss + EPS) * (DK ** -0.5)
        ch["kn"] = k * lax.rsqrt(kss + EPS)

    def gram():
        ch["gram"], ch["knT"] = [], []
        for g in range(N_GROUPS):
            lanes = slice(g * GROUP_W, (g + 1) * GROUP_W)
            lhs = jnp.concatenate([ch["kn"][:, lanes].astype(BF16), ch["qn"][:, lanes].astype(BF16), eye_tile],
                                  axis=0)
            res = _dot_nt(lhs, _block_diag(ch["kn"][:, lanes], half_masks))
            ch["gram"].append(res[0:2 * CHUNK])
            ch["knT"].append(res[2 * CHUNK:3 * CHUNK])

    def decay():
        ba = pd_ref[rows, OFF_B:OFF_B + LANE]
        sig = _sigmoid(ba)
        glog = a_neg * _softplus(ba + dtb)
        ch["decay"] = {}
        for d in dirs:
            gcum = _const_lhs_split(tri[d], glog, 3)
            beta = _expand_heads(sig, d * H_A)
            gcc8 = _expand_heads(gcum, (2 + d) * H_A)
            ch["decay"][d] = (beta, gcc8)

    return ch, [conv, norms, gram, decay]


def _delta_kernel(nb, zero_init, n_prev, *refs):
    shared = nb == 1
    it = iter(refs)
    blocks = [(next(it), next(it), next(it))]
    if not shared:
        blocks.append((next(it), next(it), next(it)))
    cw_ref, par_ref, ones_ref = next(it), next(it), next(it)
    s0_ref = None if zero_init else next(it)
    sprev_ref = next(it) if n_prev else None
    o_refs = (next(it), next(it))
    sfin_ref = next(it)
    u_s, w_s, p_s, qg_s, kg_s, gl_s, st_s = (next(it) for _ in range(7))
    j = pl.program_id(1)

    ri = lax.broadcasted_iota(jnp.int32, (CHUNK, GROUP_W), 0)
    cj = lax.broadcasted_iota(jnp.int32, (CHUNK, GROUP_W), 1) % CHUNK
    ixj = ri ^ cj
    eye = ixj == 0
    eye_tile = jnp.where(eye, 1.0, 0.0).astype(BF16)
    hl =lax.broadcasted_iota(jnp.int32, (CHUNK, LANE), 1) // DK
    half_masks = tuple(jnp.where(hl == h, 1.0, 0.0).astype(BF16) for h in range(HEADS_PER_LANE_TILE))
    ti = lax.broadcasted_iota(jnp.int32, (CHUNK, CHUNK), 0)
    tm = lax.broadcasted_iota(jnp.int32, (CHUNK, CHUNK), 1)
    ri8 = lax.broadcasted_iota(jnp.int32, (CHUNK, H_A * DK), 0)
    cj8 = lax.broadcasted_iota(jnp.int32, (CHUNK, H_A * DK), 1) % CHUNK
    incl = (ri >= cj, ri <= cj)
    strict = (ri > cj, ri < cj)
    tri = tuple(jnp.where(m, 1.0, 0.0).astype(BF16) for m in (tm <= ti, tm >= ti))
    eye8 = ri8 == cj8
    last_row = (CHUNK - 1, 0)

    @pl.when(j == 0)
    def _():
        for d in range(2):
            for g in range(N_GROUPS):
                if zero_init:
                    st_s[d, g] = jnp.zeros((DK, GROUP_W), F32)
                else:
                    r0 = g * GROUP_W
                    st_s[d, g] = jnp.concatenate(
                        [s0_ref[0, d, r0 + hh * DK:r0 + (hh + 1) * DK, :] for hh in range(HEAD_GROUP)], axis=1)

    cw = cw_ref[...]
    a_neg = -jnp.exp(par_ref[0:1, :])
    dtb = par_ref[1:2, :]
    pos = (j, nb - 1 - j)
    scan_order = (tuple(range(CHUNKS_PER_RB)), tuple(reversed(range(CHUNKS_PER_RB))))

    def unit_thunk(d, r, ch, g, units):
        def run():
            rows = slice(r * CHUNK, (r + 1) * CHUNK)
            lanes = slice(g * GROUP_W, (g + 1) * GROUP_W)
            beta, gcc8 = ch["decay"][d]
            gcr = jnp.sum(jnp.where(eye8, gcc8, 0.0), axis=0, keepdims=True)[:, lanes]
            qn, kn, be, gcc = ch["qn"][:, lanes], ch["kn"][:, lanes], beta[:, lanes], gcc8[:, lanes]
            kq = ch["gram"][g]
            dec = jnp.exp(jnp.where(incl[d], gcc - gcr, -1e30))
            a = jnp.where(strict[d], kq[0:CHUNK] * be * dec, 0.0)
            eg = jnp.exp(gcc)
            gcl = gcc[last_row[d]:last_row[d] + 1, :]
            units.append((d, rows, lanes, a, ch["v"][:, lanes] * be, kn * be * eg))
            p_s[d, rows, lanes] = (kq[CHUNK:2 * CHUNK] * dec).astype(BF16)
            qg_s[d, rows, lanes] = (qn * eg).astype(BF16)
            kg_s[d, rows, lanes] = (ch["knT"][g] * jnp.exp(gcl - gcr)).astype(BF16)
            gl_s[d, r * 8:(r + 1) * 8, lanes] = jnp.broadcast_to(jnp.exp(gcl), (8, GROUP_W))
        return run

    def prep_thunks(units):
        thunks = []
        if shared:
            todo = [(0, r, (0, 1)) for r in range(CHUNKS_PER_RB)]
        else:
            todo = [(d, r, (d,)) for d in range(2) for r in scan_order[d]]
        for b, r, dirs in todo:
            ch, stages = _delta_chunk_stages(*blocks[b], r, pos[b] > 0, pos[b] < nb - 1, dirs, cw, a_neg, dtb,
                                             ones_ref[...], tri, half_masks, eye_tile)
            thunks += stages
            thunks += [unit_thunk(d, r, ch, g, units) for d in dirs for g in range(N_GROUPS)]
        return thunks

    def scan_thunk(d, c):
        def run():
            r = scan_order[d][c]
            rows = slice(r * CHUNK, (r + 1) * CHUNK)
            for g in range(N_GROUPS):
                lanes = slice(g * GROUP_W, (g + 1) * GROUP_W)
                s = st_s[d, g]
                wq = jnp.concatenate([w_s[d, rows, lanes], qg_s[d, rows, lanes]], axis=0)
                ws_qs = _dot(wq, _block_diag(s, half_masks))
                v_new = u_s[d, rows, lanes] - ws_qs[0:CHUNK]
                pk = jnp.concatenate([p_s[d, rows, lanes], kg_s[d, rows, lanes]], axis=0)
                po = _dot(pk, _block_diag(v_new, half_masks))
                st_s[d, g] = s * gl_s[d, r * 8:r * 8 + 1, lanes] + po[CHUNK:2 * CHUNK]
                o_refs[d][rows, lanes] = ws_qs[CHUNK:2 * CHUNK] + po[0:CHUNK]
        return run

    def solve(units):
        r0s = [jnp.where(ixj == 1, un[3], 0.0) for un in units]
        zxs = [jnp.concatenate([un[3] - _dot(un[3].astype(BF16), _block_diag(r0, half_masks)),
                                jnp.where(eye, 1.0, 0.0) - r0], axis=0) for un, r0 in zip(units, r0s)]
        for lvl in range(1, 5):
            links = [_block_diag(jnp.where((ixj >> lvl) == 1, zx[0:CHUNK], 0.0), half_masks) for zx in zxs]
            zxs = [zx - _dot(zx.astype(BF16), lk) for zx, lk in zip(zxs, links)]
        links = [_block_diag(jnp.where((ixj >> 5) == 1, zx[0:CHUNK], 0.0), half_masks) for zx in zxs]
        xs = [zx[CHUNK:2 * CHUNK] - _dot(zx[CHUNK:2 * CHUNK].astype(BF16), lk) for zx, lk in zip(zxs, links)]
        for x, (d, rows, lanes, _, vb, kbe) in zip(xs, units):
            rhs = jnp.concatenate([_block_diag(vb, half_masks), _block_diag(kbe, half_masks)], axis=1)
            uw = _dot(x.astype(BF16), rhs)
            u_s[d, rows, lanes] = uw[:, 0:GROUP_W]
            w_s[d, rows, lanes] = uw[:, GROUP_W:2 * GROUP_W].astype(BF16)

    units = []
    for f in prep_thunks(units):
        f()
    for i in range(0, len(units), SOLVE_BATCH):
        solve(units[i:i + SOLVE_BATCH])
    for c in range(CHUNKS_PER_RB):
        for d in range(2):
            scan_thunk(d, c)()

    @pl.when(j == nb - 1)
    def _():
        for d in range(2):
            for g in range(N_GROUPS):
                s = st_s[d, g]
                for hh in range(HEAD_GROUP):
                    r0 = g * GROUP_W + hh * DK
                    sfin_ref[0, n_prev, d, r0:r0 + DK, :] = s[:, hh * DV:(hh + 1) * DV]
        for p in range(n_prev):
            sfin_ref[0, p] = sprev_ref[0, p]


def _delta_call(pd, conv_w, par, ones_bd, s0, n_seq, nb, blk0, prev_states=None):
    zero_init = s0 is None
    n_prev = 0 if prev_states is None else prev_states.shape[1]
    n_blocks_all = T_ALL // DELTA_RB
    blk_of = (lambda s, j: blk0 + s * nb + j, lambda s, j: blk0 + s * nb + nb - 1 - j)
    in_specs, args = [], []
    for d in range(1 if nb == 1 else 2):
        prev, nxt = _halo_specs(QKV_W, DELTA_RB, n_blocks_all, blk_of[d])
        in_specs += [pl.BlockSpec((DELTA_RB, PD_W), lambda s, j, d=d: (blk_of[d](s, j), 0)), prev, nxt]
        args += [pd, pd, pd]
    in_specs += [
        pl.BlockSpec((3, QKV_W), lambda s, j: (0, 0)),
        pl.BlockSpec((8, LANE), lambda s, j: (0, 0)),
        pl.BlockSpec((GROUP_W, GROUP_W), lambda s, j: (0, 0)),
    ]
    args += [conv_w, par, ones_bd]
    if not zero_init:
        in_specs.append(pl.BlockSpec((1, 2, H_A * DK, DV), lambda s, j: (s, 0, 0, 0)))
        args.append(s0)
    if n_prev:
        in_specs.append(pl.BlockSpec((1, n_prev, 2, H_A * DK, DV), lambda s, j: (s, 0, 0, 0, 0)))
        args.append(prev_states)
    rows = n_seq * nb * DELTA_RB
    dir_buf = lambda n, dt=F32: pltpu.VMEM((2, n, H_A * DK), dt)
    return pl.pallas_call(
        functools.partial(_delta_kernel, nb, zero_init, n_prev),
        grid=(n_seq, nb),
        in_specs=in_specs,
        out_specs=[pl.BlockSpec((DELTA_RB, W_A), lambda s, j: (s * nb + j, 0)),
                   pl.BlockSpec((DELTA_RB, W_A), lambda s, j: (s * nb + nb - 1 - j, 0)),
                   pl.BlockSpec((1, n_prev + 1, 2, H_A * DK, DV), lambda s, j: (s, 0, 0, 0, 0))],
        out_shape=[jax.ShapeDtypeStruct((rows, W_A), F32),
                   jax.ShapeDtypeStruct((rows, W_A), F32),
                   jax.ShapeDtypeStruct((n_seq, n_prev + 1, 2, H_A * DK, DV), F32)],
        scratch_shapes=[dir_buf(DELTA_RB)] + [dir_buf(DELTA_RB, BF16) for _ in range(4)]
        + [dir_buf(CHUNKS_PER_RB * 8), pltpu.VMEM((2, N_GROUPS, DK, GROUP_W), F32)],
        compiler_params=_cparams(("parallel", "arbitrary"), VMEM_LIMIT),
        name=f"deltanet_nb{nb}",
    )(*args)


FFN_TN = D_FF


def _postmix_kernel(final_norm, tile0, ofc_ref, ofl_ref, obc_ref, obl_ref, z_ref, ybc_ref, ybl_ref,
                    ycc_ref, ycl_ref, pg_ref, x_ref, mod_ref, na_ref, ones_ref, wpa_ref, wpb_ref, wpc_ref,
                    wo_ref, g2_ref, wgu_ref, wdn_ref, nf_ref, out_ref):
    is_ctx = pl.program_id(0) + tile0 < N_CTX_TILES
    m = mod_ref[0]
    o = jnp.where(is_ctx, ofc_ref[...] + obc_ref[...], ofl_ref[...] + obl_ref[...])
    yb = jnp.where(is_ctx, ybc_ref[...], ybl_ref[...])
    yc = jnp.where(is_ctx, ycc_ref[...], ycl_ref[...])
    ms = _head_sums([o * o], ones_ref[...])[0] * (1.0 / DV)
    ya = (o * lax.rsqrt(ms + EPS) * na_ref[...]) * _silu(z_ref[...])
    merged = (pg_ref[:, 0:D_MODEL].astype(F32) * _dot(ya.astype(BF16), wpa_ref[0])
              + pg_ref[:, D_MODEL:2 * D_MODEL].astype(F32) * _dot(yb.astype(BF16), wpb_ref[0])
              + pg_ref[:, 2 * D_MODEL:3 * D_MODEL].astype(F32) * _dot(yc.astype(BF16), wpc_ref[0]))
    x = x_ref[...] + m[2:3] * _dot(merged.astype(BF16), wo_ref[0])

    h = _rms_mod(x, g2_ref[...], m[4:5], m[3:4]).astype(BF16)
    acc = None
    for c in range(0, D_FF, FFN_TN):
        gate = _dot(h, wgu_ref[0, :, c:c + FFN_TN])
        up = _dot(h, wgu_ref[0, :, D_FF + c:D_FF + c + FFN_TN])
        part = _dot((_silu(gate) * up).astype(BF16), wdn_ref[0, c:c + FFN_TN, :])
        acc = part if acc is None else acc + part
    xn = x + m[5:6] * acc
    if final_norm:
        ms = jnp.mean(xn * xn, axis=-1, keepdims=True)
        xn = xn * lax.rsqrt(ms + EPS) * nf_ref[...]
    out_ref[...] = xn


def _postmix_call(o_ctx, o_lat, pd, yb, yc, pg, x, mod3, na512, ones_bd, wpa, wpb, wpc, wo,
                  g2, wgu, wdn, nf, layer, final_norm, tile0=0, n_tiles=T_ALL // ROW_TILE):
    row = lambda i: (i + tile0, 0)
    const = lambda i: (0, 0)
    lyr = lambda i: (layer, 0, 0)
    assert W_A == W_B == W_C
    return pl.pallas_call(
        functools.partial(_postmix_kernel, final_norm, tile0),
        grid=(n_tiles,),
        in_specs=[
            *_ctx_lat_specs(W_A, tile0), *_ctx_lat_specs(W_A, tile0),
            pl.BlockSpec((ROW_TILE, W_A), lambda i: (i + tile0, OFF_Z // W_A)),
            *_ctx_lat_specs(W_B, tile0), *_ctx_lat_specs(W_C, tile0),
            pl.BlockSpec((ROW_TILE, 3 * D_MODEL), row),
            pl.BlockSpec((ROW_TILE, D_MODEL), row),
            pl.BlockSpec((1, 6, D_MODEL), lambda i: (_mod_row_block(i + tile0), 0, 0)),
            pl.BlockSpec((1, W_A), const),
            _resident((GROUP_W, GROUP_W), const),
            _resident((1, W_A, D_MODEL), lyr),
            _resident((1, W_B, D_MODEL), lyr),
            _resident((1, W_C, D_MODEL), lyr),
            _resident((1, D_MODEL, D_MODEL), lyr),
            pl.BlockSpec((1, D_MODEL), const),
            _resident((1, D_MODEL, 2 * D_FF), lyr),
            _resident((1, D_FF, D_MODEL), lyr),
            pl.BlockSpec((1, D_MODEL), const),
        ],
        out_specs=pl.BlockSpec((ROW_TILE, D_MODEL), lambda i: (i, 0)),
        out_shape=jax.ShapeDtypeStruct((n_tiles * ROW_TILE, D_MODEL), F32),
        compiler_params=_cparams(("parallel",), VMEM_LIMIT),
        name="postmix_final" if final_norm else "postmix",
    )(o_ctx[0], o_lat[0], o_ctx[1], o_lat[1], pd, yb[0], yb[1], yc[0], yc[1], pg, x, mod3, na512,
      ones_bd, wpa, wpb, wpc, wo, g2.reshape(1, D_MODEL), wgu, wdn, nf.reshape(1, D_MODEL))


TABLE_SPLIT = 64


def _grid_pos_embed(n_tokens):
    rows = n_tokens // GRID_W
    quarter = D_MODEL // 4
    omega = 1.0 / (10000.0 ** (jnp.arange(quarter, dtype=F32) / quarter))

    def emb(pos):
        a = pos[:, None] * omega[None, :]
        return jnp.concatenate([jnp.sin(a), jnp.cos(a)], axis=-1)

    e_row, e_col = lax.optimization_barrier((emb(jnp.arange(rows).astype(F32)),
                                             emb(jnp.arange(GRID_W).astype(F32))))
    return jnp.concatenate([jnp.repeat(e_row, GRID_W, axis=0), jnp.tile(e_col, (rows, 1))], axis=-1)


def _cos_nsin_tables(n, period):
    t = jnp.arange(n, dtype=jnp.int32)[None, :]

    def cs(r):
        ang = ((r * t) % period).astype(F32) * (2.0 * math.pi / period)
        return jnp.cos(ang), jnp.sin(ang)

    ca, sa = cs(jnp.arange(n // TABLE_SPLIT, dtype=jnp.int32)[:, None] * TABLE_SPLIT)
    cb, sb = cs(jnp.arange(TABLE_SPLIT, dtype=jnp.int32)[:, None])
    ca, sa, cb, sb = lax.optimization_barrier((ca, sa, cb, sb))
    ca, sa = ca[:, None, :], sa[:, None, :]
    cos = (ca * cb[None] - sa * sb[None]).reshape(n, n)
    nsin = (-(sa * cb[None] + ca * sb[None])).reshape(n, n)
    return cos.astype(BF16), nsin.astype(BF16)


def _hyena_positions(L):
    bands = (HY_EMB - 1) // 2
    t = jnp.linspace(0.0, 1.0, L, dtype=F32)[:, None]
    wpos = (2.0 * math.pi / L) * jnp.arange(L, dtype=F32)[:, None]
    fr = jnp.linspace(1e-4, bands - 1, bands, dtype=F32)[None, :]
    zpos = jnp.concatenate([t, jnp.cos(fr * wpos), -jnp.sin(fr * wpos)], axis=-1)
    zpos = jnp.pad(zpos, ((0, 0), (0, LANE - HY_EMB)))
    deltas = jnp.abs(jnp.linspace(math.log(HY_DECAY_TARGET) / HY_SLOW_PCT,
                                  math.log(HY_DECAY_TARGET) / HY_FAST_PCT, W_B, dtype=F32))
    window = jnp.exp(-t * deltas[None, :])
    return zpos, window


def _group_tables():
    r = jnp.arange(DC, dtype=jnp.int32)
    ang = ((r[:, None] * r[None, :]) % DC).astype(F32) * (2.0 * math.pi / DC)
    eye = jnp.eye(G_C, dtype=F32)
    return jnp.kron(eye, jnp.cos(ang)).astype(BF16), jnp.kron(eye, jnp.sin(ang)).astype(BF16)


def _head_ones():
    return jnp.kron(jnp.eye(HEAD_GROUP, dtype=F32), jnp.ones((DK, DK), F32)).astype(BF16)


def kernel(x_prompt, x_sample, state_delta, c, c_ctx, w_mod, b_mod, norm1_g, norm2_g, w_in, conv_qkv, a_log, dt_bias, norm_a, conv_hy, hy_w1, hy_b1, hy_freq, hy_w2, hy_b2, hy_w3, hy_bias, w_pa, w_pb, w_pc, w_o, w_gu, w_down, norm_f):
    assert x_prompt.shape == (N_CTX_SEQ, L_CTX, D_MODEL) and x_sample.shape == (N_LAT_SEQ, L_LAT, D_MODEL)
    st = jnp.pad(jnp.concatenate([c_ctx[None], c], axis=0).T, ((0, 0), (0, 8 - 1 - N_LAT_SEQ)))
    mod = _mod_call(st, w_mod, b_mod).reshape(DEPTH, 8, 6, D_MODEL)

    ones_bd = _head_ones()
    bdc, bds = _group_tables()
    seqs = ((L_CTX, N_CTX_SEQ, 0), (L_LAT, N_LAT_SEQ, T_CTX // L_LAT))
    tables = {L: (_cos_nsin_tables(L, 2 * L), _cos_nsin_tables(L, L), _hyena_positions(L))
              for L, _, _ in seqs}

    w_in_b = jnp.swapaxes(w_in, 1, 2).astype(BF16)
    w_pa_b, w_pb_b, w_pc_b, w_o_b, w_gu_b, w_down_b = (
        w.astype(BF16) for w in (w_pa, w_pb, w_pc, w_o, w_gu, w_down))

    x = None
    ctx_states = None
    for l in range(DEPTH):
        mod3 = mod[l, 0:3]
        if l == 0:
            xs = (x_prompt.reshape(T_CTX, D_MODEL), x_sample.reshape(T_LAT, D_MODEL), _grid_pos_embed(L_LAT))
            pd, ph, pf, pg, x = _inproj_call(xs, mod3, norm1_g[l], w_in_b, l)
        else:
            pd, ph, pf, pg = _inproj_call((x,), mod3, norm1_g[l], w_in_b, l)

        par = jnp.zeros((8, LANE), F32)
        par = par.at[0, 2 * H_A:4 * H_A].set(a_log[l].reshape(-1))
        par = par.at[1, 2 * H_A:4 * H_A].set(dt_bias[l].reshape(-1))
        *o_ctx, ctx_states = _delta_call(pd, conv_qkv[l], par, ones_bd, None,
                                         N_CTX_SEQ, L_CTX // DELTA_RB, 0, ctx_states)
        s0 = state_delta[:, l].astype(F32).reshape(N_LAT_SEQ, 2, H_A * DK, DV)
        *o_lat, _ = _delta_call(pd, conv_qkv[l], par, ones_bd, s0,
                                N_LAT_SEQ, L_LAT // DELTA_RB, T_CTX // DELTA_RB)

        w1p = jnp.pad(hy_w1[l], ((0, LANE - HY_EMB), (0, 0)))
        yb, yc = [], []
        for L, n_seq, blk0 in seqs:
            (cos2, nsin2), (cos1, nsin1), (zpos, window) = tables[L]
            kspec = _filter_call(L, zpos, w1p, hy_b1[l][None], hy_freq[l][None], hy_w2[l],
                                 hy_b2[l][None], hy_w3[l], window, cos2, nsin2)
            bias = hy_bias[l][:, None, :]
            if L == HYENA_FT:
                assert blk0 == 0
                yb.append(_hyena_short_call(ph, conv_hy[l], kspec, bias, cos2, nsin2, n_seq, L))
                yc.append(_fnet_short_call(pf, cos1, nsin1, bdc, bds, n_seq, L))
            else:
                yb.append(_hyena_call(ph, conv_hy[l], kspec, bias, cos2, nsin2, n_seq, L, blk0, HYENA_FT))
                yc.append(_fnet_call(pf, cos1, nsin1, bdc, bds, n_seq, L, blk0))

        na512 = jnp.tile(norm_a[l], H_A)[None]
        post = functools.partial(_postmix_call, o_ctx, o_lat, pd, yb, yc, pg, x, mod3, na512, ones_bd,
                                 w_pa_b, w_pb_b, w_pc_b, w_o_b, norm2_g[l], w_gu_b, w_down_b, norm_f, l)
        if l < DEPTH - 1:
            x = post(False)
        else:
            y_prompt = post(True, 0, N_CTX_TILES).reshape(N_CTX_SEQ, L_CTX, D_MODEL)
            y_sample = post(True, N_CTX_TILES, N_LAT_TILES).reshape(N_LAT_SEQ, L_LAT, D_MODEL)

    new_state = ctx_states.reshape(N_CTX_SEQ, DEPTH, 2, H_A, DK, DV).astype(x_prompt.dtype)
    return (y_prompt, y_sample, new_state)
```

```python
import functools
import math

import jax
import jax.numpy as jnp
from jax import lax
from jax.experimental import pallas as pl
from jax.experimental.pallas import tpu as pltpu

F32 = jnp.float32
BF16 = jnp.bfloat16

D_MODEL = 1024
N_CTX_SEQ = 32
L_CTX = 256
DEPTH = 2
N_LAT_SEQ = 2
L_LAT = 2048
GRID_W = 64
EPS = 1e-6
H_A = 8
DK = 64
DV = 64
W_A = H_A * DV
QKV_W = 2 * H_A * DK + H_A * DV
CHUNK = 64
W_B = 512
HY_EMB = 33
HY_HID = 64
HY_DECAY_TARGET = 1e-2
HY_FAST_PCT = 0.3
HY_SLOW_PCT = 1.5
G_C = 8
DC = 64
W_C = G_C * DC
D_FF = ((8 * D_MODEL + 3 * 256 - 1) // (3 * 256)) * 256
OFF_Z = QKV_W
OFF_B = OFF_Z + W_A
OFF_A = OFF_B + 2 * H_A
OFF_HY = OFF_A + 2 * H_A
OFF_FN = OFF_HY + 3 * W_B
OFF_GATE = OFF_FN + W_C

T_CTX = N_CTX_SEQ * L_CTX
T_LAT = N_LAT_SEQ * L_LAT
T_ALL = T_CTX + T_LAT
ROW_TILE = 256
N_CTX_TILES = T_CTX // ROW_TILE
N_LAT_TILES = T_LAT // ROW_TILE
LANE = 128
PD_W = QKV_W + W_A + LANE
HEAD_GROUP = 4
GROUP_W = HEAD_GROUP * DK
N_GROUPS = H_A // HEAD_GROUP
VMEM_LIMIT = 56 * 1024 * 1024


def _cparams(sem, vmem=None):
    return pltpu.CompilerParams(dimension_semantics=sem, vmem_limit_bytes=vmem)


def _dot(a, b):
    return jnp.dot(a, b, preferred_element_type=F32)


def _dot_nt(a, b):
    return lax.dot_general(a, b, (((1,), (1,)), ((), ())), preferred_element_type=F32)


def _split(a, n):
    parts = []
    rem = a
    for i in range(n):
        p = rem.astype(BF16)
        parts.append(p)
        if i + 1 < n:
            rem = rem - p.astype(F32)
    return parts


def _mm3(a, b):
    ah, al = _split(a, 2)
    bh, bl = _split(b, 2)
    return _dot(ah, bh) + (_dot(ah, bl) + _dot(al, bh))


def _sigmoid(x):
    return 1.0 / (1.0 + jnp.exp(-x))


def _silu(x):
    hx = 0.5 * x
    return hx + hx * jnp.tanh(hx)


def _softplus(x):
    return jnp.maximum(x, 0.0) + jnp.log(1.0 + jnp.exp(-jnp.abs(x)))


def _mod_row_block(i):
    per_lat = L_LAT // ROW_TILE
    return jnp.where(i < N_CTX_TILES, 0, 1 + (i - N_CTX_TILES) // per_lat)


def _ctx_tile(i):
    return jnp.minimum(i, N_CTX_TILES - 1)


def _lat_tile(i):
    return jnp.maximum(i - N_CTX_TILES, 0)


def _ctx_lat_specs(width, tile0=0):
    return (pl.BlockSpec((ROW_TILE, width), lambda i: (_ctx_tile(i + tile0), 0)),
            pl.BlockSpec((ROW_TILE, width), lambda i: (_lat_tile(i + tile0), 0)))


MOD_TN = 1536


def _mod_kernel(st_ref, w_ref, b_ref, out_ref):
    s = _silu(st_ref[...])
    w = w_ref[0]
    rows = [jnp.sum(s[:, r:r + 1] * w, axis=0, keepdims=True) + b_ref[0] for r in range(3)]
    rows.append(jnp.zeros((5, MOD_TN), F32))
    out_ref[0] = jnp.concatenate(rows, axis=0)


def _mod_call(st, w_mod, b_mod):
    n6 = 6 * D_MODEL
    return pl.pallas_call(
        _mod_kernel,
        grid=(DEPTH, n6 // MOD_TN),
        in_specs=[
            pl.BlockSpec((D_MODEL, 8), lambda l, j: (0, 0)),
            pl.BlockSpec((1, D_MODEL, MOD_TN), lambda l, j: (l, 0, j)),
            pl.BlockSpec((1, 1, MOD_TN), lambda l, j: (l, 0, j)),
        ],
        out_specs=pl.BlockSpec((1, 8, MOD_TN), lambda l, j: (l, 0, j)),
        out_shape=jax.ShapeDtypeStruct((DEPTH, 8, n6), F32),
        compiler_params=_cparams(("parallel", "parallel")),
        name="adaln_mod",
    )(st, w_mod, b_mod.reshape(DEPTH, 1, n6))


INPROJ_TN = 512
INPROJ_WIDTHS = (PD_W, 3 * W_B, W_C, 3 * D_MODEL)
INPROJ_FEATURES = (OFF_HY, 3 * W_B, W_C, 3 * D_MODEL)


def _rms_mod(x, g, scale, shift):
    ms = jnp.mean(x * x, axis=-1, keepdims=True)
    return (x * lax.rsqrt(ms + EPS) * g) * (1.0 + scale) + shift


def _inproj_kernel(first, *refs):
    if first:
        (xc_ref, xl_ref, pos_ref, mod_ref, g_ref, w_ref, pd_ref, ph_ref, pf_ref, pg_ref, x_ref) = refs
        x = jnp.where(pl.program_id(0) < N_CTX_TILES, xc_ref[...], xl_ref[...] + pos_ref[...])
        x_ref[...] = x
    else:
        (xin_ref, mod_ref, g_ref, w_ref, pd_ref, ph_ref, pf_ref, pg_ref) = refs
        x = xin_ref[...]
    m = mod_ref[0]
    h = _rms_mod(x, g_ref[...], m[1:2], m[0:1]).astype(BF16)
    row0 = 0
    for o_ref, n_feat in zip((pd_ref, ph_ref, pf_ref, pg_ref), INPROJ_FEATURES):
        n = o_ref.shape[1]
        for c in range(0, n, INPROJ_TN):
            e = min(c + INPROJ_TN, n)
            ef = min(e, n_feat)
            y = _dot_nt(h, w_ref[0, row0 + c:row0 + ef, :])
            if ef < e:
                y = jnp.concatenate([y, jnp.zeros((y.shape[0], e - ef), F32)], axis=1)
            if o_ref is pg_ref:
                y = _sigmoid(y)
            o_ref[:, c:e] = y.astype(o_ref.dtype)
        row0 += n_feat


def _resident(shape, index_map):
    return pl.BlockSpec(shape, index_map, pipeline_mode=pl.Buffered(1))


def _inproj_call(xs, mod3, g, w_all, layer):
    first = len(xs) == 3
    widths = INPROJ_WIDTHS
    row = lambda i: (i, 0)
    const = lambda i: (0, 0)
    if first:
        per_lat = L_LAT // ROW_TILE
        x_specs = list(_ctx_lat_specs(D_MODEL)) + [
            pl.BlockSpec((ROW_TILE, D_MODEL), lambda i: (_lat_tile(i) % per_lat, 0))]
    else:
        x_specs = [pl.BlockSpec((ROW_TILE, D_MODEL), row)]
    out_widths = widths + ((D_MODEL,) if first else ())
    out_dtypes = (F32, F32, F32, BF16) + ((F32,) if first else ())
    return pl.pallas_call(
        functools.partial(_inproj_kernel, first),
        grid=(T_ALL // ROW_TILE,),
        in_specs=x_specs + [
            pl.BlockSpec((1, 6, D_MODEL), lambda i: (_mod_row_block(i), 0, 0)),
            pl.BlockSpec((1, D_MODEL), const),
            _resident((1, sum(INPROJ_FEATURES), D_MODEL), lambda i: (layer, 0, 0)),
        ],
        out_specs=[pl.BlockSpec((ROW_TILE, w), row) for w in out_widths],
        out_shape=[jax.ShapeDtypeStruct((T_ALL, w), dt) for w, dt in zip(out_widths, out_dtypes)],
        compiler_params=_cparams(("parallel",), VMEM_LIMIT),
        name="inproj_first" if first else "inproj",
    )(*xs, mod3, g.reshape(1, D_MODEL), w_all)


def _conv3_rows(cur, prev_row, next_row, w):
    n = cur.shape[0]
    sub = lax.broadcasted_iota(jnp.int32, (8, cur.shape[1]), 0)
    up = pltpu.roll(cur, 1, 0)
    up = jnp.concatenate([jnp.where(sub == 0, prev_row, up[0:8]), up[8:]], axis=0)
    dn = pltpu.roll(cur, n - 1, 0)
    dn = jnp.concatenate([dn[0:n - 8], jnp.where(sub == 7, next_row, dn[n - 8:])], axis=0)
    return up * w[0:1] + cur * w[1:2] + dn * w[2:3]


def _halo_specs(width, rows_per_block, n_row_blocks, blk_of):
    per = rows_per_block // 8
    last = n_row_blocks * per - 1
    prev = pl.BlockSpec((8, width), lambda *a: (jnp.maximum(blk_of(*a) * per - 1, 0), 0))
    nxt = pl.BlockSpec((8, width), lambda *a: (jnp.minimum((blk_of(*a) + 1) * per, last), 0))
    return prev, nxt


FILT_RT = 256


def _alternating_sum(x):
    t = lax.broadcasted_iota(jnp.int32, x.shape, 0)
    return jnp.sum(jnp.where(t % 2 == 0, x, -x), axis=0, keepdims=True)


def _filter_kernel(L, zpos_ref, w1_ref, b1_ref, fq_ref, w2_ref, b2_ref, w3_ref, win_ref,
                   cos_ref, nsin_ref, k_ref, hs_s, hm_s, krl_s, h2_s):
    o = pl.program_id(0)
    rt = pl.program_id(1)

    @pl.when((o == 0) & (rt == 0))
    def _():
        fq = fq_ref[...]
        for r0 in range(0, L, FILT_RT):
            rows = slice(r0, r0 + FILT_RT)
            h = jnp.sin(fq * (_mm3(zpos_ref[rows, :], w1_ref[...]) + b1_ref[...]))
            h2_s[rows, :] = jnp.sin(fq * (_mm3(h, w2_ref[...]) + b2_ref[...]))

    @pl.when(rt == 0)
    def _():
        alt_acc = jnp.zeros((1, W_B), F32)
        for r0 in range(0, L, FILT_RT):
            rows = slice(r0, r0 + FILT_RT)
            hf = _mm3(h2_s[rows, :], w3_ref[...])
            win = win_ref[rows, :]
            fw = hf[:, 0:W_B] * win
            bw = hf[:, W_B:2 * W_B] * win
            hsum = fw + bw
            hs_s[rows, :] = hsum.astype(BF16)
            hm_s[rows, :] = (fw - bw).astype(BF16)
            alt_acc = alt_acc + _alternating_sum(hsum)
        krl_s[...] = jnp.broadcast_to(alt_acc, krl_s.shape)

    p1 = _dot(cos_ref[...], hs_s[...])
    p2 = _dot(nsin_ref[...], hm_s[...])
    first = (rt * FILT_RT + lax.broadcasted_iota(jnp.int32, p1.shape, 0)) == 0
    k_ref[0, 0] = p1
    k_ref[0, 1] = jnp.where(first, krl_s[0:1, :], p1)
    k_ref[0, 2] = jnp.where(first, 0.0, p2)


def _filter_call(L, zpos, w1p, b1, fq, w2, b2, w3, win, cos, nsin):
    nrt = L // FILT_RT
    c2 = lambda o, r: (0, 0)
    return pl.pallas_call(
        functools.partial(_filter_kernel, L),
        grid=(2, nrt),
        in_specs=[
            pl.BlockSpec((L, LANE), c2),
            pl.BlockSpec((LANE, HY_HID), c2),
            pl.BlockSpec((1, HY_HID), c2),
            pl.BlockSpec((1, HY_HID), c2),
            pl.BlockSpec((HY_HID, HY_HID), c2),
            pl.BlockSpec((1, HY_HID), c2),
            pl.BlockSpec((HY_HID, 2 * W_B), lambda o, r: (0, o)),
            pl.BlockSpec((L, W_B), c2),
            pl.BlockSpec((FILT_RT, L), lambda o, r: (r, 0)),
            pl.BlockSpec((FILT_RT, L), lambda o, r: (r, 0)),
        ],
        out_specs=pl.BlockSpec((1, 3, FILT_RT, W_B), lambda o, r: (o, 0, r, 0)),
        out_shape=jax.ShapeDtypeStruct((2, 3, L, W_B), F32),
        scratch_shapes=[pltpu.VMEM((L, W_B), BF16), pltpu.VMEM((L, W_B), BF16),
                        pltpu.VMEM((8, W_B), F32), pltpu.VMEM((L, HY_HID), F32)],
        compiler_params=_cparams(("arbitrary", "arbitrary"), VMEM_LIMIT),
        name=f"hyena_filter_{L}",
    )(zpos, w1p, b1, fq, w2, b2, w3, win, cos, nsin)


HYENA_FT = 256

HYENA_CONV_ROWS = 256


HYENA_CH = 256
HYENA_OUT_ROWS = 512


def _hyena_kernel(L, ft, x1_ref, x2_ref, v_ref, cw1_ref, cw2_ref, cwv_ref, k_ref, bias_ref, cos_ref, nsin_ref,
                  out_ref, gate_s, zf_s, zb_s, yt_s, yb_s):
    o = pl.program_id(2)

    @pl.when(o == 0)
    def _():
        for src, cw_ref, dst in ((x1_ref, cw1_ref, 0), (x2_ref, cw2_ref, 1), (v_ref, cwv_ref, None)):
            cw = cw_ref[...]
            for r0 in range(0, L, HYENA_CONV_ROWS):
                r1 = r0 + HYENA_CONV_ROWS
                prev_row = src[r0 - 1:r0, :] if r0 > 0 else 0.0
                next_row = src[r1:r1 + 1, :] if r1 < L else 0.0
                uc = _conv3_rows(src[r0:r1, :], prev_row, next_row, cw)
                if dst is None:
                    zf_s[r0:r1, :] = uc
                    zb_s[r0:r1, :] = uc.astype(BF16)
                else:
                    gate_s[dst, r0:r1, :] = uc

    zb = zb_s[...]
    nyq = _alternating_sum(zf_s[...])
    for f in range(L // ft):
        rows = slice(f * ft, (f + 1) * ft)
        top = _dot(cos_ref[rows, :], zb)
        bot = _dot(nsin_ref[rows, :], zb)
        if f == 0:
            first = lax.broadcasted_iota(jnp.int32, top.shape, 0) == 0
            bot = jnp.where(first, nyq, bot)
        krt = k_ref[0, 0, rows, :]
        krb = k_ref[0, 1, rows, :]
        ki = k_ref[0, 2, rows, :]
        yt = top * krt - bot * ki
        yb = top * ki + bot * krb
        if f == 0:
            yt = jnp.where(first, 0.5 * yt, yt)
            y_nyq = yb[0:1, :]
        yt_s[rows, :] = yt.astype(BF16)
        yb_s[rows, :] = yb.astype(BF16)

    t = lax.broadcasted_iota(jnp.int32, (HYENA_OUT_ROWS, HYENA_CH), 0)
    alt_half = jnp.where(t % 2 == 0, 0.5, -0.5)
    for r0 in range(0, L, HYENA_OUT_ROWS):
        rows = slice(r0, r0 + HYENA_OUT_ROWS)
        acc = _dot(cos_ref[rows, :], yt_s[...]) + _dot(nsin_ref[rows, :], yb_s[...]) + alt_half * y_nyq
        znew = gate_s[o, rows, :] * (acc * (1.0 / L) + bias_ref[0] * zf_s[rows, :])
        zf_s[rows, :] = znew
        zb_s[rows, :] = znew.astype(BF16)
        out_ref[rows, :] = znew.astype(out_ref.dtype)


def _hyena_call(ph, conv_w, kspec, bias, cos, nsin, n_seq, L, row_blk0, ft):
    nch = W_B // HYENA_CH
    once = pl.Buffered(1)
    col = lambda part: (lambda s, c, o: (row_blk0 + s, part * nch + c))
    cwcol = lambda part: (lambda s, c, o: (0, part * nch + c))
    return pl.pallas_call(
        functools.partial(_hyena_kernel, L, ft),
        grid=(n_seq, nch, 2),
        in_specs=[pl.BlockSpec((L, HYENA_CH), col(part)) for part in range(3)]
        + [pl.BlockSpec((3, HYENA_CH), cwcol(part)) for part in range(3)]
        + [
            pl.BlockSpec((1, 3, L, HYENA_CH), lambda s, c, o: (o, 0, 0, c)),
            pl.BlockSpec((1, 1, HYENA_CH), lambda s, c, o: (o, 0, c)),
            pl.BlockSpec((L, L), lambda s, c, o: (0, 0), pipeline_mode=once),
            pl.BlockSpec((L, L), lambda s, c, o: (0, 0), pipeline_mode=once),
        ],
        out_specs=pl.BlockSpec((L, HYENA_CH), lambda s, c, o: (s, c)),
        out_shape=jax.ShapeDtypeStruct((n_seq * L, W_B), BF16),
        scratch_shapes=[pltpu.VMEM((2, L, HYENA_CH), F32), pltpu.VMEM((L, HYENA_CH), F32),
                        pltpu.VMEM((L, HYENA_CH), BF16), pltpu.VMEM((L, HYENA_CH), BF16),
                        pltpu.VMEM((L, HYENA_CH), BF16)],
        compiler_params=_cparams(("parallel", "parallel", "arbitrary"), VMEM_LIMIT),
        name=f"hyena_conv_{L}",
    )(ph, ph, ph, conv_w, conv_w, conv_w, kspec, bias, cos, nsin)


HYENA_SHORT_SB = 4


def _hyena_short_kernel(L, ph_ref, cw_ref, k_ref, bias_ref, cos_ref, nsin_f_ref, cos_i_ref, nsin_i_ref, out_ref):
    cos, nsin_f, cos_i, nsin_i = cos_ref[...], nsin_f_ref[...], cos_i_ref[...], nsin_i_ref[...]
    cw = cw_ref[...]
    rows = [slice(s * L, (s + 1) * L) for s in range(HYENA_SHORT_SB)]
    ucs = [_conv3_rows(ph_ref[r, :], 0.0, 0.0, cw) for r in rows]
    zs = [uc[:, 2 * W_B:3 * W_B] for uc in ucs]
    for o in range(2):
        krt, krb, ki = k_ref[o, 0], k_ref[o, 1], k_ref[o, 2]
        zbs = [z.astype(BF16) for z in zs]
        tops = [_dot(cos, zb) for zb in zbs]
        bots = [_dot(nsin_f, zb) for zb in zbs]
        yts = [top * krt - bot * ki for top, bot in zip(tops, bots)]
        ybs = [top * ki + bot * krb for top, bot in zip(tops, bots)]
        accs = [_dot(cos_i, yt.astype(BF16)) + _dot(nsin_i, yb.astype(BF16)) for yt, yb in zip(yts, ybs)]
        zs = [uc[:, o * W_B:(o + 1) * W_B] * (acc * (1.0 / L) + bias_ref[o] * z)
              for uc, acc, z in zip(ucs, accs, zs)]
    for r, z in zip(rows, zs):
        out_ref[r, :] = z.astype(out_ref.dtype)


def _hyena_short_call(ph, conv_w, kspec, bias, cos, nsin, n_seq, L):
    rows = HYENA_SHORT_SB * L
    alt = jnp.where(jnp.arange(L) % 2 == 0, 1.0, -1.0).astype(BF16)
    nsin_f = nsin.at[0, :].set(alt)
    cos_i = cos.at[:, 0].set(0.5)
    nsin_i = nsin.at[:, 0].set(0.5 * alt)
    return pl.pallas_call(
        functools.partial(_hyena_short_kernel, L),
        grid=(n_seq // HYENA_SHORT_SB,),
        in_specs=[
            pl.BlockSpec((rows, 3 * W_B), lambda i: (i, 0)),
            pl.BlockSpec((3, 3 * W_B), lambda i: (0, 0)),
            pl.BlockSpec((2, 3, L, W_B), lambda i: (0, 0, 0, 0)),
            pl.BlockSpec((2, 1, W_B), lambda i: (0, 0, 0)),
        ] + [pl.BlockSpec((L, L), lambda i: (0, 0)) for _ in range(4)],
        out_specs=pl.BlockSpec((rows, W_B), lambda i: (i, 0)),
        out_shape=jax.ShapeDtypeStruct((n_seq * L, W_B), BF16),
        compiler_params=_cparams(("parallel",), VMEM_LIMIT),
        name=f"hyena_conv_{L}",
    )(ph, conv_w, kspec, bias, cos, nsin_f, cos_i, nsin_i)


FNET_RT = 512


def _fnet_kernel(L, x_ref, cos_ref, nsin_ref, bdc_ref, bds_ref, out_ref, xc_s, xs_s):
    for r0 in range(0, L, FNET_RT):
        rows = slice(r0, r0 + FNET_RT)
        xb = x_ref[rows, :].astype(BF16)
        xc_s[rows, :] = _dot(xb, bdc_ref[...]).astype(BF16)
        xs_s[rows, :] = _dot(xb, bds_ref[...]).astype(BF16)
    for r0 in range(0, L, FNET_RT):
        rows = slice(r0, r0 + FNET_RT)
        y = _dot(cos_ref[rows, :], xc_s[...]) + _dot(nsin_ref[rows, :], xs_s[...])
        out_ref[rows, :] = (y * (1.0 / math.sqrt(DC * L))).astype(out_ref.dtype)


def _fnet_call(pf, cos, nsin, bdc, bds, n_seq, L, row_blk0):
    const = lambda s: (0, 0)
    return pl.pallas_call(
        functools.partial(_fnet_kernel, L),
        grid=(n_seq,),
        in_specs=[
            pl.BlockSpec((L, W_C), lambda s: (row_blk0 + s, 0)),
            _resident((L, L), const),
            _resident((L, L), const),
            pl.BlockSpec((W_C, W_C), const),
            pl.BlockSpec((W_C, W_C), const),
        ],
        out_specs=pl.BlockSpec((L, W_C), lambda s: (s, 0)),
        out_shape=jax.ShapeDtypeStruct((n_seq * L, W_C), BF16),
        scratch_shapes=[pltpu.VMEM((L, W_C), BF16), pltpu.VMEM((L, W_C), BF16)],
        compiler_params=_cparams(("parallel",), VMEM_LIMIT),
        name=f"fnet_{L}",
    )(pf, cos, nsin, bdc, bds)


FNET_SHORT_SB = 4


def _fnet_short_kernel(L, x_ref, cos_ref, nsin_ref, bdc_ref, bds_ref, out_ref):
    xb = x_ref[...].astype(BF16)
    xc = _dot(xb, bdc_ref[...]).astype(BF16)
    xs = _dot(xb, bds_ref[...]).astype(BF16)
    for s in range(FNET_SHORT_SB):
        rows = slice(s * L, (s + 1) * L)
        y = _dot(cos_ref[...], xc[rows]) + _dot(nsin_ref[...], xs[rows])
        out_ref[rows, :] = (y * (1.0 / math.sqrt(DC * L))).astype(out_ref.dtype)


def _fnet_short_call(pf, cos, nsin, bdc, bds, n_seq, L):
    rows = FNET_SHORT_SB * L
    const = lambda i: (0, 0)
    return pl.pallas_call(
        functools.partial(_fnet_short_kernel, L),
        grid=(n_seq // FNET_SHORT_SB,),
        in_specs=[pl.BlockSpec((rows, W_C), lambda i: (i, 0)),
                  pl.BlockSpec((L, L), const), pl.BlockSpec((L, L), const),
                  pl.BlockSpec((W_C, W_C), const), pl.BlockSpec((W_C, W_C), const)],
        out_specs=pl.BlockSpec((rows, W_C), lambda i: (i, 0)),
        out_shape=jax.ShapeDtypeStruct((n_seq * L, W_C), BF16),
        compiler_params=_cparams(("parallel",), VMEM_LIMIT),
        name=f"fnet_{L}",
    )(pf, cos, nsin, bdc, bds)


DELTA_RB = 256
SOLVE_BATCH = 8
CHUNKS_PER_RB = DELTA_RB // CHUNK


HEADS_PER_LANE_TILE = LANE // DK


def _block_diag(y, half_masks):
    yb = y.astype(BF16)
    zero = jnp.zeros((CHUNK, LANE), BF16)
    row_blocks = []
    for h in range(HEAD_GROUP):
        tile = h // HEADS_PER_LANE_TILE
        piece = yb[:, tile * LANE:(tile + 1) * LANE] * half_masks[h % HEADS_PER_LANE_TILE]
        row_blocks.append(jnp.concatenate(
            [piece if t == tile else zero for t in range(GROUP_W // LANE)], axis=1))
    return jnp.concatenate(row_blocks, axis=0)


def _stacked_const_rhs(arrs, c, n):
    m = arrs[0].shape[0]
    parts = [p for a in arrs for p in _split(a, n)]
    y = _dot(jnp.concatenate(parts, axis=0), c)
    outs = []
    for i in range(len(arrs)):
        acc = y[i * n * m:(i * n + 1) * m]
        for t in range(1, n):
            acc = acc + y[(i * n + t) * m:(i * n + t + 1) * m]
        outs.append(acc)
    return outs


def _head_sums(arrs, ones_group, n_split=2):
    groups = [_stacked_const_rhs([a[:, g * GROUP_W:(g + 1) * GROUP_W] for a in arrs], ones_group, n_split)
              for g in range(N_GROUPS)]
    return [jnp.concatenate([groups[g][i] for g in range(N_GROUPS)], axis=1) for i in range(len(arrs))]


def _expand_heads(x, lane0):
    cols = [jnp.broadcast_to(x[:, lane0 + h:lane0 + h + 1], (x.shape[0], DK)) for h in range(H_A)]
    return jnp.concatenate(cols, axis=1)


def _const_lhs_split(c, b, n):
    w = b.shape[1]
    y = _dot(c, jnp.concatenate(_split(b, n), axis=1))
    acc = y[:, 0:w]
    for t in range(1, n):
        acc = acc + y[:, t * w:(t + 1) * w]
    return acc


def _delta_chunk_stages(pd_ref, prev_ref, next_ref, r, has_prev, has_next, dirs, cw, a_neg, dtb, ones_bd,
                        tri, half_masks, eye_tile):
    ch = {}
    rows = slice(r * CHUNK, (r + 1) * CHUNK)

    def conv():
        cur = pd_ref[rows, 0:QKV_W]
        if r == 0:
            prev_row = jnp.where(has_prev, prev_ref[7:8, :], 0.0)
        else:
            prev_row = pd_ref[r * CHUNK - 1:r * CHUNK, 0:QKV_W]
        if r == CHUNKS_PER_RB - 1:
            next_row = jnp.where(has_next, next_ref[0:1, :], 0.0)
        else:
            next_row = pd_ref[(r + 1) * CHUNK:(r + 1) * CHUNK + 1, 0:QKV_W]
        qkv = _silu(_conv3_rows(cur, prev_row, next_row, cw))
        ch["q"] = qkv[:, 0:H_A * DK]
        ch["k"] = qkv[:, H_A * DK:2 * H_A * DK]
        ch["v"] = qkv[:, 2 * H_A * DK:]

    def norms():
        q, k = ch.pop("q"), ch.pop("k")
        qss, kss = _head_sums([q * q, k * k], ones_bd, 1)
        ch["qn"] = q * lax.rsqrt(qss + EPS) * (DK ** -0.5)
        ch["kn"] = k * lax.rsqrt(kss + EPS)

    def gram():
        ch["gram"], ch["knT"] = [], []
        for g in range(N_GROUPS):
            lanes = slice(g * GROUP_W, (g + 1) * GROUP_W)
            lhs = jnp.concatenate([ch["kn"][:, lanes].astype(BF16), ch["qn"][:, lanes].astype(BF16), eye_tile],
                                  axis=0)
            res = _dot_nt(lhs, _block_diag(ch["kn"][:, lanes], half_masks))
            ch["gram"].append(res[0:2 * CHUNK])
            ch["knT"].append(res[2 * CHUNK:3 * CHUNK])

    def decay():
        ba = pd_ref[rows, OFF_B:OFF_B + LANE]
        sig = _sigmoid(ba)
        glog = a_neg * _softplus(ba + dtb)
        ch["decay"] = {}
        for d in dirs:
            gcum = _const_lhs_split(tri[d], glog, 3)
            beta = _expand_heads(sig, d * H_A)
            gcc8 = _expand_heads(gcum, (2 + d) * H_A)
            ch["decay"][d] = (beta, gcc8)

    return ch, [conv, norms, gram, decay]


def _delta_kernel(nb, zero_init, n_prev, *refs):
    shared = nb == 1
    it = iter(refs)
    blocks = [(next(it), next(it), next(it))]
    if not shared:
        blocks.append((next(it), next(it), next(it)))
    cw_ref, par_ref, ones_ref = next(it), next(it), next(it)
    s0_ref = None if zero_init else next(it)
    sprev_ref = next(it) if n_prev else None
    o_refs = (next(it), next(it))
    sfin_ref = next(it)
    u_s, w_s, p_s, qg_s, kg_s, gl_s, st_s = (next(it) for _ in range(7))
    j = pl.program_id(1)

    ri = lax.broadcasted_iota(jnp.int32, (CHUNK, GROUP_W), 0)
    cj = lax.broadcasted_iota(jnp.int32, (CHUNK, GROUP_W), 1) % CHUNK
    ixj = ri ^ cj
    eye = ixj == 0
    eye_tile = jnp.where(eye, 1.0, 0.0).astype(BF16)
    hl =lax.broadcasted_iota(jnp.int32, (CHUNK, LANE), 1) // DK
    half_masks = tuple(jnp.where(hl == h, 1.0, 0.0).astype(BF16) for h in range(HEADS_PER_LANE_TILE))
    ti = lax.broadcasted_iota(jnp.int32, (CHUNK, CHUNK), 0)
    tm = lax.broadcasted_iota(jnp.int32, (CHUNK, CHUNK), 1)
    ri8 = lax.broadcasted_iota(jnp.int32, (CHUNK, H_A * DK), 0)
    cj8 = lax.broadcasted_iota(jnp.int32, (CHUNK, H_A * DK), 1) % CHUNK
    incl = (ri >= cj, ri <= cj)
    strict = (ri > cj, ri < cj)
    tri = tuple(jnp.where(m, 1.0, 0.0).astype(BF16) for m in (tm <= ti, tm >= ti))
    eye8 = ri8 == cj8
    last_row = (CHUNK - 1, 0)

    @pl.when(j == 0)
    def _():
        for d in range(2):
            for g in range(N_GROUPS):
                if zero_init:
                    st_s[d, g] = jnp.zeros((DK, GROUP_W), F32)
                else:
                    r0 = g * GROUP_W
                    st_s[d, g] = jnp.concatenate(
                        [s0_ref[0, d, r0 + hh * DK:r0 + (hh + 1) * DK, :] for hh in range(HEAD_GROUP)], axis=1)

    cw = cw_ref[...]
    a_neg = -jnp.exp(par_ref[0:1, :])
    dtb = par_ref[1:2, :]
    pos = (j, nb - 1 - j)
    scan_order = (tuple(range(CHUNKS_PER_RB)), tuple(reversed(range(CHUNKS_PER_RB))))

    def unit_thunk(d, r, ch, g, units):
        def run():
            rows = slice(r * CHUNK, (r + 1) * CHUNK)
            lanes = slice(g * GROUP_W, (g + 1) * GROUP_W)
            beta, gcc8 = ch["decay"][d]
            gcr = jnp.sum(jnp.where(eye8, gcc8, 0.0), axis=0, keepdims=True)[:, lanes]
            qn, kn, be, gcc = ch["qn"][:, lanes], ch["kn"][:, lanes], beta[:, lanes], gcc8[:, lanes]
            kq = ch["gram"][g]
            dec = jnp.exp(jnp.where(incl[d], gcc - gcr, -1e30))
            a = jnp.where(strict[d], kq[0:CHUNK] * be * dec, 0.0)
            eg = jnp.exp(gcc)
            gcl = gcc[last_row[d]:last_row[d] + 1, :]
            units.append((d, rows, lanes, a, ch["v"][:, lanes] * be, kn * be * eg))
            p_s[d, rows, lanes] = (kq[CHUNK:2 * CHUNK] * dec).astype(BF16)
            qg_s[d, rows, lanes] = (qn * eg).astype(BF16)
            kg_s[d, rows, lanes] = (ch["knT"][g] * jnp.exp(gcl - gcr)).astype(BF16)
            gl_s[d, r * 8:(r + 1) * 8, lanes] = jnp.broadcast_to(jnp.exp(gcl), (8, GROUP_W))
        return run

    def prep_thunks(units):
        thunks = []
        if shared:
            todo = [(0, r, (0, 1)) for r in range(CHUNKS_PER_RB)]
        else:
            todo = [(d, r, (d,)) for d in range(2) for r in scan_order[d]]
        for b, r, dirs in todo:
            ch, stages = _delta_chunk_stages(*blocks[b], r, pos[b] > 0, pos[b] < nb - 1, dirs, cw, a_neg, dtb,
                                             ones_ref[...], tri, half_masks, eye_tile)
            thunks += stages
            thunks += [unit_thunk(d, r, ch, g, units) for d in dirs for g in range(N_GROUPS)]
        return thunks

    def scan_thunk(d, c):
        def run():
            r = scan_order[d][c]
            rows = slice(r * CHUNK, (r + 1) * CHUNK)
            for g in range(N_GROUPS):
                lanes = slice(g * GROUP_W, (g + 1) * GROUP_W)
                s = st_s[d, g]
                wq = jnp.concatenate([w_s[d, rows, lanes], qg_s[d, rows, lanes]], axis=0)
                ws_qs = _dot(wq, _block_diag(s, half_masks))
                v_new = u_s[d, rows, lanes] - ws_qs[0:CHUNK]
                pk = jnp.concatenate([p_s[d, rows, lanes], kg_s[d, rows, lanes]], axis=0)
                po = _dot(pk, _block_diag(v_new, half_masks))
                st_s[d, g] = s * gl_s[d, r * 8:r * 8 + 1, lanes] + po[CHUNK:2 * CHUNK]
                o_refs[d][rows, lanes] = ws_qs[CHUNK:2 * CHUNK] + po[0:CHUNK]
        return run

    def solve(units):
        r0s = [jnp.where(ixj == 1, un[3], 0.0) for un in units]
        zxs = [jnp.concatenate([un[3] - _dot(un[3].astype(BF16), _block_diag(r0, half_masks)),
                                jnp.where(eye, 1.0, 0.0) - r0], axis=0) for un, r0 in zip(units, r0s)]
        for lvl in range(1, 5):
            links = [_block_diag(jnp.where((ixj >> lvl) == 1, zx[0:CHUNK], 0.0), half_masks) for zx in zxs]
            zxs = [zx - _dot(zx.astype(BF16), lk) for zx, lk in zip(zxs, links)]
        links = [_block_diag(jnp.where((ixj >> 5) == 1, zx[0:CHUNK], 0.0), half_masks) for zx in zxs]
        xs = [zx[CHUNK:2 * CHUNK] - _dot(zx[CHUNK:2 * CHUNK].astype(BF16), lk) for zx, lk in zip(zxs, links)]
        for x, (d, rows, lanes, _, vb, kbe) in zip(xs, units):
            rhs = jnp.concatenate([_block_diag(vb, half_masks), _block_diag(kbe, half_masks)], axis=1)
            uw = _dot(x.astype(BF16), rhs)
            u_s[d, rows, lanes] = uw[:, 0:GROUP_W]
            w_s[d, rows, lanes] = uw[:, GROUP_W:2 * GROUP_W].astype(BF16)

    units = []
    for f in prep_thunks(units):
        f()
    for i in range(0, len(units), SOLVE_BATCH):
        solve(units[i:i + SOLVE_BATCH])
    for c in range(CHUNKS_PER_RB):
        for d in range(2):
            scan_thunk(d, c)()

    @pl.when(j == nb - 1)
    def _():
        for d in range(2):
            for g in range(N_GROUPS):
                s = st_s[d, g]
                for hh in range(HEAD_GROUP):
                    r0 = g * GROUP_W + hh * DK
                    sfin_ref[0, n_prev, d, r0:r0 + DK, :] = s[:, hh * DV:(hh + 1) * DV]
        for p in range(n_prev):
            sfin_ref[0, p] = sprev_ref[0, p]


def _delta_call(pd, conv_w, par, ones_bd, s0, n_seq, nb, blk0, prev_states=None):
    zero_init = s0 is None
    n_prev = 0 if prev_states is None else prev_states.shape[1]
    n_blocks_all = T_ALL // DELTA_RB
    blk_of = (lambda s, j: blk0 + s * nb + j, lambda s, j: blk0 + s * nb + nb - 1 - j)
    in_specs, args = [], []
    for d in range(1 if nb == 1 else 2):
        prev, nxt = _halo_specs(QKV_W, DELTA_RB, n_blocks_all, blk_of[d])
        in_specs += [pl.BlockSpec((DELTA_RB, PD_W), lambda s, j, d=d: (blk_of[d](s, j), 0)), prev, nxt]
        args += [pd, pd, pd]
    in_specs += [
        pl.BlockSpec((3, QKV_W), lambda s, j: (0, 0)),
        pl.BlockSpec((8, LANE), lambda s, j: (0, 0)),
        pl.BlockSpec((GROUP_W, GROUP_W), lambda s, j: (0, 0)),
    ]
    args += [conv_w, par, ones_bd]
    if not zero_init:
        in_specs.append(pl.BlockSpec((1, 2, H_A * DK, DV), lambda s, j: (s, 0, 0, 0)))
        args.append(s0)
    if n_prev:
        in_specs.append(pl.BlockSpec((1, n_prev, 2, H_A * DK, DV), lambda s, j: (s, 0, 0, 0, 0)))
        args.append(prev_states)
    rows = n_seq * nb * DELTA_RB
    dir_buf = lambda n, dt=F32: pltpu.VMEM((2, n, H_A * DK), dt)
    return pl.pallas_call(
        functools.partial(_delta_kernel, nb, zero_init, n_prev),
        grid=(n_seq, nb),
        in_specs=in_specs,
        out_specs=[pl.BlockSpec((DELTA_RB, W_A), lambda s, j: (s * nb + j, 0)),
                   pl.BlockSpec((DELTA_RB, W_A), lambda s, j: (s * nb + nb - 1 - j, 0)),
                   pl.BlockSpec((1, n_prev + 1, 2, H_A * DK, DV), lambda s, j: (s, 0, 0, 0, 0))],
        out_shape=[jax.ShapeDtypeStruct((rows, W_A), F32),
                   jax.ShapeDtypeStruct((rows, W_A), F32),
                   jax.ShapeDtypeStruct((n_seq, n_prev + 1, 2, H_A * DK, DV), F32)],
        scratch_shapes=[dir_buf(DELTA_RB)] + [dir_buf(DELTA_RB, BF16) for _ in range(4)]
        + [dir_buf(CHUNKS_PER_RB * 8), pltpu.VMEM((2, N_GROUPS, DK, GROUP_W), F32)],
        compiler_params=_cparams(("parallel", "arbitrary"), VMEM_LIMIT),
        name=f"deltanet_nb{nb}",
    )(*args)


FFN_TN = D_FF


def _postmix_kernel(final_norm, tile0, ofc_ref, ofl_ref, obc_ref, obl_ref, z_ref, ybc_ref, ybl_ref,
                    ycc_ref, ycl_ref, pg_ref, x_ref, mod_ref, na_ref, ones_ref, wpa_ref, wpb_ref, wpc_ref,
                    wo_ref, g2_ref, wgu_ref, wdn_ref, nf_ref, out_ref):
    is_ctx = pl.program_id(0) + tile0 < N_CTX_TILES
    m = mod_ref[0]
    o = jnp.where(is_ctx, ofc_ref[...] + obc_ref[...], ofl_ref[...] + obl_ref[...])
    yb = jnp.where(is_ctx, ybc_ref[...], ybl_ref[...])
    yc = jnp.where(is_ctx, ycc_ref[...], ycl_ref[...])
    ms = _head_sums([o * o], ones_ref[...])[0] * (1.0 / DV)
    ya = (o * lax.rsqrt(ms + EPS) * na_ref[...]) * _silu(z_ref[...])
    merged = (pg_ref[:, 0:D_MODEL].astype(F32) * _dot(ya.astype(BF16), wpa_ref[0])
              + pg_ref[:, D_MODEL:2 * D_MODEL].astype(F32) * _dot(yb.astype(BF16), wpb_ref[0])
              + pg_ref[:, 2 * D_MODEL:3 * D_MODEL].astype(F32) * _dot(yc.astype(BF16), wpc_ref[0]))
    x = x_ref[...] + m[2:3] * _dot(merged.astype(BF16), wo_ref[0])

    h = _rms_mod(x, g2_ref[...], m[4:5], m[3:4]).astype(BF16)
    acc = None
    for c in range(0, D_FF, FFN_TN):
        gate = _dot(h, wgu_ref[0, :, c:c + FFN_TN])
        up = _dot(h, wgu_ref[0, :, D_FF + c:D_FF + c + FFN_TN])
        part = _dot((_silu(gate) * up).astype(BF16), wdn_ref[0, c:c + FFN_TN, :])
        acc = part if acc is None else acc + part
    xn = x + m[5:6] * acc
    if final_norm:
        ms = jnp.mean(xn * xn, axis=-1, keepdims=True)
        xn = xn * lax.rsqrt(ms + EPS) * nf_ref[...]
    out_ref[...] = xn


def _postmix_call(o_ctx, o_lat, pd, yb, yc, pg, x, mod3, na512, ones_bd, wpa, wpb, wpc, wo,
                  g2, wgu, wdn, nf, layer, final_norm, tile0=0, n_tiles=T_ALL // ROW_TILE):
    row = lambda i: (i + tile0, 0)
    const = lambda i: (0, 0)
    lyr = lambda i: (layer, 0, 0)
    assert W_A == W_B == W_C
    return pl.pallas_call(
        functools.partial(_postmix_kernel, final_norm, tile0),
        grid=(n_tiles,),
        in_specs=[
            *_ctx_lat_specs(W_A, tile0), *_ctx_lat_specs(W_A, tile0),
            pl.BlockSpec((ROW_TILE, W_A), lambda i: (i + tile0, OFF_Z // W_A)),
            *_ctx_lat_specs(W_B, tile0), *_ctx_lat_specs(W_C, tile0),
            pl.BlockSpec((ROW_TILE, 3 * D_MODEL), row),
            pl.BlockSpec((ROW_TILE, D_MODEL), row),
            pl.BlockSpec((1, 6, D_MODEL), lambda i: (_mod_row_block(i + tile0), 0, 0)),
            pl.BlockSpec((1, W_A), const),
            _resident((GROUP_W, GROUP_W), const),
            _resident((1, W_A, D_MODEL), lyr),
            _resident((1, W_B, D_MODEL), lyr),
            _resident((1, W_C, D_MODEL), lyr),
            _resident((1, D_MODEL, D_MODEL), lyr),
            pl.BlockSpec((1, D_MODEL), const),
            _resident((1, D_MODEL, 2 * D_FF), lyr),
            _resident((1, D_FF, D_MODEL), lyr),
            pl.BlockSpec((1, D_MODEL), const),
        ],
        out_specs=pl.BlockSpec((ROW_TILE, D_MODEL), lambda i: (i, 0)),
        out_shape=jax.ShapeDtypeStruct((n_tiles * ROW_TILE, D_MODEL), F32),
        compiler_params=_cparams(("parallel",), VMEM_LIMIT),
        name="postmix_final" if final_norm else "postmix",
    )(o_ctx[0], o_lat[0], o_ctx[1], o_lat[1], pd, yb[0], yb[1], yc[0], yc[1], pg, x, mod3, na512,
      ones_bd, wpa, wpb, wpc, wo, g2.reshape(1, D_MODEL), wgu, wdn, nf.reshape(1, D_MODEL))


TABLE_SPLIT = 64


def _grid_pos_embed(n_tokens):
    rows = n_tokens // GRID_W
    quarter = D_MODEL // 4
    omega = 1.0 / (10000.0 ** (jnp.arange(quarter, dtype=F32) / quarter))

    def emb(pos):
        a = pos[:, None] * omega[None, :]
        return jnp.concatenate([jnp.sin(a), jnp.cos(a)], axis=-1)

    e_row, e_col = lax.optimization_barrier((emb(jnp.arange(rows).astype(F32)),
                                             emb(jnp.arange(GRID_W).astype(F32))))
    return jnp.concatenate([jnp.repeat(e_row, GRID_W, axis=0), jnp.tile(e_col, (rows, 1))], axis=-1)


def _cos_nsin_tables(n, period):
    t = jnp.arange(n, dtype=jnp.int32)[None, :]

    def cs(r):
        ang = ((r * t) % period).astype(F32) * (2.0 * math.pi / period)
        return jnp.cos(ang), jnp.sin(ang)

    ca, sa = cs(jnp.arange(n // TABLE_SPLIT, dtype=jnp.int32)[:, None] * TABLE_SPLIT)
    cb, sb = cs(jnp.arange(TABLE_SPLIT, dtype=jnp.int32)[:, None])
    ca, sa, cb, sb = lax.optimization_barrier((ca, sa, cb, sb))
    ca, sa = ca[:, None, :], sa[:, None, :]
    cos = (ca * cb[None] - sa * sb[None]).reshape(n, n)
    nsin = (-(sa * cb[None] + ca * sb[None])).reshape(n, n)
    return cos.astype(BF16), nsin.astype(BF16)


def _hyena_positions(L):
    bands = (HY_EMB - 1) // 2
    t = jnp.linspace(0.0, 1.0, L, dtype=F32)[:, None]
    wpos = (2.0 * math.pi / L) * jnp.arange(L, dtype=F32)[:, None]
    fr = jnp.linspace(1e-4, bands - 1, bands, dtype=F32)[None, :]
    zpos = jnp.concatenate([t, jnp.cos(fr * wpos), -jnp.sin(fr * wpos)], axis=-1)
    zpos = jnp.pad(zpos, ((0, 0), (0, LANE - HY_EMB)))
    deltas = jnp.abs(jnp.linspace(math.log(HY_DECAY_TARGET) / HY_SLOW_PCT,
                                  math.log(HY_DECAY_TARGET) / HY_FAST_PCT, W_B, dtype=F32))
    window = jnp.exp(-t * deltas[None, :])
    return zpos, window


def _group_tables():
    r = jnp.arange(DC, dtype=jnp.int32)
    ang = ((r[:, None] * r[None, :]) % DC).astype(F32) * (2.0 * math.pi / DC)
    eye = jnp.eye(G_C, dtype=F32)
    return jnp.kron(eye, jnp.cos(ang)).astype(BF16), jnp.kron(eye, jnp.sin(ang)).astype(BF16)


def _head_ones():
    return jnp.kron(jnp.eye(HEAD_GROUP, dtype=F32), jnp.ones((DK, DK), F32)).astype(BF16)


def kernel(x_prompt, x_sample, state_delta, c, c_ctx, w_mod, b_mod, norm1_g, norm2_g, w_in, conv_qkv, a_log, dt_bias, norm_a, conv_hy, hy_w1, hy_b1, hy_freq, hy_w2, hy_b2, hy_w3, hy_bias, w_pa, w_pb, w_pc, w_o, w_gu, w_down, norm_f):
    assert x_prompt.shape == (N_CTX_SEQ, L_CTX, D_MODEL) and x_sample.shape == (N_LAT_SEQ, L_LAT, D_MODEL)
    st = jnp.pad(jnp.concatenate([c_ctx[None], c], axis=0).T, ((0, 0), (0, 8 - 1 - N_LAT_SEQ)))
    mod = _mod_call(st, w_mod, b_mod).reshape(DEPTH, 8, 6, D_MODEL)

    ones_bd = _head_ones()
    bdc, bds = _group_tables()
    seqs = ((L_CTX, N_CTX_SEQ, 0), (L_LAT, N_LAT_SEQ, T_CTX // L_LAT))
    tables = {L: (_cos_nsin_tables(L, 2 * L), _cos_nsin_tables(L, L), _hyena_positions(L))
              for L, _, _ in seqs}

    w_in_b = jnp.swapaxes(w_in, 1, 2).astype(BF16)
    w_pa_b, w_pb_b, w_pc_b, w_o_b, w_gu_b, w_down_b = (
        w.astype(BF16) for w in (w_pa, w_pb, w_pc, w_o, w_gu, w_down))

    x = None
    ctx_states = None
    for l in range(DEPTH):
        mod3 = mod[l, 0:3]
        if l == 0:
            xs = (x_prompt.reshape(T_CTX, D_MODEL), x_sample.reshape(T_LAT, D_MODEL), _grid_pos_embed(L_LAT))
            pd, ph, pf, pg, x = _inproj_call(xs, mod3, norm1_g[l], w_in_b, l)
        else:
            pd, ph, pf, pg = _inproj_call((x,), mod3, norm1_g[l], w_in_b, l)

        par = jnp.zeros((8, LANE), F32)
        par = par.at[0, 2 * H_A:4 * H_A].set(a_log[l].reshape(-1))
        par = par.at[1, 2 * H_A:4 * H_A].set(dt_bias[l].reshape(-1))
        *o_ctx, ctx_states = _delta_call(pd, conv_qkv[l], par, ones_bd, None,
                                         N_CTX_SEQ, L_CTX // DELTA_RB, 0, ctx_states)
        s0 = state_delta[:, l].astype(F32).reshape(N_LAT_SEQ, 2, H_A * DK, DV)
        *o_lat, _ = _delta_call(pd, conv_qkv[l], par, ones_bd, s0,
                                N_LAT_SEQ, L_LAT // DELTA_RB, T_CTX // DELTA_RB)

        w1p = jnp.pad(hy_w1[l], ((0, LANE - HY_EMB), (0, 0)))
        yb, yc = [], []
        for L, n_seq, blk0 in seqs:
            (cos2, nsin2), (cos1, nsin1), (zpos, window) = tables[L]
            kspec = _filter_call(L, zpos, w1p, hy_b1[l][None], hy_freq[l][None], hy_w2[l],
                                 hy_b2[l][None], hy_w3[l], window, cos2, nsin2)
            bias = hy_bias[l][:, None, :]
            if L == HYENA_FT:
                assert blk0 == 0
                yb.append(_hyena_short_call(ph, conv_hy[l], kspec, bias, cos2, nsin2, n_seq, L))
                yc.append(_fnet_short_call(pf, cos1, nsin1, bdc, bds, n_seq, L))
            else:
                yb.append(_hyena_call(ph, conv_hy[l], kspec, bias, cos2, nsin2, n_seq, L, blk0, HYENA_FT))
                yc.append(_fnet_call(pf, cos1, nsin1, bdc, bds, n_seq, L, blk0))

        na512 = jnp.tile(norm_a[l], H_A)[None]
        post = functools.partial(_postmix_call, o_ctx, o_lat, pd, yb, yc, pg, x, mod3, na512, ones_bd,
                                 w_pa_b, w_pb_b, w_pc_b, w_o_b, norm2_g[l], w_gu_b, w_down_b, norm_f, l)
        if l < DEPTH - 1:
            x = post(False)
        else:
            y_prompt = post(True, 0, N_CTX_TILES).reshape(N_CTX_SEQ, L_CTX, D_MODEL)
            y_sample = post(True, N_CTX_TILES, N_LAT_TILES).reshape(N_LAT_SEQ, L_LAT, D_MODEL)

    new_state = ctx_states.reshape(N_CTX_SEQ, DEPTH, 2, H_A, DK, DV).astype(x_prompt.dtype)
    return (y_prompt, y_sample, new_state)
```
